```python
import jax, jax.numpy as jnp
from jax import lax
import numpy as np

D_MODEL = 1024
BATCH = 16
SEQ = 2048
DEPTH = 2

D_FF = ((8 * D_MODEL // 3 + 255) // 256) * 256
D_CONV = D_MODEL // 2
CONV_WIDTH = 31
D_SHORT = D_MODEL // 4
SHORT_WIDTH = 3
D_POOL = D_MODEL // 4
POOL_WINDOWS = (2, 4, 8, 16)
POOL_GROUPS = len(POOL_WINDOWS)
POOL_GROUP_DIM = D_POOL // POOL_GROUPS
N_BRANCH = 3
IN_COLS = 2 * D_CONV + 3 * D_SHORT + D_POOL + N_BRANCH * D_MODEL
EPS = 1e-6

kernel_name = "hybrid_gated_conv_shortconv_pool_macaron"

_SPLITS = list(np.cumsum([D_CONV, D_CONV, D_SHORT, D_SHORT, D_SHORT, D_POOL]).tolist())


def rms_norm(x, g):
    xf = x.astype(jnp.float32)
    y = xf * lax.rsqrt(jnp.mean(xf * xf, axis=-1, keepdims=True) + EPS)
    return (y * g.astype(jnp.float32)).astype(x.dtype)


def layer_norm(x, g, b):
    xf = x.astype(jnp.float32)
    mu = jnp.mean(xf, axis=-1, keepdims=True)
    xc = xf - mu
    y = xc * lax.rsqrt(jnp.mean(xc * xc, axis=-1, keepdims=True) + EPS)
    return (y * g.astype(jnp.float32) + b.astype(jnp.float32)).astype(x.dtype)


def swiglu(h, w_gate, w_up, w_down):
    return (jax.nn.silu(h @ w_gate) * (h @ w_up)) @ w_down


def causal_depthwise_conv(u, w):
    k_width, channels = w.shape
    return lax.conv_general_dilated(
        u, w[:, None, :].astype(u.dtype),
        window_strides=(1,), padding=[(k_width - 1, 0)],
        dimension_numbers=("NWC", "WIO", "NWC"),
        feature_group_count=channels)


def multiscale_causal_pool(u):
    seq = u.shape[1]
    uf = u.astype(jnp.float32)
    cs = jnp.cumsum(uf, axis=1)
    count_base = jnp.arange(1, seq + 1, dtype=jnp.float32)[None, :, None]
    means = []
    for g, w in enumerate(POOL_WINDOWS):
        csg = cs[..., g * POOL_GROUP_DIM:(g + 1) * POOL_GROUP_DIM]
        lagged = jnp.pad(csg, ((0, 0), (w, 0), (0, 0)))[:, :seq]
        means.append((csg - lagged) / jnp.minimum(count_base, float(w)))
    return (jnp.concatenate(means, axis=-1) - uf).astype(u.dtype)


def hybrid_mixer(h, w_in, conv_dw, conv_b, conv_ln_g, conv_ln_b, w_pa,
                 short_dw, w_pb, pool_w, pool_scale, w_pc, w_o):
    bsz, seq, _ = h.shape
    u = h @ w_in
    a_val, a_gate, b_gate, c_gate, b_x, p_in, gate_logits = jnp.split(u, _SPLITS, axis=-1)

    a = a_val * jax.nn.sigmoid(a_gate)
    a = causal_depthwise_conv(a, conv_dw) + conv_b
    a = jax.nn.silu(layer_norm(a, conv_ln_g, conv_ln_b))
    y_a = a @ w_pa

    s = causal_depthwise_conv(c_gate * b_x, short_dw)
    y_b = (b_gate * s) @ w_pb

    p = multiscale_causal_pool(p_in).reshape(bsz, seq, POOL_GROUPS, POOL_GROUP_DIM)
    p = jnp.einsum("bsgc,gcd->bsgd", p, pool_w).reshape(bsz, seq, D_POOL) * pool_scale
    y_c = p @ w_pc

    g = jax.nn.sigmoid(gate_logits).reshape(bsz, seq, N_BRANCH, D_MODEL)
    merged = g[..., 0, :] * y_a + g[..., 1, :] * y_b + g[..., 2, :] * y_c
    return merged @ w_o


def _fwd_setup_inputs(seed: int = 0) -> dict:
    key = jax.random.key(seed)
    ks = jax.random.split(key, 24)

    def dense(k, shape):
        return jax.random.normal(k, shape, jnp.float32) * (shape[-2] ** -0.5)

    def gain(k, shape, s=0.05):
        return 1.0 + s * jax.random.normal(k, shape, jnp.float32)

    def small(k, shape, s=0.02):
        return s * jax.random.normal(k, shape, jnp.float32)

    return {
        "x": jax.random.normal(ks[0], (BATCH, SEQ, D_MODEL), jnp.float32),
        "norm_ffn1_g": gain(ks[1], (DEPTH, D_MODEL)),
        "ffn1_w_gate": dense(ks[2], (DEPTH, D_MODEL, D_FF)),
        "ffn1_w_up": dense(ks[3], (DEPTH, D_MODEL, D_FF)),
        "ffn1_w_down": dense(ks[4], (DEPTH, D_FF, D_MODEL)),
        "norm_mix_g": gain(ks[5], (DEPTH, D_MODEL)),
        "w_in": dense(ks[6], (DEPTH, D_MODEL, IN_COLS)),
        "conv_dw": jax.random.normal(ks[7], (DEPTH, CONV_WIDTH, D_CONV), jnp.float32) * (CONV_WIDTH ** -0.5),
        "conv_b": small(ks[8], (DEPTH, D_CONV)),
        "conv_ln_g": gain(ks[9], (DEPTH, D_CONV)),
        "conv_ln_b": small(ks[10], (DEPTH, D_CONV)),
        "w_pa": dense(ks[11], (DEPTH, D_CONV, D_MODEL)),
        "short_dw": jax.random.normal(ks[12], (DEPTH, SHORT_WIDTH, D_SHORT), jnp.float32) * (SHORT_WIDTH ** -0.5),
        "w_pb": dense(ks[13], (DEPTH, D_SHORT, D_MODEL)),
        "pool_w": dense(ks[14], (DEPTH, POOL_GROUPS, POOL_GROUP_DIM, POOL_GROUP_DIM)),
        "pool_scale": gain(ks[15], (DEPTH, D_POOL), 0.1),
        "w_pc": dense(ks[16], (DEPTH, D_POOL, D_MODEL)),
        "w_o": dense(ks[17], (DEPTH, D_MODEL, D_MODEL)),
        "norm_ffn2_g": gain(ks[18], (DEPTH, D_MODEL)),
        "ffn2_w_gate": dense(ks[19], (DEPTH, D_MODEL, D_FF)),
        "ffn2_w_up": dense(ks[20], (DEPTH, D_MODEL, D_FF)),
        "ffn2_w_down": dense(ks[21], (DEPTH, D_FF, D_MODEL)),
        "final_norm_g": gain(ks[22], (D_MODEL,)),
    }


def _fwd_reference(x, norm_ffn1_g, ffn1_w_gate, ffn1_w_up, ffn1_w_down, norm_mix_g, w_in,
              conv_dw, conv_b, conv_ln_g, conv_ln_b, w_pa, short_dw, w_pb, pool_w,
              pool_scale, w_pc, w_o, norm_ffn2_g, ffn2_w_gate, ffn2_w_up, ffn2_w_down,
              final_norm_g):
    for l in range(DEPTH):
        x = x + 0.5 * swiglu(rms_norm(x, norm_ffn1_g[l]), ffn1_w_gate[l], ffn1_w_up[l], ffn1_w_down[l])
        x = x + hybrid_mixer(rms_norm(x, norm_mix_g[l]), w_in[l], conv_dw[l], conv_b[l],
                             conv_ln_g[l], conv_ln_b[l], w_pa[l], short_dw[l], w_pb[l],
                             pool_w[l], pool_scale[l], w_pc[l], w_o[l])
        x = x + 0.5 * swiglu(rms_norm(x, norm_ffn2_g[l]), ffn2_w_gate[l], ffn2_w_up[l], ffn2_w_down[l])
    return rms_norm(x, final_norm_g)


import jax as _jax
import jax.numpy as _jnp

TWIN_FORMAT = 'train_step'
FWD_PARAMS = ['x', 'norm_ffn1_g', 'ffn1_w_gate', 'ffn1_w_up', 'ffn1_w_down', 'norm_mix_g', 'w_in', 'conv_dw', 'conv_b', 'conv_ln_g', 'conv_ln_b', 'w_pa', 'short_dw', 'w_pb', 'pool_w', 'pool_scale', 'w_pc', 'w_o', 'norm_ffn2_g', 'ffn2_w_gate', 'ffn2_w_up', 'ffn2_w_down', 'final_norm_g']
TWIN_WEIGHTS = ['norm_ffn1_g', 'ffn1_w_gate', 'ffn1_w_up', 'ffn1_w_down', 'norm_mix_g', 'w_in', 'conv_dw', 'conv_b', 'conv_ln_g', 'conv_ln_b', 'w_pa', 'short_dw', 'w_pb', 'pool_w', 'pool_scale', 'w_pc', 'w_o', 'norm_ffn2_g', 'ffn2_w_gate', 'ffn2_w_up', 'ffn2_w_down', 'final_norm_g']
TWIN_DIFF_INPUT = 'x'
TWIN_INPUTS = ['x', 'norm_ffn1_g', 'ffn1_w_gate', 'ffn1_w_up', 'ffn1_w_down', 'norm_mix_g', 'w_in', 'conv_dw', 'conv_b', 'conv_ln_g', 'conv_ln_b', 'w_pa', 'short_dw', 'w_pb', 'pool_w', 'pool_scale', 'w_pc', 'w_o', 'norm_ffn2_g', 'ffn2_w_gate', 'ffn2_w_up', 'ffn2_w_down', 'final_norm_g', 'loss_target', 'm_norm_ffn1_g', 'm_ffn1_w_gate', 'm_ffn1_w_up', 'm_ffn1_w_down', 'm_norm_mix_g', 'm_w_in', 'm_conv_dw', 'm_conv_b', 'm_conv_ln_g', 'm_conv_ln_b', 'm_w_pa', 'm_short_dw', 'm_w_pb', 'm_pool_w', 'm_pool_scale', 'm_w_pc', 'm_w_o', 'm_norm_ffn2_g', 'm_ffn2_w_gate', 'm_ffn2_w_up', 'm_ffn2_w_down', 'm_final_norm_g', 'v_norm_ffn1_g', 'v_ffn1_w_gate', 'v_ffn1_w_up', 'v_ffn1_w_down', 'v_norm_mix_g', 'v_w_in', 'v_conv_dw', 'v_conv_b', 'v_conv_ln_g', 'v_conv_ln_b', 'v_w_pa', 'v_short_dw', 'v_w_pb', 'v_pool_w', 'v_pool_scale', 'v_w_pc', 'v_w_o', 'v_norm_ffn2_g', 'v_ffn2_w_gate', 'v_ffn2_w_up', 'v_ffn2_w_down', 'v_final_norm_g']
TWIN_OUTPUTS = ['loss', 'grad_x', 'grad_norm_ffn1_g', 'grad_ffn1_w_gate', 'grad_ffn1_w_up', 'grad_ffn1_w_down', 'grad_norm_mix_g', 'grad_w_in', 'grad_conv_dw', 'grad_conv_b', 'grad_conv_ln_g', 'grad_conv_ln_b', 'grad_w_pa', 'grad_short_dw', 'grad_w_pb', 'grad_pool_w', 'grad_pool_scale', 'grad_w_pc', 'grad_w_o', 'grad_norm_ffn2_g', 'grad_ffn2_w_gate', 'grad_ffn2_w_up', 'grad_ffn2_w_down', 'grad_final_norm_g', 'delta_norm_ffn1_g', 'delta_ffn1_w_gate', 'delta_ffn1_w_up', 'delta_ffn1_w_down', 'delta_norm_mix_g', 'delta_w_in', 'delta_conv_dw', 'delta_conv_b', 'delta_conv_ln_g', 'delta_conv_ln_b', 'delta_w_pa', 'delta_short_dw', 'delta_w_pb', 'delta_pool_w', 'delta_pool_scale', 'delta_w_pc', 'delta_w_o', 'delta_norm_ffn2_g', 'delta_ffn2_w_gate', 'delta_ffn2_w_up', 'delta_ffn2_w_down', 'delta_final_norm_g', 'new_m_norm_ffn1_g', 'new_m_ffn1_w_gate', 'new_m_ffn1_w_up', 'new_m_ffn1_w_down', 'new_m_norm_mix_g', 'new_m_w_in', 'new_m_conv_dw', 'new_m_conv_b', 'new_m_conv_ln_g', 'new_m_conv_ln_b', 'new_m_w_pa', 'new_m_short_dw', 'new_m_w_pb', 'new_m_pool_w', 'new_m_pool_scale', 'new_m_w_pc', 'new_m_w_o', 'new_m_norm_ffn2_g', 'new_m_ffn2_w_gate', 'new_m_ffn2_w_up', 'new_m_ffn2_w_down', 'new_m_final_norm_g', 'new_v_norm_ffn1_g', 'new_v_ffn1_w_gate', 'new_v_ffn1_w_up', 'new_v_ffn1_w_down', 'new_v_norm_mix_g', 'new_v_w_in', 'new_v_conv_dw', 'new_v_conv_b', 'new_v_conv_ln_g', 'new_v_conv_ln_b', 'new_v_w_pa', 'new_v_short_dw', 'new_v_w_pb', 'new_v_pool_w', 'new_v_pool_scale', 'new_v_w_pc', 'new_v_w_o', 'new_v_norm_ffn2_g', 'new_v_ffn2_w_gate', 'new_v_ffn2_w_up', 'new_v_ffn2_w_down', 'new_v_final_norm_g']
TWIN_LEAF_KINDS = {'loss': 'loss', 'grad_x': 'grad_x', 'grad_norm_ffn1_g': 'grad_w', 'grad_ffn1_w_gate': 'grad_w', 'grad_ffn1_w_up': 'grad_w', 'grad_ffn1_w_down': 'grad_w', 'grad_norm_mix_g': 'grad_w', 'grad_w_in': 'grad_w', 'grad_conv_dw': 'grad_w', 'grad_conv_b': 'grad_w', 'grad_conv_ln_g': 'grad_w', 'grad_conv_ln_b': 'grad_w', 'grad_w_pa': 'grad_w', 'grad_short_dw': 'grad_w', 'grad_w_pb': 'grad_w', 'grad_pool_w': 'grad_w', 'grad_pool_scale': 'grad_w', 'grad_w_pc': 'grad_w', 'grad_w_o': 'grad_w', 'grad_norm_ffn2_g': 'grad_w', 'grad_ffn2_w_gate': 'grad_w', 'grad_ffn2_w_up': 'grad_w', 'grad_ffn2_w_down': 'grad_w', 'grad_final_norm_g': 'grad_w', 'delta_norm_ffn1_g': 'delta_w', 'delta_ffn1_w_gate': 'delta_w', 'delta_ffn1_w_up': 'delta_w', 'delta_ffn1_w_down': 'delta_w', 'delta_norm_mix_g': 'delta_w', 'delta_w_in': 'delta_w', 'delta_conv_dw': 'delta_w', 'delta_conv_b': 'delta_w', 'delta_conv_ln_g': 'delta_w', 'delta_conv_ln_b': 'delta_w', 'delta_w_pa': 'delta_w', 'delta_short_dw': 'delta_w', 'delta_w_pb': 'delta_w', 'delta_pool_w': 'delta_w', 'delta_pool_scale': 'delta_w', 'delta_w_pc': 'delta_w', 'delta_w_o': 'delta_w', 'delta_norm_ffn2_g': 'delta_w', 'delta_ffn2_w_gate': 'delta_w', 'delta_ffn2_w_up': 'delta_w', 'delta_ffn2_w_down': 'delta_w', 'delta_final_norm_g': 'delta_w', 'new_m_norm_ffn1_g': 'new_m', 'new_m_ffn1_w_gate': 'new_m', 'new_m_ffn1_w_up': 'new_m', 'new_m_ffn1_w_down': 'new_m', 'new_m_norm_mix_g': 'new_m', 'new_m_w_in': 'new_m', 'new_m_conv_dw': 'new_m', 'new_m_conv_b': 'new_m', 'new_m_conv_ln_g': 'new_m', 'new_m_conv_ln_b': 'new_m', 'new_m_w_pa': 'new_m', 'new_m_short_dw': 'new_m', 'new_m_w_pb': 'new_m', 'new_m_pool_w': 'new_m', 'new_m_pool_scale': 'new_m', 'new_m_w_pc': 'new_m', 'new_m_w_o': 'new_m', 'new_m_norm_ffn2_g': 'new_m', 'new_m_ffn2_w_gate': 'new_m', 'new_m_ffn2_w_up': 'new_m', 'new_m_ffn2_w_down': 'new_m', 'new_m_final_norm_g': 'new_m', 'new_v_norm_ffn1_g': 'new_v', 'new_v_ffn1_w_gate': 'new_v', 'new_v_ffn1_w_up': 'new_v', 'new_v_ffn1_w_down': 'new_v', 'new_v_norm_mix_g': 'new_v', 'new_v_w_in': 'new_v', 'new_v_conv_dw': 'new_v', 'new_v_conv_b': 'new_v', 'new_v_conv_ln_g': 'new_v', 'new_v_conv_ln_b': 'new_v', 'new_v_w_pa': 'new_v', 'new_v_short_dw': 'new_v', 'new_v_w_pb': 'new_v', 'new_v_pool_w': 'new_v', 'new_v_pool_scale': 'new_v', 'new_v_w_pc': 'new_v', 'new_v_w_o': 'new_v', 'new_v_norm_ffn2_g': 'new_v', 'new_v_ffn2_w_gate': 'new_v', 'new_v_ffn2_w_up': 'new_v', 'new_v_ffn2_w_down': 'new_v', 'new_v_final_norm_g': 'new_v'}


def _forward(args):
    return _fwd_reference(*[args[k] for k in FWD_PARAMS])


def _output_shape():
    out = _jax.eval_shape(lambda: _forward(_fwd_setup_inputs(0)))
    return out.shape, out.dtype

N_MICROBATCH = 1
ADAM_LR = 0.001
ADAM_B1 = 0.9
ADAM_B2 = 0.999
ADAM_EPS = 1e-08
ADAM_WD = 0.01
ADAM_STEP = 10
PER_EXAMPLE_BATCH_AXIS = {'x': 0, 'loss_target': 0}
SHARED_INPUTS = []
_WEIGHT_DTYPES = {'norm_ffn1_g': _jnp.float32, 'ffn1_w_gate': _jnp.float32, 'ffn1_w_up': _jnp.float32, 'ffn1_w_down': _jnp.float32, 'norm_mix_g': _jnp.float32, 'w_in': _jnp.float32, 'conv_dw': _jnp.float32, 'conv_b': _jnp.float32, 'conv_ln_g': _jnp.float32, 'conv_ln_b': _jnp.float32, 'w_pa': _jnp.float32, 'short_dw': _jnp.float32, 'w_pb': _jnp.float32, 'pool_w': _jnp.float32, 'pool_scale': _jnp.float32, 'w_pc': _jnp.float32, 'w_o': _jnp.float32, 'norm_ffn2_g': _jnp.float32, 'ffn2_w_gate': _jnp.float32, 'ffn2_w_up': _jnp.float32, 'ffn2_w_down': _jnp.float32, 'final_norm_g': _jnp.float32}
MOMENT_SCALE = {'norm_ffn1_g': 8.870728e-02, 'ffn1_w_gate': 3.771401e-02, 'ffn1_w_up': 3.659257e-02, 'ffn1_w_down': 6.061989e-02, 'norm_mix_g': 1.694951e-01, 'w_in': 7.314897e-02, 'conv_dw': 6.798261e-02, 'conv_b': 1.531186e-01, 'conv_ln_g': 8.264866e-02, 'conv_ln_b': 7.179163e-02, 'w_pa': 4.688923e-02, 'short_dw': 1.645556e-01, 'w_pb': 7.739588e-02, 'pool_w': 1.392374e-01, 'pool_scale': 1.349260e-01, 'w_pc': 7.050809e-02, 'w_o': 1.151455e-01, 'norm_ffn2_g': 5.902232e-02, 'ffn2_w_gate': 2.535633e-02, 'ffn2_w_up': 2.469728e-02, 'ffn2_w_down': 4.076692e-02, 'final_norm_g': 3.204066e+01}


def _to_microbatches(a, axis):
    t = _jnp.moveaxis(a, axis, 0)
    t = t.reshape((N_MICROBATCH, t.shape[0] // N_MICROBATCH) + t.shape[1:])
    return _jnp.moveaxis(t, 1, axis + 1)


def setup_inputs(seed: int = 0) -> dict:
    inp = _fwd_setup_inputs(seed)
    key = _jax.random.fold_in(_jax.random.key(seed), 7919)
    shape, _ = _output_shape()
    out = dict(inp)
    out["loss_target"] = _jax.random.normal(_jax.random.fold_in(key, 0), shape, _jnp.float32)
    for i, name in enumerate(TWIN_WEIGHTS):
        w = inp[name].astype(_jnp.float32)
        if MOMENT_SCALE is None:
            s = _jnp.sqrt(_jnp.mean(_jnp.square(w)) + 1e-30)
        else:
            s = MOMENT_SCALE[name]
        km, kv = _jax.random.split(_jax.random.fold_in(key, i + 1))
        out[name] = w
        out["m_" + name] = s * _jax.random.normal(km, w.shape, _jnp.float32)
        out["v_" + name] = (s * s) * _jax.random.uniform(kv, w.shape, _jnp.float32, 0.5, 1.5)
    if N_MICROBATCH > 1:
        for name, axis in PER_EXAMPLE_BATCH_AXIS.items():
            out[name] = _to_microbatches(out[name], axis)
    return {'x': out['x'], 'norm_ffn1_g': out['norm_ffn1_g'], 'ffn1_w_gate': out['ffn1_w_gate'], 'ffn1_w_up': out['ffn1_w_up'], 'ffn1_w_down': out['ffn1_w_down'], 'norm_mix_g': out['norm_mix_g'], 'w_in': out['w_in'], 'conv_dw': out['conv_dw'], 'conv_b': out['conv_b'], 'conv_ln_g': out['conv_ln_g'], 'conv_ln_b': out['conv_ln_b'], 'w_pa': out['w_pa'], 'short_dw': out['short_dw'], 'w_pb': out['w_pb'], 'pool_w': out['pool_w'], 'pool_scale': out['pool_scale'], 'w_pc': out['w_pc'], 'w_o': out['w_o'], 'norm_ffn2_g': out['norm_ffn2_g'], 'ffn2_w_gate': out['ffn2_w_gate'], 'ffn2_w_up': out['ffn2_w_up'], 'ffn2_w_down': out['ffn2_w_down'], 'final_norm_g': out['final_norm_g'], 'loss_target': out['loss_target'], 'm_norm_ffn1_g': out['m_norm_ffn1_g'], 'm_ffn1_w_gate': out['m_ffn1_w_gate'], 'm_ffn1_w_up': out['m_ffn1_w_up'], 'm_ffn1_w_down': out['m_ffn1_w_down'], 'm_norm_mix_g': out['m_norm_mix_g'], 'm_w_in': out['m_w_in'], 'm_conv_dw': out['m_conv_dw'], 'm_conv_b': out['m_conv_b'], 'm_conv_ln_g': out['m_conv_ln_g'], 'm_conv_ln_b': out['m_conv_ln_b'], 'm_w_pa': out['m_w_pa'], 'm_short_dw': out['m_short_dw'], 'm_w_pb': out['m_w_pb'], 'm_pool_w': out['m_pool_w'], 'm_pool_scale': out['m_pool_scale'], 'm_w_pc': out['m_w_pc'], 'm_w_o': out['m_w_o'], 'm_norm_ffn2_g': out['m_norm_ffn2_g'], 'm_ffn2_w_gate': out['m_ffn2_w_gate'], 'm_ffn2_w_up': out['m_ffn2_w_up'], 'm_ffn2_w_down': out['m_ffn2_w_down'], 'm_final_norm_g': out['m_final_norm_g'], 'v_norm_ffn1_g': out['v_norm_ffn1_g'], 'v_ffn1_w_gate': out['v_ffn1_w_gate'], 'v_ffn1_w_up': out['v_ffn1_w_up'], 'v_ffn1_w_down': out['v_ffn1_w_down'], 'v_norm_mix_g': out['v_norm_mix_g'], 'v_w_in': out['v_w_in'], 'v_conv_dw': out['v_conv_dw'], 'v_conv_b': out['v_conv_b'], 'v_conv_ln_g': out['v_conv_ln_g'], 'v_conv_ln_b': out['v_conv_ln_b'], 'v_w_pa': out['v_w_pa'], 'v_short_dw': out['v_short_dw'], 'v_w_pb': out['v_w_pb'], 'v_pool_w': out['v_pool_w'], 'v_pool_scale': out['v_pool_scale'], 'v_w_pc': out['v_w_pc'], 'v_w_o': out['v_w_o'], 'v_norm_ffn2_g': out['v_norm_ffn2_g'], 'v_ffn2_w_gate': out['v_ffn2_w_gate'], 'v_ffn2_w_up': out['v_ffn2_w_up'], 'v_ffn2_w_down': out['v_ffn2_w_down'], 'v_final_norm_g': out['v_final_norm_g']}


def _loss(weights, diff, rest, loss_target):
    with _jax.named_scope("forward"):
        args = {**rest, TWIN_DIFF_INPUT: diff, **{k: w.astype(_WEIGHT_DTYPES[k]) for k, w in weights.items()}}
        y = _forward(args)
    with _jax.named_scope("loss_head"):
        err = _jnp.square(y.astype(_jnp.float32) - loss_target)
        return 0.5 * _jnp.sum(_jnp.mean(err, axis=-1)) if err.ndim else 0.5 * err


def _adamw(w, g, m, v):
    m = ADAM_B1 * m + (1.0 - ADAM_B1) * g
    v = ADAM_B2 * v + (1.0 - ADAM_B2) * _jnp.square(g)
    m_hat = m / (1.0 - ADAM_B1 ** ADAM_STEP)
    v_hat = v / (1.0 - ADAM_B2 ** ADAM_STEP)
    delta = -ADAM_LR * (m_hat / (_jnp.sqrt(v_hat) + ADAM_EPS) + ADAM_WD * w)
    return delta, m, v


def reference(x, norm_ffn1_g, ffn1_w_gate, ffn1_w_up, ffn1_w_down, norm_mix_g, w_in, conv_dw, conv_b, conv_ln_g, conv_ln_b, w_pa, short_dw, w_pb, pool_w, pool_scale, w_pc, w_o, norm_ffn2_g, ffn2_w_gate, ffn2_w_up, ffn2_w_down, final_norm_g, loss_target, m_norm_ffn1_g, m_ffn1_w_gate, m_ffn1_w_up, m_ffn1_w_down, m_norm_mix_g, m_w_in, m_conv_dw, m_conv_b, m_conv_ln_g, m_conv_ln_b, m_w_pa, m_short_dw, m_w_pb, m_pool_w, m_pool_scale, m_w_pc, m_w_o, m_norm_ffn2_g, m_ffn2_w_gate, m_ffn2_w_up, m_ffn2_w_down, m_final_norm_g, v_norm_ffn1_g, v_ffn1_w_gate, v_ffn1_w_up, v_ffn1_w_down, v_norm_mix_g, v_w_in, v_conv_dw, v_conv_b, v_conv_ln_g, v_conv_ln_b, v_w_pa, v_short_dw, v_w_pb, v_pool_w, v_pool_scale, v_w_pc, v_w_o, v_norm_ffn2_g, v_ffn2_w_gate, v_ffn2_w_up, v_ffn2_w_down, v_final_norm_g):
    given = dict(x=x, norm_ffn1_g=norm_ffn1_g, ffn1_w_gate=ffn1_w_gate, ffn1_w_up=ffn1_w_up, ffn1_w_down=ffn1_w_down, norm_mix_g=norm_mix_g, w_in=w_in, conv_dw=conv_dw, conv_b=conv_b, conv_ln_g=conv_ln_g, conv_ln_b=conv_ln_b, w_pa=w_pa, short_dw=short_dw, w_pb=w_pb, pool_w=pool_w, pool_scale=pool_scale, w_pc=w_pc, w_o=w_o, norm_ffn2_g=norm_ffn2_g, ffn2_w_gate=ffn2_w_gate, ffn2_w_up=ffn2_w_up, ffn2_w_down=ffn2_w_down, final_norm_g=final_norm_g, loss_target=loss_target, m_norm_ffn1_g=m_norm_ffn1_g, m_ffn1_w_gate=m_ffn1_w_gate, m_ffn1_w_up=m_ffn1_w_up, m_ffn1_w_down=m_ffn1_w_down, m_norm_mix_g=m_norm_mix_g, m_w_in=m_w_in, m_conv_dw=m_conv_dw, m_conv_b=m_conv_b, m_conv_ln_g=m_conv_ln_g, m_conv_ln_b=m_conv_ln_b, m_w_pa=m_w_pa, m_short_dw=m_short_dw, m_w_pb=m_w_pb, m_pool_w=m_pool_w, m_pool_scale=m_pool_scale, m_w_pc=m_w_pc, m_w_o=m_w_o, m_norm_ffn2_g=m_norm_ffn2_g, m_ffn2_w_gate=m_ffn2_w_gate, m_ffn2_w_up=m_ffn2_w_up, m_ffn2_w_down=m_ffn2_w_down, m_final_norm_g=m_final_norm_g, v_norm_ffn1_g=v_norm_ffn1_g, v_ffn1_w_gate=v_ffn1_w_gate, v_ffn1_w_up=v_ffn1_w_up, v_ffn1_w_down=v_ffn1_w_down, v_norm_mix_g=v_norm_mix_g, v_w_in=v_w_in, v_conv_dw=v_conv_dw, v_conv_b=v_conv_b, v_conv_ln_g=v_conv_ln_g, v_conv_ln_b=v_conv_ln_b, v_w_pa=v_w_pa, v_short_dw=v_short_dw, v_w_pb=v_w_pb, v_pool_w=v_pool_w, v_pool_scale=v_pool_scale, v_w_pc=v_w_pc, v_w_o=v_w_o, v_norm_ffn2_g=v_norm_ffn2_g, v_ffn2_w_gate=v_ffn2_w_gate, v_ffn2_w_up=v_ffn2_w_up, v_ffn2_w_down=v_ffn2_w_down, v_final_norm_g=v_final_norm_g)
    weights = {n: given[n] for n in TWIN_WEIGHTS}
    shared = {n: given[n] for n in SHARED_INPUTS}
    per_example = {n: given[n] for n in ['x']}
    grad_fn = _jax.value_and_grad(_loss, argnums=(0, 1))

    def one_microbatch(ex, loss_target):
        ex = dict(ex)
        diff = ex.pop(TWIN_DIFF_INPUT)
        return grad_fn(weights, diff, {**shared, **ex}, loss_target)

    if N_MICROBATCH == 1:
        loss, (grad_w, grad_x) = one_microbatch(per_example, given["loss_target"])
    else:
        def body(carry, xs):
            loss_sum, grad_sum = carry
            l_k, (gw_k, gx_k) = one_microbatch(xs[0], xs[1])
            with _jax.named_scope("update"):
                return (loss_sum + l_k, _jax.tree.map(_jnp.add, grad_sum, gw_k)), gx_k

        init = (_jnp.zeros((), _jnp.float32), _jax.tree.map(_jnp.zeros_like, weights))
        (loss, grad_w), grad_x = _jax.lax.scan(body, init, (per_example, given["loss_target"]))
    with _jax.named_scope("update"):
        delta_w, new_m, new_v = {}, {}, {}
        for n in TWIN_WEIGHTS:
            delta_w[n], new_m[n], new_v[n] = _adamw(weights[n], grad_w[n], given["m_" + n], given["v_" + n])
    return (loss, grad_x, *[grad_w[n] for n in TWIN_WEIGHTS], *[delta_w[n] for n in TWIN_WEIGHTS],
            *[new_m[n] for n in TWIN_WEIGHTS], *[new_v[n] for n in TWIN_WEIGHTS])
```

```python
import functools

import jax
import jax.numpy as jnp
from jax import lax
from jax.experimental import pallas as pl
from jax.experimental.pallas import tpu as pltpu

F32 = jnp.float32
BF16 = jnp.bfloat16
EPS = 1e-6
NS = 4
CONV_W = 31
SHORT_W = 3
POOL_WINDOWS = (2, 4, 8, 16)
HALO = 32
ADAM_LR, ADAM_B1, ADAM_B2, ADAM_EPS, ADAM_WD, ADAM_STEP = 0.001, 0.9, 0.999, 1e-08, 0.01, 10
MESH = pl.DeviceIdType.MESH
ANY = pl.BlockSpec(memory_space=pl.ANY)
VMEM_LIMIT = 56 * 1024 * 1024


def _pallas(body, **kw):
    return pl.pallas_call(body, **kw)


def _cp(n_axes):
    return pltpu.CompilerParams(dimension_semantics=("arbitrary",) * n_axes, vmem_limit_bytes=VMEM_LIMIT)


def _nn(a, b):
    return jnp.dot(a, b, preferred_element_type=F32)


def _nt(a, b):
    return lax.dot_general(a, b, (((1,), (1,)), ((), ())), preferred_element_type=F32)


def _tn(a, b):
    return lax.dot_general(a, b, (((0,), (0,)), ((), ())), preferred_element_type=F32)


def _sigmoid(v):
    return 1.0 / (1.0 + jnp.exp(-v))


def _rms_stats(x):
    rs = lax.rsqrt(jnp.mean(x * x, axis=-1, keepdims=True) + EPS)
    return x * rs, rs


def _rms_bwd(dh, x, g):
    xh, rs = _rms_stats(x)
    dhg = dh * g
    dx = rs * (dhg - xh * jnp.mean(dhg * xh, axis=-1, keepdims=True))
    return dx, jnp.sum(dh * xh, axis=0, keepdims=True)


def _tile(n, pref):
    t = min(n, pref)
    assert n % t == 0, (n, t)
    return t


def _ffn_fwd(x, g, colw, roww, l, f):
    T, D = x.shape
    F4 = colw.shape[-1]
    tm = _tile(T, 512)

    def body(x_ref, g_ref, wg_ref, wu_ref, wd_ref, xo_ref, h_ref, a_ref, b_ref, acc):
        j = pl.program_id(1)

        @pl.when(j == 0)
        def _():
            xh, _ = _rms_stats(x_ref[...])
            h_ref[...] = (xh * g_ref[...]).astype(BF16)
            acc[...] = jnp.zeros_like(acc)

        h = h_ref[...]
        a = _nn(h, wg_ref[...])
        b = _nn(h, wu_ref[...])
        a_ref[...] = a.astype(BF16)
        b_ref[...] = b.astype(BF16)
        z = (a * _sigmoid(a) * b).astype(BF16)
        acc[...] += _nn(z, wd_ref[...])

        @pl.when(j == NS - 1)
        def _():
            xo_ref[...] = x_ref[...] + 0.5 * acc[...]

    return _pallas(
        body, name=f"ffn_fwd_{l}_{f}", grid=(T // tm, NS),
        in_specs=[pl.BlockSpec((tm, D), lambda i, j: (i, 0)),
                  pl.BlockSpec((1, D), lambda i, j: (0, 0)),
                  pl.BlockSpec((None, None, None, D, F4), lambda i, j: (j, l, 2 * f, 0, 0)),
                  pl.BlockSpec((None, None, None, D, F4), lambda i, j: (j, l, 2 * f + 1, 0, 0)),
                  pl.BlockSpec((None, None, None, F4, D), lambda i, j: (j, l, f, 0, 0))],
        out_specs=[pl.BlockSpec((tm, D), lambda i, j: (i, 0)),
                   pl.BlockSpec((tm, D), lambda i, j: (i, 0)),
                   pl.BlockSpec((None, tm, F4), lambda i, j: (j, i, 0)),
                   pl.BlockSpec((None, tm, F4), lambda i, j: (j, i, 0))],
        out_shape=[jax.ShapeDtypeStruct((T, D), F32), jax.ShapeDtypeStruct((T, D), BF16),
                   jax.ShapeDtypeStruct((NS, T, F4), BF16), jax.ShapeDtypeStruct((NS, T, F4), BF16)],
        scratch_shapes=[pltpu.VMEM((tm, D), F32)],
        compiler_params=_cp(2),
    )(x, g, colw, colw, roww)


def _ffn_bwd(x, g, dy, a, b, colw, roww, l, f):
    T, D = x.shape
    F4 = colw.shape[-1]
    tm = _tile(T, 512)

    def body(x_ref, g_ref, dy_ref, a_ref, b_ref, wg_ref, wu_ref, wd_ref,
             dx_ref, dg_ref, da_ref, db_ref, z_ref, dyb_ref, dh):
        i = pl.program_id(0)
        j = pl.program_id(1)

        @pl.when(j == 0)
        def _():
            dyb_ref[...] = (0.5 * dy_ref[...]).astype(BF16)
            dh[...] = jnp.zeros_like(dh)

        @pl.when((i == 0) & (j == 0))
        def _():
            dg_ref[...] = jnp.zeros_like(dg_ref)

        dz = _nt(dyb_ref[...], wd_ref[...])
        av = a_ref[...].astype(F32)
        bv = b_ref[...].astype(F32)
        sg = _sigmoid(av)
        silu = av * sg
        da = (dz * bv * (sg * (1.0 + av * (1.0 - sg)))).astype(BF16)
        db = (dz * silu).astype(BF16)
        da_ref[...] = da
        db_ref[...] = db
        z_ref[...] = (silu * bv).astype(BF16)
        dh[...] += _nt(da, wg_ref[...]) + _nt(db, wu_ref[...])

        @pl.when(j == NS - 1)
        def _():
            dxn, dg = _rms_bwd(dh[...], x_ref[...], g_ref[...])
            dx_ref[...] = dy_ref[...] + dxn
            dg_ref[...] += dg

    tok = pl.BlockSpec((tm, D), lambda i, j: (i, 0))
    vec = pl.BlockSpec((1, D), lambda i, j: (0, 0))
    chunk = pl.BlockSpec((None, tm, F4), lambda i, j: (j, i, 0))
    return _pallas(
        body, name=f"ffn_bwd_{l}_{f}", grid=(T // tm, NS),
        in_specs=[tok, vec, tok, chunk, chunk,
                  pl.BlockSpec((None, None, None, D, F4), lambda i, j: (j, l, 2 * f, 0, 0)),
                  pl.BlockSpec((None, None, None, D, F4), lambda i, j: (j, l, 2 * f + 1, 0, 0)),
                  pl.BlockSpec((None, None, None, F4, D), lambda i, j: (j, l, f, 0, 0))],
        out_specs=[tok, vec, chunk, chunk, chunk, tok],
        out_shape=[jax.ShapeDtypeStruct((T, D), F32), jax.ShapeDtypeStruct((1, D), F32),
                   jax.ShapeDtypeStruct((NS, T, F4), BF16), jax.ShapeDtypeStruct((NS, T, F4), BF16),
                   jax.ShapeDtypeStruct((NS, T, F4), BF16), jax.ShapeDtypeStruct((T, D), BF16)],
        scratch_shapes=[pltpu.VMEM((tm, D), F32)],
        compiler_params=_cp(2),
    )(x, g, dy, a, b, colw, colw, roww)


def _ffn_dw(h, dyb, da, db, z, l, f, prev):
    T, D = h.shape
    F4 = da.shape[-1]
    tk = _tile(T, 512)
    nt = T // tk

    def body(*refs):
        h_ref, dyb_ref, da_ref, db_ref, z_ref = refs[:5]
        gc_ref, gr_ref, accg, accu, accd = refs[-5:]
        t = pl.program_id(1)

        @pl.when(t == 0)
        def _():
            accg[...] = jnp.zeros_like(accg)
            accu[...] = jnp.zeros_like(accu)
            accd[...] = jnp.zeros_like(accd)

        hv = h_ref[...]
        accg[...] += _tn(hv, da_ref[...])
        accu[...] += _tn(hv, db_ref[...])
        accd[...] += _tn(z_ref[...], dyb_ref[...])

        @pl.when(t == nt - 1)
        def _():
            gc_ref[0] = accg[...].astype(BF16)
            gc_ref[1] = accu[...].astype(BF16)
            gr_ref[...] = accd[...].astype(BF16)

    tok = pl.BlockSpec((tk, D), lambda s, t: (t, 0))
    chunk = pl.BlockSpec((None, tk, F4), lambda s, t: (s, t, 0))
    in_specs = [tok, tok, chunk, chunk, chunk]
    args = [h, dyb, da, db, z]
    aliases = {}
    if prev is not None:
        in_specs += [ANY, ANY]
        args += list(prev)
        aliases = {5: 0, 6: 1}
    return _pallas(
        body, name=f"ffn_dw_{l}_{f}", grid=(NS, nt),
        in_specs=in_specs,
        out_specs=[pl.BlockSpec((None, None, 2, D, F4), lambda s, t: (l, s, 0, 0, 0)),
                   pl.BlockSpec((None, None, F4, D), lambda s, t: (l, s, 0, 0))],
        out_shape=[jax.ShapeDtypeStruct((2, NS, 2, D, F4), BF16), jax.ShapeDtypeStruct((2, NS, F4, D), BF16)],
        scratch_shapes=[pltpu.VMEM((D, F4), F32), pltpu.VMEM((D, F4), F32), pltpu.VMEM((F4, D), F32)],
        input_output_aliases=aliases,
        compiler_params=_cp(2),
    )(*args)


def _mix_in(x, g, win, l):
    T, D = x.shape
    C4 = win.shape[-1]
    tm = _tile(T, 512)

    def body(x_ref, g_ref, w_ref, h_ref, u_ref):
        @pl.when(pl.program_id(1) == 0)
        def _():
            xh, _ = _rms_stats(x_ref[...])
            h_ref[...] = (xh * g_ref[...]).astype(BF16)

        u_ref[...] = _nn(h_ref[...], w_ref[...]).astype(BF16)

    return _pallas(
        body, name=f"mix_in_{l}", grid=(T // tm, NS),
        in_specs=[pl.BlockSpec((tm, D), lambda i, j: (i, 0)), pl.BlockSpec((1, D), lambda i, j: (0, 0)),
                  pl.BlockSpec((None, None, D, C4), lambda i, j: (j, l, 0, 0))],
        out_specs=[pl.BlockSpec((tm, D), lambda i, j: (i, 0)), pl.BlockSpec((tm, C4), lambda i, j: (i, j))],
        out_shape=[jax.ShapeDtypeStruct((T, D), BF16), jax.ShapeDtypeStruct((T, NS * C4), BF16)],
        compiler_params=_cp(2),
    )(x, g, win)


def _pool_lane_window(n):
    lane = lax.broadcasted_iota(jnp.int32, (1, n), 1) // (n // len(POOL_WINDOWS))
    w = jnp.full((1, n), float(POOL_WINDOWS[-1]), F32)
    for gi in range(len(POOL_WINDOWS) - 1):
        w = jnp.where(lane == gi, float(POOL_WINDOWS[gi]), w)
    return lane, w


def _pool_select(lane, sums):
    out = sums[-1]
    for gi in range(len(sums) - 1):
        out = jnp.where(lane == gi, sums[gi], out)
    return out


def _back(v, s):
    return v if s == 0 else pltpu.roll(v, s, 0)


def _fwd_shift(v, s):
    return v if s == 0 else pltpu.roll(v, v.shape[0] - s, 0)


def _mix_seq_fwd(u, cdw, cb, lg, lb, sdw, pwblk, ps, bl, l):
    T = u.shape[0]
    S = T // bl
    ts = _tile(S, 256)
    nt = S // ts
    DC, DS = cdw.shape[-1], sdw.shape[-1]
    o_ag, o_bg, o_cg, o_bx, o_p, o_end = DC, 2 * DC, 2 * DC + DS, 2 * DC + 2 * DS, 2 * DC + 3 * DS, 2 * DC + 4 * DS

    def body(up_ref, uc_ref, cdw_ref, cb_ref, lg_ref, lb_ref, sdw_ref, pw_ref, ps_ref, act_ref, cv_ref):
        i = pl.program_id(1)
        keep = jnp.where(i > 0, 1.0, 0.0).astype(F32)

        def ext(lo, hi):
            p = up_ref[ts - HALO:ts, lo:hi].astype(F32) * keep
            return jnp.concatenate([p, uc_ref[:, lo:hi].astype(F32)], axis=0)

        glu = ext(0, o_ag) * _sigmoid(ext(o_ag, o_bg))
        cv = jnp.zeros((ts, DC), F32) + cb_ref[...]
        for s in range(CONV_W):
            cv = cv + _back(glu, s)[HALO:, :] * cdw_ref[CONV_W - 1 - s:CONV_W - s, :]
        cv_ref[...] = cv
        mu = jnp.mean(cv, axis=-1, keepdims=True)
        xc = cv - mu
        lnv = xc * lax.rsqrt(jnp.mean(xc * xc, axis=-1, keepdims=True) + EPS) * lg_ref[...] + lb_ref[...]
        act_ref[:, 0:DC] = (lnv * _sigmoid(lnv)).astype(BF16)

        q = ext(o_cg, o_bx) * ext(o_bx, o_p)
        sc = jnp.zeros((ts, DS), F32)
        for s in range(SHORT_W):
            sc = sc + _back(q, s)[HALO:, :] * sdw_ref[SHORT_W - 1 - s:SHORT_W - s, :]
        act_ref[:, DC:DC + DS] = (uc_ref[:, o_bg:o_cg].astype(F32) * sc).astype(BF16)

        p = ext(o_p, o_end)
        lane, wl = _pool_lane_window(DS)
        sums, cur, sh = [], p, 1
        for _ in POOL_WINDOWS:
            cur = cur + _back(cur, sh)
            sums.append(cur[HALO:, :])
            sh *= 2
        pos = (i * ts + lax.broadcasted_iota(jnp.int32, (ts, 1), 0) + 1).astype(F32)
        pooled = _pool_select(lane, sums) / jnp.minimum(pos, wl) - p[HALO:, :]
        act_ref[:, DC + DS:DC + 2 * DS] = (_nn(pooled.astype(BF16), pw_ref[...]) * ps_ref[...]).astype(BF16)

    ucol = 2 * DC + 4 * DS
    full = lambda a: pl.BlockSpec(a.shape, lambda b, i: (0,) * a.ndim)
    return _pallas(
        body, name=f"mix_seq_fwd_{l}", grid=(bl, nt),
        in_specs=[pl.BlockSpec((ts, ucol), lambda b, i: (b * nt + jnp.maximum(i - 1, 0), 0)),
                  pl.BlockSpec((ts, ucol), lambda b, i: (b * nt + i, 0)),
                  full(cdw), full(cb), full(lg), full(lb), full(sdw), full(pwblk), full(ps)],
        out_specs=[pl.BlockSpec((ts, DC + 2 * DS), lambda b, i: (b * nt + i, 0)),
                   pl.BlockSpec((ts, DC), lambda b, i: (b * nt + i, 0))],
        out_shape=[jax.ShapeDtypeStruct((T, DC + 2 * DS), BF16), jax.ShapeDtypeStruct((T, DC), F32)],
        compiler_params=_cp(2),
    )(u, u, cdw, cb, lg, lb, sdw, pwblk, ps)


def _mix_out(x, act, u, wp, wo, l):
    T, D = x.shape
    tm = _tile(T, 512)
    DA = act.shape[-1]
    DC, DS = DA // 2, DA // 4
    NB = D // NS
    gcol = (2 * DC + 4 * DS) // D

    def body(x_ref, act_ref, g0_ref, g1_ref, g2_ref, wp_ref, wo_ref, xo_ref, y_ref, m_ref):
        parts = [(0, DC), (DC, DC + DS), (DC + DS, DC + 2 * DS)]
        m = jnp.zeros((tm, D), F32)
        for k, (lo, hi) in enumerate(parts):
            av = act_ref[:, lo:hi]
            y = jnp.concatenate([_nn(av, wp_ref[s, lo:hi, :]) for s in range(NS)], axis=1)
            y_ref[:, k * D:(k + 1) * D] = y.astype(BF16)
            gl = (g0_ref, g1_ref, g2_ref)[k][...].astype(F32)
            m = m + _sigmoid(gl) * y
        mb = m.astype(BF16)
        m_ref[...] = mb
        out = _nn(mb[:, 0:NB], wo_ref[0])
        for s in range(1, NS):
            out = out + _nn(mb[:, s * NB:(s + 1) * NB], wo_ref[s])
        xo_ref[...] = x_ref[...] + out

    tok = lambda w: pl.BlockSpec((tm, w), lambda i: (i, 0))
    return _pallas(
        body, name=f"mix_out_{l}", grid=(T // tm,),
        in_specs=[tok(D), tok(DA),
                  pl.BlockSpec((tm, D), lambda i: (i, gcol)), pl.BlockSpec((tm, D), lambda i: (i, gcol + 1)),
                  pl.BlockSpec((tm, D), lambda i: (i, gcol + 2)),
                  pl.BlockSpec((NS, None, DA, NB), lambda i: (0, l, 0, 0)),
                  pl.BlockSpec((NS, None, NB, D), lambda i: (0, l, 0, 0))],
        out_specs=[tok(D), tok(3 * D), tok(D)],
        out_shape=[jax.ShapeDtypeStruct((T, D), F32), jax.ShapeDtypeStruct((T, 3 * D), BF16),
                   jax.ShapeDtypeStruct((T, D), BF16)],
        compiler_params=_cp(1),
    )(x, act, u, u, u, wp, wo)


def _mix_out_bwd(dxn, y, u, act, m, wp, wo, l, prev):
    T, D = dxn.shape
    tm = _tile(T, 256)
    nt = T // tm
    DA = act.shape[-1]
    DC, DS = DA // 2, DA // 4
    NB = D // NS
    UC = u.shape[-1]
    g_lo = 2 * DC + 4 * DS
    gcol = g_lo // D
    parts = [(0, DC), (DC, DC + DS), (DC + DS, DC + 2 * DS)]

    def body(*refs):
        dx_ref, y_ref, g0_ref, g1_ref, g2_ref, act_ref, m_ref, wp_ref, wo_ref = refs[:9]
        du_ref, dact_ref, gwo_ref, gwp_ref, acc_wo, acc_wp = refs[-6:]
        i = pl.program_id(0)

        @pl.when(i == 0)
        def _():
            acc_wo[...] = jnp.zeros_like(acc_wo)
            acc_wp[...] = jnp.zeros_like(acc_wp)

        dxb = dx_ref[...].astype(BF16)
        dm = jnp.concatenate([_nt(dxb, wo_ref[s]) for s in range(NS)], axis=1)
        acc_wo[...] += _tn(m_ref[...], dxb)
        du_ref[:, 0:g_lo] = jnp.zeros((tm, g_lo), BF16)
        for k, (lo, hi) in enumerate(parts):
            sg = _sigmoid((g0_ref, g1_ref, g2_ref)[k][...].astype(F32))
            yk = y_ref[:, k * D:(k + 1) * D].astype(F32)
            du_ref[:, g_lo + k * D:g_lo + (k + 1) * D] = (dm * yk * sg * (1.0 - sg)).astype(BF16)
            dyk = (dm * sg).astype(BF16)
            dk = _nt(dyk[:, 0:NB], wp_ref[0, lo:hi, :])
            for s in range(1, NS):
                dk = dk + _nt(dyk[:, s * NB:(s + 1) * NB], wp_ref[s, lo:hi, :])
            dact_ref[:, lo:hi] = dk
            acc_wp[lo:hi, :] += _tn(act_ref[:, lo:hi], dyk)

        @pl.when(i == nt - 1)
        def _():
            for s in range(NS):
                gwo_ref[s] = acc_wo[s * NB:(s + 1) * NB, :].astype(BF16)
                gwp_ref[s] = acc_wp[:, s * NB:(s + 1) * NB].astype(BF16)

    tok = lambda w: pl.BlockSpec((tm, w), lambda i: (i, 0))
    in_specs = [tok(D), tok(3 * D),
                pl.BlockSpec((tm, D), lambda i: (i, gcol)), pl.BlockSpec((tm, D), lambda i: (i, gcol + 1)),
                pl.BlockSpec((tm, D), lambda i: (i, gcol + 2)),
                tok(DA), tok(D),
                pl.BlockSpec((NS, None, DA, NB), lambda i: (0, l, 0, 0)),
                pl.BlockSpec((NS, None, NB, D), lambda i: (0, l, 0, 0))]
    args = [dxn, y, u, u, u, act, m, wp, wo]
    aliases = {}
    if prev is not None:
        in_specs += [ANY, ANY]
        args += list(prev)
        aliases = {9: 2, 10: 3}
    return _pallas(
        body, name=f"mix_out_bwd_{l}", grid=(nt,),
        in_specs=in_specs,
        out_specs=[tok(UC), tok(DA),
                   pl.BlockSpec((None, NS, NB, D), lambda i: (l, 0, 0, 0)),
                   pl.BlockSpec((None, NS, DA, NB), lambda i: (l, 0, 0, 0))],
        out_shape=[jax.ShapeDtypeStruct((T, UC), BF16), jax.ShapeDtypeStruct((T, DA), F32),
                   jax.ShapeDtypeStruct((2, NS, NB, D), BF16), jax.ShapeDtypeStruct((2, NS, DA, NB), BF16)],
        scratch_shapes=[pltpu.VMEM((D, D), F32), pltpu.VMEM((DA, D), F32)],
        input_output_aliases=aliases,
        compiler_params=_cp(1),
    )(*args)


def _mix_seq_bwd(du, u, dact, cv, cdw, cb, lg, lb, sdw, pwblk, ps, bl, l):
    T = u.shape[0]
    S = T // bl
    ts = _tile(S, 256)
    nt = S // ts
    DC, DS = cdw.shape[-1], sdw.shape[-1]
    DA = DC + 2 * DS
    o_ag, o_bg, o_cg, o_bx, o_p, o_end = DC, 2 * DC, 2 * DC + DS, 2 * DC + 2 * DS, 2 * DC + 3 * DS, 2 * DC + 4 * DS
    n_f = ts + HALO

    def body(du_in, up_ref, uc_ref, un_ref, dac_ref, dan_ref, cvc_ref, cvn_ref,
             cdw_ref, lg_ref, lb_ref, sdw_ref, pw_ref, ps_ref,
             du_ref, gcdw_ref, g512_ref, g256_ref, gpw_ref):
        del du_in
        b = pl.program_id(0)
        i = pl.program_id(1)
        keep_p = jnp.where(i > 0, 1.0, 0.0).astype(F32)
        keep_n = jnp.where(i < nt - 1, 1.0, 0.0).astype(F32)

        @pl.when((b == 0) & (i == 0))
        def _():
            gcdw_ref[...] = jnp.zeros_like(gcdw_ref)
            g512_ref[...] = jnp.zeros_like(g512_ref)
            g256_ref[...] = jnp.zeros_like(g256_ref)
            gpw_ref[...] = jnp.zeros_like(gpw_ref)

        def back(lo, hi):
            p = up_ref[ts - HALO:ts, lo:hi].astype(F32) * keep_p
            return jnp.concatenate([p, uc_ref[:, lo:hi].astype(F32)], axis=0)

        def fwd(cur, nxt, lo, hi, mask):
            n = nxt[0:HALO, lo:hi].astype(F32)
            if mask:
                n = n * keep_n
            return jnp.concatenate([cur[:, lo:hi].astype(F32), n], axis=0)

        cvx = fwd(cvc_ref, cvn_ref, 0, DC, False)
        dA = fwd(dac_ref, dan_ref, 0, DC, True)
        mu = jnp.mean(cvx, axis=-1, keepdims=True)
        xc = cvx - mu
        rs = lax.rsqrt(jnp.mean(xc * xc, axis=-1, keepdims=True) + EPS)
        xh = xc * rs
        lnv = xh * lg_ref[...] + lb_ref[...]
        sg = _sigmoid(lnv)
        dln = dA * (sg * (1.0 + lnv * (1.0 - sg)))
        dxh = dln * lg_ref[...]
        dcv = rs * (dxh - jnp.mean(dxh, axis=-1, keepdims=True) - xh * jnp.mean(dxh * xh, axis=-1, keepdims=True))
        g512_ref[0:1, :] += jnp.sum(dcv[0:ts], axis=0, keepdims=True)
        g512_ref[1:2, :] += jnp.sum((dln * xh)[0:ts], axis=0, keepdims=True)
        g512_ref[2:3, :] += jnp.sum(dln[0:ts], axis=0, keepdims=True)

        av = back(0, o_ag)
        sga = _sigmoid(back(o_ag, o_bg))
        glu = av * sga
        dcv_c = dcv[0:ts]
        dglu = jnp.zeros((ts, DC), F32)
        for s in range(CONV_W):
            k = CONV_W - 1 - s
            dglu = dglu + _fwd_shift(dcv, s)[0:ts, :] * cdw_ref[k:k + 1, :]
            gcdw_ref[k:k + 1, :] += jnp.sum(_back(glu, s)[HALO:, :] * dcv_c, axis=0, keepdims=True)
        sga_c = sga[HALO:, :]
        du_ref[:, 0:o_ag] = (dglu * sga_c).astype(BF16)
        du_ref[:, o_ag:o_bg] = (dglu * av[HALO:, :] * sga_c * (1.0 - sga_c)).astype(BF16)

        cg = back(o_cg, o_bx)
        bx = back(o_bx, o_p)
        q = cg * bx
        sc = jnp.zeros((ts, DS), F32)
        for s in range(SHORT_W):
            sc = sc + _back(q, s)[HALO:, :] * sdw_ref[SHORT_W - 1 - s:SHORT_W - s, :]
        dB = fwd(dac_ref, dan_ref, DC, DC + DS, True)
        ds = dB * fwd(uc_ref, un_ref, o_bg, o_cg, False)
        du_ref[:, o_bg:o_cg] = (dB[0:ts] * sc).astype(BF16)
        ds_c = ds[0:ts]
        dq = jnp.zeros((ts, DS), F32)
        for s in range(SHORT_W):
            k = SHORT_W - 1 - s
            dq = dq + _fwd_shift(ds, s)[0:ts, :] * sdw_ref[k:k + 1, :]
            g256_ref[k:k + 1, :] += jnp.sum(_back(q, s)[HALO:, :] * ds_c, axis=0, keepdims=True)
        du_ref[:, o_cg:o_bx] = (dq * bx[HALO:, :]).astype(BF16)
        du_ref[:, o_bx:o_p] = (dq * cg[HALO:, :]).astype(BF16)

        p = back(o_p, o_end)
        lane, wl = _pool_lane_window(DS)
        sums, cur, sh = [], p, 1
        for _ in POOL_WINDOWS:
            cur = cur + _back(cur, sh)
            sums.append(cur[HALO:, :])
            sh *= 2
        pos_c = (i * ts + lax.broadcasted_iota(jnp.int32, (ts, 1), 0) + 1).astype(F32)
        pooled = (_pool_select(lane, sums) / jnp.minimum(pos_c, wl) - p[HALO:, :]).astype(BF16)
        pwv = _nn(pooled, pw_ref[...])
        dC = fwd(dac_ref, dan_ref, DC + DS, DA, True)
        g256_ref[SHORT_W:SHORT_W + 1, :] += jnp.sum(dC[0:ts] * pwv, axis=0, keepdims=True)
        dpw = (dC * ps_ref[...]).astype(BF16)
        gpw_ref[...] += _tn(pooled, dpw[0:ts])
        dpl = _nt(dpw, pw_ref[...])
        pos_f = (i * ts + lax.broadcasted_iota(jnp.int32, (n_f, 1), 0) + 1).astype(F32)
        e = dpl / jnp.minimum(pos_f, wl)
        fsums, cur, sh = [], e, 1
        for _ in POOL_WINDOWS:
            cur = cur + _fwd_shift(cur, sh)
            fsums.append(cur[0:ts, :])
            sh *= 2
        du_ref[:, o_p:o_end] = (_pool_select(lane, fsums) - dpl[0:ts]).astype(BF16)

    full = lambda a: pl.BlockSpec(a.shape, lambda b, i: (0,) * a.ndim)
    row = lambda w, f: pl.BlockSpec((ts, w), lambda b, i: (b * nt + f(i), 0))
    prv = lambda i: jnp.maximum(i - 1, 0)
    nxt = lambda i: jnp.minimum(i + 1, nt - 1)
    cur = lambda i: i
    return _pallas(
        body, name=f"mix_seq_bwd_{l}", grid=(bl, nt),
        in_specs=[ANY, row(o_end, prv), row(o_end, cur), row(o_end, nxt),
                  row(DA, cur), row(DA, nxt), row(DC, cur), row(DC, nxt),
                  full(cdw), full(lg), full(lb), full(sdw), full(pwblk), full(ps)],
        out_specs=[row(o_end, cur), full(cdw),
                   pl.BlockSpec((8, DC), lambda b, i: (0, 0)), pl.BlockSpec((8, DS), lambda b, i: (0, 0)),
                   full(pwblk)],
        out_shape=[jax.ShapeDtypeStruct(du.shape, BF16), jax.ShapeDtypeStruct(cdw.shape, F32),
                   jax.ShapeDtypeStruct((8, DC), F32), jax.ShapeDtypeStruct((8, DS), F32),
                   jax.ShapeDtypeStruct(pwblk.shape, F32)],
        input_output_aliases={0: 0},
        compiler_params=_cp(2),
    )(du, u, u, u, dact, dact, cv, cv, cdw, lg, lb, sdw, pwblk, ps)


def _mix_in_bwd(x, g, dxn, du, win, l):
    T, D = x.shape
    C4 = win.shape[-1]
    tm = _tile(T, 512)

    def body(x_ref, g_ref, dxn_ref, du_ref, w_ref, dx_ref, dg_ref, dh):
        i = pl.program_id(0)
        j = pl.program_id(1)

        @pl.when(j == 0)
        def _():
            dh[...] = jnp.zeros_like(dh)

        @pl.when((i == 0) & (j == 0))
        def _():
            dg_ref[...] = jnp.zeros_like(dg_ref)

        dh[...] += _nt(du_ref[...], w_ref[...])

        @pl.when(j == NS - 1)
        def _():
            dxr, dg = _rms_bwd(dh[...], x_ref[...], g_ref[...])
            dx_ref[...] = dxn_ref[...] + dxr
            dg_ref[...] += dg

    tok = pl.BlockSpec((tm, D), lambda i, j: (i, 0))
    vec = pl.BlockSpec((1, D), lambda i, j: (0, 0))
    return _pallas(
        body, name=f"mix_in_bwd_{l}", grid=(T // tm, NS),
        in_specs=[tok, vec, tok, pl.BlockSpec((tm, C4), lambda i, j: (i, j)),
                  pl.BlockSpec((None, None, D, C4), lambda i, j: (j, l, 0, 0))],
        out_specs=[tok, vec],
        out_shape=[jax.ShapeDtypeStruct((T, D), F32), jax.ShapeDtypeStruct((1, D), F32)],
        scratch_shapes=[pltpu.VMEM((tm, D), F32)],
        compiler_params=_cp(2),
    )(x, g, dxn, du, win)


def _mix_in_dw(h, du, l, prev):
    T, D = h.shape
    C4 = du.shape[-1] // NS
    tk = _tile(T, 512)
    nt = T // tk

    def body(*refs):
        h_ref, du_ref = refs[:2]
        g_ref, acc = refs[-2:]
        t = pl.program_id(1)

        @pl.when(t == 0)
        def _():
            acc[...] = jnp.zeros_like(acc)

        acc[...] += _tn(h_ref[...], du_ref[...])

        @pl.when(t == nt - 1)
        def _():
            g_ref[...] = acc[...].astype(BF16)

    in_specs = [pl.BlockSpec((tk, D), lambda s, t: (t, 0)), pl.BlockSpec((tk, C4), lambda s, t: (t, s))]
    args = [h, du]
    aliases = {}
    if prev is not None:
        in_specs.append(ANY)
        args.append(prev)
        aliases = {2: 0}
    return _pallas(
        body, name=f"mix_in_dw_{l}", grid=(NS, nt),
        in_specs=in_specs,
        out_specs=pl.BlockSpec((None, None, D, C4), lambda s, t: (l, s, 0, 0)),
        out_shape=jax.ShapeDtypeStruct((2, NS, D, C4), BF16),
        scratch_shapes=[pltpu.VMEM((D, C4), F32)],
        input_output_aliases=aliases,
        compiler_params=_cp(2),
    )(*args)


def _loss_head(x, g, target):
    T, D = x.shape
    tm = _tile(T, 512)

    def body(x_ref, g_ref, t_ref, dx_ref, loss_ref, dg_ref):
        @pl.when(pl.program_id(0) == 0)
        def _():
            loss_ref[...] = jnp.zeros_like(loss_ref)
            dg_ref[...] = jnp.zeros_like(dg_ref)

        xv = x_ref[...]
        xh, rs = _rms_stats(xv)
        gv = g_ref[...]
        e = xh * gv - t_ref[...]
        loss_ref[...] += 0.5 * jnp.sum(jnp.mean(e * e, axis=-1, keepdims=True))
        dy = e * (1.0 / D)
        dyg = dy * gv
        dx_ref[...] = rs * (dyg - xh * jnp.mean(dyg * xh, axis=-1, keepdims=True))
        dg_ref[...] += jnp.sum(dy * xh, axis=0, keepdims=True)

    tok = pl.BlockSpec((tm, D), lambda i: (i, 0))
    vec = pl.BlockSpec((1, D), lambda i: (0, 0))
    return _pallas(
        body, name="loss_head", grid=(T // tm,),
        in_specs=[tok, vec, tok],
        out_specs=[tok, pl.BlockSpec((8, 128), lambda i: (0, 0)), vec],
        out_shape=[jax.ShapeDtypeStruct((T, D), F32), jax.ShapeDtypeStruct((8, 128), F32),
                   jax.ShapeDtypeStruct((1, D), F32)],
        compiler_params=_cp(1),
    )(x, g, target)


def _block_diag(pw):
    G, c, _ = pw.shape
    out = jnp.zeros((G * c, G * c), pw.dtype)
    for gi in range(G):
        out = lax.dynamic_update_slice(out, pw[gi], (gi * c, gi * c))
    return out


def _pad_rows(a, n):
    return jnp.pad(a, ((0, n - a.shape[0]), (0, 0)))


def _fwd_bwd(x3, target3, gw, small):
    colw, roww, win, wp, wo = gw
    bl, S, D = x3.shape
    T = bl * S
    x = x3.reshape(T, D)
    target = target3.reshape(T, D)
    L = small["norm_mix_g"].shape[0]
    row = lambda v: v[None, :]

    saved = []
    for l in range(L):
        cdw = _pad_rows(small["conv_dw"][l], HALO)
        sdw = _pad_rows(small["short_dw"][l], 8)
        pwblk = _block_diag(small["pool_w"][l]).astype(BF16)
        sp = dict(cdw=cdw, sdw=sdw, pwblk=pwblk, cb=row(small["conv_b"][l]), lg=row(small["conv_ln_g"][l]),
                  lb=row(small["conv_ln_b"][l]), ps=row(small["pool_scale"][l]),
                  g1=row(small["norm_ffn1_g"][l]), gm=row(small["norm_mix_g"][l]), g2=row(small["norm_ffn2_g"][l]))
        x0 = x
        x1, h1, a1, b1 = _ffn_fwd(x0, sp["g1"], colw, roww, l, 0)
        hm, u = _mix_in(x1, sp["gm"], win, l)
        act, cv = _mix_seq_fwd(u, cdw, sp["cb"], sp["lg"], sp["lb"], sdw, pwblk, sp["ps"], bl, l)
        x2, y, m = _mix_out(x1, act, u, wp, wo, l)
        x, h2, a2, b2 = _ffn_fwd(x2, sp["g2"], colw, roww, l, 1)
        saved.append(dict(sp=sp, x0=x0, x1=x1, x2=x2, h1=h1, a1=a1, b1=b1, hm=hm, u=u, act=act, cv=cv, y=y, m=m,
                          h2=h2, a2=a2, b2=b2))

    dx, loss_blk, dgf = _loss_head(x, row(small["final_norm_g"]), target)
    loss = loss_blk[0, 0]

    sg = {k: [None] * L for k in ("norm_ffn1_g", "norm_mix_g", "norm_ffn2_g", "conv_dw", "conv_b", "conv_ln_g",
                                  "conv_ln_b", "short_dw", "pool_w", "pool_scale")}
    g_c1 = g_c2 = g_in = g_o = None
    G, c = small["pool_w"].shape[1:3]
    for l in reversed(range(L)):
        sv = saved[l]
        sp = sv["sp"]
        dx, dg2, da, db, z, dyb = _ffn_bwd(sv["x2"], sp["g2"], dx, sv["a2"], sv["b2"], colw, roww, l, 1)
        g_c2 = _ffn_dw(sv["h2"], dyb, da, db, z, l, 1, g_c2)
        du, dact, gwo, gwp = _mix_out_bwd(dx, sv["y"], sv["u"], sv["act"], sv["m"], wp, wo, l, g_o)
        g_o = (gwo, gwp)
        du, gcdw, g512, g256, gpw = _mix_seq_bwd(du, sv["u"], dact, sv["cv"], sp["cdw"], sp["cb"], sp["lg"], sp["lb"],
                                                 sp["sdw"], sp["pwblk"], sp["ps"], bl, l)
        dx, dgm = _mix_in_bwd(sv["x1"], sp["gm"], dx, du, win, l)
        g_in = _mix_in_dw(sv["hm"], du, l, g_in)
        dx, dg1, da, db, z, dyb = _ffn_bwd(sv["x0"], sp["g1"], dx, sv["a1"], sv["b1"], colw, roww, l, 0)
        g_c1 = _ffn_dw(sv["h1"], dyb, da, db, z, l, 0, g_c1)
        sg["norm_ffn1_g"][l], sg["norm_mix_g"][l], sg["norm_ffn2_g"][l] = dg1[0], dgm[0], dg2[0]
        sg["conv_dw"][l] = gcdw[:CONV_W]
        sg["conv_b"][l], sg["conv_ln_g"][l], sg["conv_ln_b"][l] = g512[0], g512[1], g512[2]
        sg["short_dw"][l] = g256[:SHORT_W]
        sg["pool_scale"][l] = g256[SHORT_W]
        sg["pool_w"][l] = jnp.stack([gpw[gi * c:(gi + 1) * c, gi * c:(gi + 1) * c] for gi in range(G)])
    small_g = {k: jnp.stack(v) for k, v in sg.items()}
    small_g["final_norm_g"] = dgf[0]
    big_g = (g_c1[0], g_c1[1], g_c2[0], g_c2[1], g_in, g_o[1], g_o[0])
    return loss, dx.reshape(bl, S, D), big_g, small_g


def _place():
    x, y, c = lax.axis_index("x"), lax.axis_index("y"), lax.axis_index("c")
    chips = [(1 - x, y), (x, 1 - y), (1 - x, 1 - y)]
    return x, y, c, chips


def _all_gather_weights(arrs):
    n = len(arrs)

    def body(*refs):
        ins, outs = refs[:n], refs[n:2 * n]
        send_sems, recv_sems, local_sems = refs[2 * n:]
        x, y, c, chips = _place()
        me = 2 * x + y
        sib = (x, y, 1 - c)

        def remote(ai, k, src, dst, to):
            return pltpu.make_async_remote_copy(src_ref=src, dst_ref=dst, send_sem=send_sems.at[ai, k],
                                                recv_sem=recv_sems.at[ai, k], device_id=to, device_id_type=MESH)

        started = []
        for ai in range(n):
            loc = pltpu.make_async_copy(ins[ai], outs[ai].at[me], local_sems.at[ai])
            loc.start()
            started.append(loc)
        sends = []
        for ai in range(n):
            for k, chip in enumerate(chips):
                cp = remote(ai, k, ins[ai].at[c], outs[ai].at[me, c], (*chip, c))
                cp.start()
                sends.append(cp)
        for ai in range(n):
            for k, chip in enumerate(chips):
                blk = outs[ai].at[2 * chip[0] + chip[1], c]
                remote(ai, k, blk, blk, (*chip, c)).wait_recv()
                cp = remote(ai, 3 + k, blk, blk, sib)
                cp.start()
                sends.append(cp)
        for ai in range(n):
            for k, chip in enumerate(chips):
                blk = outs[ai].at[2 * chip[0] + chip[1], 1 - c]
                remote(ai, 3 + k, blk, blk, sib).wait_recv()
        for cp in sends:
            cp.wait_send()
        for loc in started:
            loc.wait()

    return _pallas(
        body, name="all_gather_weights",
        in_specs=[ANY] * n, out_specs=[ANY] * n,
        out_shape=[jax.ShapeDtypeStruct((NS,) + a.shape, a.dtype) for a in arrs],
        scratch_shapes=[pltpu.SemaphoreType.DMA((n, 6)), pltpu.SemaphoreType.DMA((n, 6)),
                        pltpu.SemaphoreType.DMA((n,))],
        compiler_params=pltpu.CompilerParams(has_side_effects=True),
    )(*arrs)


def _exchange_sibling_layers(gs):
    n = len(gs)

    def body(*refs):
        ins, outs = refs[:n], refs[n:2 * n]
        send_sems, recv_sems = refs[2 * n:]
        x, y, c, _ = _place()
        cps = []
        for ai in range(n):
            cp = pltpu.make_async_remote_copy(src_ref=ins[ai].at[1 - c], dst_ref=outs[ai], send_sem=send_sems.at[ai],
                                              recv_sem=recv_sems.at[ai], device_id=(x, y, 1 - c), device_id_type=MESH)
            cp.start()
            cps.append(cp)
        for cp in cps:
            cp.wait()

    return _pallas(
        body, name="grad_exchange_sibling",
        in_specs=[ANY] * n, out_specs=[ANY] * n,
        out_shape=[jax.ShapeDtypeStruct(g.shape[1:], g.dtype) for g in gs],
        scratch_shapes=[pltpu.SemaphoreType.DMA((n,)), pltpu.SemaphoreType.DMA((n,))],
        compiler_params=pltpu.CompilerParams(has_side_effects=True),
    )(*gs)


def _scatter_to_chips(ps):
    n = len(ps)

    def body(*refs):
        ins, outs = refs[:n], refs[n:2 * n]
        send_sems, recv_sems = refs[2 * n:]
        x, y, c, chips = _place()
        cps = []
        for ai in range(n):
            for k, chip in enumerate(chips):
                cp = pltpu.make_async_remote_copy(
                    src_ref=ins[ai].at[2 * chip[0] + chip[1]], dst_ref=outs[ai].at[k],
                    send_sem=send_sems.at[ai, k], recv_sem=recv_sems.at[ai, k],
                    device_id=(*chip, c), device_id_type=MESH)
                cp.start()
                cps.append(cp)
        for cp in cps:
            cp.wait()

    return _pallas(
        body, name="grad_scatter_chips",
        in_specs=[ANY] * n, out_specs=[ANY] * n,
        out_shape=[jax.ShapeDtypeStruct((3,) + p.shape[1:], p.dtype) for p in ps],
        scratch_shapes=[pltpu.SemaphoreType.DMA((n, 3)), pltpu.SemaphoreType.DMA((n, 3))],
        compiler_params=pltpu.CompilerParams(has_side_effects=True),
    )(*ps)


def _share_final(fs):
    n = len(fs)

    def body(*refs):
        ins, outs = refs[:n], refs[n:2 * n]
        send_sems, recv_sems = refs[2 * n:]
        del ins
        x, y, c, _ = _place()
        cps = []
        for ai in range(n):
            cp = pltpu.make_async_remote_copy(src_ref=outs[ai].at[c], dst_ref=outs[ai].at[c], send_sem=send_sems.at[ai],
                                              recv_sem=recv_sems.at[ai], device_id=(x, y, 1 - c), device_id_type=MESH)
            cp.start()
            cps.append(cp)
        for ai, cp in enumerate(cps):
            cp.wait_send()
            blk = outs[ai].at[1 - c]
            pltpu.make_async_remote_copy(src_ref=blk, dst_ref=blk, send_sem=send_sems.at[ai], recv_sem=recv_sems.at[ai],
                                         device_id=(x, y, 1 - c), device_id_type=MESH).wait_recv()

    return _pallas(
        body, name="grad_share_final",
        in_specs=[ANY] * n, out_specs=[ANY] * n,
        out_shape=[jax.ShapeDtypeStruct(f.shape, f.dtype) for f in fs],
        scratch_shapes=[pltpu.SemaphoreType.DMA((n,)), pltpu.SemaphoreType.DMA((n,))],
        input_output_aliases={i: i for i in range(n)},
        compiler_params=pltpu.CompilerParams(has_side_effects=True),
    )(*fs)


def _all_reduce_small(v):
    R, W = v.shape

    def body(v_ref, out_ref, buf, send_sems, recv_sems):
        x, y, c, _ = _place()
        me = 4 * x + 2 * y + c
        buf[me] = v_ref[...]
        cps = []
        for k in range(1, 8):
            kx, ky, kc = (k >> 2) & 1, (k >> 1) & 1, k & 1
            to = (1 - x if kx else x, 1 - y if ky else y, 1 - c if kc else c)
            cp = pltpu.make_async_remote_copy(src_ref=v_ref, dst_ref=buf.at[me], send_sem=send_sems.at[k - 1],
                                              recv_sem=recv_sems.at[k - 1], device_id=to, device_id_type=MESH)
            cp.start()
            cps.append(cp)
        for cp in cps:
            cp.wait()
        acc = buf[0]
        for d in range(1, 8):
            acc = acc + buf[d]
        out_ref[...] = acc

    return _pallas(
        body, name="all_reduce_small",
        in_specs=[pl.BlockSpec(memory_space=pltpu.VMEM)], out_specs=pl.BlockSpec(memory_space=pltpu.VMEM),
        out_shape=jax.ShapeDtypeStruct((R, W), F32),
        scratch_shapes=[pltpu.VMEM((8, R, W), F32), pltpu.SemaphoreType.DMA((7,)), pltpu.SemaphoreType.DMA((7,))],
        compiler_params=pltpu.CompilerParams(has_side_effects=True, vmem_limit_bytes=VMEM_LIMIT),
    )(v)


def _row_tile(n, w):
    for t in (512, 352, 256, 128, 64, 32, 16):
        if n % t == 0 and t * w * 4 <= 4 * 1024 * 1024:
            return t
    raise ValueError((n, w))


def _sum_sibling(tag, cidx, g, r):
    _, N, W = g.shape
    tr = _row_tile(N, W)

    def body(c_ref, g_ref, r_ref, o_ref):
        del c_ref
        o_ref[...] = (g_ref[...].astype(F32) + r_ref[...].astype(F32)).astype(BF16)

    return _pallas(
        body, name=f"grad_sum_sibling_{tag}",
        grid_spec=pltpu.PrefetchScalarGridSpec(
            num_scalar_prefetch=1, grid=(N // tr,),
            in_specs=[pl.BlockSpec((None, tr, W), lambda i, c: (c[0], i, 0)),
                      pl.BlockSpec((tr, W), lambda i, c: (i, 0))],
            out_specs=pl.BlockSpec((tr, W), lambda i, c: (i, 0))),
        out_shape=jax.ShapeDtypeStruct((N, W), BF16),
        compiler_params=_cp(1),
    )(cidx, g, r)


def _sum_final(tag, idx, g, r1, r2):
    _, _, n, W = g.shape
    tr = _row_tile(n, W)

    def body(i_ref, g_ref, r1_ref, r2_ref, o_ref):
        del i_ref
        acc = g_ref[...].astype(F32) + r1_ref[...].astype(F32)
        for k in range(3):
            acc = acc + r2_ref[k].astype(F32)
        o_ref[...] = acc

    return _pallas(
        body, name=f"grad_sum_final_{tag}",
        grid_spec=pltpu.PrefetchScalarGridSpec(
            num_scalar_prefetch=1, grid=(n // tr,),
            in_specs=[pl.BlockSpec((None, None, tr, W), lambda i, s: (s[0], s[1], i, 0)),
                      pl.BlockSpec((None, tr, W), lambda i, s: (s[1], i, 0)),
                      pl.BlockSpec((3, tr, W), lambda i, s: (0, i, 0))],
            out_specs=pl.BlockSpec((None, tr, W), lambda i, s: (s[0], i, 0))),
        out_shape=jax.ShapeDtypeStruct((2, n, W), F32),
        compiler_params=_cp(1),
    )(idx, g, r1, r2)


def _adam_math(w, g, m, v):
    m = ADAM_B1 * m + (1.0 - ADAM_B1) * g
    v = ADAM_B2 * v + (1.0 - ADAM_B2) * (g * g)
    m_hat = m / (1.0 - ADAM_B1 ** ADAM_STEP)
    v_hat = v / (1.0 - ADAM_B2 ** ADAM_STEP)
    delta = -ADAM_LR * (m_hat / (jnp.sqrt(v_hat) + ADAM_EPS) + ADAM_WD * w)
    return delta, m, v


def _adam_big(name, w, m, v, gfull, row0):
    _, r, W = w.shape
    tr = _row_tile(r, W)
    assert row0 % tr == 0
    off = row0 // tr

    def body(w_ref, m_ref, v_ref, g_ref, go_ref, d_ref, mo_ref, vo_ref):
        g = g_ref[...]
        d, mn, vn = _adam_math(w_ref[...], g, m_ref[...], v_ref[...])
        go_ref[...] = g
        d_ref[...] = d
        mo_ref[...] = mn
        vo_ref[...] = vn

    blk = pl.BlockSpec((None, tr, W), lambda l, i: (l, i, 0))
    shp = jax.ShapeDtypeStruct(w.shape, F32)
    return _pallas(
        body, name=f"adam_{name}", grid=(2, r // tr),
        in_specs=[blk, blk, blk, pl.BlockSpec((None, tr, W), lambda l, i: (l, off + i, 0))],
        out_specs=[blk] * 4, out_shape=[shp] * 4,
        compiler_params=_cp(2),
    )(w, m, v, gfull)


def _adam_small(w, g, m, v):
    def body(w_ref, g_ref, m_ref, v_ref, d_ref, mo_ref, vo_ref):
        d, mn, vn = _adam_math(w_ref[...], g_ref[...], m_ref[...], v_ref[...])
        d_ref[...] = d
        mo_ref[...] = mn
        vo_ref[...] = vn

    spec = pl.BlockSpec(memory_space=pltpu.VMEM)
    shp = jax.ShapeDtypeStruct(w.shape, F32)
    return _pallas(body, name="adam_small", in_specs=[spec] * 4, out_specs=[spec] * 3, out_shape=[shp] * 3)(w, g, m, v)


_WEIGHTS = ['norm_ffn1_g', 'ffn1_w_gate', 'ffn1_w_up', 'ffn1_w_down', 'norm_mix_g', 'w_in', 'conv_dw', 'conv_b',
            'conv_ln_g', 'conv_ln_b', 'w_pa', 'short_dw', 'w_pb', 'pool_w', 'pool_scale', 'w_pc', 'w_o',
            'norm_ffn2_g', 'ffn2_w_gate', 'ffn2_w_up', 'ffn2_w_down', 'final_norm_g']
_BIG = ('ffn1_w_gate', 'ffn1_w_up', 'ffn1_w_down', 'w_in', 'w_pa', 'w_pb', 'w_pc', 'w_o',
        'ffn2_w_gate', 'ffn2_w_up', 'ffn2_w_down')
_SMALL = tuple(n for n in _WEIGHTS if n not in _BIG)
_SMALL_REDUCED = ('norm_ffn1_g', 'norm_mix_g', 'conv_b', 'conv_ln_g', 'conv_ln_b', 'pool_w', 'pool_scale',
                  'norm_ffn2_g', 'final_norm_g', 'conv_dw', 'short_dw')


def _pack(arrs, rows_multiple=8):
    flat = jnp.concatenate([a.reshape(-1) for a in arrs])
    n = flat.shape[0]
    per = 128 * rows_multiple
    padded = -(-n // per) * per
    return jnp.pad(flat, (0, padded - n)).reshape(padded // 128, 128)


def _unpack(buf, shapes):
    flat = buf.reshape(-1)
    out, o = [], 0
    for s in shapes:
        k = 1
        for d in s:
            k *= d
        out.append(flat[o:o + k].reshape(s))
        o += k
    return out


def _step(P, M, V, x, loss_target):
    bf = lambda a: a.astype(BF16)
    colw = jnp.stack([bf(P['ffn1_w_gate']), bf(P['ffn1_w_up']), bf(P['ffn2_w_gate']), bf(P['ffn2_w_up'])], axis=1)
    roww = jnp.stack([bf(P['ffn1_w_down']), bf(P['ffn2_w_down'])], axis=1)
    win = bf(P['w_in'])
    wp = jnp.concatenate([bf(P['w_pa']), bf(P['w_pb']), bf(P['w_pc'])], axis=1)
    wo = bf(P['w_o'])
    L = win.shape[0]
    cw, sw = P['conv_dw'].shape[-1], P['short_dw'].shape[-1]
    dws = jnp.zeros((L, 40, 128), F32)
    dws = dws.at[:, 0:CONV_W, 0:cw].set(P['conv_dw']).at[:, 32:32 + SHORT_W, 0:sw].set(P['short_dw'])

    colw_g, roww_g, win_g, wp_g, wo_g, dws_g = _all_gather_weights([colw, roww, win, wp, wo, dws])

    small = {n: P[n] for n in _SMALL}
    small['conv_dw'] = jnp.transpose(dws_g[:, :, 0:CONV_W, 0:cw], (1, 2, 0, 3)).reshape(L, CONV_W, NS * cw)
    small['short_dw'] = jnp.transpose(dws_g[:, :, 32:32 + SHORT_W, 0:sw], (1, 2, 0, 3)).reshape(L, SHORT_W, NS * sw)

    loss, dx, big_g, small_g = _fwd_bwd(x, loss_target, (colw_g, roww_g, win_g, wp_g, wo_g), small)

    xi, yi, ci = lax.axis_index("x"), lax.axis_index("y"), lax.axis_index("c")
    chip = 2 * xi + yi
    cidx = jnp.stack([ci]).astype(jnp.int32)
    idx = jnp.stack([ci, chip]).astype(jnp.int32)
    r1 = _exchange_sibling_layers(list(big_g))
    parts = []
    for t, (g, r) in enumerate(zip(big_g, r1)):
        W = g.shape[-1]
        p = _sum_sibling(t, cidx, g.reshape(2, -1, W), r.reshape(-1, W))
        parts.append(p.reshape(g.shape[1:]))
    r2 = _scatter_to_chips(parts)
    finals = []
    for t, (g, r, rr) in enumerate(zip(big_g, r1, r2)):
        W = g.shape[-1]
        finals.append(_sum_final(t, idx, g.reshape(2, NS, -1, W), r.reshape(NS, -1, W), rr.reshape(3, -1, W)))
    f_c1, f_r1, f_c2, f_r2, f_in, f_p, f_o = _share_final(finals)

    tot = _all_reduce_small(_pack([small_g[n] for n in _SMALL_REDUCED]))
    tot = dict(zip(_SMALL_REDUCED, _unpack(tot, [small_g[n].shape for n in _SMALL_REDUCED])))
    tot['conv_dw'] = lax.dynamic_slice_in_dim(tot['conv_dw'], chip * cw, cw, axis=2)
    tot['short_dw'] = lax.dynamic_slice_in_dim(tot['short_dw'], chip * sw, sw, axis=2)

    D = win.shape[1]
    dc = P['w_pa'].shape[1]
    ds = P['w_pb'].shape[1]
    src = {'ffn1_w_gate': (f_c1, 0), 'ffn1_w_up': (f_c1, D), 'ffn1_w_down': (f_r1, 0),
           'ffn2_w_gate': (f_c2, 0), 'ffn2_w_up': (f_c2, D), 'ffn2_w_down': (f_r2, 0),
           'w_in': (f_in, 0), 'w_pa': (f_p, 0), 'w_pb': (f_p, dc), 'w_pc': (f_p, dc + ds), 'w_o': (f_o, 0)}
    grads, deltas, new_m, new_v = {}, {}, {}, {}
    for n in _BIG:
        gfull, row0 = src[n]
        grads[n], deltas[n], new_m[n], new_v[n] = _adam_big(n, P[n], M[n], V[n], gfull, row0)
    shapes = [P[n].shape for n in _SMALL]
    d_s, m_s, v_s = _adam_small(_pack([P[n] for n in _SMALL]), _pack([tot[n] for n in _SMALL]),
                                _pack([M[n] for n in _SMALL]), _pack([V[n] for n in _SMALL]))
    for n, d, mm, vv in zip(_SMALL, _unpack(d_s, shapes), _unpack(m_s, shapes), _unpack(v_s, shapes)):
        grads[n], deltas[n], new_m[n], new_v[n] = tot[n], d, mm, vv

    loss = lax.psum(loss, ("x", "y", "c"))
    return (loss, dx, *[grads[n] for n in _WEIGHTS], *[deltas[n] for n in _WEIGHTS],
            *[new_m[n] for n in _WEIGHTS], *[new_v[n] for n in _WEIGHTS])


def kernel(x, norm_ffn1_g, ffn1_w_gate, ffn1_w_up, ffn1_w_down, norm_mix_g, w_in, conv_dw, conv_b, conv_ln_g, conv_ln_b, w_pa, short_dw, w_pb, pool_w, pool_scale, w_pc, w_o, norm_ffn2_g, ffn2_w_gate, ffn2_w_up, ffn2_w_down, final_norm_g, loss_target, m_norm_ffn1_g, m_ffn1_w_gate, m_ffn1_w_up, m_ffn1_w_down, m_norm_mix_g, m_w_in, m_conv_dw, m_conv_b, m_conv_ln_g, m_conv_ln_b, m_w_pa, m_short_dw, m_w_pb, m_pool_w, m_pool_scale, m_w_pc, m_w_o, m_norm_ffn2_g, m_ffn2_w_gate, m_ffn2_w_up, m_ffn2_w_down, m_final_norm_g, v_norm_ffn1_g, v_ffn1_w_gate, v_ffn1_w_up, v_ffn1_w_down, v_norm_mix_g, v_w_in, v_conv_dw, v_conv_b, v_conv_ln_g, v_conv_ln_b, v_w_pa, v_short_dw, v_w_pb, v_pool_w, v_pool_scale, v_w_pc, v_w_o, v_norm_ffn2_g, v_ffn2_w_gate, v_ffn2_w_up, v_ffn2_w_down, v_final_norm_g):
    args = locals()
    P = {n: args[n] for n in _WEIGHTS}
    M = {n: args["m_" + n] for n in _WEIGHTS}
    V = {n: args["v_" + n] for n in _WEIGHTS}
    return _step(P, M, V, x, loss_target)
```

```python
import jax
import jax.numpy as jnp
from jax import lax
from jax.experimental import pallas as pl
from jax.experimental.pallas import tpu as pltpu

F32 = jnp.float32
BF16 = jnp.bfloat16
EPS = 1e-6
NS = 4
CONV_W = 31
SHORT_W = 3
POOL_WINDOWS = (2, 4, 8, 16)
HALO = 32
ADAM_LR, ADAM_B1, ADAM_B2, ADAM_EPS, ADAM_WD, ADAM_STEP = 0.001, 0.9, 0.999, 1e-08, 0.01, 10
MESH = pl.DeviceIdType.MESH
ANY = pl.BlockSpec(memory_space=pl.ANY)
VMEM_LIMIT = 56 * 1024 * 1024


def _pallas(body, **kw):
    return pl.pallas_call(body, **kw)


def _cp(n_axes):
    return pltpu.CompilerParams(dimension_semantics=("arbitrary",) * n_axes, vmem_limit_bytes=VMEM_LIMIT)


def _nn(a, b):
    return jnp.dot(a, b, preferred_element_type=F32)


def _nt(a, b):
    return lax.dot_general(a, b, (((1,), (1,)), ((), ())), preferred_element_type=F32)


def _tn(a, b):
    return lax.dot_general(a, b, (((0,), (0,)), ((), ())), preferred_element_type=F32)


def _sigmoid(v):
    return 1.0 / (1.0 + jnp.exp(-v))


def _rms_stats(x):
    rs = lax.rsqrt(jnp.mean(x * x, axis=-1, keepdims=True) + EPS)
    return x * rs, rs


def _rms_bwd(dh, x, g):
    xh, rs = _rms_stats(x)
    dhg = dh * g
    dx = rs * (dhg - xh * jnp.mean(dhg * xh, axis=-1, keepdims=True))
    return dx, jnp.sum(dh * xh, axis=0, keepdims=True)


def _tile(n, pref):
    t = min(n, pref)
    assert n % t == 0, (n, t)
    return t


def _place():
    x, y, c = lax.axis_index("x"), lax.axis_index("y"), lax.axis_index("c")
    chips = [(1 - x, y), (x, 1 - y), (1 - x, 1 - y)]
    return x, y, c, chips


def _gather_shapes(arrs):
    n = len(arrs)
    if n == 0:
        return [], []
    out_shape = [jax.ShapeDtypeStruct((NS,) + a.shape, a.dtype) for a in arrs]
    scratch = [pltpu.SemaphoreType.DMA((n, 6)), pltpu.SemaphoreType.DMA((n, 6)), pltpu.SemaphoreType.DMA((n,))]
    return out_shape, scratch


def _gather_copies(ins, outs, sems):
    send_sems, recv_sems, local_sems = sems
    x, y, c, chips = _place()
    me = 2 * x + y
    sib = (x, y, 1 - c)

    def remote(ai, k, src, dst, to):
        return pltpu.make_async_remote_copy(src_ref=src, dst_ref=dst, send_sem=send_sems.at[ai, k],
                                            recv_sem=recv_sems.at[ai, k], device_id=to, device_id_type=MESH)

    local, first, landed, passed, from_sib = [], [], [], [], []
    for ai in range(len(ins)):
        local.append(pltpu.make_async_copy(ins[ai], outs[ai].at[me], local_sems.at[ai]))
        for k, chip in enumerate(chips):
            theirs = 2 * chip[0] + chip[1]
            first.append(remote(ai, k, ins[ai].at[c], outs[ai].at[me, c], (*chip, c)))
            blk = outs[ai].at[theirs, c]
            landed.append(remote(ai, k, blk, blk, (*chip, c)))
            passed.append(remote(ai, 3 + k, blk, blk, sib))
            blk2 = outs[ai].at[theirs, 1 - c]
            from_sib.append(remote(ai, 3 + k, blk2, blk2, sib))
    return local, first, landed, passed, from_sib


def _gather_start(ins, outs, sems):
    local, first, _, _, _ = _gather_copies(ins, outs, sems)
    for cp in local + first:
        cp.start()


def _gather_finish(ins, outs, sems):
    local, first, landed, passed, from_sib = _gather_copies(ins, outs, sems)
    for arrive, fwd in zip(landed, passed):
        arrive.wait_recv()
        fwd.start()
    for cp in from_sib:
        cp.wait_recv()
    for cp in first + passed:
        cp.wait_send()
    for cp in local:
        cp.wait()


def _all_gather(arrs, tag):
    n = len(arrs)
    out_shape, scratch = _gather_shapes(arrs)

    def body(*refs):
        ins, outs, sems = refs[:n], refs[n:2 * n], refs[2 * n:]
        _gather_start(ins, outs, sems)
        _gather_finish(ins, outs, sems)

    return _pallas(
        body, name=f"all_gather_{tag}",
        in_specs=[ANY] * n, out_specs=[ANY] * n, out_shape=out_shape, scratch_shapes=scratch,
        compiler_params=pltpu.CompilerParams(has_side_effects=True),
    )(*arrs)


def _ffn_fwd(x, g, wf, l, f, gather=()):
    T, D = x.shape
    F4 = wf.shape[1] // 3
    tm = _tile(T, 512)
    ni = T // tm
    n = len(gather)
    g_shape, g_scratch = _gather_shapes(gather)

    def body(*refs):
        x_ref, g_ref, wg_ref, wu_ref, wd_ref = refs[:5]
        gin = refs[5:5 + n]
        xo_ref, h_ref, a_ref, b_ref = refs[5 + n:9 + n]
        gout = refs[9 + n:9 + 2 * n]
        acc = refs[9 + 2 * n]
        gsems = refs[10 + 2 * n:]
        i = pl.program_id(0)
        j = pl.program_id(1)

        if n:
            @pl.when((i == 0) & (j == 0))
            def _():
                _gather_start(gin, gout, gsems)

        @pl.when(j == 0)
        def _():
            xh, _ = _rms_stats(x_ref[...])
            h_ref[...] = (xh * g_ref[...]).astype(BF16)
            acc[...] = jnp.zeros_like(acc)

        h = h_ref[...]
        a = _nt(h, wg_ref[...])
        b = _nt(h, wu_ref[...])
        a_ref[...] = a.astype(BF16)
        b_ref[...] = b.astype(BF16)
        z = (a * _sigmoid(a) * b).astype(BF16)
        acc[...] += _nn(z, wd_ref[...])

        @pl.when(j == NS - 1)
        def _():
            xo_ref[...] = x_ref[...] + 0.5 * acc[...]

        if n:
            @pl.when((i == ni - 1) & (j == NS - 1))
            def _():
                _gather_finish(gin, gout, gsems)

    wspec = lambda k: pl.BlockSpec((None, F4, D), lambda i, j: (j, k, 0))
    return _pallas(
        body, name=f"ffn_fwd_{l}_{f}", grid=(ni, NS),
        in_specs=[pl.BlockSpec((tm, D), lambda i, j: (i, 0)), pl.BlockSpec((1, D), lambda i, j: (0, 0)),
                  wspec(0), wspec(1), wspec(2)] + [ANY] * n,
        out_specs=[pl.BlockSpec((tm, D), lambda i, j: (i, 0)),
                   pl.BlockSpec((tm, D), lambda i, j: (i, 0)),
                   pl.BlockSpec((None, tm, F4), lambda i, j: (j, i, 0)),
                   pl.BlockSpec((None, tm, F4), lambda i, j: (j, i, 0))] + [ANY] * n,
        out_shape=[jax.ShapeDtypeStruct((T, D), F32), jax.ShapeDtypeStruct((T, D), BF16),
                   jax.ShapeDtypeStruct((NS, T, F4), BF16), jax.ShapeDtypeStruct((NS, T, F4), BF16)] + g_shape,
        scratch_shapes=[pltpu.VMEM((tm, D), F32)] + g_scratch,
        compiler_params=_cp(2),
    )(x, g, wf, wf, wf, *gather)


def _ffn_bwd(x, g, dy, a, b, wf, l, f):
    T, D = x.shape
    F4 = wf.shape[1] // 3
    tm = _tile(T, 512)

    def body(x_ref, g_ref, dy_ref, a_ref, b_ref, wg_ref, wu_ref, wd_ref,
             dx_ref, dg_ref, da_ref, db_ref, z_ref, dyb_ref, dh):
        i = pl.program_id(0)
        j = pl.program_id(1)

        @pl.when(j == 0)
        def _():
            dyb_ref[...] = (0.5 * dy_ref[...]).astype(BF16)
            dh[...] = jnp.zeros_like(dh)

        @pl.when((i == 0) & (j == 0))
        def _():
            dg_ref[...] = jnp.zeros_like(dg_ref)

        dz = _nt(dyb_ref[...], wd_ref[...])
        av = a_ref[...].astype(F32)
        bv = b_ref[...].astype(F32)
        sg = _sigmoid(av)
        silu = av * sg
        da = (dz * bv * (sg * (1.0 + av * (1.0 - sg)))).astype(BF16)
        db = (dz * silu).astype(BF16)
        da_ref[...] = da
        db_ref[...] = db
        z_ref[...] = (silu * bv).astype(BF16)
        dh[...] += _nn(da, wg_ref[...]) + _nn(db, wu_ref[...])

        @pl.when(j == NS - 1)
        def _():
            dxn, dg = _rms_bwd(dh[...], x_ref[...], g_ref[...])
            dx_ref[...] = dy_ref[...] + dxn
            dg_ref[...] += dg

    tok = pl.BlockSpec((tm, D), lambda i, j: (i, 0))
    vec = pl.BlockSpec((1, D), lambda i, j: (0, 0))
    chunk = pl.BlockSpec((None, tm, F4), lambda i, j: (j, i, 0))
    wspec = lambda k: pl.BlockSpec((None, F4, D), lambda i, j: (j, k, 0))
    return _pallas(
        body, name=f"ffn_bwd_{l}_{f}", grid=(T // tm, NS),
        in_specs=[tok, vec, tok, chunk, chunk, wspec(0), wspec(1), wspec(2)],
        out_specs=[tok, vec, chunk, chunk, chunk, tok],
        out_shape=[jax.ShapeDtypeStruct((T, D), F32), jax.ShapeDtypeStruct((1, D), F32),
                   jax.ShapeDtypeStruct((NS, T, F4), BF16), jax.ShapeDtypeStruct((NS, T, F4), BF16),
                   jax.ShapeDtypeStruct((NS, T, F4), BF16), jax.ShapeDtypeStruct((T, D), BF16)],
        scratch_shapes=[pltpu.VMEM((tm, D), F32)],
        compiler_params=_cp(2),
    )(x, g, dy, a, b, wf, wf, wf)


def _ffn_dw(h, dyb, da, db, z, l, f):
    T, D = h.shape
    F4 = da.shape[-1]
    tk = _tile(T, 512)
    nt = T // tk
    R2 = 3 * F4 // 2

    def body(h_ref, dyb_ref, da_ref, db_ref, z_ref, g_ref, accg, accu, accd):
        t = pl.program_id(1)

        @pl.when(t == 0)
        def _():
            accg[...] = jnp.zeros_like(accg)
            accu[...] = jnp.zeros_like(accu)
            accd[...] = jnp.zeros_like(accd)

        hv = h_ref[...]
        accg[...] += _tn(da_ref[...], hv)
        accu[...] += _tn(db_ref[...], hv)
        accd[...] += _tn(z_ref[...], dyb_ref[...])

        @pl.when(t == nt - 1)
        def _():
            g_ref[0, 0:F4, :] = accg[...].astype(BF16)
            g_ref[0, F4:R2, :] = accu[0:R2 - F4, :].astype(BF16)
            g_ref[1, 0:2 * F4 - R2, :] = accu[R2 - F4:F4, :].astype(BF16)
            g_ref[1, 2 * F4 - R2:R2, :] = accd[...].astype(BF16)

    tok = pl.BlockSpec((tk, D), lambda s, t: (t, 0))
    chunk = pl.BlockSpec((None, tk, F4), lambda s, t: (s, t, 0))
    return _pallas(
        body, name=f"ffn_dw_{l}_{f}", grid=(NS, nt),
        in_specs=[tok, tok, chunk, chunk, chunk],
        out_specs=pl.BlockSpec((2, None, R2, D), lambda s, t: (0, s, 0, 0)),
        out_shape=jax.ShapeDtypeStruct((2, NS, R2, D), BF16),
        scratch_shapes=[pltpu.VMEM((F4, D), F32), pltpu.VMEM((F4, D), F32), pltpu.VMEM((F4, D), F32)],
        compiler_params=_cp(2),
    )(h, dyb, da, db, z)


def _mix_in(x, g, win, l, gather=()):
    T, D = x.shape
    C4 = win.shape[-1]
    tm = _tile(T, 512)
    ni = T // tm
    n = len(gather)
    g_shape, g_scratch = _gather_shapes(gather)

    def body(*refs):
        x_ref, g_ref, w_ref = refs[:3]
        gin = refs[3:3 + n]
        h_ref, u_ref = refs[3 + n:5 + n]
        gout = refs[5 + n:5 + 2 * n]
        gsems = refs[5 + 2 * n:]
        i = pl.program_id(0)
        j = pl.program_id(1)

        if n:
            @pl.when((i == 0) & (j == 0))
            def _():
                _gather_start(gin, gout, gsems)

        @pl.when(j == 0)
        def _():
            xh, _ = _rms_stats(x_ref[...])
            h_ref[...] = (xh * g_ref[...]).astype(BF16)

        u_ref[...] = _nn(h_ref[...], w_ref[...]).astype(BF16)

        if n:
            @pl.when((i == ni - 1) & (j == NS - 1))
            def _():
                _gather_finish(gin, gout, gsems)

    return _pallas(
        body, name=f"mix_in_{l}", grid=(ni, NS),
        in_specs=[pl.BlockSpec((tm, D), lambda i, j: (i, 0)), pl.BlockSpec((1, D), lambda i, j: (0, 0)),
                  pl.BlockSpec((None, D, C4), lambda i, j: (j, 0, 0))] + [ANY] * n,
        out_specs=[pl.BlockSpec((tm, D), lambda i, j: (i, 0)), pl.BlockSpec((tm, C4), lambda i, j: (i, j))] + [ANY] * n,
        out_shape=[jax.ShapeDtypeStruct((T, D), BF16), jax.ShapeDtypeStruct((T, NS * C4), BF16)] + g_shape,
        scratch_shapes=g_scratch,
        compiler_params=_cp(2),
    )(x, g, win, *gather)


def _pool_lane_window(n):
    lane = lax.broadcasted_iota(jnp.int32, (1, n), 1) // (n // len(POOL_WINDOWS))
    w = jnp.full((1, n), float(POOL_WINDOWS[-1]), F32)
    for gi in range(len(POOL_WINDOWS) - 1):
        w = jnp.where(lane == gi, float(POOL_WINDOWS[gi]), w)
    return lane, w


def _pool_select(lane, sums):
    out = sums[-1]
    for gi in range(len(sums) - 1):
        out = jnp.where(lane == gi, sums[gi], out)
    return out


def _back(v, s):
    return v if s == 0 else pltpu.roll(v, s, 0)


def _fwd_shift(v, s):
    return v if s == 0 else pltpu.roll(v, v.shape[0] - s, 0)


def _mix_seq_fwd(u, cdw, cb, lg, lb, sdw, pwblk, ps, bl, l):
    T = u.shape[0]
    S = T // bl
    ts = _tile(S, 256)
    nt = S // ts
    DC, DS = cdw.shape[-1], sdw.shape[-1]
    o_ag, o_bg, o_cg, o_bx, o_p, o_end = DC, 2 * DC, 2 * DC + DS, 2 * DC + 2 * DS, 2 * DC + 3 * DS, 2 * DC + 4 * DS

    def body(up_ref, uc_ref, cdw_ref, cb_ref, lg_ref, lb_ref, sdw_ref, pw_ref, ps_ref, act_ref, cv_ref):
        i = pl.program_id(1)
        keep = jnp.where(i > 0, 1.0, 0.0).astype(F32)

        def ext(lo, hi):
            p = up_ref[ts - HALO:ts, lo:hi].astype(F32) * keep
            return jnp.concatenate([p, uc_ref[:, lo:hi].astype(F32)], axis=0)

        glu = ext(0, o_ag) * _sigmoid(ext(o_ag, o_bg))
        cv = jnp.zeros((ts, DC), F32) + cb_ref[...]
        for s in range(CONV_W):
            cv = cv + _back(glu, s)[HALO:, :] * cdw_ref[CONV_W - 1 - s:CONV_W - s, :]
        cv_ref[...] = cv
        mu = jnp.mean(cv, axis=-1, keepdims=True)
        xc = cv - mu
        lnv = xc * lax.rsqrt(jnp.mean(xc * xc, axis=-1, keepdims=True) + EPS) * lg_ref[...] + lb_ref[...]
        act_ref[:, 0:DC] = (lnv * _sigmoid(lnv)).astype(BF16)

        q = ext(o_cg, o_bx) * ext(o_bx, o_p)
        sc = jnp.zeros((ts, DS), F32)
        for s in range(SHORT_W):
            sc = sc + _back(q, s)[HALO:, :] * sdw_ref[SHORT_W - 1 - s:SHORT_W - s, :]
        act_ref[:, DC:DC + DS] = (uc_ref[:, o_bg:o_cg].astype(F32) * sc).astype(BF16)

        p = ext(o_p, o_end)
        lane, wl = _pool_lane_window(DS)
        sums, cur, sh = [], p, 1
        for _ in POOL_WINDOWS:
            cur = cur + _back(cur, sh)
            sums.append(cur[HALO:, :])
            sh *= 2
        pos = (i * ts + lax.broadcasted_iota(jnp.int32, (ts, 1), 0) + 1).astype(F32)
        pooled = _pool_select(lane, sums) / jnp.minimum(pos, wl) - p[HALO:, :]
        act_ref[:, DC + DS:DC + 2 * DS] = (_nn(pooled.astype(BF16), pw_ref[...]) * ps_ref[...]).astype(BF16)

    ucol = 2 * DC + 4 * DS
    full = lambda a: pl.BlockSpec(a.shape, lambda b, i: (0,) * a.ndim)
    return _pallas(
        body, name=f"mix_seq_fwd_{l}", grid=(bl, nt),
        in_specs=[pl.BlockSpec((ts, ucol), lambda b, i: (b * nt + jnp.maximum(i - 1, 0), 0)),
                  pl.BlockSpec((ts, ucol), lambda b, i: (b * nt + i, 0)),
                  full(cdw), full(cb), full(lg), full(lb), full(sdw), full(pwblk), full(ps)],
        out_specs=[pl.BlockSpec((ts, DC + 2 * DS), lambda b, i: (b * nt + i, 0)),
                   pl.BlockSpec((ts, DC), lambda b, i: (b * nt + i, 0))],
        out_shape=[jax.ShapeDtypeStruct((T, DC + 2 * DS), BF16), jax.ShapeDtypeStruct((T, DC), F32)],
        compiler_params=_cp(2),
    )(u, u, cdw, cb, lg, lb, sdw, pwblk, ps)


def _mix_out(x, act, u, wp, wo, l):
    T, D = x.shape
    tm = _tile(T, 512)
    DA = act.shape[-1]
    DC, DS = DA // 2, DA // 4
    NB = D // NS
    gcol = (2 * DC + 4 * DS) // D

    def body(x_ref, act_ref, g0_ref, g1_ref, g2_ref, wp_ref, wo_ref, xo_ref, y_ref, m_ref):
        parts = [(0, DC), (DC, DC + DS), (DC + DS, DC + 2 * DS)]
        m = jnp.zeros((tm, D), F32)
        for k, (lo, hi) in enumerate(parts):
            av = act_ref[:, lo:hi]
            y = jnp.concatenate([_nn(av, wp_ref[s, lo:hi, :]) for s in range(NS)], axis=1)
            y_ref[:, k * D:(k + 1) * D] = y.astype(BF16)
            gl = (g0_ref, g1_ref, g2_ref)[k][...].astype(F32)
            m = m + _sigmoid(gl) * y
        mb = m.astype(BF16)
        m_ref[...] = mb
        out = _nn(mb[:, 0:NB], wo_ref[0])
        for s in range(1, NS):
            out = out + _nn(mb[:, s * NB:(s + 1) * NB], wo_ref[s])
        xo_ref[...] = x_ref[...] + out

    tok = lambda w: pl.BlockSpec((tm, w), lambda i: (i, 0))
    return _pallas(
        body, name=f"mix_out_{l}", grid=(T // tm,),
        in_specs=[tok(D), tok(DA),
                  pl.BlockSpec((tm, D), lambda i: (i, gcol)), pl.BlockSpec((tm, D), lambda i: (i, gcol + 1)),
                  pl.BlockSpec((tm, D), lambda i: (i, gcol + 2)),
                  pl.BlockSpec((NS, DA, NB), lambda i: (0, 0, 0)),
                  pl.BlockSpec((NS, NB, D), lambda i: (0, 0, 0))],
        out_specs=[tok(D), tok(3 * D), tok(D)],
        out_shape=[jax.ShapeDtypeStruct((T, D), F32), jax.ShapeDtypeStruct((T, 3 * D), BF16),
                   jax.ShapeDtypeStruct((T, D), BF16)],
        compiler_params=_cp(1),
    )(x, act, u, u, u, wp, wo)


def _mix_out_bwd(dxn, y, u, act, m, wp, wo, l):
    T, D = dxn.shape
    tm = _tile(T, 256)
    nt = T // tm
    DA = act.shape[-1]
    DC, DS = DA // 2, DA // 4
    NB = D // NS
    UC = u.shape[-1]
    g_lo = 2 * DC + 4 * DS
    gcol = g_lo // D
    parts = [(0, DC), (DC, DC + DS), (DC + DS, DC + 2 * DS)]

    def body(dx_ref, y_ref, g0_ref, g1_ref, g2_ref, act_ref, m_ref, wp_ref, wo_ref,
             du_ref, dact_ref, gwo_ref, gwp_ref, acc_wo, acc_wp):
        i = pl.program_id(0)

        @pl.when(i == 0)
        def _():
            acc_wo[...] = jnp.zeros_like(acc_wo)
            acc_wp[...] = jnp.zeros_like(acc_wp)

        dxb = dx_ref[...].astype(BF16)
        dm = jnp.concatenate([_nt(dxb, wo_ref[s]) for s in range(NS)], axis=1)
        acc_wo[...] += _tn(m_ref[...], dxb)
        du_ref[:, 0:g_lo] = jnp.zeros((tm, g_lo), BF16)
        for k, (lo, hi) in enumerate(parts):
            sg = _sigmoid((g0_ref, g1_ref, g2_ref)[k][...].astype(F32))
            yk = y_ref[:, k * D:(k + 1) * D].astype(F32)
            du_ref[:, g_lo + k * D:g_lo + (k + 1) * D] = (dm * yk * sg * (1.0 - sg)).astype(BF16)
            dyk = (dm * sg).astype(BF16)
            dk = _nt(dyk[:, 0:NB], wp_ref[0, lo:hi, :])
            for s in range(1, NS):
                dk = dk + _nt(dyk[:, s * NB:(s + 1) * NB], wp_ref[s, lo:hi, :])
            dact_ref[:, lo:hi] = dk
            acc_wp[lo:hi, :] += _tn(act_ref[:, lo:hi], dyk)

        @pl.when(i == nt - 1)
        def _():
            for s in range(NS):
                for hf in range(2):
                    r0 = s * NB + hf * (NB // 2)
                    gwo_ref[hf, s] = acc_wo[r0:r0 + NB // 2, :].astype(BF16)
                    gwp_ref[hf, s] = acc_wp[hf * (DA // 2):(hf + 1) * (DA // 2), s * NB:(s + 1) * NB].astype(BF16)

    tok = lambda w: pl.BlockSpec((tm, w), lambda i: (i, 0))
    whole = lambda shp: pl.BlockSpec(shp, lambda i: (0,) * len(shp))
    return _pallas(
        body, name=f"mix_out_bwd_{l}", grid=(nt,),
        in_specs=[tok(D), tok(3 * D),
                  pl.BlockSpec((tm, D), lambda i: (i, gcol)), pl.BlockSpec((tm, D), lambda i: (i, gcol + 1)),
                  pl.BlockSpec((tm, D), lambda i: (i, gcol + 2)),
                  tok(DA), tok(D), whole((NS, DA, NB)), whole((NS, NB, D))],
        out_specs=[tok(UC), tok(DA), whole((2, NS, NB // 2, D)), whole((2, NS, DA // 2, NB))],
        out_shape=[jax.ShapeDtypeStruct((T, UC), BF16), jax.ShapeDtypeStruct((T, DA), F32),
                   jax.ShapeDtypeStruct((2, NS, NB // 2, D), BF16), jax.ShapeDtypeStruct((2, NS, DA // 2, NB), BF16)],
        scratch_shapes=[pltpu.VMEM((D, D), F32), pltpu.VMEM((DA, D), F32)],
        compiler_params=_cp(1),
    )(dxn, y, u, u, u, act, m, wp, wo)


def _mix_seq_bwd(du, u, dact, cv, cdw, lg, lb, sdw, pwblk, ps, bl, l):
    T = u.shape[0]
    S = T // bl
    ts = _tile(S, 256)
    nt = S // ts
    DC, DS = cdw.shape[-1], sdw.shape[-1]
    DA = DC + 2 * DS
    o_ag, o_bg, o_cg, o_bx, o_p, o_end = DC, 2 * DC, 2 * DC + DS, 2 * DC + 2 * DS, 2 * DC + 3 * DS, 2 * DC + 4 * DS
    n_f = ts + HALO

    def body(du_in, up_ref, uc_ref, un_ref, dac_ref, dan_ref, cvc_ref, cvn_ref,
             cdw_ref, lg_ref, lb_ref, sdw_ref, pw_ref, ps_ref,
             du_ref, gcdw_ref, g512_ref, g256_ref, gpw_ref):
        del du_in
        b = pl.program_id(0)
        i = pl.program_id(1)
        keep_p = jnp.where(i > 0, 1.0, 0.0).astype(F32)
        keep_n = jnp.where(i < nt - 1, 1.0, 0.0).astype(F32)

        @pl.when((b == 0) & (i == 0))
        def _():
            gcdw_ref[...] = jnp.zeros_like(gcdw_ref)
            g512_ref[...] = jnp.zeros_like(g512_ref)
            g256_ref[...] = jnp.zeros_like(g256_ref)
            gpw_ref[...] = jnp.zeros_like(gpw_ref)

        def back(lo, hi):
            p = up_ref[ts - HALO:ts, lo:hi].astype(F32) * keep_p
            return jnp.concatenate([p, uc_ref[:, lo:hi].astype(F32)], axis=0)

        def fwd(cur, nxt, lo, hi, mask):
            n = nxt[0:HALO, lo:hi].astype(F32)
            if mask:
                n = n * keep_n
            return jnp.concatenate([cur[:, lo:hi].astype(F32), n], axis=0)

        cvx = fwd(cvc_ref, cvn_ref, 0, DC, False)
        dA = fwd(dac_ref, dan_ref, 0, DC, True)
        mu = jnp.mean(cvx, axis=-1, keepdims=True)
        xc = cvx - mu
        rs = lax.rsqrt(jnp.mean(xc * xc, axis=-1, keepdims=True) + EPS)
        xh = xc * rs
        lnv = xh * lg_ref[...] + lb_ref[...]
        sg = _sigmoid(lnv)
        dln = dA * (sg * (1.0 + lnv * (1.0 - sg)))
        dxh = dln * lg_ref[...]
        dcv = rs * (dxh - jnp.mean(dxh, axis=-1, keepdims=True) - xh * jnp.mean(dxh * xh, axis=-1, keepdims=True))
        g512_ref[0:1, :] += jnp.sum(dcv[0:ts], axis=0, keepdims=True)
        g512_ref[1:2, :] += jnp.sum((dln * xh)[0:ts], axis=0, keepdims=True)
        g512_ref[2:3, :] += jnp.sum(dln[0:ts], axis=0, keepdims=True)

        av = back(0, o_ag)
        sga = _sigmoid(back(o_ag, o_bg))
        glu = av * sga
        dcv_c = dcv[0:ts]
        dglu = jnp.zeros((ts, DC), F32)
        for s in range(CONV_W):
            k = CONV_W - 1 - s
            dglu = dglu + _fwd_shift(dcv, s)[0:ts, :] * cdw_ref[k:k + 1, :]
            gcdw_ref[k:k + 1, :] += jnp.sum(_back(glu, s)[HALO:, :] * dcv_c, axis=0, keepdims=True)
        sga_c = sga[HALO:, :]
        du_ref[:, 0:o_ag] = (dglu * sga_c).astype(BF16)
        du_ref[:, o_ag:o_bg] = (dglu * av[HALO:, :] * sga_c * (1.0 - sga_c)).astype(BF16)

        cg = back(o_cg, o_bx)
        bx = back(o_bx, o_p)
        q = cg * bx
        sc = jnp.zeros((ts, DS), F32)
        for s in range(SHORT_W):
            sc = sc + _back(q, s)[HALO:, :] * sdw_ref[SHORT_W - 1 - s:SHORT_W - s, :]
        dB = fwd(dac_ref, dan_ref, DC, DC + DS, True)
        ds = dB * fwd(uc_ref, un_ref, o_bg, o_cg, False)
        du_ref[:, o_bg:o_cg] = (dB[0:ts] * sc).astype(BF16)
        ds_c = ds[0:ts]
        dq = jnp.zeros((ts, DS), F32)
        for s in range(SHORT_W):
            k = SHORT_W - 1 - s
            dq = dq + _fwd_shift(ds, s)[0:ts, :] * sdw_ref[k:k + 1, :]
            g256_ref[k:k + 1, :] += jnp.sum(_back(q, s)[HALO:, :] * ds_c, axis=0, keepdims=True)
        du_ref[:, o_cg:o_bx] = (dq * bx[HALO:, :]).astype(BF16)
        du_ref[:, o_bx:o_p] = (dq * cg[HALO:, :]).astype(BF16)

        p = back(o_p, o_end)
        lane, wl = _pool_lane_window(DS)
        sums, cur, sh = [], p, 1
        for _ in POOL_WINDOWS:
            cur = cur + _back(cur, sh)
            sums.append(cur[HALO:, :])
            sh *= 2
        pos_c = (i * ts + lax.broadcasted_iota(jnp.int32, (ts, 1), 0) + 1).astype(F32)
        pooled = (_pool_select(lane, sums) / jnp.minimum(pos_c, wl) - p[HALO:, :]).astype(BF16)
        pwv = _nn(pooled, pw_ref[...])
        dC = fwd(dac_ref, dan_ref, DC + DS, DA, True)
        g256_ref[SHORT_W:SHORT_W + 1, :] += jnp.sum(dC[0:ts] * pwv, axis=0, keepdims=True)
        dpw = (dC * ps_ref[...]).astype(BF16)
        gpw_ref[...] += _tn(pooled, dpw[0:ts])
        dpl = _nt(dpw, pw_ref[...])
        pos_f = (i * ts + lax.broadcasted_iota(jnp.int32, (n_f, 1), 0) + 1).astype(F32)
        e = dpl / jnp.minimum(pos_f, wl)
        fsums, cur, sh = [], e, 1
        for _ in POOL_WINDOWS:
            cur = cur + _fwd_shift(cur, sh)
            fsums.append(cur[0:ts, :])
            sh *= 2
        du_ref[:, o_p:o_end] = (_pool_select(lane, fsums) - dpl[0:ts]).astype(BF16)

    full = lambda a: pl.BlockSpec(a.shape, lambda b, i: (0,) * a.ndim)
    row = lambda w, f: pl.BlockSpec((ts, w), lambda b, i: (b * nt + f(i), 0))
    prv = lambda i: jnp.maximum(i - 1, 0)
    nxt = lambda i: jnp.minimum(i + 1, nt - 1)
    cur = lambda i: i
    return _pallas(
        body, name=f"mix_seq_bwd_{l}", grid=(bl, nt),
        in_specs=[ANY, row(o_end, prv), row(o_end, cur), row(o_end, nxt),
                  row(DA, cur), row(DA, nxt), row(DC, cur), row(DC, nxt),
                  full(cdw), full(lg), full(lb), full(sdw), full(pwblk), full(ps)],
        out_specs=[row(o_end, cur), full(cdw),
                   pl.BlockSpec((8, DC), lambda b, i: (0, 0)), pl.BlockSpec((8, DS), lambda b, i: (0, 0)),
                   full(pwblk)],
        out_shape=[jax.ShapeDtypeStruct(du.shape, BF16), jax.ShapeDtypeStruct(cdw.shape, F32),
                   jax.ShapeDtypeStruct((8, DC), F32), jax.ShapeDtypeStruct((8, DS), F32),
                   jax.ShapeDtypeStruct(pwblk.shape, F32)],
        input_output_aliases={0: 0},
        compiler_params=_cp(2),
    )(du, u, u, u, dact, dact, cv, cv, cdw, lg, lb, sdw, pwblk, ps)


def _mix_in_bwd(x, g, dxn, du, win, l):
    T, D = x.shape
    C4 = win.shape[-1]
    tm = _tile(T, 512)

    def body(x_ref, g_ref, dxn_ref, du_ref, w_ref, dx_ref, dg_ref, dh):
        i = pl.program_id(0)
        j = pl.program_id(1)

        @pl.when(j == 0)
        def _():
            dh[...] = jnp.zeros_like(dh)

        @pl.when((i == 0) & (j == 0))
        def _():
            dg_ref[...] = jnp.zeros_like(dg_ref)

        dh[...] += _nt(du_ref[...], w_ref[...])

        @pl.when(j == NS - 1)
        def _():
            dxr, dg = _rms_bwd(dh[...], x_ref[...], g_ref[...])
            dx_ref[...] = dxn_ref[...] + dxr
            dg_ref[...] += dg

    tok = pl.BlockSpec((tm, D), lambda i, j: (i, 0))
    vec = pl.BlockSpec((1, D), lambda i, j: (0, 0))
    return _pallas(
        body, name=f"mix_in_bwd_{l}", grid=(T // tm, NS),
        in_specs=[tok, vec, tok, pl.BlockSpec((tm, C4), lambda i, j: (i, j)),
                  pl.BlockSpec((None, D, C4), lambda i, j: (j, 0, 0))],
        out_specs=[tok, vec],
        out_shape=[jax.ShapeDtypeStruct((T, D), F32), jax.ShapeDtypeStruct((1, D), F32)],
        scratch_shapes=[pltpu.VMEM((tm, D), F32)],
        compiler_params=_cp(2),
    )(x, g, dxn, du, win)


def _mix_in_dw(h, du, l):
    T, D = h.shape
    C4 = du.shape[-1] // NS
    tk = _tile(T, 512)
    nt = T // tk

    def body(h_ref, du_ref, g_ref, acc):
        t = pl.program_id(1)

        @pl.when(t == 0)
        def _():
            acc[...] = jnp.zeros_like(acc)

        acc[...] += _tn(h_ref[...], du_ref[...])

        @pl.when(t == nt - 1)
        def _():
            g_ref[0] = acc[0:D // 2, :].astype(BF16)
            g_ref[1] = acc[D // 2:D, :].astype(BF16)

    return _pallas(
        body, name=f"mix_in_dw_{l}", grid=(NS, nt),
        in_specs=[pl.BlockSpec((tk, D), lambda s, t: (t, 0)), pl.BlockSpec((tk, C4), lambda s, t: (t, s))],
        out_specs=pl.BlockSpec((2, None, D // 2, C4), lambda s, t: (0, s, 0, 0)),
        out_shape=jax.ShapeDtypeStruct((2, NS, D // 2, C4), BF16),
        scratch_shapes=[pltpu.VMEM((D, C4), F32)],
        compiler_params=_cp(2),
    )(h, du)


def _loss_head(x, g, target):
    T, D = x.shape
    tm = _tile(T, 512)

    def body(x_ref, g_ref, t_ref, dx_ref, loss_ref, dg_ref):
        @pl.when(pl.program_id(0) == 0)
        def _():
            loss_ref[...] = jnp.zeros_like(loss_ref)
            dg_ref[...] = jnp.zeros_like(dg_ref)

        xv = x_ref[...]
        xh, rs = _rms_stats(xv)
        gv = g_ref[...]
        e = xh * gv - t_ref[...]
        loss_ref[...] += 0.5 * jnp.sum(jnp.mean(e * e, axis=-1, keepdims=True))
        dy = e * (1.0 / D)
        dyg = dy * gv
        dx_ref[...] = rs * (dyg - xh * jnp.mean(dyg * xh, axis=-1, keepdims=True))
        dg_ref[...] += jnp.sum(dy * xh, axis=0, keepdims=True)

    tok = pl.BlockSpec((tm, D), lambda i: (i, 0))
    vec = pl.BlockSpec((1, D), lambda i: (0, 0))
    return _pallas(
        body, name="loss_head", grid=(T // tm,),
        in_specs=[tok, vec, tok],
        out_specs=[tok, pl.BlockSpec((8, 128), lambda i: (0, 0)), vec],
        out_shape=[jax.ShapeDtypeStruct((T, D), F32), jax.ShapeDtypeStruct((8, 128), F32),
                   jax.ShapeDtypeStruct((1, D), F32)],
        compiler_params=_cp(1),
    )(x, g, target)


def _block_diag(pw):
    G, c, _ = pw.shape
    out = jnp.zeros((G * c, G * c), pw.dtype)
    for gi in range(G):
        out = lax.dynamic_update_slice(out, pw[gi], (gi * c, gi * c))
    return out


def _pad_rows(a, n):
    return jnp.pad(a, ((0, n - a.shape[0]), (0, 0)))


def _merge(g):
    return g.reshape(g.shape[0], g.shape[1] * g.shape[2], g.shape[3])


def _split_dws(dws_g, cw, sw):
    cdw = jnp.transpose(dws_g[:, 0:CONV_W, 0:cw], (1, 0, 2)).reshape(CONV_W, NS * cw)
    sdw = jnp.transpose(dws_g[:, 32:32 + SHORT_W, 0:sw], (1, 0, 2)).reshape(SHORT_W, NS * sw)
    return cdw, sdw


def _fwd_bwd(x3, target3, shards, small, dw_widths, first=None):
    bl, S, D = x3.shape
    T = bl * S
    x = x3.reshape(T, D)
    target = target3.reshape(T, D)
    L = len(shards)
    cw, sw = dw_widths
    row = lambda v: v[None, :]

    saved = []
    wf1 = first
    for l in range(L):
        sp = dict(pwblk=_block_diag(small["pool_w"][l]).astype(BF16), cb=row(small["conv_b"][l]),
                  lg=row(small["conv_ln_g"][l]), lb=row(small["conv_ln_b"][l]), ps=row(small["pool_scale"][l]),
                  g1=row(small["norm_ffn1_g"][l]), gm=row(small["norm_mix_g"][l]), g2=row(small["norm_ffn2_g"][l]))
        x0 = x
        x1, h1, a1, b1, win_g, wp_g, wo_g, dws_g = _ffn_fwd(x0, sp["g1"], wf1, l, 0, gather=shards[l][1])
        win_g, wp_g, wo_g = _merge(win_g), _merge(wp_g), _merge(wo_g)
        cdw, sdw = _split_dws(_merge(dws_g), cw, sw)
        sp["cdw"], sp["sdw"] = _pad_rows(cdw, HALO), _pad_rows(sdw, 8)
        hm, u, wf2 = _mix_in(x1, sp["gm"], win_g, l, gather=(shards[l][2],))
        wf2 = _merge(wf2)
        act, cv = _mix_seq_fwd(u, sp["cdw"], sp["cb"], sp["lg"], sp["lb"], sp["sdw"], sp["pwblk"], sp["ps"], bl, l)
        x2, y, m = _mix_out(x1, act, u, wp_g, wo_g, l)
        nxt = (shards[l + 1][0],) if l + 1 < L else ()
        res = _ffn_fwd(x2, sp["g2"], wf2, l, 1, gather=nxt)
        x, h2, a2, b2 = res[:4]
        saved.append(dict(sp=sp, x0=x0, x1=x1, x2=x2, h1=h1, a1=a1, b1=b1, hm=hm, u=u, act=act, cv=cv, y=y, m=m,
                          h2=h2, a2=a2, b2=b2, wf1=wf1, wf2=wf2, win=win_g, wp=wp_g, wo=wo_g))
        if nxt:
            wf1 = _merge(res[4])

    dx, loss_blk, dgf = _loss_head(x, row(small["final_norm_g"]), target)
    loss = loss_blk[0, 0]

    sg = {k: [None] * L for k in ("norm_ffn1_g", "norm_mix_g", "norm_ffn2_g", "conv_dw", "conv_b", "conv_ln_g",
                                  "conv_ln_b", "short_dw", "pool_w", "pool_scale")}
    big = [None] * L
    G, c = small["pool_w"].shape[1:3]
    for l in reversed(range(L)):
        sv = saved[l]
        sp = sv["sp"]
        dx, dg2, da, db, z, dyb = _ffn_bwd(sv["x2"], sp["g2"], dx, sv["a2"], sv["b2"], sv["wf2"], l, 1)
        g_f2 = _ffn_dw(sv["h2"], dyb, da, db, z, l, 1)
        du, dact, g_o, g_p = _mix_out_bwd(dx, sv["y"], sv["u"], sv["act"], sv["m"], sv["wp"], sv["wo"], l)
        du, gcdw, g512, g256, gpw = _mix_seq_bwd(du, sv["u"], dact, sv["cv"], sp["cdw"], sp["lg"], sp["lb"],
                                                 sp["sdw"], sp["pwblk"], sp["ps"], bl, l)
        dx, dgm = _mix_in_bwd(sv["x1"], sp["gm"], dx, du, sv["win"], l)
        g_in = _mix_in_dw(sv["hm"], du, l)
        dx, dg1, da, db, z, dyb = _ffn_bwd(sv["x0"], sp["g1"], dx, sv["a1"], sv["b1"], sv["wf1"], l, 0)
        g_f1 = _ffn_dw(sv["h1"], dyb, da, db, z, l, 0)
        big[l] = (g_f1, g_f2, g_in, g_p, g_o)
        sg["norm_ffn1_g"][l], sg["norm_mix_g"][l], sg["norm_ffn2_g"][l] = dg1[0], dgm[0], dg2[0]
        sg["conv_dw"][l] = gcdw[:CONV_W]
        sg["conv_b"][l], sg["conv_ln_g"][l], sg["conv_ln_b"][l] = g512[0], g512[1], g512[2]
        sg["short_dw"][l] = g256[:SHORT_W]
        sg["pool_scale"][l] = g256[SHORT_W]
        sg["pool_w"][l] = jnp.stack([gpw[gi * c:(gi + 1) * c, gi * c:(gi + 1) * c] for gi in range(G)])
    small_g = {k: jnp.stack(v) for k, v in sg.items()}
    small_g["final_norm_g"] = dgf[0]
    return loss, dx.reshape(bl, S, D), big, small_g


def _exchange_sibling_halves(gs):
    n = len(gs)

    def body(*refs):
        ins, outs = refs[:n], refs[n:2 * n]
        send_sems, recv_sems = refs[2 * n:]
        x, y, c, _ = _place()
        cps = []
        for ai in range(n):
            cp = pltpu.make_async_remote_copy(src_ref=ins[ai].at[1 - c], dst_ref=outs[ai], send_sem=send_sems.at[ai],
                                              recv_sem=recv_sems.at[ai], device_id=(x, y, 1 - c), device_id_type=MESH)
            cp.start()
            cps.append(cp)
        for cp in cps:
            cp.wait()

    return _pallas(
        body, name="grad_exchange_sibling",
        in_specs=[ANY] * n, out_specs=[ANY] * n,
        out_shape=[jax.ShapeDtypeStruct(g.shape[1:], g.dtype) for g in gs],
        scratch_shapes=[pltpu.SemaphoreType.DMA((n,)), pltpu.SemaphoreType.DMA((n,))],
        compiler_params=pltpu.CompilerParams(has_side_effects=True),
    )(*gs)


def _scatter_to_chips(ps):
    n = len(ps)

    def body(*refs):
        ins, outs = refs[:n], refs[n:2 * n]
        send_sems, recv_sems = refs[2 * n:]
        x, y, c, chips = _place()
        cps = []
        for ai in range(n):
            for k, chip in enumerate(chips):
                cp = pltpu.make_async_remote_copy(
                    src_ref=ins[ai].at[2 * chip[0] + chip[1]], dst_ref=outs[ai].at[k],
                    send_sem=send_sems.at[ai, k], recv_sem=recv_sems.at[ai, k],
                    device_id=(*chip, c), device_id_type=MESH)
                cp.start()
                cps.append(cp)
        for cp in cps:
            cp.wait()

    return _pallas(
        body, name="grad_scatter_chips",
        in_specs=[ANY] * n, out_specs=[ANY] * n,
        out_shape=[jax.ShapeDtypeStruct((3,) + p.shape[1:], p.dtype) for p in ps],
        scratch_shapes=[pltpu.SemaphoreType.DMA((n, 3)), pltpu.SemaphoreType.DMA((n, 3))],
        compiler_params=pltpu.CompilerParams(has_side_effects=True),
    )(*ps)


def _share_final(fs):
    n = len(fs)
    L = fs[0].shape[0]

    def body(*refs):
        outs = refs[n:2 * n]
        send_sems, recv_sems = refs[2 * n:]
        x, y, c, _ = _place()
        sib = (x, y, 1 - c)
        cps = []
        for ai in range(n):
            for l in range(L):
                cp = pltpu.make_async_remote_copy(src_ref=outs[ai].at[l, c], dst_ref=outs[ai].at[l, c],
                                                  send_sem=send_sems.at[ai, l], recv_sem=recv_sems.at[ai, l],
                                                  device_id=sib, device_id_type=MESH)
                cp.start()
                cps.append(cp)
        for ai in range(n):
            for l in range(L):
                blk = outs[ai].at[l, 1 - c]
                pltpu.make_async_remote_copy(src_ref=blk, dst_ref=blk, send_sem=send_sems.at[ai, l],
                                             recv_sem=recv_sems.at[ai, l], device_id=sib, device_id_type=MESH).wait_recv()
        for cp in cps:
            cp.wait_send()

    return _pallas(
        body, name="grad_share_final",
        in_specs=[ANY] * n, out_specs=[ANY] * n,
        out_shape=[jax.ShapeDtypeStruct(f.shape, f.dtype) for f in fs],
        scratch_shapes=[pltpu.SemaphoreType.DMA((n, L)), pltpu.SemaphoreType.DMA((n, L))],
        input_output_aliases={i: i for i in range(n)},
        compiler_params=pltpu.CompilerParams(has_side_effects=True),
    )(*fs)


def _all_reduce_small(v):
    R, W = v.shape

    def body(v_ref, out_ref, buf, send_sems, recv_sems):
        x, y, c, _ = _place()
        me = 4 * x + 2 * y + c
        buf[me] = v_ref[...]
        cps = []
        for k in range(1, 8):
            kx, ky, kc = (k >> 2) & 1, (k >> 1) & 1, k & 1
            to = (1 - x if kx else x, 1 - y if ky else y, 1 - c if kc else c)
            cp = pltpu.make_async_remote_copy(src_ref=v_ref, dst_ref=buf.at[me], send_sem=send_sems.at[k - 1],
                                              recv_sem=recv_sems.at[k - 1], device_id=to, device_id_type=MESH)
            cp.start()
            cps.append(cp)
        for cp in cps:
            cp.wait()
        acc = buf[0]
        for d in range(1, 8):
            acc = acc + buf[d]
        out_ref[...] = acc

    return _pallas(
        body, name="all_reduce_small",
        in_specs=[pl.BlockSpec(memory_space=pltpu.VMEM)], out_specs=pl.BlockSpec(memory_space=pltpu.VMEM),
        out_shape=jax.ShapeDtypeStruct((R, W), F32),
        scratch_shapes=[pltpu.VMEM((8, R, W), F32), pltpu.SemaphoreType.DMA((7,)), pltpu.SemaphoreType.DMA((7,))],
        compiler_params=pltpu.CompilerParams(has_side_effects=True, vmem_limit_bytes=VMEM_LIMIT),
    )(v)


def _row_tile(n, w):
    for t in (512, 352, 256, 128, 64, 32, 16):
        if n % t == 0 and t * w * 4 <= 4 * 1024 * 1024:
            return t
    raise ValueError((n, w))


def _sum_sibling(tag, cidx, g, r):
    _, N, W = g.shape
    tr = _row_tile(N, W)

    def body(c_ref, g_ref, r_ref, o_ref):
        del c_ref
        o_ref[...] = (g_ref[...].astype(F32) + r_ref[...].astype(F32)).astype(BF16)

    return _pallas(
        body, name=f"grad_sum_sibling_{tag}",
        grid_spec=pltpu.PrefetchScalarGridSpec(
            num_scalar_prefetch=1, grid=(N // tr,),
            in_specs=[pl.BlockSpec((None, tr, W), lambda i, c: (c[0], i, 0)),
                      pl.BlockSpec((tr, W), lambda i, c: (i, 0))],
            out_specs=pl.BlockSpec((tr, W), lambda i, c: (i, 0))),
        out_shape=jax.ShapeDtypeStruct((N, W), BF16),
        compiler_params=_cp(1),
    )(cidx, g, r)


def _sum_final(tag, idx, p, r2, l, L, prev):
    _, r, W = p.shape
    tr = _row_tile(r, W)

    def body(*refs):
        p_ref, r2_ref = refs[1:3]
        o_ref = refs[-1]
        acc = p_ref[...].astype(F32)
        for k in range(3):
            acc = acc + r2_ref[k].astype(F32)
        o_ref[...] = acc

    in_specs = [pl.BlockSpec((None, tr, W), lambda i, s: (s[1], i, 0)),
                pl.BlockSpec((3, tr, W), lambda i, s: (0, i, 0))]
    args = [idx, p, r2]
    aliases = {}
    if prev is not None:
        in_specs.append(ANY)
        args.append(prev)
        aliases = {3: 0}
    return _pallas(
        body, name=f"grad_sum_final_{tag}",
        grid_spec=pltpu.PrefetchScalarGridSpec(
            num_scalar_prefetch=1, grid=(r // tr,), in_specs=in_specs,
            out_specs=pl.BlockSpec((None, None, tr, W), lambda i, s: (l, s[0], i, 0))),
        out_shape=jax.ShapeDtypeStruct((L, 2, r, W), F32),
        input_output_aliases=aliases,
        compiler_params=_cp(1),
    )(*args)


def _adam_math(w, g, m, v):
    m = ADAM_B1 * m + (1.0 - ADAM_B1) * g
    v = ADAM_B2 * v + (1.0 - ADAM_B2) * (g * g)
    m_hat = m / (1.0 - ADAM_B1 ** ADAM_STEP)
    v_hat = v / (1.0 - ADAM_B2 ** ADAM_STEP)
    delta = -ADAM_LR * (m_hat / (jnp.sqrt(v_hat) + ADAM_EPS) + ADAM_WD * w)
    return delta, m, v


def _adam_big(name, w, m, v, gfull, row0):
    L, r, W = w.shape
    tr = _row_tile(r, W)
    assert row0 % tr == 0
    off = row0 // tr

    def body(w_ref, m_ref, v_ref, g_ref, go_ref, d_ref, mo_ref, vo_ref):
        g = g_ref[...]
        d, mn, vn = _adam_math(w_ref[...], g, m_ref[...], v_ref[...])
        go_ref[...] = g
        d_ref[...] = d
        mo_ref[...] = mn
        vo_ref[...] = vn

    blk = pl.BlockSpec((None, tr, W), lambda l, i: (l, i, 0))
    shp = jax.ShapeDtypeStruct(w.shape, F32)
    return _pallas(
        body, name=f"adam_{name}", grid=(L, r // tr),
        in_specs=[blk, blk, blk, pl.BlockSpec((None, tr, W), lambda l, i: (l, off + i, 0))],
        out_specs=[blk] * 4, out_shape=[shp] * 4,
        compiler_params=_cp(2),
    )(w, m, v, gfull)


def _adam_small(w, g, m, v):
    def body(w_ref, g_ref, m_ref, v_ref, d_ref, mo_ref, vo_ref):
        d, mn, vn = _adam_math(w_ref[...], g_ref[...], m_ref[...], v_ref[...])
        d_ref[...] = d
        mo_ref[...] = mn
        vo_ref[...] = vn

    spec = pl.BlockSpec(memory_space=pltpu.VMEM)
    shp = jax.ShapeDtypeStruct(w.shape, F32)
    return _pallas(body, name="adam_small", in_specs=[spec] * 4, out_specs=[spec] * 3, out_shape=[shp] * 3)(w, g, m, v)


_WEIGHTS = ['norm_ffn1_g', 'ffn1_w_gate', 'ffn1_w_up', 'ffn1_w_down', 'norm_mix_g', 'w_in', 'conv_dw', 'conv_b',
            'conv_ln_g', 'conv_ln_b', 'w_pa', 'short_dw', 'w_pb', 'pool_w', 'pool_scale', 'w_pc', 'w_o',
            'norm_ffn2_g', 'ffn2_w_gate', 'ffn2_w_up', 'ffn2_w_down', 'final_norm_g']
_BIG = ('ffn1_w_gate', 'ffn1_w_up', 'ffn1_w_down', 'w_in', 'w_pa', 'w_pb', 'w_pc', 'w_o',
        'ffn2_w_gate', 'ffn2_w_up', 'ffn2_w_down')
_TRANSPOSED = ('ffn1_w_gate', 'ffn1_w_up', 'ffn2_w_gate', 'ffn2_w_up')
_SMALL = tuple(n for n in _WEIGHTS if n not in _BIG)
_SMALL_REDUCED = ('norm_ffn1_g', 'norm_mix_g', 'conv_b', 'conv_ln_g', 'conv_ln_b', 'pool_w', 'pool_scale',
                  'norm_ffn2_g', 'final_norm_g', 'conv_dw', 'short_dw')


def _pack(arrs, rows_multiple=8):
    flat = jnp.concatenate([a.reshape(-1) for a in arrs])
    n = flat.shape[0]
    per = 128 * rows_multiple
    padded = -(-n // per) * per
    return jnp.pad(flat, (0, padded - n)).reshape(padded // 128, 128)


def _unpack(buf, shapes):
    flat = buf.reshape(-1)
    out, o = [], 0
    for s in shapes:
        k = 1
        for d in s:
            k *= d
        out.append(flat[o:o + k].reshape(s))
        o += k
    return out


def _halves(a):
    return a.reshape(2, a.shape[0] // 2, a.shape[1])


def _step(P, M, V, x, loss_target):
    tr = lambda a: jnp.transpose(a, (0, 2, 1))
    bf = lambda a: a.astype(BF16)
    L = P['w_in'].shape[0]
    cw, sw = P['conv_dw'].shape[-1], P['short_dw'].shape[-1]
    ffn = [jnp.concatenate([bf(tr(P[f'ffn{f}_w_gate'])), bf(tr(P[f'ffn{f}_w_up'])), bf(P[f'ffn{f}_w_down'])], axis=1)
           for f in (1, 2)]
    win = bf(P['w_in'])
    wp = jnp.concatenate([bf(P['w_pa']), bf(P['w_pb']), bf(P['w_pc'])], axis=1)
    wo = bf(P['w_o'])
    dws = jnp.zeros((L, 64, 128), F32)
    dws = dws.at[:, 0:CONV_W, 0:cw].set(P['conv_dw']).at[:, 32:32 + SHORT_W, 0:sw].set(P['short_dw'])
    shards = [(_halves(ffn[0][l]), (_halves(win[l]), _halves(wp[l]), _halves(wo[l]), _halves(dws[l])),
               _halves(ffn[1][l])) for l in range(L)]

    (first,) = _all_gather([shards[0][0]], "first")
    small = {n: P[n] for n in _SMALL if n not in ('conv_dw', 'short_dw')}
    loss, dx, big, small_g = _fwd_bwd(x, loss_target, shards, small, (cw, sw), first=_merge(first))

    xi, yi, ci = lax.axis_index("x"), lax.axis_index("y"), lax.axis_index("c")
    chip = 2 * xi + yi
    cidx = jnp.stack([ci]).astype(jnp.int32)
    idx = jnp.stack([ci, chip]).astype(jnp.int32)
    flat = [g for l in range(L) for g in big[l]]
    r1 = _exchange_sibling_halves(flat)
    parts = []
    for t, (g, r) in enumerate(zip(flat, r1)):
        W = g.shape[-1]
        p = _sum_sibling(t, cidx, g.reshape(2, -1, W), r.reshape(-1, W))
        parts.append(p.reshape(g.shape[1:]))
    r2 = _scatter_to_chips(parts)
    nb = len(big[0])
    finals = [None] * nb
    for l in range(L):
        for k in range(nb):
            t = l * nb + k
            finals[k] = _sum_final(t, idx, parts[t], r2[t], l, L, finals[k])
    finals = _share_final(finals)
    f_f1, f_f2, f_in, f_p, f_o = [f.reshape(L, -1, f.shape[-1]) for f in finals]

    tot = _all_reduce_small(_pack([small_g[n] for n in _SMALL_REDUCED]))
    tot = dict(zip(_SMALL_REDUCED, _unpack(tot, [small_g[n].shape for n in _SMALL_REDUCED])))
    tot['conv_dw'] = lax.dynamic_slice_in_dim(tot['conv_dw'], chip * cw, cw, axis=2)
    tot['short_dw'] = lax.dynamic_slice_in_dim(tot['short_dw'], chip * sw, sw, axis=2)

    F4 = P['ffn1_w_down'].shape[1]
    dc, ds = P['w_pa'].shape[1], P['w_pb'].shape[1]
    src = {'ffn1_w_gate': (f_f1, 0), 'ffn1_w_up': (f_f1, F4), 'ffn1_w_down': (f_f1, 2 * F4),
           'ffn2_w_gate': (f_f2, 0), 'ffn2_w_up': (f_f2, F4), 'ffn2_w_down': (f_f2, 2 * F4),
           'w_in': (f_in, 0), 'w_pa': (f_p, 0), 'w_pb': (f_p, dc), 'w_pc': (f_p, dc + ds), 'w_o': (f_o, 0)}
    grads, deltas, new_m, new_v = {}, {}, {}, {}
    for n in _BIG:
        gfull, row0 = src[n]
        if n in _TRANSPOSED:
            outs = _adam_big(n, tr(P[n]), tr(M[n]), tr(V[n]), gfull, row0)
            grads[n], deltas[n], new_m[n], new_v[n] = [tr(o) for o in outs]
        else:
            grads[n], deltas[n], new_m[n], new_v[n] = _adam_big(n, P[n], M[n], V[n], gfull, row0)
    shapes = [P[n].shape for n in _SMALL]
    d_s, m_s, v_s = _adam_small(_pack([P[n] for n in _SMALL]), _pack([tot[n] for n in _SMALL]),
                                _pack([M[n] for n in _SMALL]), _pack([V[n] for n in _SMALL]))
    for n, d, mm, vv in zip(_SMALL, _unpack(d_s, shapes), _unpack(m_s, shapes), _unpack(v_s, shapes)):
        grads[n], deltas[n], new_m[n], new_v[n] = tot[n], d, mm, vv

    loss = lax.psum(loss, ("x", "y", "c"))
    return (loss, dx, *[grads[n] for n in _WEIGHTS], *[deltas[n] for n in _WEIGHTS],
            *[new_m[n] for n in _WEIGHTS], *[new_v[n] for n in _WEIGHTS])


def kernel(x, norm_ffn1_g, ffn1_w_gate, ffn1_w_up, ffn1_w_down, norm_mix_g, w_in, conv_dw, conv_b, conv_ln_g, conv_ln_b, w_pa, short_dw, w_pb, pool_w, pool_scale, w_pc, w_o, norm_ffn2_g, ffn2_w_gate, ffn2_w_up, ffn2_w_down, final_norm_g, loss_target, m_norm_ffn1_g, m_ffn1_w_gate, m_ffn1_w_up, m_ffn1_w_down, m_norm_mix_g, m_w_in, m_conv_dw, m_conv_b, m_conv_ln_g, m_conv_ln_b, m_w_pa, m_short_dw, m_w_pb, m_pool_w, m_pool_scale, m_w_pc, m_w_o, m_norm_ffn2_g, m_ffn2_w_gate, m_ffn2_w_up, m_ffn2_w_down, m_final_norm_g, v_norm_ffn1_g, v_ffn1_w_gate, v_ffn1_w_up, v_ffn1_w_down, v_norm_mix_g, v_w_in, v_conv_dw, v_conv_b, v_conv_ln_g, v_conv_ln_b, v_w_pa, v_short_dw, v_w_pb, v_pool_w, v_pool_scale, v_w_pc, v_w_o, v_norm_ffn2_g, v_ffn2_w_gate, v_ffn2_w_up, v_ffn2_w_down, v_final_norm_g):
    args = locals()
    P = {n: args[n] for n in _WEIGHTS}
    M = {n: args["m_" + n] for n in _WEIGHTS}
    V = {n: args["v_" + n] for n in _WEIGHTS}
    return _step(P, M, V, x, loss_target)
```

```python
import jax
import jax.numpy as jnp
from jax import lax
from jax.experimental import pallas as pl
from jax.experimental.pallas import tpu as pltpu

F32 = jnp.float32
BF16 = jnp.bfloat16
EPS = 1e-6
NS = 4
CONV_W = 31
SHORT_W = 3
POOL_WINDOWS = (2, 4, 8, 16)
HALO = 32
ADAM_LR, ADAM_B1, ADAM_B2, ADAM_EPS, ADAM_WD, ADAM_STEP = 0.001, 0.9, 0.999, 1e-08, 0.01, 10
MESH = pl.DeviceIdType.MESH
ANY = pl.BlockSpec(memory_space=pl.ANY)
VMEM_LIMIT = 56 * 1024 * 1024


def _pallas(body, **kw):
    return pl.pallas_call(body, **kw)


def _cp(n_axes):
    return pltpu.CompilerParams(dimension_semantics=("arbitrary",) * n_axes, vmem_limit_bytes=VMEM_LIMIT)


def _nn(a, b):
    return jnp.dot(a, b, preferred_element_type=F32)


def _nt(a, b):
    return lax.dot_general(a, b, (((1,), (1,)), ((), ())), preferred_element_type=F32)


def _tn(a, b):
    return lax.dot_general(a, b, (((0,), (0,)), ((), ())), preferred_element_type=F32)


def _sigmoid(v):
    return 1.0 / (1.0 + jnp.exp(-v))


def _rms_stats(x):
    rs = lax.rsqrt(jnp.mean(x * x, axis=-1, keepdims=True) + EPS)
    return x * rs, rs


def _rms_bwd(dh, x, g):
    xh, rs = _rms_stats(x)
    dhg = dh * g
    dx = rs * (dhg - xh * jnp.mean(dhg * xh, axis=-1, keepdims=True))
    return dx, jnp.sum(dh * xh, axis=0, keepdims=True)


def _tile(n, pref):
    t = min(n, pref)
    assert n % t == 0, (n, t)
    return t


def _place():
    x, y, c = lax.axis_index("x"), lax.axis_index("y"), lax.axis_index("c")
    chips = [(1 - x, y), (x, 1 - y), (1 - x, 1 - y)]
    return x, y, c, chips


def _gather_copies(ins, outs, sems):
    send_sems, recv_sems, local_sems = sems
    x, y, c, chips = _place()
    me = 2 * x + y
    sib = (x, y, 1 - c)

    def remote(ai, k, src, dst, to):
        return pltpu.make_async_remote_copy(src_ref=src, dst_ref=dst, send_sem=send_sems.at[ai, k],
                                            recv_sem=recv_sems.at[ai, k], device_id=to, device_id_type=MESH)

    local, first, landed, passed, from_sib = [], [], [], [], []
    for ai in range(len(ins)):
        local.append(pltpu.make_async_copy(ins[ai], outs[ai].at[me], local_sems.at[ai]))
        for k, chip in enumerate(chips):
            theirs = 2 * chip[0] + chip[1]
            first.append(remote(ai, k, ins[ai].at[c], outs[ai].at[me, c], (*chip, c)))
            blk = outs[ai].at[theirs, c]
            landed.append(remote(ai, k, blk, blk, (*chip, c)))
            passed.append(remote(ai, 3 + k, blk, blk, sib))
            blk2 = outs[ai].at[theirs, 1 - c]
            from_sib.append(remote(ai, 3 + k, blk2, blk2, sib))
    return local, first, landed, passed, from_sib


def _gather_start(ins, outs, sems):
    local, first, _, _, _ = _gather_copies(ins, outs, sems)
    for cp in local + first:
        cp.start()


def _gather_finish(ins, outs, sems):
    local, first, landed, passed, from_sib = _gather_copies(ins, outs, sems)
    for arrive, fwd in zip(landed, passed):
        arrive.wait_recv()
        fwd.start()
    for cp in from_sib:
        cp.wait_recv()
    for cp in first + passed:
        cp.wait_send()
    for cp in local:
        cp.wait()


def _scatter_copies(ins, outs, sems):
    send_sems, recv_sems = sems
    x, y, c, chips = _place()
    return [pltpu.make_async_remote_copy(
        src_ref=ins[ai].at[2 * chip[0] + chip[1]], dst_ref=outs[ai].at[k],
        send_sem=send_sems.at[ai, k], recv_sem=recv_sems.at[ai, k], device_id=(*chip, c), device_id_type=MESH)
        for ai in range(len(ins)) for k, chip in enumerate(chips)]


class _Carried:
    def __init__(self, kind="gather", arrs=()):
        self.kind, self.arrs, self.n = kind, tuple(arrs), len(arrs)
        self.specs = [ANY] * self.n
        if kind == "gather":
            self.out_shape = [jax.ShapeDtypeStruct((NS,) + a.shape, a.dtype) for a in self.arrs]
            sems = [(self.n, 6), (self.n, 6), (self.n,)]
        else:
            self.out_shape = [jax.ShapeDtypeStruct((3,) + a.shape[1:], a.dtype) for a in self.arrs]
            sems = [(self.n, 3), (self.n, 3)]
        self.scratch = [pltpu.SemaphoreType.DMA(s) for s in sems] if self.n else []

    def split(self, refs, n_in, n_out):
        n = self.n
        a, b, c = n_in + n, n_in + n + n_out, n_in + 2 * n + n_out
        n_sem = len(self.scratch)
        own_scr = refs[c:len(refs) - n_sem]
        return refs[:n_in], refs[a:b], own_scr, (refs[n_in:a], refs[b:c], refs[len(refs) - n_sem:])

    def start(self, carried):
        ins, outs, sems = carried
        if self.kind == "gather":
            _gather_start(ins, outs, sems)
        else:
            for cp in _scatter_copies(ins, outs, sems):
                cp.start()

    def finish(self, carried):
        ins, outs, sems = carried
        if self.kind == "gather":
            _gather_finish(ins, outs, sems)
        else:
            for cp in _scatter_copies(ins, outs, sems):
                cp.wait()

    def when(self, cond, carried, what):
        if self.n:
            pl.when(cond)(lambda: what(carried))


_NOTHING = _Carried()


def _run_carried(car, tag):
    def body(*refs):
        _, _, _, carried = car.split(refs, 0, 0)
        car.start(carried)
        car.finish(carried)

    return _pallas(
        body, name=f"{car.kind}_{tag}",
        in_specs=car.specs, out_specs=car.specs, out_shape=car.out_shape, scratch_shapes=car.scratch,
        compiler_params=pltpu.CompilerParams(has_side_effects=True),
    )(*car.arrs)


def _ffn_weight_specs(F4, D):
    return [pl.BlockSpec((None, None, F4, D), lambda i, j: (j, 0, 0, 0)),
            pl.BlockSpec((None, None, F4, D), lambda i, j: (j, 1, 0, 0)),
            pl.BlockSpec((None, F4, D), lambda i, j: (j, 0, 0))]


def _ffn_fwd(x, g, wa, wb, l, f, car=_NOTHING):
    T, D = x.shape
    F4 = wb.shape[1]
    tm = _tile(T, 512)
    ni = T // tm

    def body(*refs):
        (x_ref, g_ref, wg_ref, wu_ref, wd_ref), (xo_ref, h_ref, a_ref, b_ref), (acc,), carried = car.split(refs, 5, 4)
        i = pl.program_id(0)
        j = pl.program_id(1)
        car.when((i == 0) & (j == 0), carried, car.start)

        @pl.when(j == 0)
        def _():
            xh, _ = _rms_stats(x_ref[...])
            h_ref[...] = (xh * g_ref[...]).astype(BF16)
            acc[...] = jnp.zeros_like(acc)

        h = h_ref[...]
        a = _nt(h, wg_ref[...])
        b = _nt(h, wu_ref[...])
        a_ref[...] = a.astype(BF16)
        b_ref[...] = b.astype(BF16)
        z = (a * _sigmoid(a) * b).astype(BF16)
        acc[...] += _nn(z, wd_ref[...])

        @pl.when(j == NS - 1)
        def _():
            xo_ref[...] = x_ref[...] + 0.5 * acc[...]

        car.when((i == ni - 1) & (j == NS - 1), carried, car.finish)

    return _pallas(
        body, name=f"ffn_fwd_{l}_{f}", grid=(ni, NS),
        in_specs=[pl.BlockSpec((tm, D), lambda i, j: (i, 0)), pl.BlockSpec((1, D), lambda i, j: (0, 0))]
        + _ffn_weight_specs(F4, D) + car.specs,
        out_specs=[pl.BlockSpec((tm, D), lambda i, j: (i, 0)),
                   pl.BlockSpec((tm, D), lambda i, j: (i, 0)),
                   pl.BlockSpec((None, tm, F4), lambda i, j: (j, i, 0)),
                   pl.BlockSpec((None, tm, F4), lambda i, j: (j, i, 0))] + car.specs,
        out_shape=[jax.ShapeDtypeStruct((T, D), F32), jax.ShapeDtypeStruct((T, D), BF16),
                   jax.ShapeDtypeStruct((NS, T, F4), BF16), jax.ShapeDtypeStruct((NS, T, F4), BF16)] + car.out_shape,
        scratch_shapes=[pltpu.VMEM((tm, D), F32)] + car.scratch,
        compiler_params=_cp(2),
    )(x, g, wa, wa, wb, *car.arrs)


def _ffn_bwd(x, g, dy, a, b, wa, wb, l, f, car=_NOTHING):
    T, D = x.shape
    F4 = wb.shape[1]
    tm = _tile(T, 512)
    ni = T // tm

    def body(*refs):
        ((x_ref, g_ref, dy_ref, a_ref, b_ref, wg_ref, wu_ref, wd_ref),
         (dx_ref, dg_ref, da_ref, db_ref, z_ref, dyb_ref), (dh,), carried) = car.split(refs, 8, 6)
        i = pl.program_id(0)
        j = pl.program_id(1)
        car.when((i == 0) & (j == 0), carried, car.start)

        @pl.when(j == 0)
        def _():
            dyb_ref[...] = (0.5 * dy_ref[...]).astype(BF16)
            dh[...] = jnp.zeros_like(dh)

        @pl.when((i == 0) & (j == 0))
        def _():
            dg_ref[...] = jnp.zeros_like(dg_ref)

        dz = _nt(dyb_ref[...], wd_ref[...])
        av = a_ref[...].astype(F32)
        bv = b_ref[...].astype(F32)
        sg = _sigmoid(av)
        silu = av * sg
        da = (dz * bv * (sg * (1.0 + av * (1.0 - sg)))).astype(BF16)
        db = (dz * silu).astype(BF16)
        da_ref[...] = da
        db_ref[...] = db
        z_ref[...] = (silu * bv).astype(BF16)
        dh[...] += _nn(da, wg_ref[...]) + _nn(db, wu_ref[...])

        @pl.when(j == NS - 1)
        def _():
            dxn, dg = _rms_bwd(dh[...], x_ref[...], g_ref[...])
            dx_ref[...] = dy_ref[...] + dxn
            dg_ref[...] += dg

        car.when((i == ni - 1) & (j == NS - 1), carried, car.finish)

    tok = pl.BlockSpec((tm, D), lambda i, j: (i, 0))
    vec = pl.BlockSpec((1, D), lambda i, j: (0, 0))
    chunk = pl.BlockSpec((None, tm, F4), lambda i, j: (j, i, 0))
    return _pallas(
        body, name=f"ffn_bwd_{l}_{f}", grid=(ni, NS),
        in_specs=[tok, vec, tok, chunk, chunk] + _ffn_weight_specs(F4, D) + car.specs,
        out_specs=[tok, vec, chunk, chunk, chunk, tok] + car.specs,
        out_shape=[jax.ShapeDtypeStruct((T, D), F32), jax.ShapeDtypeStruct((1, D), F32),
                   jax.ShapeDtypeStruct((NS, T, F4), BF16), jax.ShapeDtypeStruct((NS, T, F4), BF16),
                   jax.ShapeDtypeStruct((NS, T, F4), BF16), jax.ShapeDtypeStruct((T, D), BF16)] + car.out_shape,
        scratch_shapes=[pltpu.VMEM((tm, D), F32)] + car.scratch,
        compiler_params=_cp(2),
    )(x, g, dy, a, b, wa, wa, wb, *car.arrs)


def _ffn_dw(h, dyb, da, db, z, l, f):
    T, D = h.shape
    F4 = da.shape[-1]
    tk = _tile(T, 512)
    nt = T // tk
    R2 = 3 * F4 // 2

    def body(h_ref, dyb_ref, da_ref, db_ref, z_ref, g_ref, accg, accu, accd):
        t = pl.program_id(1)

        @pl.when(t == 0)
        def _():
            accg[...] = jnp.zeros_like(accg)
            accu[...] = jnp.zeros_like(accu)
            accd[...] = jnp.zeros_like(accd)

        hv = h_ref[...]
        accg[...] += _tn(da_ref[...], hv)
        accu[...] += _tn(db_ref[...], hv)
        accd[...] += _tn(z_ref[...], dyb_ref[...])

        @pl.when(t == nt - 1)
        def _():
            g_ref[0, 0:F4, :] = accg[...].astype(BF16)
            g_ref[0, F4:R2, :] = accu[0:R2 - F4, :].astype(BF16)
            g_ref[1, 0:2 * F4 - R2, :] = accu[R2 - F4:F4, :].astype(BF16)
            g_ref[1, 2 * F4 - R2:R2, :] = accd[...].astype(BF16)

    tok = pl.BlockSpec((tk, D), lambda s, t: (t, 0))
    chunk = pl.BlockSpec((None, tk, F4), lambda s, t: (s, t, 0))
    return _pallas(
        body, name=f"ffn_dw_{l}_{f}", grid=(NS, nt),
        in_specs=[tok, tok, chunk, chunk, chunk],
        out_specs=pl.BlockSpec((2, None, R2, D), lambda s, t: (0, s, 0, 0)),
        out_shape=jax.ShapeDtypeStruct((2, NS, R2, D), BF16),
        scratch_shapes=[pltpu.VMEM((F4, D), F32), pltpu.VMEM((F4, D), F32), pltpu.VMEM((F4, D), F32)],
        compiler_params=_cp(2),
    )(h, dyb, da, db, z)


def _mix_in(x, g, win, l, car=_NOTHING):
    T, D = x.shape
    C4 = win.shape[-1]
    tm = _tile(T, 512)
    ni = T // tm

    def body(*refs):
        (x_ref, g_ref, w_ref), (h_ref, u_ref), _, carried = car.split(refs, 3, 2)
        i = pl.program_id(0)
        j = pl.program_id(1)
        car.when((i == 0) & (j == 0), carried, car.start)

        @pl.when(j == 0)
        def _():
            xh, _ = _rms_stats(x_ref[...])
            h_ref[...] = (xh * g_ref[...]).astype(BF16)

        u_ref[...] = _nn(h_ref[...], w_ref[...]).astype(BF16)
        car.when((i == ni - 1) & (j == NS - 1), carried, car.finish)

    return _pallas(
        body, name=f"mix_in_{l}", grid=(ni, NS),
        in_specs=[pl.BlockSpec((tm, D), lambda i, j: (i, 0)), pl.BlockSpec((1, D), lambda i, j: (0, 0)),
                  pl.BlockSpec((None, D, C4), lambda i, j: (j, 0, 0))] + car.specs,
        out_specs=[pl.BlockSpec((tm, D), lambda i, j: (i, 0)), pl.BlockSpec((tm, C4), lambda i, j: (i, j))] + car.specs,
        out_shape=[jax.ShapeDtypeStruct((T, D), BF16), jax.ShapeDtypeStruct((T, NS * C4), BF16)] + car.out_shape,
        scratch_shapes=car.scratch,
        compiler_params=_cp(2),
    )(x, g, win, *car.arrs)


def _pool_lane_window(n):
    lane = lax.broadcasted_iota(jnp.int32, (1, n), 1) // (n // len(POOL_WINDOWS))
    w = jnp.full((1, n), float(POOL_WINDOWS[-1]), F32)
    for gi in range(len(POOL_WINDOWS) - 1):
        w = jnp.where(lane == gi, float(POOL_WINDOWS[gi]), w)
    return lane, w


def _pool_select(lane, sums):
    out = sums[-1]
    for gi in range(len(sums) - 1):
        out = jnp.where(lane == gi, sums[gi], out)
    return out


def _back(v, s):
    return v if s == 0 else pltpu.roll(v, s, 0)


def _fwd_shift(v, s):
    return v if s == 0 else pltpu.roll(v, v.shape[0] - s, 0)


def _mix_seq_fwd(u, cdw, cb, lg, lb, sdw, pwblk, ps, bl, l, car=_NOTHING):
    T = u.shape[0]
    S = T // bl
    ts = _tile(S, 256)
    nt = S // ts
    DC, DS = cdw.shape[-1], sdw.shape[-1]
    o_ag, o_bg, o_cg, o_bx, o_p, o_end = DC, 2 * DC, 2 * DC + DS, 2 * DC + 2 * DS, 2 * DC + 3 * DS, 2 * DC + 4 * DS

    def body(*refs):
        ((up_ref, uc_ref, cdw_ref, cb_ref, lg_ref, lb_ref, sdw_ref, pw_ref, ps_ref),
         (act_ref, cv_ref), _, carried) = car.split(refs, 9, 2)
        i = pl.program_id(1)
        car.when((pl.program_id(0) == 0) & (i == 0), carried, car.start)
        keep = jnp.where(i > 0, 1.0, 0.0).astype(F32)

        def ext(lo, hi):
            p = up_ref[ts - HALO:ts, lo:hi].astype(F32) * keep
            return jnp.concatenate([p, uc_ref[:, lo:hi].astype(F32)], axis=0)

        glu = ext(0, o_ag) * _sigmoid(ext(o_ag, o_bg))
        cv = jnp.zeros((ts, DC), F32) + cb_ref[...]
        for s in range(CONV_W):
            cv = cv + _back(glu, s)[HALO:, :] * cdw_ref[CONV_W - 1 - s:CONV_W - s, :]
        cv_ref[...] = cv
        mu = jnp.mean(cv, axis=-1, keepdims=True)
        xc = cv - mu
        lnv = xc * lax.rsqrt(jnp.mean(xc * xc, axis=-1, keepdims=True) + EPS) * lg_ref[...] + lb_ref[...]
        act_ref[:, 0:DC] = (lnv * _sigmoid(lnv)).astype(BF16)

        q = ext(o_cg, o_bx) * ext(o_bx, o_p)
        sc = jnp.zeros((ts, DS), F32)
        for s in range(SHORT_W):
            sc = sc + _back(q, s)[HALO:, :] * sdw_ref[SHORT_W - 1 - s:SHORT_W - s, :]
        act_ref[:, DC:DC + DS] = (uc_ref[:, o_bg:o_cg].astype(F32) * sc).astype(BF16)

        p = ext(o_p, o_end)
        lane, wl = _pool_lane_window(DS)
        sums, cur, sh = [], p, 1
        for _ in POOL_WINDOWS:
            cur = cur + _back(cur, sh)
            sums.append(cur[HALO:, :])
            sh *= 2
        pos = (i * ts + lax.broadcasted_iota(jnp.int32, (ts, 1), 0) + 1).astype(F32)
        pooled = _pool_select(lane, sums) / jnp.minimum(pos, wl) - p[HALO:, :]
        act_ref[:, DC + DS:DC + 2 * DS] = (_nn(pooled.astype(BF16), pw_ref[...]) * ps_ref[...]).astype(BF16)
        car.when((pl.program_id(0) == bl - 1) & (i == nt - 1), carried, car.finish)

    ucol = 2 * DC + 4 * DS
    full = lambda a: pl.BlockSpec(a.shape, lambda b, i: (0,) * a.ndim)
    return _pallas(
        body, name=f"mix_seq_fwd_{l}", grid=(bl, nt),
        in_specs=[pl.BlockSpec((ts, ucol), lambda b, i: (b * nt + jnp.maximum(i - 1, 0), 0)),
                  pl.BlockSpec((ts, ucol), lambda b, i: (b * nt + i, 0)),
                  full(cdw), full(cb), full(lg), full(lb), full(sdw), full(pwblk), full(ps)] + car.specs,
        out_specs=[pl.BlockSpec((ts, DC + 2 * DS), lambda b, i: (b * nt + i, 0)),
                   pl.BlockSpec((ts, DC), lambda b, i: (b * nt + i, 0))] + car.specs,
        out_shape=[jax.ShapeDtypeStruct((T, DC + 2 * DS), BF16), jax.ShapeDtypeStruct((T, DC), F32)] + car.out_shape,
        scratch_shapes=car.scratch,
        compiler_params=_cp(2),
    )(u, u, cdw, cb, lg, lb, sdw, pwblk, ps, *car.arrs)


def _mix_out(x, act, u, wp, wo, l, car=_NOTHING):
    T, D = x.shape
    tm = _tile(T, 512)
    ni = T // tm
    DA = act.shape[-1]
    DC, DS = DA // 2, DA // 4
    NB = D // NS
    gcol = (2 * DC + 4 * DS) // D

    def body(*refs):
        ((x_ref, act_ref, g0_ref, g1_ref, g2_ref, wp_ref, wo_ref),
         (xo_ref, y_ref, m_ref), _, carried) = car.split(refs, 7, 3)
        car.when(pl.program_id(0) == 0, carried, car.start)
        parts = [(0, DC), (DC, DC + DS), (DC + DS, DC + 2 * DS)]
        m = jnp.zeros((tm, D), F32)
        for k, (lo, hi) in enumerate(parts):
            av = act_ref[:, lo:hi]
            y = jnp.concatenate([_nn(av, wp_ref[s, lo:hi, :]) for s in range(NS)], axis=1)
            y_ref[:, k * D:(k + 1) * D] = y.astype(BF16)
            gl = (g0_ref, g1_ref, g2_ref)[k][...].astype(F32)
            m = m + _sigmoid(gl) * y
        mb = m.astype(BF16)
        m_ref[...] = mb
        out = _nn(mb[:, 0:NB], wo_ref[0])
        for s in range(1, NS):
            out = out + _nn(mb[:, s * NB:(s + 1) * NB], wo_ref[s])
        xo_ref[...] = x_ref[...] + out
        car.when(pl.program_id(0) == ni - 1, carried, car.finish)

    tok = lambda w: pl.BlockSpec((tm, w), lambda i: (i, 0))
    return _pallas(
        body, name=f"mix_out_{l}", grid=(ni,),
        in_specs=[tok(D), tok(DA),
                  pl.BlockSpec((tm, D), lambda i: (i, gcol)), pl.BlockSpec((tm, D), lambda i: (i, gcol + 1)),
                  pl.BlockSpec((tm, D), lambda i: (i, gcol + 2)),
                  pl.BlockSpec((NS, DA, NB), lambda i: (0, 0, 0)),
                  pl.BlockSpec((NS, NB, D), lambda i: (0, 0, 0))] + car.specs,
        out_specs=[tok(D), tok(3 * D), tok(D)] + car.specs,
        out_shape=[jax.ShapeDtypeStruct((T, D), F32), jax.ShapeDtypeStruct((T, 3 * D), BF16),
                   jax.ShapeDtypeStruct((T, D), BF16)] + car.out_shape,
        scratch_shapes=car.scratch,
        compiler_params=_cp(1),
    )(x, act, u, u, u, wp, wo, *car.arrs)


def _mix_out_bwd(dxn, y, u, act, m, wp, wo, l):
    T, D = dxn.shape
    tm = _tile(T, 256)
    nt = T // tm
    DA = act.shape[-1]
    DC, DS = DA // 2, DA // 4
    NB = D // NS
    UC = u.shape[-1]
    g_lo = 2 * DC + 4 * DS
    gcol = g_lo // D
    parts = [(0, DC), (DC, DC + DS), (DC + DS, DC + 2 * DS)]

    def body(dx_ref, y_ref, g0_ref, g1_ref, g2_ref, act_ref, m_ref, wp_ref, wo_ref,
             du_ref, dact_ref, gwo_ref, gwp_ref, acc_wo, acc_wp):
        i = pl.program_id(0)

        @pl.when(i == 0)
        def _():
            acc_wo[...] = jnp.zeros_like(acc_wo)
            acc_wp[...] = jnp.zeros_like(acc_wp)

        dxb = dx_ref[...].astype(BF16)
        dm = jnp.concatenate([_nt(dxb, wo_ref[s]) for s in range(NS)], axis=1)
        acc_wo[...] += _tn(m_ref[...], dxb)
        du_ref[:, 0:g_lo] = jnp.zeros((tm, g_lo), BF16)
        for k, (lo, hi) in enumerate(parts):
            sg = _sigmoid((g0_ref, g1_ref, g2_ref)[k][...].astype(F32))
            yk = y_ref[:, k * D:(k + 1) * D].astype(F32)
            du_ref[:, g_lo + k * D:g_lo + (k + 1) * D] = (dm * yk * sg * (1.0 - sg)).astype(BF16)
            dyk = (dm * sg).astype(BF16)
            dk = _nt(dyk[:, 0:NB], wp_ref[0, lo:hi, :])
            for s in range(1, NS):
                dk = dk + _nt(dyk[:, s * NB:(s + 1) * NB], wp_ref[s, lo:hi, :])
            dact_ref[:, lo:hi] = dk
            acc_wp[lo:hi, :] += _tn(act_ref[:, lo:hi], dyk)

        @pl.when(i == nt - 1)
        def _():
            for s in range(NS):
                for hf in range(2):
                    r0 = s * NB + hf * (NB // 2)
                    gwo_ref[hf, s] = acc_wo[r0:r0 + NB // 2, :].astype(BF16)
                    gwp_ref[hf, s] = acc_wp[hf * (DA // 2):(hf + 1) * (DA // 2), s * NB:(s + 1) * NB].astype(BF16)

    tok = lambda w: pl.BlockSpec((tm, w), lambda i: (i, 0))
    whole = lambda shp: pl.BlockSpec(shp, lambda i: (0,) * len(shp))
    return _pallas(
        body, name=f"mix_out_bwd_{l}", grid=(nt,),
        in_specs=[tok(D), tok(3 * D),
                  pl.BlockSpec((tm, D), lambda i: (i, gcol)), pl.BlockSpec((tm, D), lambda i: (i, gcol + 1)),
                  pl.BlockSpec((tm, D), lambda i: (i, gcol + 2)),
                  tok(DA), tok(D), whole((NS, DA, NB)), whole((NS, NB, D))],
        out_specs=[tok(UC), tok(DA), whole((2, NS, NB // 2, D)), whole((2, NS, DA // 2, NB))],
        out_shape=[jax.ShapeDtypeStruct((T, UC), BF16), jax.ShapeDtypeStruct((T, DA), F32),
                   jax.ShapeDtypeStruct((2, NS, NB // 2, D), BF16), jax.ShapeDtypeStruct((2, NS, DA // 2, NB), BF16)],
        scratch_shapes=[pltpu.VMEM((D, D), F32), pltpu.VMEM((DA, D), F32)],
        compiler_params=_cp(1),
    )(dxn, y, u, u, u, act, m, wp, wo)


def _mix_seq_bwd(du, u, dact, cv, cdw, lg, lb, sdw, pwblk, ps, bl, l, car=_NOTHING):
    T = u.shape[0]
    S = T // bl
    ts = _tile(S, 256)
    nt = S // ts
    DC, DS = cdw.shape[-1], sdw.shape[-1]
    DA = DC + 2 * DS
    o_ag, o_bg, o_cg, o_bx, o_p, o_end = DC, 2 * DC, 2 * DC + DS, 2 * DC + 2 * DS, 2 * DC + 3 * DS, 2 * DC + 4 * DS
    n_f = ts + HALO

    def body(*refs):
        ((_, up_ref, uc_ref, un_ref, dac_ref, dan_ref, cvc_ref, cvn_ref,
          cdw_ref, lg_ref, lb_ref, sdw_ref, pw_ref, ps_ref),
         (du_ref, gcdw_ref, g512_ref, g256_ref, gpw_ref), _, carried) = car.split(refs, 14, 5)
        b = pl.program_id(0)
        i = pl.program_id(1)
        car.when((b == 0) & (i == 0), carried, car.start)
        keep_p = jnp.where(i > 0, 1.0, 0.0).astype(F32)
        keep_n = jnp.where(i < nt - 1, 1.0, 0.0).astype(F32)

        @pl.when((b == 0) & (i == 0))
        def _():
            gcdw_ref[...] = jnp.zeros_like(gcdw_ref)
            g512_ref[...] = jnp.zeros_like(g512_ref)
            g256_ref[...] = jnp.zeros_like(g256_ref)
            gpw_ref[...] = jnp.zeros_like(gpw_ref)

        def back(lo, hi):
            p = up_ref[ts - HALO:ts, lo:hi].astype(F32) * keep_p
            return jnp.concatenate([p, uc_ref[:, lo:hi].astype(F32)], axis=0)

        def fwd(cur, nxt, lo, hi, mask):
            n = nxt[0:HALO, lo:hi].astype(F32)
            if mask:
                n = n * keep_n
            return jnp.concatenate([cur[:, lo:hi].astype(F32), n], axis=0)

        cvx = fwd(cvc_ref, cvn_ref, 0, DC, False)
        dA = fwd(dac_ref, dan_ref, 0, DC, True)
        mu = jnp.mean(cvx, axis=-1, keepdims=True)
        xc = cvx - mu
        rs = lax.rsqrt(jnp.mean(xc * xc, axis=-1, keepdims=True) + EPS)
        xh = xc * rs
        lnv = xh * lg_ref[...] + lb_ref[...]
        sg = _sigmoid(lnv)
        dln = dA * (sg * (1.0 + lnv * (1.0 - sg)))
        dxh = dln * lg_ref[...]
        dcv = rs * (dxh - jnp.mean(dxh, axis=-1, keepdims=True) - xh * jnp.mean(dxh * xh, axis=-1, keepdims=True))
        g512_ref[0:1, :] += jnp.sum(dcv[0:ts], axis=0, keepdims=True)
        g512_ref[1:2, :] += jnp.sum((dln * xh)[0:ts], axis=0, keepdims=True)
        g512_ref[2:3, :] += jnp.sum(dln[0:ts], axis=0, keepdims=True)

        av = back(0, o_ag)
        sga = _sigmoid(back(o_ag, o_bg))
        glu = av * sga
        dcv_c = dcv[0:ts]
        dglu = jnp.zeros((ts, DC), F32)
        for s in range(CONV_W):
            k = CONV_W - 1 - s
            dglu = dglu + _fwd_shift(dcv, s)[0:ts, :] * cdw_ref[k:k + 1, :]
            gcdw_ref[k:k + 1, :] += jnp.sum(_back(glu, s)[HALO:, :] * dcv_c, axis=0, keepdims=True)
        sga_c = sga[HALO:, :]
        du_ref[:, 0:o_ag] = (dglu * sga_c).astype(BF16)
        du_ref[:, o_ag:o_bg] = (dglu * av[HALO:, :] * sga_c * (1.0 - sga_c)).astype(BF16)

        cg = back(o_cg, o_bx)
        bx = back(o_bx, o_p)
        q = cg * bx
        sc = jnp.zeros((ts, DS), F32)
        for s in range(SHORT_W):
            sc = sc + _back(q, s)[HALO:, :] * sdw_ref[SHORT_W - 1 - s:SHORT_W - s, :]
        dB = fwd(dac_ref, dan_ref, DC, DC + DS, True)
        ds = dB * fwd(uc_ref, un_ref, o_bg, o_cg, False)
        du_ref[:, o_bg:o_cg] = (dB[0:ts] * sc).astype(BF16)
        ds_c = ds[0:ts]
        dq = jnp.zeros((ts, DS), F32)
        for s in range(SHORT_W):
            k = SHORT_W - 1 - s
            dq = dq + _fwd_shift(ds, s)[0:ts, :] * sdw_ref[k:k + 1, :]
            g256_ref[k:k + 1, :] += jnp.sum(_back(q, s)[HALO:, :] * ds_c, axis=0, keepdims=True)
        du_ref[:, o_cg:o_bx] = (dq * bx[HALO:, :]).astype(BF16)
        du_ref[:, o_bx:o_p] = (dq * cg[HALO:, :]).astype(BF16)

        p = back(o_p, o_end)
        lane, wl = _pool_lane_window(DS)
        sums, cur, sh = [], p, 1
        for _ in POOL_WINDOWS:
            cur = cur + _back(cur, sh)
            sums.append(cur[HALO:, :])
            sh *= 2
        pos_c = (i * ts + lax.broadcasted_iota(jnp.int32, (ts, 1), 0) + 1).astype(F32)
        pooled = (_pool_select(lane, sums) / jnp.minimum(pos_c, wl) - p[HALO:, :]).astype(BF16)
        pwv = _nn(pooled, pw_ref[...])
        dC = fwd(dac_ref, dan_ref, DC + DS, DA, True)
        g256_ref[SHORT_W:SHORT_W + 1, :] += jnp.sum(dC[0:ts] * pwv, axis=0, keepdims=True)
        dpw = (dC * ps_ref[...]).astype(BF16)
        gpw_ref[...] += _tn(pooled, dpw[0:ts])
        dpl = _nt(dpw, pw_ref[...])
        pos_f = (i * ts + lax.broadcasted_iota(jnp.int32, (n_f, 1), 0) + 1).astype(F32)
        e = dpl / jnp.minimum(pos_f, wl)
        fsums, cur, sh = [], e, 1
        for _ in POOL_WINDOWS:
            cur = cur + _fwd_shift(cur, sh)
            fsums.append(cur[0:ts, :])
            sh *= 2
        du_ref[:, o_p:o_end] = (_pool_select(lane, fsums) - dpl[0:ts]).astype(BF16)
        car.when((b == bl - 1) & (i == nt - 1), carried, car.finish)

    full = lambda a: pl.BlockSpec(a.shape, lambda b, i: (0,) * a.ndim)
    row = lambda w, f: pl.BlockSpec((ts, w), lambda b, i: (b * nt + f(i), 0))
    prv = lambda i: jnp.maximum(i - 1, 0)
    nxt = lambda i: jnp.minimum(i + 1, nt - 1)
    cur = lambda i: i
    return _pallas(
        body, name=f"mix_seq_bwd_{l}", grid=(bl, nt),
        in_specs=[ANY, row(o_end, prv), row(o_end, cur), row(o_end, nxt),
                  row(DA, cur), row(DA, nxt), row(DC, cur), row(DC, nxt),
                  full(cdw), full(lg), full(lb), full(sdw), full(pwblk), full(ps)] + car.specs,
        out_specs=[row(o_end, cur), full(cdw),
                   pl.BlockSpec((8, DC), lambda b, i: (0, 0)), pl.BlockSpec((8, DS), lambda b, i: (0, 0)),
                   full(pwblk)] + car.specs,
        out_shape=[jax.ShapeDtypeStruct(du.shape, BF16), jax.ShapeDtypeStruct(cdw.shape, F32),
                   jax.ShapeDtypeStruct((8, DC), F32), jax.ShapeDtypeStruct((8, DS), F32),
                   jax.ShapeDtypeStruct(pwblk.shape, F32)] + car.out_shape,
        scratch_shapes=car.scratch,
        input_output_aliases={0: 0},
        compiler_params=_cp(2),
    )(du, u, u, u, dact, dact, cv, cv, cdw, lg, lb, sdw, pwblk, ps, *car.arrs)


def _mix_in_bwd(x, g, dxn, du, win, l):
    T, D = x.shape
    C4 = win.shape[-1]
    tm = _tile(T, 512)

    def body(x_ref, g_ref, dxn_ref, du_ref, w_ref, dx_ref, dg_ref, dh):
        i = pl.program_id(0)
        j = pl.program_id(1)

        @pl.when(j == 0)
        def _():
            dh[...] = jnp.zeros_like(dh)

        @pl.when((i == 0) & (j == 0))
        def _():
            dg_ref[...] = jnp.zeros_like(dg_ref)

        dh[...] += _nt(du_ref[...], w_ref[...])

        @pl.when(j == NS - 1)
        def _():
            dxr, dg = _rms_bwd(dh[...], x_ref[...], g_ref[...])
            dx_ref[...] = dxn_ref[...] + dxr
            dg_ref[...] += dg

    tok = pl.BlockSpec((tm, D), lambda i, j: (i, 0))
    vec = pl.BlockSpec((1, D), lambda i, j: (0, 0))
    return _pallas(
        body, name=f"mix_in_bwd_{l}", grid=(T // tm, NS),
        in_specs=[tok, vec, tok, pl.BlockSpec((tm, C4), lambda i, j: (i, j)),
                  pl.BlockSpec((None, D, C4), lambda i, j: (j, 0, 0))],
        out_specs=[tok, vec],
        out_shape=[jax.ShapeDtypeStruct((T, D), F32), jax.ShapeDtypeStruct((1, D), F32)],
        scratch_shapes=[pltpu.VMEM((tm, D), F32)],
        compiler_params=_cp(2),
    )(x, g, dxn, du, win)


def _mix_in_dw(h, du, l):
    T, D = h.shape
    C4 = du.shape[-1] // NS
    tk = _tile(T, 512)
    nt = T // tk

    def body(h_ref, du_ref, g_ref, acc):
        t = pl.program_id(1)

        @pl.when(t == 0)
        def _():
            acc[...] = jnp.zeros_like(acc)

        acc[...] += _tn(h_ref[...], du_ref[...])

        @pl.when(t == nt - 1)
        def _():
            g_ref[0] = acc[0:D // 2, :].astype(BF16)
            g_ref[1] = acc[D // 2:D, :].astype(BF16)

    return _pallas(
        body, name=f"mix_in_dw_{l}", grid=(NS, nt),
        in_specs=[pl.BlockSpec((tk, D), lambda s, t: (t, 0)), pl.BlockSpec((tk, C4), lambda s, t: (t, s))],
        out_specs=pl.BlockSpec((2, None, D // 2, C4), lambda s, t: (0, s, 0, 0)),
        out_shape=jax.ShapeDtypeStruct((2, NS, D // 2, C4), BF16),
        scratch_shapes=[pltpu.VMEM((D, C4), F32)],
        compiler_params=_cp(2),
    )(h, du)


def _loss_head(x, g, target):
    T, D = x.shape
    tm = _tile(T, 512)

    def body(x_ref, g_ref, t_ref, dx_ref, loss_ref, dg_ref):
        @pl.when(pl.program_id(0) == 0)
        def _():
            loss_ref[...] = jnp.zeros_like(loss_ref)
            dg_ref[...] = jnp.zeros_like(dg_ref)

        xv = x_ref[...]
        xh, rs = _rms_stats(xv)
        gv = g_ref[...]
        e = xh * gv - t_ref[...]
        loss_ref[...] += 0.5 * jnp.sum(jnp.mean(e * e, axis=-1, keepdims=True))
        dy = e * (1.0 / D)
        dyg = dy * gv
        dx_ref[...] = rs * (dyg - xh * jnp.mean(dyg * xh, axis=-1, keepdims=True))
        dg_ref[...] += jnp.sum(dy * xh, axis=0, keepdims=True)

    tok = pl.BlockSpec((tm, D), lambda i: (i, 0))
    vec = pl.BlockSpec((1, D), lambda i: (0, 0))
    return _pallas(
        body, name="loss_head", grid=(T // tm,),
        in_specs=[tok, vec, tok],
        out_specs=[tok, pl.BlockSpec((8, 128), lambda i: (0, 0)), vec],
        out_shape=[jax.ShapeDtypeStruct((T, D), F32), jax.ShapeDtypeStruct((8, 128), F32),
                   jax.ShapeDtypeStruct((1, D), F32)],
        compiler_params=_cp(1),
    )(x, g, target)


def _block_diag(pw):
    G, c, _ = pw.shape
    out = jnp.zeros((G * c, G * c), pw.dtype)
    for gi in range(G):
        out = lax.dynamic_update_slice(out, pw[gi], (gi * c, gi * c))
    return out


def _pad_rows(a, n):
    return jnp.pad(a, ((0, n - a.shape[0]), (0, 0)))


def _merge(g):
    return g.reshape(g.shape[0], g.shape[1] * g.shape[2], g.shape[3])


def _split_dws(dws_g, cw, sw):
    cdw = jnp.transpose(dws_g[:, 0:CONV_W, 0:cw], (1, 0, 2)).reshape(CONV_W, NS * cw)
    sdw = jnp.transpose(dws_g[:, 32:32 + SHORT_W, 0:sw], (1, 0, 2)).reshape(SHORT_W, NS * sw)
    return cdw, sdw


def _fwd_bwd(x3, target3, shards, small, dw_widths, first, reduce_block):
    bl, S, D = x3.shape
    T = bl * S
    x = x3.reshape(T, D)
    target = target3.reshape(T, D)
    L = len(shards)
    cw, sw = dw_widths
    row = lambda v: v[None, :]
    gather = lambda arrs: _Carried("gather", arrs)
    scatter = lambda arrs: _Carried("scatter", arrs)

    saved = []
    wa1, wb1 = first
    for l in range(L):
        sh = shards[l]
        nxt = shards[l + 1] if l + 1 < L else None
        sp = dict(pwblk=_block_diag(small["pool_w"][l]).astype(BF16), cb=row(small["conv_b"][l]),
                  lg=row(small["conv_ln_g"][l]), lb=row(small["conv_ln_b"][l]), ps=row(small["pool_scale"][l]),
                  g1=row(small["norm_ffn1_g"][l]), gm=row(small["norm_mix_g"][l]), g2=row(small["norm_ffn2_g"][l]))
        x0 = x
        x1, h1, a1, b1, win_g, wp_g, wo_g, dws_g = _ffn_fwd(x0, sp["g1"], wa1, wb1, l, 0, gather(sh["mx"]))
        win_g, wp_g, wo_g = _merge(win_g), _merge(wp_g), _merge(wo_g)
        cdw, sdw = _split_dws(_merge(dws_g), cw, sw)
        sp["cdw"], sp["sdw"] = _pad_rows(cdw, HALO), _pad_rows(sdw, 8)
        hm, u, wa2 = _mix_in(x1, sp["gm"], win_g, l, gather([sh["f2a"]]))
        act, cv, wb2 = _mix_seq_fwd(u, sp["cdw"], sp["cb"], sp["lg"], sp["lb"], sp["sdw"], sp["pwblk"], sp["ps"],
                                    bl, l, gather([sh["f2b"]]))
        wb2 = _merge(wb2)
        res_o = _mix_out(x1, act, u, wp_g, wo_g, l, gather([nxt["f1b"]]) if nxt else _NOTHING)
        x2, y, m = res_o[:3]
        res_f = _ffn_fwd(x2, sp["g2"], wa2, wb2, l, 1, gather([nxt["f1a"]]) if nxt else _NOTHING)
        x, h2, a2, b2 = res_f[:4]
        saved.append(dict(sp=sp, x0=x0, x1=x1, x2=x2, h1=h1, a1=a1, b1=b1, hm=hm, u=u, act=act, cv=cv, y=y, m=m,
                          h2=h2, a2=a2, b2=b2, wa1=wa1, wb1=wb1, wa2=wa2, wb2=wb2, win=win_g, wp=wp_g, wo=wo_g))
        if nxt:
            wa1, wb1 = res_f[4], _merge(res_o[3])

    dx, loss_blk, dgf = _loss_head(x, row(small["final_norm_g"]), target)
    loss = loss_blk[0, 0]

    sg = {k: [None] * L for k in ("norm_ffn1_g", "norm_mix_g", "norm_ffn2_g", "conv_dw", "conv_b", "conv_ln_g",
                                  "conv_ln_b", "short_dw", "pool_w", "pool_scale")}
    blocks = []
    pend, pend_l = [], None
    G, c = small["pool_w"].shape[1:3]
    for l in reversed(range(L)):
        sv = saved[l]
        sp = sv["sp"]
        res = _ffn_bwd(sv["x2"], sp["g2"], dx, sv["a2"], sv["b2"], sv["wa2"], sv["wb2"], l, 1, scatter(pend))
        dx, dg2, da, db, z, dyb = res[:6]
        if pend:
            blocks.append((pend_l, "f1", pend, res[6:]))
        p_f2 = reduce_block([_ffn_dw(sv["h2"], dyb, da, db, z, l, 1)])
        du, dact, g_o, g_p = _mix_out_bwd(dx, sv["y"], sv["u"], sv["act"], sv["m"], sv["wp"], sv["wo"], l)
        res = _mix_seq_bwd(du, sv["u"], dact, sv["cv"], sp["cdw"], sp["lg"], sp["lb"],
                           sp["sdw"], sp["pwblk"], sp["ps"], bl, l, scatter(p_f2))
        du, gcdw, g512, g256, gpw = res[:5]
        blocks.append((l, "f2", p_f2, res[5:]))
        dx, dgm = _mix_in_bwd(sv["x1"], sp["gm"], dx, du, sv["win"], l)
        p_mx = reduce_block([_mix_in_dw(sv["hm"], du, l), g_p, g_o])
        res = _ffn_bwd(sv["x0"], sp["g1"], dx, sv["a1"], sv["b1"], sv["wa1"], sv["wb1"], l, 0, scatter(p_mx))
        dx, dg1, da, db, z, dyb = res[:6]
        blocks.append((l, "mx", p_mx, res[6:]))
        pend, pend_l = reduce_block([_ffn_dw(sv["h1"], dyb, da, db, z, l, 0)]), l
        sg["norm_ffn1_g"][l], sg["norm_mix_g"][l], sg["norm_ffn2_g"][l] = dg1[0], dgm[0], dg2[0]
        sg["conv_dw"][l] = gcdw[:CONV_W]
        sg["conv_b"][l], sg["conv_ln_g"][l], sg["conv_ln_b"][l] = g512[0], g512[1], g512[2]
        sg["short_dw"][l] = g256[:SHORT_W]
        sg["pool_scale"][l] = g256[SHORT_W]
        sg["pool_w"][l] = jnp.stack([gpw[gi * c:(gi + 1) * c, gi * c:(gi + 1) * c] for gi in range(G)])
    if pend:
        blocks.append((pend_l, "f1", pend, _run_carried(scatter(pend), "last")))
    small_g = {k: jnp.stack(v) for k, v in sg.items()}
    small_g["final_norm_g"] = dgf[0]
    return loss, dx.reshape(bl, S, D), blocks, small_g


def _exchange_sibling_halves(gs, tag):
    n = len(gs)

    def body(*refs):
        ins, outs = refs[:n], refs[n:2 * n]
        send_sems, recv_sems = refs[2 * n:]
        x, y, c, _ = _place()
        cps = []
        for ai in range(n):
            cp = pltpu.make_async_remote_copy(src_ref=ins[ai].at[1 - c], dst_ref=outs[ai], send_sem=send_sems.at[ai],
                                              recv_sem=recv_sems.at[ai], device_id=(x, y, 1 - c), device_id_type=MESH)
            cp.start()
            cps.append(cp)
        for cp in cps:
            cp.wait()

    return _pallas(
        body, name=f"grad_exchange_sibling_{tag}",
        in_specs=[ANY] * n, out_specs=[ANY] * n,
        out_shape=[jax.ShapeDtypeStruct(g.shape[1:], g.dtype) for g in gs],
        scratch_shapes=[pltpu.SemaphoreType.DMA((n,)), pltpu.SemaphoreType.DMA((n,))],
        compiler_params=pltpu.CompilerParams(has_side_effects=True),
    )(*gs)


def _share_final(fs):
    n = len(fs)
    L = fs[0].shape[0]

    def body(*refs):
        outs = refs[n:2 * n]
        send_sems, recv_sems = refs[2 * n:]
        x, y, c, _ = _place()
        sib = (x, y, 1 - c)
        cps = []
        for ai in range(n):
            for l in range(L):
                cp = pltpu.make_async_remote_copy(src_ref=outs[ai].at[l, c], dst_ref=outs[ai].at[l, c],
                                                  send_sem=send_sems.at[ai, l], recv_sem=recv_sems.at[ai, l],
                                                  device_id=sib, device_id_type=MESH)
                cp.start()
                cps.append(cp)
        for ai in range(n):
            for l in range(L):
                blk = outs[ai].at[l, 1 - c]
                pltpu.make_async_remote_copy(src_ref=blk, dst_ref=blk, send_sem=send_sems.at[ai, l],
                                             recv_sem=recv_sems.at[ai, l], device_id=sib, device_id_type=MESH).wait_recv()
        for cp in cps:
            cp.wait_send()

    return _pallas(
        body, name="grad_share_final",
        in_specs=[ANY] * n, out_specs=[ANY] * n,
        out_shape=[jax.ShapeDtypeStruct(f.shape, f.dtype) for f in fs],
        scratch_shapes=[pltpu.SemaphoreType.DMA((n, L)), pltpu.SemaphoreType.DMA((n, L))],
        input_output_aliases={i: i for i in range(n)},
        compiler_params=pltpu.CompilerParams(has_side_effects=True),
    )(*fs)


def _all_reduce_small(v):
    R, W = v.shape

    def body(v_ref, out_ref, buf, send_sems, recv_sems):
        x, y, c, _ = _place()
        me = 4 * x + 2 * y + c
        buf[me] = v_ref[...]
        cps = []
        for k in range(1, 8):
            kx, ky, kc = (k >> 2) & 1, (k >> 1) & 1, k & 1
            to = (1 - x if kx else x, 1 - y if ky else y, 1 - c if kc else c)
            cp = pltpu.make_async_remote_copy(src_ref=v_ref, dst_ref=buf.at[me], send_sem=send_sems.at[k - 1],
                                              recv_sem=recv_sems.at[k - 1], device_id=to, device_id_type=MESH)
            cp.start()
            cps.append(cp)
        for cp in cps:
            cp.wait()
        acc = buf[0]
        for d in range(1, 8):
            acc = acc + buf[d]
        out_ref[...] = acc

    return _pallas(
        body, name="all_reduce_small",
        in_specs=[pl.BlockSpec(memory_space=pltpu.VMEM)], out_specs=pl.BlockSpec(memory_space=pltpu.VMEM),
        out_shape=jax.ShapeDtypeStruct((R, W), F32),
        scratch_shapes=[pltpu.VMEM((8, R, W), F32), pltpu.SemaphoreType.DMA((7,)), pltpu.SemaphoreType.DMA((7,))],
        compiler_params=pltpu.CompilerParams(has_side_effects=True, vmem_limit_bytes=VMEM_LIMIT),
    )(v)


def _row_tile(n, w):
    for t in (512, 352, 256, 128, 64, 32, 16):
        if n % t == 0 and t * w * 4 <= 4 * 1024 * 1024:
            return t
    raise ValueError((n, w))


def _sum_sibling(tag, cidx, g, r):
    _, N, W = g.shape
    tr = _row_tile(N, W)

    def body(c_ref, g_ref, r_ref, o_ref):
        del c_ref
        o_ref[...] = (g_ref[...].astype(F32) + r_ref[...].astype(F32)).astype(BF16)

    return _pallas(
        body, name=f"grad_sum_sibling_{tag}",
        grid_spec=pltpu.PrefetchScalarGridSpec(
            num_scalar_prefetch=1, grid=(N // tr,),
            in_specs=[pl.BlockSpec((None, tr, W), lambda i, c: (c[0], i, 0)),
                      pl.BlockSpec((tr, W), lambda i, c: (i, 0))],
            out_specs=pl.BlockSpec((tr, W), lambda i, c: (i, 0))),
        out_shape=jax.ShapeDtypeStruct((N, W), BF16),
        compiler_params=_cp(1),
    )(cidx, g, r)


def _sum_final(tag, idx, p, r2, l, L, prev):
    _, r, W = p.shape
    tr = _row_tile(r, W)

    def body(*refs):
        p_ref, r2_ref = refs[1:3]
        o_ref = refs[-1]
        acc = p_ref[...].astype(F32)
        for k in range(3):
            acc = acc + r2_ref[k].astype(F32)
        o_ref[...] = acc

    in_specs = [pl.BlockSpec((None, tr, W), lambda i, s: (s[1], i, 0)),
                pl.BlockSpec((3, tr, W), lambda i, s: (0, i, 0))]
    args = [idx, p, r2]
    aliases = {}
    if prev is not None:
        in_specs.append(ANY)
        args.append(prev)
        aliases = {3: 0}
    return _pallas(
        body, name=f"grad_sum_final_{tag}",
        grid_spec=pltpu.PrefetchScalarGridSpec(
            num_scalar_prefetch=1, grid=(r // tr,), in_specs=in_specs,
            out_specs=pl.BlockSpec((None, None, tr, W), lambda i, s: (l, s[0], i, 0))),
        out_shape=jax.ShapeDtypeStruct((L, 2, r, W), F32),
        input_output_aliases=aliases,
        compiler_params=_cp(1),
    )(*args)


def _adam_math(w, g, m, v):
    m = ADAM_B1 * m + (1.0 - ADAM_B1) * g
    v = ADAM_B2 * v + (1.0 - ADAM_B2) * (g * g)
    m_hat = m / (1.0 - ADAM_B1 ** ADAM_STEP)
    v_hat = v / (1.0 - ADAM_B2 ** ADAM_STEP)
    delta = -ADAM_LR * (m_hat / (jnp.sqrt(v_hat) + ADAM_EPS) + ADAM_WD * w)
    return delta, m, v


def _adam_big(name, w, m, v, gfull, row0):
    L, r, W = w.shape
    tr = _row_tile(r, W)
    assert row0 % tr == 0
    off = row0 // tr

    def body(w_ref, m_ref, v_ref, g_ref, go_ref, d_ref, mo_ref, vo_ref):
        g = g_ref[...]
        d, mn, vn = _adam_math(w_ref[...], g, m_ref[...], v_ref[...])
        go_ref[...] = g
        d_ref[...] = d
        mo_ref[...] = mn
        vo_ref[...] = vn

    blk = pl.BlockSpec((None, tr, W), lambda l, i: (l, i, 0))
    shp = jax.ShapeDtypeStruct(w.shape, F32)
    return _pallas(
        body, name=f"adam_{name}", grid=(L, r // tr),
        in_specs=[blk, blk, blk, pl.BlockSpec((None, tr, W), lambda l, i: (l, off + i, 0))],
        out_specs=[blk] * 4, out_shape=[shp] * 4,
        compiler_params=_cp(2),
    )(w, m, v, gfull)


def _adam_small(w, g, m, v):
    def body(w_ref, g_ref, m_ref, v_ref, d_ref, mo_ref, vo_ref):
        d, mn, vn = _adam_math(w_ref[...], g_ref[...], m_ref[...], v_ref[...])
        d_ref[...] = d
        mo_ref[...] = mn
        vo_ref[...] = vn

    spec = pl.BlockSpec(memory_space=pltpu.VMEM)
    shp = jax.ShapeDtypeStruct(w.shape, F32)
    return _pallas(body, name="adam_small", in_specs=[spec] * 4, out_specs=[spec] * 3, out_shape=[shp] * 3)(w, g, m, v)


_WEIGHTS = ['norm_ffn1_g', 'ffn1_w_gate', 'ffn1_w_up', 'ffn1_w_down', 'norm_mix_g', 'w_in', 'conv_dw', 'conv_b',
            'conv_ln_g', 'conv_ln_b', 'w_pa', 'short_dw', 'w_pb', 'pool_w', 'pool_scale', 'w_pc', 'w_o',
            'norm_ffn2_g', 'ffn2_w_gate', 'ffn2_w_up', 'ffn2_w_down', 'final_norm_g']
_BIG = ('ffn1_w_gate', 'ffn1_w_up', 'ffn1_w_down', 'w_in', 'w_pa', 'w_pb', 'w_pc', 'w_o',
        'ffn2_w_gate', 'ffn2_w_up', 'ffn2_w_down')
_TRANSPOSED = ('ffn1_w_gate', 'ffn1_w_up', 'ffn2_w_gate', 'ffn2_w_up')
_SMALL = tuple(n for n in _WEIGHTS if n not in _BIG)
_SMALL_REDUCED = ('norm_ffn1_g', 'norm_mix_g', 'conv_b', 'conv_ln_g', 'conv_ln_b', 'pool_w', 'pool_scale',
                  'norm_ffn2_g', 'final_norm_g', 'conv_dw', 'short_dw')


def _pack(arrs, rows_multiple=8):
    flat = jnp.concatenate([a.reshape(-1) for a in arrs])
    n = flat.shape[0]
    per = 128 * rows_multiple
    padded = -(-n // per) * per
    return jnp.pad(flat, (0, padded - n)).reshape(padded // 128, 128)


def _unpack(buf, shapes):
    flat = buf.reshape(-1)
    out, o = [], 0
    for s in shapes:
        k = 1
        for d in s:
            k *= d
        out.append(flat[o:o + k].reshape(s))
        o += k
    return out


def _halves(a):
    return a.reshape(2, a.shape[0] // 2, a.shape[1])


def _step(P, M, V, x, loss_target):
    tr = lambda a: jnp.transpose(a, (0, 2, 1))
    bf = lambda a: a.astype(BF16)
    L = P['w_in'].shape[0]
    cw, sw = P['conv_dw'].shape[-1], P['short_dw'].shape[-1]
    ffa = [jnp.stack([bf(tr(P[f'ffn{f}_w_gate'])), bf(tr(P[f'ffn{f}_w_up']))], axis=1) for f in (1, 2)]
    ffb = [bf(P[f'ffn{f}_w_down']) for f in (1, 2)]
    win = bf(P['w_in'])
    wp = jnp.concatenate([bf(P['w_pa']), bf(P['w_pb']), bf(P['w_pc'])], axis=1)
    wo = bf(P['w_o'])
    dws = jnp.zeros((L, 64, 128), F32)
    dws = dws.at[:, 0:CONV_W, 0:cw].set(P['conv_dw']).at[:, 32:32 + SHORT_W, 0:sw].set(P['short_dw'])
    shards = [dict(f1a=ffa[0][l], f1b=_halves(ffb[0][l]), f2a=ffa[1][l], f2b=_halves(ffb[1][l]),
                   mx=(_halves(win[l]), _halves(wp[l]), _halves(wo[l]), _halves(dws[l]))) for l in range(L)]

    xi, yi, ci = lax.axis_index("x"), lax.axis_index("y"), lax.axis_index("c")
    chip = 2 * xi + yi
    cidx = jnp.stack([ci]).astype(jnp.int32)
    idx = jnp.stack([ci, chip]).astype(jnp.int32)
    count = [0]

    def reduce_block(gs):
        t0 = count[0]
        count[0] += len(gs)
        r1 = _exchange_sibling_halves(gs, t0)
        parts = []
        for k, (g, r) in enumerate(zip(gs, r1)):
            W = g.shape[-1]
            p = _sum_sibling(t0 + k, cidx, g.reshape(2, -1, W), r.reshape(-1, W))
            parts.append(p.reshape(g.shape[1:]))
        return parts

    wa1, wb1 = _run_carried(_Carried("gather", [shards[0]["f1a"], shards[0]["f1b"]]), "first")
    small = {n: P[n] for n in _SMALL if n not in ('conv_dw', 'short_dw')}
    loss, dx, blocks, small_g = _fwd_bwd(x, loss_target, shards, small, (cw, sw), (wa1, _merge(wb1)), reduce_block)

    finals = {}
    for t, (l, name, parts, r2) in enumerate(blocks):
        prev = finals.get(name, [None] * len(parts))
        finals[name] = [_sum_final(f"{t}_{k}", idx, p, r, l, L, pv) for k, (p, r, pv) in enumerate(zip(parts, r2, prev))]
    shared = _share_final(finals["f1"] + finals["f2"] + finals["mx"])
    f_f1, f_f2, f_in, f_p, f_o = [f.reshape(L, -1, f.shape[-1]) for f in shared]

    tot = _all_reduce_small(_pack([small_g[n] for n in _SMALL_REDUCED]))
    tot = dict(zip(_SMALL_REDUCED, _unpack(tot, [small_g[n].shape for n in _SMALL_REDUCED])))
    tot['conv_dw'] = lax.dynamic_slice_in_dim(tot['conv_dw'], chip * cw, cw, axis=2)
    tot['short_dw'] = lax.dynamic_slice_in_dim(tot['short_dw'], chip * sw, sw, axis=2)

    F4 = P['ffn1_w_down'].shape[1]
    dc, ds = P['w_pa'].shape[1], P['w_pb'].shape[1]
    src = {'ffn1_w_gate': (f_f1, 0), 'ffn1_w_up': (f_f1, F4), 'ffn1_w_down': (f_f1, 2 * F4),
           'ffn2_w_gate': (f_f2, 0), 'ffn2_w_up': (f_f2, F4), 'ffn2_w_down': (f_f2, 2 * F4),
           'w_in': (f_in, 0), 'w_pa': (f_p, 0), 'w_pb': (f_p, dc), 'w_pc': (f_p, dc + ds), 'w_o': (f_o, 0)}
    grads, deltas, new_m, new_v = {}, {}, {}, {}
    for n in _BIG:
        gfull, row0 = src[n]
        if n in _TRANSPOSED:
            outs = _adam_big(n, tr(P[n]), tr(M[n]), tr(V[n]), gfull, row0)
            grads[n], deltas[n], new_m[n], new_v[n] = [tr(o) for o in outs]
        else:
            grads[n], deltas[n], new_m[n], new_v[n] = _adam_big(n, P[n], M[n], V[n], gfull, row0)
    shapes = [P[n].shape for n in _SMALL]
    d_s, m_s, v_s = _adam_small(_pack([P[n] for n in _SMALL]), _pack([tot[n] for n in _SMALL]),
                                _pack([M[n] for n in _SMALL]), _pack([V[n] for n in _SMALL]))
    for n, d, mm, vv in zip(_SMALL, _unpack(d_s, shapes), _unpack(m_s, shapes), _unpack(v_s, shapes)):
        grads[n], deltas[n], new_m[n], new_v[n] = tot[n], d, mm, vv

    loss = lax.psum(loss, ("x", "y", "c"))
    return (loss, dx, *[grads[n] for n in _WEIGHTS], *[deltas[n] for n in _WEIGHTS],
            *[new_m[n] for n in _WEIGHTS], *[new_v[n] for n in _WEIGHTS])


def kernel(x, norm_ffn1_g, ffn1_w_gate, ffn1_w_up, ffn1_w_down, norm_mix_g, w_in, conv_dw, conv_b, conv_ln_g, conv_ln_b, w_pa, short_dw, w_pb, pool_w, pool_scale, w_pc, w_o, norm_ffn2_g, ffn2_w_gate, ffn2_w_up, ffn2_w_down, final_norm_g, loss_target, m_norm_ffn1_g, m_ffn1_w_gate, m_ffn1_w_up, m_ffn1_w_down, m_norm_mix_g, m_w_in, m_conv_dw, m_conv_b, m_conv_ln_g, m_conv_ln_b, m_w_pa, m_short_dw, m_w_pb, m_pool_w, m_pool_scale, m_w_pc, m_w_o, m_norm_ffn2_g, m_ffn2_w_gate, m_ffn2_w_up, m_ffn2_w_down, m_final_norm_g, v_norm_ffn1_g, v_ffn1_w_gate, v_ffn1_w_up, v_ffn1_w_down, v_norm_mix_g, v_w_in, v_conv_dw, v_conv_b, v_conv_ln_g, v_conv_ln_b, v_w_pa, v_short_dw, v_w_pb, v_pool_w, v_pool_scale, v_w_pc, v_w_o, v_norm_ffn2_g, v_ffn2_w_gate, v_ffn2_w_up, v_ffn2_w_down, v_final_norm_g):
    args = locals()
    P = {n: args[n] for n in _WEIGHTS}
    M = {n: args["m_" + n] for n in _WEIGHTS}
    V = {n: args["v_" + n] for n in _WEIGHTS}
    return _step(P, M, V, x, loss_target)
```

```python
import jax
import jax.numpy as jnp
from jax import lax
from jax.experimental import pallas as pl
from jax.experimental.pallas import tpu as pltpu

F32 = jnp.float32
BF16 = jnp.bfloat16
EPS = 1e-6
NS = 4
CONV_W = 31
SHORT_W = 3
POOL_WINDOWS = (2, 4, 8, 16)
HALO = 32
ADAM_LR, ADAM_B1, ADAM_B2, ADAM_EPS, ADAM_WD, ADAM_STEP = 0.001, 0.9, 0.999, 1e-08, 0.01, 10
MESH = pl.DeviceIdType.MESH
ANY = pl.BlockSpec(memory_space=pl.ANY)
VMEM_LIMIT = 56 * 1024 * 1024
FFN_BWD_SUBTILES = 2


def _pallas(body, **kw):
    return pl.pallas_call(body, **kw)


def _cp(n_axes):
    return pltpu.CompilerParams(dimension_semantics=("arbitrary",) * n_axes, vmem_limit_bytes=VMEM_LIMIT)


def _nn(a, b):
    return jnp.dot(a, b, preferred_element_type=F32)


def _nt(a, b):
    return lax.dot_general(a, b, (((1,), (1,)), ((), ())), preferred_element_type=F32)


def _tn(a, b):
    return lax.dot_general(a, b, (((0,), (0,)), ((), ())), preferred_element_type=F32)


def _sigmoid(v):
    return 1.0 / (1.0 + jnp.exp(-v))


def _rms_stats(x):
    rs = lax.rsqrt(jnp.mean(x * x, axis=-1, keepdims=True) + EPS)
    return x * rs, rs


def _rms_bwd(dh, x, g):
    xh, rs = _rms_stats(x)
    dhg = dh * g
    dx = rs * (dhg - xh * jnp.mean(dhg * xh, axis=-1, keepdims=True))
    return dx, jnp.sum(dh * xh, axis=0, keepdims=True)


def _tile(n, pref):
    t = min(n, pref)
    assert n % t == 0, (n, t)
    return t


def _place():
    x, y, c = lax.axis_index("x"), lax.axis_index("y"), lax.axis_index("c")
    chips = [(1 - x, y), (x, 1 - y), (1 - x, 1 - y)]
    return x, y, c, chips


def _gather_copies(ins, outs, sems):
    send_sems, recv_sems, local_sems = sems
    x, y, c, chips = _place()
    me = 2 * x + y
    sib = (x, y, 1 - c)

    def remote(ai, k, src, dst, to):
        return pltpu.make_async_remote_copy(src_ref=src, dst_ref=dst, send_sem=send_sems.at[ai, k],
                                            recv_sem=recv_sems.at[ai, k], device_id=to, device_id_type=MESH)

    local, first, landed, passed, from_sib = [], [], [], [], []
    for ai in range(len(ins)):
        local.append(pltpu.make_async_copy(ins[ai], outs[ai].at[me], local_sems.at[ai]))
        for k, chip in enumerate(chips):
            theirs = 2 * chip[0] + chip[1]
            first.append(remote(ai, k, ins[ai].at[c], outs[ai].at[me, c], (*chip, c)))
            blk = outs[ai].at[theirs, c]
            landed.append(remote(ai, k, blk, blk, (*chip, c)))
            passed.append(remote(ai, 3 + k, blk, blk, sib))
            blk2 = outs[ai].at[theirs, 1 - c]
            from_sib.append(remote(ai, 3 + k, blk2, blk2, sib))
    return local, first, landed, passed, from_sib


def _gather_start(ins, outs, sems):
    local, first, _, _, _ = _gather_copies(ins, outs, sems)
    for cp in local + first:
        cp.start()


def _gather_finish(ins, outs, sems):
    local, first, landed, passed, from_sib = _gather_copies(ins, outs, sems)
    for arrive, fwd in zip(landed, passed):
        arrive.wait_recv()
        fwd.start()
    for cp in from_sib:
        cp.wait_recv()
    for cp in first + passed:
        cp.wait_send()
    for cp in local:
        cp.wait()


def _scatter_copies(ins, outs, sems):
    send_sems, recv_sems = sems
    x, y, c, chips = _place()
    return [pltpu.make_async_remote_copy(
        src_ref=ins[ai].at[2 * chip[0] + chip[1]], dst_ref=outs[ai].at[k],
        send_sem=send_sems.at[ai, k], recv_sem=recv_sems.at[ai, k], device_id=(*chip, c), device_id_type=MESH)
        for ai in range(len(ins)) for k, chip in enumerate(chips)]


def _exchange_copies(ins, outs, sems):
    send_sems, recv_sems = sems
    x, y, c, _ = _place()
    return [pltpu.make_async_remote_copy(
        src_ref=ins[ai].at[1 - c], dst_ref=outs[ai], send_sem=send_sems.at[ai], recv_sem=recv_sems.at[ai],
        device_id=(x, y, 1 - c), device_id_type=MESH) for ai in range(len(ins))]


class _Carried:
    def __init__(self, kind="gather", arrs=()):
        self.kind, self.arrs, self.n = kind, tuple(arrs), len(arrs)
        self.specs = [ANY] * self.n
        if kind == "gather":
            self.out_shape = [jax.ShapeDtypeStruct((NS,) + a.shape, a.dtype) for a in self.arrs]
            sems = [(self.n, 6), (self.n, 6), (self.n,)]
        elif kind == "exchange":
            self.out_shape = [jax.ShapeDtypeStruct(a.shape[1:], a.dtype) for a in self.arrs]
            sems = [(self.n,), (self.n,)]
        else:
            self.out_shape = [jax.ShapeDtypeStruct((3,) + a.shape[1:], a.dtype) for a in self.arrs]
            sems = [(self.n, 3), (self.n, 3)]
        self.scratch = [pltpu.SemaphoreType.DMA(s) for s in sems] if self.n else []

    def split(self, refs, n_in, n_out):
        n = self.n
        a, b, c = n_in + n, n_in + n + n_out, n_in + 2 * n + n_out
        n_sem = len(self.scratch)
        own_scr = refs[c:len(refs) - n_sem]
        return refs[:n_in], refs[a:b], own_scr, (refs[n_in:a], refs[b:c], refs[len(refs) - n_sem:])

    def start(self, carried):
        ins, outs, sems = carried
        if self.kind == "gather":
            _gather_start(ins, outs, sems)
        else:
            for cp in (_exchange_copies if self.kind == "exchange" else _scatter_copies)(ins, outs, sems):
                cp.start()

    def finish(self, carried):
        ins, outs, sems = carried
        if self.kind == "gather":
            _gather_finish(ins, outs, sems)
        else:
            for cp in (_exchange_copies if self.kind == "exchange" else _scatter_copies)(ins, outs, sems):
                cp.wait()

    def when(self, cond, carried, what):
        if self.n:
            pl.when(cond)(lambda: what(carried))


_NOTHING = _Carried()


def _run_carried(car, tag):
    def body(*refs):
        _, _, _, carried = car.split(refs, 0, 0)
        car.start(carried)
        car.finish(carried)

    return _pallas(
        body, name=f"{car.kind}_{tag}",
        in_specs=car.specs, out_specs=car.specs, out_shape=car.out_shape, scratch_shapes=car.scratch,
        compiler_params=pltpu.CompilerParams(has_side_effects=True),
    )(*car.arrs)


def _ffn_weight_specs(F4, D):
    return [pl.BlockSpec((None, None, F4, D), lambda i, j: (j, 0, 0, 0)),
            pl.BlockSpec((None, None, F4, D), lambda i, j: (j, 1, 0, 0)),
            pl.BlockSpec((None, F4, D), lambda i, j: (j, 0, 0))]


def _ffn_fwd(x, g, wa, wb, l, f, car=_NOTHING):
    T, D = x.shape
    F4 = wb.shape[1]
    tm = _tile(T, 512)
    ni = T // tm

    def body(*refs):
        (x_ref, g_ref, wg_ref, wu_ref, wd_ref), (xo_ref, h_ref, a_ref, b_ref), (acc,), carried = car.split(refs, 5, 4)
        i = pl.program_id(0)
        j = pl.program_id(1)
        car.when((i == 0) & (j == 0), carried, car.start)

        @pl.when(j == 0)
        def _():
            xh, _ = _rms_stats(x_ref[...])
            h_ref[...] = (xh * g_ref[...]).astype(BF16)
            acc[...] = jnp.zeros_like(acc)

        h = h_ref[...]
        a = _nt(h, wg_ref[...])
        b = _nt(h, wu_ref[...])
        a_ref[...] = a.astype(BF16)
        b_ref[...] = b.astype(BF16)
        z = (a * _sigmoid(a) * b).astype(BF16)
        acc[...] += _nn(z, wd_ref[...])

        @pl.when(j == NS - 1)
        def _():
            xo_ref[...] = x_ref[...] + 0.5 * acc[...]

        car.when((i == ni - 1) & (j == NS - 1), carried, car.finish)

    return _pallas(
        body, name=f"ffn_fwd_{l}_{f}", grid=(ni, NS),
        in_specs=[pl.BlockSpec((tm, D), lambda i, j: (i, 0)), pl.BlockSpec((1, D), lambda i, j: (0, 0))]
        + _ffn_weight_specs(F4, D) + car.specs,
        out_specs=[pl.BlockSpec((tm, D), lambda i, j: (i, 0)),
                   pl.BlockSpec((tm, D), lambda i, j: (i, 0)),
                   pl.BlockSpec((None, tm, F4), lambda i, j: (j, i, 0)),
                   pl.BlockSpec((None, tm, F4), lambda i, j: (j, i, 0))] + car.specs,
        out_shape=[jax.ShapeDtypeStruct((T, D), F32), jax.ShapeDtypeStruct((T, D), BF16),
                   jax.ShapeDtypeStruct((NS, T, F4), BF16), jax.ShapeDtypeStruct((NS, T, F4), BF16)] + car.out_shape,
        scratch_shapes=[pltpu.VMEM((tm, D), F32)] + car.scratch,
        compiler_params=_cp(2),
    )(x, g, wa, wa, wb, *car.arrs)


def _ffn_bwd(x, g, dy, a, b, wa, wb, l, f, car=_NOTHING):
    T, D = x.shape
    F4 = wb.shape[1]
    tm = _tile(T, 512)
    ni = T // tm

    def body(*refs):
        ((x_ref, g_ref, dy_ref, a_ref, b_ref, wg_ref, wu_ref, wd_ref),
         (dx_ref, dg_ref, da_ref, db_ref, z_ref, dyb_ref), (dh,), carried) = car.split(refs, 8, 6)
        i = pl.program_id(0)
        j = pl.program_id(1)
        car.when((i == 0) & (j == 0), carried, car.start)

        @pl.when(j == 0)
        def _():
            dyb_ref[...] = (0.5 * dy_ref[...]).astype(BF16)
            dh[...] = jnp.zeros_like(dh)

        @pl.when((i == 0) & (j == 0))
        def _():
            dg_ref[...] = jnp.zeros_like(dg_ref)

        for r in range(FFN_BWD_SUBTILES):
            rows = slice(r * (tm // FFN_BWD_SUBTILES), (r + 1) * (tm // FFN_BWD_SUBTILES))
            dz = _nt(dyb_ref[rows, :], wd_ref[...])
            av = a_ref[rows, :].astype(F32)
            bv = b_ref[rows, :].astype(F32)
            sg = _sigmoid(av)
            silu = av * sg
            da = (dz * bv * (sg * (1.0 + av * (1.0 - sg)))).astype(BF16)
            db = (dz * silu).astype(BF16)
            da_ref[rows, :] = da
            db_ref[rows, :] = db
            z_ref[rows, :] = (silu * bv).astype(BF16)
            dh[rows, :] += _nn(da, wg_ref[...]) + _nn(db, wu_ref[...])

        @pl.when(j == NS - 1)
        def _():
            dxn, dg = _rms_bwd(dh[...], x_ref[...], g_ref[...])
            dx_ref[...] = dy_ref[...] + dxn
            dg_ref[...] += dg

        car.when((i == ni - 1) & (j == NS - 1), carried, car.finish)

    tok = pl.BlockSpec((tm, D), lambda i, j: (i, 0))
    vec = pl.BlockSpec((1, D), lambda i, j: (0, 0))
    chunk = pl.BlockSpec((None, tm, F4), lambda i, j: (j, i, 0))
    return _pallas(
        body, name=f"ffn_bwd_{l}_{f}", grid=(ni, NS),
        in_specs=[tok, vec, tok, chunk, chunk] + _ffn_weight_specs(F4, D) + car.specs,
        out_specs=[tok, vec, chunk, chunk, chunk, tok] + car.specs,
        out_shape=[jax.ShapeDtypeStruct((T, D), F32), jax.ShapeDtypeStruct((1, D), F32),
                   jax.ShapeDtypeStruct((NS, T, F4), BF16), jax.ShapeDtypeStruct((NS, T, F4), BF16),
                   jax.ShapeDtypeStruct((NS, T, F4), BF16), jax.ShapeDtypeStruct((T, D), BF16)] + car.out_shape,
        scratch_shapes=[pltpu.VMEM((tm, D), F32)] + car.scratch,
        compiler_params=_cp(2),
    )(x, g, dy, a, b, wa, wa, wb, *car.arrs)


def _ffn_bwd_first(dy, a, b, wb, l, f, car=_NOTHING):
    T, D = dy.shape
    F4 = wb.shape[1]
    tm = _tile(T, 512)
    ni = T // tm

    def body(*refs):
        (dy_ref, a_ref, b_ref, wd_ref), (da_ref, db_ref, z_ref, dyb_ref), _, carried = car.split(refs, 4, 4)
        i = pl.program_id(0)
        j = pl.program_id(1)
        car.when((i == 0) & (j == 0), carried, car.start)

        @pl.when(j == 0)
        def _():
            dyb_ref[...] = (0.5 * dy_ref[...]).astype(BF16)

        for r in range(FFN_BWD_SUBTILES):
            rows = slice(r * (tm // FFN_BWD_SUBTILES), (r + 1) * (tm // FFN_BWD_SUBTILES))
            dz = _nt(dyb_ref[rows, :], wd_ref[...])
            av = a_ref[rows, :].astype(F32)
            bv = b_ref[rows, :].astype(F32)
            sg = _sigmoid(av)
            silu = av * sg
            da_ref[rows, :] = (dz * bv * (sg * (1.0 + av * (1.0 - sg)))).astype(BF16)
            db_ref[rows, :] = (dz * silu).astype(BF16)
            z_ref[rows, :] = (silu * bv).astype(BF16)

        car.when((i == ni - 1) & (j == NS - 1), carried, car.finish)

    tok = pl.BlockSpec((tm, D), lambda i, j: (i, 0))
    chunk = pl.BlockSpec((None, tm, F4), lambda i, j: (j, i, 0))
    return _pallas(
        body, name=f"ffn_bwd_first_{l}_{f}", grid=(ni, NS),
        in_specs=[tok, chunk, chunk, _ffn_weight_specs(F4, D)[2]] + car.specs,
        out_specs=[chunk, chunk, chunk, tok] + car.specs,
        out_shape=[jax.ShapeDtypeStruct((NS, T, F4), BF16), jax.ShapeDtypeStruct((NS, T, F4), BF16),
                   jax.ShapeDtypeStruct((NS, T, F4), BF16), jax.ShapeDtypeStruct((T, D), BF16)] + car.out_shape,
        scratch_shapes=car.scratch,
        compiler_params=_cp(2),
    )(dy, a, b, wb, *car.arrs)


def _ffn_bwd_second(x, g, dy, da, db, wa, l, f, car=_NOTHING):
    T, D = x.shape
    F4 = da.shape[-1]
    tm = _tile(T, 512)
    ni = T // tm

    def body(*refs):
        (x_ref, g_ref, dy_ref, da_ref, db_ref, wg_ref, wu_ref), (dx_ref, dg_ref), (dh,), carried = car.split(refs, 7, 2)
        i = pl.program_id(0)
        j = pl.program_id(1)
        car.when((i == 0) & (j == 0), carried, car.start)

        @pl.when(j == 0)
        def _():
            dh[...] = jnp.zeros_like(dh)

        @pl.when((i == 0) & (j == 0))
        def _():
            dg_ref[...] = jnp.zeros_like(dg_ref)

        dh[...] += _nn(da_ref[...], wg_ref[...]) + _nn(db_ref[...], wu_ref[...])

        @pl.when(j == NS - 1)
        def _():
            dxn, dg = _rms_bwd(dh[...], x_ref[...], g_ref[...])
            dx_ref[...] = dy_ref[...] + dxn
            dg_ref[...] += dg

        car.when((i == ni - 1) & (j == NS - 1), carried, car.finish)

    tok = pl.BlockSpec((tm, D), lambda i, j: (i, 0))
    vec = pl.BlockSpec((1, D), lambda i, j: (0, 0))
    chunk = pl.BlockSpec((None, tm, F4), lambda i, j: (j, i, 0))
    return _pallas(
        body, name=f"ffn_bwd_second_{l}_{f}", grid=(ni, NS),
        in_specs=[tok, vec, tok, chunk, chunk] + _ffn_weight_specs(F4, D)[:2] + car.specs,
        out_specs=[tok, vec] + car.specs,
        out_shape=[jax.ShapeDtypeStruct((T, D), F32), jax.ShapeDtypeStruct((1, D), F32)] + car.out_shape,
        scratch_shapes=[pltpu.VMEM((tm, D), F32)] + car.scratch,
        compiler_params=_cp(2),
    )(x, g, dy, da, db, wa, wa, *car.arrs)


def _ffn_dw(h, dyb, da, db, z, l, f, car=_NOTHING):
    T, D = h.shape
    F4 = da.shape[-1]
    tk = _tile(T, 512)
    nt = T // tk
    R2 = 3 * F4 // 2

    def body(*refs):
        (h_ref, dyb_ref, da_ref, db_ref, z_ref), (g_ref,), (accg, accu, accd), carried = car.split(refs, 5, 1)
        t = pl.program_id(1)
        car.when((pl.program_id(0) == 0) & (t == 0), carried, car.start)

        @pl.when(t == 0)
        def _():
            accg[...] = jnp.zeros_like(accg)
            accu[...] = jnp.zeros_like(accu)
            accd[...] = jnp.zeros_like(accd)

        hv = h_ref[...]
        accg[...] += _tn(da_ref[...], hv)
        accu[...] += _tn(db_ref[...], hv)
        accd[...] += _tn(z_ref[...], dyb_ref[...])

        @pl.when(t == nt - 1)
        def _():
            g_ref[0, 0:F4, :] = accg[...].astype(BF16)
            g_ref[0, F4:R2, :] = accu[0:R2 - F4, :].astype(BF16)
            g_ref[1, 0:2 * F4 - R2, :] = accu[R2 - F4:F4, :].astype(BF16)
            g_ref[1, 2 * F4 - R2:R2, :] = accd[...].astype(BF16)

        car.when((pl.program_id(0) == NS - 1) & (t == nt - 1), carried, car.finish)

    tok = pl.BlockSpec((tk, D), lambda s, t: (t, 0))
    chunk = pl.BlockSpec((None, tk, F4), lambda s, t: (s, t, 0))
    return _pallas(
        body, name=f"ffn_dw_{l}_{f}", grid=(NS, nt),
        in_specs=[tok, tok, chunk, chunk, chunk] + car.specs,
        out_specs=[pl.BlockSpec((2, None, R2, D), lambda s, t: (0, s, 0, 0))] + car.specs,
        out_shape=[jax.ShapeDtypeStruct((2, NS, R2, D), BF16)] + car.out_shape,
        scratch_shapes=[pltpu.VMEM((F4, D), F32), pltpu.VMEM((F4, D), F32), pltpu.VMEM((F4, D), F32)] + car.scratch,
        compiler_params=_cp(2),
    )(h, dyb, da, db, z, *car.arrs)


def _mix_in(x, g, win, l, car=_NOTHING):
    T, D = x.shape
    C4 = win.shape[-1]
    tm = _tile(T, 512)
    ni = T // tm

    def body(*refs):
        (x_ref, g_ref, w_ref), (h_ref, u_ref), _, carried = car.split(refs, 3, 2)
        i = pl.program_id(0)
        j = pl.program_id(1)
        car.when((i == 0) & (j == 0), carried, car.start)

        @pl.when(j == 0)
        def _():
            xh, _ = _rms_stats(x_ref[...])
            h_ref[...] = (xh * g_ref[...]).astype(BF16)

        u_ref[...] = _nn(h_ref[...], w_ref[...]).astype(BF16)
        car.when((i == ni - 1) & (j == NS - 1), carried, car.finish)

    return _pallas(
        body, name=f"mix_in_{l}", grid=(ni, NS),
        in_specs=[pl.BlockSpec((tm, D), lambda i, j: (i, 0)), pl.BlockSpec((1, D), lambda i, j: (0, 0)),
                  pl.BlockSpec((None, D, C4), lambda i, j: (j, 0, 0))] + car.specs,
        out_specs=[pl.BlockSpec((tm, D), lambda i, j: (i, 0)), pl.BlockSpec((tm, C4), lambda i, j: (i, j))] + car.specs,
        out_shape=[jax.ShapeDtypeStruct((T, D), BF16), jax.ShapeDtypeStruct((T, NS * C4), BF16)] + car.out_shape,
        scratch_shapes=car.scratch,
        compiler_params=_cp(2),
    )(x, g, win, *car.arrs)


def _pool_lane_window(n):
    lane = lax.broadcasted_iota(jnp.int32, (1, n), 1) // (n // len(POOL_WINDOWS))
    w = jnp.full((1, n), float(POOL_WINDOWS[-1]), F32)
    for gi in range(len(POOL_WINDOWS) - 1):
        w = jnp.where(lane == gi, float(POOL_WINDOWS[gi]), w)
    return lane, w


def _pool_select(lane, sums):
    out = sums[-1]
    for gi in range(len(sums) - 1):
        out = jnp.where(lane == gi, sums[gi], out)
    return out


def _back(v, s):
    return v if s == 0 else pltpu.roll(v, s, 0)


def _fwd_shift(v, s):
    return v if s == 0 else pltpu.roll(v, v.shape[0] - s, 0)


def _mix_seq_fwd(u, cdw, cb, lg, lb, sdw, pwblk, ps, bl, l, car=_NOTHING):
    T = u.shape[0]
    S = T // bl
    ts = _tile(S, 256)
    nt = S // ts
    DC, DS = cdw.shape[-1], sdw.shape[-1]
    o_ag, o_bg, o_cg, o_bx, o_p, o_end = DC, 2 * DC, 2 * DC + DS, 2 * DC + 2 * DS, 2 * DC + 3 * DS, 2 * DC + 4 * DS

    def body(*refs):
        ((up_ref, uc_ref, cdw_ref, cb_ref, lg_ref, lb_ref, sdw_ref, pw_ref, ps_ref),
         (act_ref, cv_ref), _, carried) = car.split(refs, 9, 2)
        i = pl.program_id(1)
        car.when((pl.program_id(0) == 0) & (i == 0), carried, car.start)
        keep = jnp.where(i > 0, 1.0, 0.0).astype(F32)

        def ext(lo, hi):
            p = up_ref[ts - HALO:ts, lo:hi].astype(F32) * keep
            return jnp.concatenate([p, uc_ref[:, lo:hi].astype(F32)], axis=0)

        glu = ext(0, o_ag) * _sigmoid(ext(o_ag, o_bg))
        cv = jnp.zeros((ts, DC), F32) + cb_ref[...]
        for s in range(CONV_W):
            cv = cv + _back(glu, s)[HALO:, :] * cdw_ref[CONV_W - 1 - s:CONV_W - s, :]
        cv_ref[...] = cv
        mu = jnp.mean(cv, axis=-1, keepdims=True)
        xc = cv - mu
        lnv = xc * lax.rsqrt(jnp.mean(xc * xc, axis=-1, keepdims=True) + EPS) * lg_ref[...] + lb_ref[...]
        act_ref[:, 0:DC] = (lnv * _sigmoid(lnv)).astype(BF16)

        q = ext(o_cg, o_bx) * ext(o_bx, o_p)
        sc = jnp.zeros((ts, DS), F32)
        for s in range(SHORT_W):
            sc = sc + _back(q, s)[HALO:, :] * sdw_ref[SHORT_W - 1 - s:SHORT_W - s, :]
        act_ref[:, DC:DC + DS] = (uc_ref[:, o_bg:o_cg].astype(F32) * sc).astype(BF16)

        p = ext(o_p, o_end)
        lane, wl = _pool_lane_window(DS)
        sums, cur, sh = [], p, 1
        for _ in POOL_WINDOWS:
            cur = cur + _back(cur, sh)
            sums.append(cur[HALO:, :])
            sh *= 2
        pos = (i * ts + lax.broadcasted_iota(jnp.int32, (ts, 1), 0) + 1).astype(F32)
        pooled = _pool_select(lane, sums) / jnp.minimum(pos, wl) - p[HALO:, :]
        act_ref[:, DC + DS:DC + 2 * DS] = (_nn(pooled.astype(BF16), pw_ref[...]) * ps_ref[...]).astype(BF16)
        car.when((pl.program_id(0) == bl - 1) & (i == nt - 1), carried, car.finish)

    ucol = 2 * DC + 4 * DS
    full = lambda a: pl.BlockSpec(a.shape, lambda b, i: (0,) * a.ndim)
    return _pallas(
        body, name=f"mix_seq_fwd_{l}", grid=(bl, nt),
        in_specs=[pl.BlockSpec((ts, ucol), lambda b, i: (b * nt + jnp.maximum(i - 1, 0), 0)),
                  pl.BlockSpec((ts, ucol), lambda b, i: (b * nt + i, 0)),
                  full(cdw), full(cb), full(lg), full(lb), full(sdw), full(pwblk), full(ps)] + car.specs,
        out_specs=[pl.BlockSpec((ts, DC + 2 * DS), lambda b, i: (b * nt + i, 0)),
                   pl.BlockSpec((ts, DC), lambda b, i: (b * nt + i, 0))] + car.specs,
        out_shape=[jax.ShapeDtypeStruct((T, DC + 2 * DS), BF16), jax.ShapeDtypeStruct((T, DC), F32)] + car.out_shape,
        scratch_shapes=car.scratch,
        compiler_params=_cp(2),
    )(u, u, cdw, cb, lg, lb, sdw, pwblk, ps, *car.arrs)


def _mix_out(x, act, u, wp, wo, l, car=_NOTHING):
    T, D = x.shape
    tm = _tile(T, 512)
    ni = T // tm
    DA = act.shape[-1]
    DC, DS = DA // 2, DA // 4
    NB = D // NS
    gcol = (2 * DC + 4 * DS) // D

    def body(*refs):
        ((x_ref, act_ref, g0_ref, g1_ref, g2_ref, wp_ref, wo_ref),
         (xo_ref, y_ref, m_ref), _, carried) = car.split(refs, 7, 3)
        car.when(pl.program_id(0) == 0, carried, car.start)
        parts = [(0, DC), (DC, DC + DS), (DC + DS, DC + 2 * DS)]
        m = jnp.zeros((tm, D), F32)
        for k, (lo, hi) in enumerate(parts):
            av = act_ref[:, lo:hi]
            y = jnp.concatenate([_nn(av, wp_ref[s, lo:hi, :]) for s in range(NS)], axis=1)
            y_ref[:, k * D:(k + 1) * D] = y.astype(BF16)
            gl = (g0_ref, g1_ref, g2_ref)[k][...].astype(F32)
            m = m + _sigmoid(gl) * y
        mb = m.astype(BF16)
        m_ref[...] = mb
        out = _nn(mb[:, 0:NB], wo_ref[0])
        for s in range(1, NS):
            out = out + _nn(mb[:, s * NB:(s + 1) * NB], wo_ref[s])
        xo_ref[...] = x_ref[...] + out
        car.when(pl.program_id(0) == ni - 1, carried, car.finish)

    tok = lambda w: pl.BlockSpec((tm, w), lambda i: (i, 0))
    return _pallas(
        body, name=f"mix_out_{l}", grid=(ni,),
        in_specs=[tok(D), tok(DA),
                  pl.BlockSpec((tm, D), lambda i: (i, gcol)), pl.BlockSpec((tm, D), lambda i: (i, gcol + 1)),
                  pl.BlockSpec((tm, D), lambda i: (i, gcol + 2)),
                  pl.BlockSpec((NS, DA, NB), lambda i: (0, 0, 0)),
                  pl.BlockSpec((NS, NB, D), lambda i: (0, 0, 0))] + car.specs,
        out_specs=[tok(D), tok(3 * D), tok(D)] + car.specs,
        out_shape=[jax.ShapeDtypeStruct((T, D), F32), jax.ShapeDtypeStruct((T, 3 * D), BF16),
                   jax.ShapeDtypeStruct((T, D), BF16)] + car.out_shape,
        scratch_shapes=car.scratch,
        compiler_params=_cp(1),
    )(x, act, u, u, u, wp, wo, *car.arrs)


def _mix_out_bwd(dxn, y, u, act, m, wp, wo, l, car=_NOTHING):
    T, D = dxn.shape
    tm = _tile(T, 256)
    nt = T // tm
    DA = act.shape[-1]
    DC, DS = DA // 2, DA // 4
    NB = D // NS
    UC = u.shape[-1]
    g_lo = 2 * DC + 4 * DS
    gcol = g_lo // D
    parts = [(0, DC), (DC, DC + DS), (DC + DS, DC + 2 * DS)]

    def body(*refs):
        ((dx_ref, y_ref, g0_ref, g1_ref, g2_ref, act_ref, m_ref, wp_ref, wo_ref),
         (du_ref, dact_ref, gwo_ref, gwp_ref), (acc_wo, acc_wp), carried) = car.split(refs, 9, 4)
        i = pl.program_id(0)
        car.when(i == 0, carried, car.start)

        @pl.when(i == 0)
        def _():
            acc_wo[...] = jnp.zeros_like(acc_wo)
            acc_wp[...] = jnp.zeros_like(acc_wp)

        dxb = dx_ref[...].astype(BF16)
        dm = jnp.concatenate([_nt(dxb, wo_ref[s]) for s in range(NS)], axis=1)
        acc_wo[...] += _tn(m_ref[...], dxb)
        du_ref[:, 0:g_lo] = jnp.zeros((tm, g_lo), BF16)
        for k, (lo, hi) in enumerate(parts):
            sg = _sigmoid((g0_ref, g1_ref, g2_ref)[k][...].astype(F32))
            yk = y_ref[:, k * D:(k + 1) * D].astype(F32)
            du_ref[:, g_lo + k * D:g_lo + (k + 1) * D] = (dm * yk * sg * (1.0 - sg)).astype(BF16)
            dyk = (dm * sg).astype(BF16)
            dk = _nt(dyk[:, 0:NB], wp_ref[0, lo:hi, :])
            for s in range(1, NS):
                dk = dk + _nt(dyk[:, s * NB:(s + 1) * NB], wp_ref[s, lo:hi, :])
            dact_ref[:, lo:hi] = dk
            acc_wp[lo:hi, :] += _tn(act_ref[:, lo:hi], dyk)

        @pl.when(i == nt - 1)
        def _():
            for s in range(NS):
                for hf in range(2):
                    r0 = s * NB + hf * (NB // 2)
                    gwo_ref[hf, s] = acc_wo[r0:r0 + NB // 2, :].astype(BF16)
                    gwp_ref[hf, s] = acc_wp[hf * (DA // 2):(hf + 1) * (DA // 2), s * NB:(s + 1) * NB].astype(BF16)

        car.when(i == nt - 1, carried, car.finish)

    tok = lambda w: pl.BlockSpec((tm, w), lambda i: (i, 0))
    whole = lambda shp: pl.BlockSpec(shp, lambda i: (0,) * len(shp))
    return _pallas(
        body, name=f"mix_out_bwd_{l}", grid=(nt,),
        in_specs=[tok(D), tok(3 * D),
                  pl.BlockSpec((tm, D), lambda i: (i, gcol)), pl.BlockSpec((tm, D), lambda i: (i, gcol + 1)),
                  pl.BlockSpec((tm, D), lambda i: (i, gcol + 2)),
                  tok(DA), tok(D), whole((NS, DA, NB)), whole((NS, NB, D))] + car.specs,
        out_specs=[tok(UC), tok(DA), whole((2, NS, NB // 2, D)), whole((2, NS, DA // 2, NB))] + car.specs,
        out_shape=[jax.ShapeDtypeStruct((T, UC), BF16), jax.ShapeDtypeStruct((T, DA), F32),
                   jax.ShapeDtypeStruct((2, NS, NB // 2, D), BF16),
                   jax.ShapeDtypeStruct((2, NS, DA // 2, NB), BF16)] + car.out_shape,
        scratch_shapes=[pltpu.VMEM((D, D), F32), pltpu.VMEM((DA, D), F32)] + car.scratch,
        compiler_params=_cp(1),
    )(dxn, y, u, u, u, act, m, wp, wo, *car.arrs)


def _mix_seq_bwd(du, u, dact, cv, cdw, lg, lb, sdw, pwblk, ps, bl, l, car=_NOTHING):
    T = u.shape[0]
    S = T // bl
    ts = _tile(S, 256)
    nt = S // ts
    DC, DS = cdw.shape[-1], sdw.shape[-1]
    DA = DC + 2 * DS
    o_ag, o_bg, o_cg, o_bx, o_p, o_end = DC, 2 * DC, 2 * DC + DS, 2 * DC + 2 * DS, 2 * DC + 3 * DS, 2 * DC + 4 * DS
    n_f = ts + HALO

    def body(*refs):
        ((_, up_ref, uc_ref, un_ref, dac_ref, dan_ref, cvc_ref, cvn_ref,
          cdw_ref, lg_ref, lb_ref, sdw_ref, pw_ref, ps_ref),
         (du_ref, gcdw_ref, g512_ref, g256_ref, gpw_ref), _, carried) = car.split(refs, 14, 5)
        b = pl.program_id(0)
        i = pl.program_id(1)
        car.when((b == 0) & (i == 0), carried, car.start)
        keep_p = jnp.where(i > 0, 1.0, 0.0).astype(F32)
        keep_n = jnp.where(i < nt - 1, 1.0, 0.0).astype(F32)

        @pl.when((b == 0) & (i == 0))
        def _():
            gcdw_ref[...] = jnp.zeros_like(gcdw_ref)
            g512_ref[...] = jnp.zeros_like(g512_ref)
            g256_ref[...] = jnp.zeros_like(g256_ref)
            gpw_ref[...] = jnp.zeros_like(gpw_ref)

        def back(lo, hi):
            p = up_ref[ts - HALO:ts, lo:hi].astype(F32) * keep_p
            return jnp.concatenate([p, uc_ref[:, lo:hi].astype(F32)], axis=0)

        def fwd(cur, nxt, lo, hi, mask):
            n = nxt[0:HALO, lo:hi].astype(F32)
            if mask:
                n = n * keep_n
            return jnp.concatenate([cur[:, lo:hi].astype(F32), n], axis=0)

        cvx = fwd(cvc_ref, cvn_ref, 0, DC, False)
        dA = fwd(dac_ref, dan_ref, 0, DC, True)
        mu = jnp.mean(cvx, axis=-1, keepdims=True)
        xc = cvx - mu
        rs = lax.rsqrt(jnp.mean(xc * xc, axis=-1, keepdims=True) + EPS)
        xh = xc * rs
        lnv = xh * lg_ref[...] + lb_ref[...]
        sg = _sigmoid(lnv)
        dln = dA * (sg * (1.0 + lnv * (1.0 - sg)))
        dxh = dln * lg_ref[...]
        dcv = rs * (dxh - jnp.mean(dxh, axis=-1, keepdims=True) - xh * jnp.mean(dxh * xh, axis=-1, keepdims=True))
        g512_ref[0:1, :] += jnp.sum(dcv[0:ts], axis=0, keepdims=True)
        g512_ref[1:2, :] += jnp.sum((dln * xh)[0:ts], axis=0, keepdims=True)
        g512_ref[2:3, :] += jnp.sum(dln[0:ts], axis=0, keepdims=True)

        av = back(0, o_ag)
        sga = _sigmoid(back(o_ag, o_bg))
        glu = av * sga
        dcv_c = dcv[0:ts]
        dglu = jnp.zeros((ts, DC), F32)
        for s in range(CONV_W):
            k = CONV_W - 1 - s
            dglu = dglu + _fwd_shift(dcv, s)[0:ts, :] * cdw_ref[k:k + 1, :]
            gcdw_ref[k:k + 1, :] += jnp.sum(_back(glu, s)[HALO:, :] * dcv_c, axis=0, keepdims=True)
        sga_c = sga[HALO:, :]
        du_ref[:, 0:o_ag] = (dglu * sga_c).astype(BF16)
        du_ref[:, o_ag:o_bg] = (dglu * av[HALO:, :] * sga_c * (1.0 - sga_c)).astype(BF16)

        cg = back(o_cg, o_bx)
        bx = back(o_bx, o_p)
        q = cg * bx
        sc = jnp.zeros((ts, DS), F32)
        for s in range(SHORT_W):
            sc = sc + _back(q, s)[HALO:, :] * sdw_ref[SHORT_W - 1 - s:SHORT_W - s, :]
        dB = fwd(dac_ref, dan_ref, DC, DC + DS, True)
        ds = dB * fwd(uc_ref, un_ref, o_bg, o_cg, False)
        du_ref[:, o_bg:o_cg] = (dB[0:ts] * sc).astype(BF16)
        ds_c = ds[0:ts]
        dq = jnp.zeros((ts, DS), F32)
        for s in range(SHORT_W):
            k = SHORT_W - 1 - s
            dq = dq + _fwd_shift(ds, s)[0:ts, :] * sdw_ref[k:k + 1, :]
            g256_ref[k:k + 1, :] += jnp.sum(_back(q, s)[HALO:, :] * ds_c, axis=0, keepdims=True)
        du_ref[:, o_cg:o_bx] = (dq * bx[HALO:, :]).astype(BF16)
        du_ref[:, o_bx:o_p] = (dq * cg[HALO:, :]).astype(BF16)

        p = back(o_p, o_end)
        lane, wl = _pool_lane_window(DS)
        sums, cur, sh = [], p, 1
        for _ in POOL_WINDOWS:
            cur = cur + _back(cur, sh)
            sums.append(cur[HALO:, :])
            sh *= 2
        pos_c = (i * ts + lax.broadcasted_iota(jnp.int32, (ts, 1), 0) + 1).astype(F32)
        pooled = (_pool_select(lane, sums) / jnp.minimum(pos_c, wl) - p[HALO:, :]).astype(BF16)
        pwv = _nn(pooled, pw_ref[...])
        dC = fwd(dac_ref, dan_ref, DC + DS, DA, True)
        g256_ref[SHORT_W:SHORT_W + 1, :] += jnp.sum(dC[0:ts] * pwv, axis=0, keepdims=True)
        dpw = (dC * ps_ref[...]).astype(BF16)
        gpw_ref[...] += _tn(pooled, dpw[0:ts])
        dpl = _nt(dpw, pw_ref[...])
        pos_f = (i * ts + lax.broadcasted_iota(jnp.int32, (n_f, 1), 0) + 1).astype(F32)
        e = dpl / jnp.minimum(pos_f, wl)
        fsums, cur, sh = [], e, 1
        for _ in POOL_WINDOWS:
            cur = cur + _fwd_shift(cur, sh)
            fsums.append(cur[0:ts, :])
            sh *= 2
        du_ref[:, o_p:o_end] = (_pool_select(lane, fsums) - dpl[0:ts]).astype(BF16)
        car.when((b == bl - 1) & (i == nt - 1), carried, car.finish)

    full = lambda a: pl.BlockSpec(a.shape, lambda b, i: (0,) * a.ndim)
    row = lambda w, f: pl.BlockSpec((ts, w), lambda b, i: (b * nt + f(i), 0))
    prv = lambda i: jnp.maximum(i - 1, 0)
    nxt = lambda i: jnp.minimum(i + 1, nt - 1)
    cur = lambda i: i
    return _pallas(
        body, name=f"mix_seq_bwd_{l}", grid=(bl, nt),
        in_specs=[ANY, row(o_end, prv), row(o_end, cur), row(o_end, nxt),
                  row(DA, cur), row(DA, nxt), row(DC, cur), row(DC, nxt),
                  full(cdw), full(lg), full(lb), full(sdw), full(pwblk), full(ps)] + car.specs,
        out_specs=[row(o_end, cur), full(cdw),
                   pl.BlockSpec((8, DC), lambda b, i: (0, 0)), pl.BlockSpec((8, DS), lambda b, i: (0, 0)),
                   full(pwblk)] + car.specs,
        out_shape=[jax.ShapeDtypeStruct(du.shape, BF16), jax.ShapeDtypeStruct(cdw.shape, F32),
                   jax.ShapeDtypeStruct((8, DC), F32), jax.ShapeDtypeStruct((8, DS), F32),
                   jax.ShapeDtypeStruct(pwblk.shape, F32)] + car.out_shape,
        scratch_shapes=car.scratch,
        input_output_aliases={0: 0},
        compiler_params=_cp(2),
    )(du, u, u, u, dact, dact, cv, cv, cdw, lg, lb, sdw, pwblk, ps, *car.arrs)


def _mix_in_bwd(x, g, dxn, du, win, l):
    T, D = x.shape
    C4 = win.shape[-1]
    tm = _tile(T, 512)

    def body(x_ref, g_ref, dxn_ref, du_ref, w_ref, dx_ref, dg_ref, dh):
        i = pl.program_id(0)
        j = pl.program_id(1)

        @pl.when(j == 0)
        def _():
            dh[...] = jnp.zeros_like(dh)

        @pl.when((i == 0) & (j == 0))
        def _():
            dg_ref[...] = jnp.zeros_like(dg_ref)

        dh[...] += _nt(du_ref[...], w_ref[...])

        @pl.when(j == NS - 1)
        def _():
            dxr, dg = _rms_bwd(dh[...], x_ref[...], g_ref[...])
            dx_ref[...] = dxn_ref[...] + dxr
            dg_ref[...] += dg

    tok = pl.BlockSpec((tm, D), lambda i, j: (i, 0))
    vec = pl.BlockSpec((1, D), lambda i, j: (0, 0))
    return _pallas(
        body, name=f"mix_in_bwd_{l}", grid=(T // tm, NS),
        in_specs=[tok, vec, tok, pl.BlockSpec((tm, C4), lambda i, j: (i, j)),
                  pl.BlockSpec((None, D, C4), lambda i, j: (j, 0, 0))],
        out_specs=[tok, vec],
        out_shape=[jax.ShapeDtypeStruct((T, D), F32), jax.ShapeDtypeStruct((1, D), F32)],
        scratch_shapes=[pltpu.VMEM((tm, D), F32)],
        compiler_params=_cp(2),
    )(x, g, dxn, du, win)


def _mix_in_dw(h, du, l):
    T, D = h.shape
    C4 = du.shape[-1] // NS
    tk = _tile(T, 512)
    nt = T // tk

    def body(h_ref, du_ref, g_ref, acc):
        t = pl.program_id(1)

        @pl.when(t == 0)
        def _():
            acc[...] = jnp.zeros_like(acc)

        acc[...] += _tn(h_ref[...], du_ref[...])

        @pl.when(t == nt - 1)
        def _():
            g_ref[0] = acc[0:D // 2, :].astype(BF16)
            g_ref[1] = acc[D // 2:D, :].astype(BF16)

    return _pallas(
        body, name=f"mix_in_dw_{l}", grid=(NS, nt),
        in_specs=[pl.BlockSpec((tk, D), lambda s, t: (t, 0)), pl.BlockSpec((tk, C4), lambda s, t: (t, s))],
        out_specs=pl.BlockSpec((2, None, D // 2, C4), lambda s, t: (0, s, 0, 0)),
        out_shape=jax.ShapeDtypeStruct((2, NS, D // 2, C4), BF16),
        scratch_shapes=[pltpu.VMEM((D, C4), F32)],
        compiler_params=_cp(2),
    )(h, du)


def _loss_head(x, g, target):
    T, D = x.shape
    tm = _tile(T, 512)

    def body(x_ref, g_ref, t_ref, dx_ref, loss_ref, dg_ref):
        @pl.when(pl.program_id(0) == 0)
        def _():
            loss_ref[...] = jnp.zeros_like(loss_ref)
            dg_ref[...] = jnp.zeros_like(dg_ref)

        xv = x_ref[...]
        xh, rs = _rms_stats(xv)
        gv = g_ref[...]
        e = xh * gv - t_ref[...]
        loss_ref[...] += 0.5 * jnp.sum(jnp.mean(e * e, axis=-1, keepdims=True))
        dy = e * (1.0 / D)
        dyg = dy * gv
        dx_ref[...] = rs * (dyg - xh * jnp.mean(dyg * xh, axis=-1, keepdims=True))
        dg_ref[...] += jnp.sum(dy * xh, axis=0, keepdims=True)

    tok = pl.BlockSpec((tm, D), lambda i: (i, 0))
    vec = pl.BlockSpec((1, D), lambda i: (0, 0))
    return _pallas(
        body, name="loss_head", grid=(T // tm,),
        in_specs=[tok, vec, tok],
        out_specs=[tok, pl.BlockSpec((8, 128), lambda i: (0, 0)), vec],
        out_shape=[jax.ShapeDtypeStruct((T, D), F32), jax.ShapeDtypeStruct((8, 128), F32),
                   jax.ShapeDtypeStruct((1, D), F32)],
        compiler_params=_cp(1),
    )(x, g, target)


def _block_diag(pw):
    G, c, _ = pw.shape
    out = jnp.zeros((G * c, G * c), pw.dtype)
    for gi in range(G):
        out = lax.dynamic_update_slice(out, pw[gi], (gi * c, gi * c))
    return out


def _pad_rows(a, n):
    return jnp.pad(a, ((0, n - a.shape[0]), (0, 0)))


def _merge(g):
    return g.reshape(g.shape[0], g.shape[1] * g.shape[2], g.shape[3])


def _split_dws(dws_g, cw, sw):
    cdw = jnp.transpose(dws_g[:, 0:CONV_W, 0:cw], (1, 0, 2)).reshape(CONV_W, NS * cw)
    sdw = jnp.transpose(dws_g[:, 32:32 + SHORT_W, 0:sw], (1, 0, 2)).reshape(SHORT_W, NS * sw)
    return cdw, sdw


def _fwd_bwd(x3, target3, shards, small, dw_widths, first, sum_block):
    bl, S, D = x3.shape
    T = bl * S
    x = x3.reshape(T, D)
    target = target3.reshape(T, D)
    L = len(shards)
    cw, sw = dw_widths
    row = lambda v: v[None, :]
    gather = lambda arrs: _Carried("gather", arrs)
    exchange = lambda arrs: _Carried("exchange", arrs)
    scatter = lambda arrs: _Carried("scatter", arrs)

    saved = []
    wa1, wb1 = first
    for l in range(L):
        sh = shards[l]
        nxt = shards[l + 1] if l + 1 < L else None
        sp = dict(pwblk=_block_diag(small["pool_w"][l]).astype(BF16), cb=row(small["conv_b"][l]),
                  lg=row(small["conv_ln_g"][l]), lb=row(small["conv_ln_b"][l]), ps=row(small["pool_scale"][l]),
                  g1=row(small["norm_ffn1_g"][l]), gm=row(small["norm_mix_g"][l]), g2=row(small["norm_ffn2_g"][l]))
        x0 = x
        x1, h1, a1, b1, win_g, wp_g, wo_g, dws_g = _ffn_fwd(x0, sp["g1"], wa1, wb1, l, 0, gather(sh["mx"]))
        win_g, wp_g, wo_g = _merge(win_g), _merge(wp_g), _merge(wo_g)
        cdw, sdw = _split_dws(_merge(dws_g), cw, sw)
        sp["cdw"], sp["sdw"] = _pad_rows(cdw, HALO), _pad_rows(sdw, 8)
        hm, u, wa2 = _mix_in(x1, sp["gm"], win_g, l, gather([sh["f2a"]]))
        act, cv, wb2 = _mix_seq_fwd(u, sp["cdw"], sp["cb"], sp["lg"], sp["lb"], sp["sdw"], sp["pwblk"], sp["ps"],
                                    bl, l, gather([sh["f2b"]]))
        wb2 = _merge(wb2)
        res_o = _mix_out(x1, act, u, wp_g, wo_g, l, gather([nxt["f1b"]]) if nxt else _NOTHING)
        x2, y, m = res_o[:3]
        res_f = _ffn_fwd(x2, sp["g2"], wa2, wb2, l, 1, gather([nxt["f1a"]]) if nxt else _NOTHING)
        x, h2, a2, b2 = res_f[:4]
        saved.append(dict(sp=sp, x0=x0, x1=x1, x2=x2, h1=h1, a1=a1, b1=b1, hm=hm, u=u, act=act, cv=cv, y=y, m=m,
                          h2=h2, a2=a2, b2=b2, wa1=wa1, wb1=wb1, wa2=wa2, wb2=wb2, win=win_g, wp=wp_g, wo=wo_g))
        if nxt:
            wa1, wb1 = res_f[4], _merge(res_o[3])

    dx, loss_blk, dgf = _loss_head(x, row(small["final_norm_g"]), target)
    loss = loss_blk[0, 0]

    sg = {k: [None] * L for k in ("norm_ffn1_g", "norm_mix_g", "norm_ffn2_g", "conv_dw", "conv_b", "conv_ln_g",
                                  "conv_ln_b", "short_dw", "pool_w", "pool_scale")}
    blocks = []
    g_up, l_up = [], None
    G, c = small["pool_w"].shape[1:3]
    for l in reversed(range(L)):
        sv = saved[l]
        sp = sv["sp"]
        res = _ffn_bwd(sv["x2"], sp["g2"], dx, sv["a2"], sv["b2"], sv["wa2"], sv["wb2"], l, 1, exchange(g_up))
        dx, dg2, da, db, z, dyb = res[:6]
        p_up = sum_block(g_up, res[6:]) if g_up else []
        res = _ffn_dw(sv["h2"], dyb, da, db, z, l, 1, scatter(p_up))
        g_f2 = [res[0]]
        if p_up:
            blocks.append((l_up, "f1", p_up, res[1:]))
        res = _mix_out_bwd(dx, sv["y"], sv["u"], sv["act"], sv["m"], sv["wp"], sv["wo"], l, exchange(g_f2))
        du, dact, g_o, g_p = res[:4]
        p_f2 = sum_block(g_f2, res[4:])
        res = _mix_seq_bwd(du, sv["u"], dact, sv["cv"], sp["cdw"], sp["lg"], sp["lb"],
                           sp["sdw"], sp["pwblk"], sp["ps"], bl, l, scatter(p_f2))
        du, gcdw, g512, g256, gpw = res[:5]
        blocks.append((l, "f2", p_f2, res[5:]))
        dx, dgm = _mix_in_bwd(sv["x1"], sp["gm"], dx, du, sv["win"], l)
        g_mx = [_mix_in_dw(sv["hm"], du, l), g_p, g_o]
        if l > 0:
            res = _ffn_bwd(sv["x0"], sp["g1"], dx, sv["a1"], sv["b1"], sv["wa1"], sv["wb1"], l, 0, exchange(g_mx))
            dx, dg1, da, db, z, dyb = res[:6]
            p_mx = sum_block(g_mx, res[6:])
            res = _ffn_dw(sv["h1"], dyb, da, db, z, l, 0, scatter(p_mx))
            blocks.append((l, "mx", p_mx, res[1:]))
            g_up, l_up = [res[0]], l
        else:
            res = _ffn_bwd_first(dx, sv["a1"], sv["b1"], sv["wb1"], l, 0, exchange(g_mx))
            da, db, z, dyb = res[:4]
            p_mx = sum_block(g_mx, res[4:])
            res = _ffn_dw(sv["h1"], dyb, da, db, z, l, 0, scatter(p_mx))
            blocks.append((l, "mx", p_mx, res[1:]))
            g_f1 = [res[0]]
            p_f1 = sum_block(g_f1, _run_carried(exchange(g_f1), "last"))
            res = _ffn_bwd_second(sv["x0"], sp["g1"], dx, da, db, sv["wa1"], l, 0, scatter(p_f1))
            dx, dg1 = res[:2]
            blocks.append((l, "f1", p_f1, res[2:]))
        sg["norm_ffn1_g"][l], sg["norm_mix_g"][l], sg["norm_ffn2_g"][l] = dg1[0], dgm[0], dg2[0]
        sg["conv_dw"][l] = gcdw[:CONV_W]
        sg["conv_b"][l], sg["conv_ln_g"][l], sg["conv_ln_b"][l] = g512[0], g512[1], g512[2]
        sg["short_dw"][l] = g256[:SHORT_W]
        sg["pool_scale"][l] = g256[SHORT_W]
        sg["pool_w"][l] = jnp.stack([gpw[gi * c:(gi + 1) * c, gi * c:(gi + 1) * c] for gi in range(G)])
    small_g = {k: jnp.stack(v) for k, v in sg.items()}
    small_g["final_norm_g"] = dgf[0]
    return loss, dx.reshape(bl, S, D), blocks, small_g


def _share_final(fs):
    n = len(fs)
    L = fs[0].shape[0]

    def body(*refs):
        outs = refs[n:2 * n]
        send_sems, recv_sems = refs[2 * n:]
        x, y, c, _ = _place()
        sib = (x, y, 1 - c)
        cps = []
        for ai in range(n):
            for l in range(L):
                cp = pltpu.make_async_remote_copy(src_ref=outs[ai].at[l, c], dst_ref=outs[ai].at[l, c],
                                                  send_sem=send_sems.at[ai, l], recv_sem=recv_sems.at[ai, l],
                                                  device_id=sib, device_id_type=MESH)
                cp.start()
                cps.append(cp)
        for ai in range(n):
            for l in range(L):
                blk = outs[ai].at[l, 1 - c]
                pltpu.make_async_remote_copy(src_ref=blk, dst_ref=blk, send_sem=send_sems.at[ai, l],
                                             recv_sem=recv_sems.at[ai, l], device_id=sib, device_id_type=MESH).wait_recv()
        for cp in cps:
            cp.wait_send()

    return _pallas(
        body, name="grad_share_final",
        in_specs=[ANY] * n, out_specs=[ANY] * n,
        out_shape=[jax.ShapeDtypeStruct(f.shape, f.dtype) for f in fs],
        scratch_shapes=[pltpu.SemaphoreType.DMA((n, L)), pltpu.SemaphoreType.DMA((n, L))],
        input_output_aliases={i: i for i in range(n)},
        compiler_params=pltpu.CompilerParams(has_side_effects=True),
    )(*fs)


def _all_reduce_small(v):
    R, W = v.shape

    def body(v_ref, out_ref, buf, send_sems, recv_sems):
        x, y, c, _ = _place()
        me = 4 * x + 2 * y + c
        buf[me] = v_ref[...]
        cps = []
        for k in range(1, 8):
            kx, ky, kc = (k >> 2) & 1, (k >> 1) & 1, k & 1
            to = (1 - x if kx else x, 1 - y if ky else y, 1 - c if kc else c)
            cp = pltpu.make_async_remote_copy(src_ref=v_ref, dst_ref=buf.at[me], send_sem=send_sems.at[k - 1],
                                              recv_sem=recv_sems.at[k - 1], device_id=to, device_id_type=MESH)
            cp.start()
            cps.append(cp)
        for cp in cps:
            cp.wait()
        acc = buf[0]
        for d in range(1, 8):
            acc = acc + buf[d]
        out_ref[...] = acc

    return _pallas(
        body, name="all_reduce_small",
        in_specs=[pl.BlockSpec(memory_space=pltpu.VMEM)], out_specs=pl.BlockSpec(memory_space=pltpu.VMEM),
        out_shape=jax.ShapeDtypeStruct((R, W), F32),
        scratch_shapes=[pltpu.VMEM((8, R, W), F32), pltpu.SemaphoreType.DMA((7,)), pltpu.SemaphoreType.DMA((7,))],
        compiler_params=pltpu.CompilerParams(has_side_effects=True, vmem_limit_bytes=VMEM_LIMIT),
    )(v)


def _row_tile(n, w):
    for t in (512, 352, 256, 128, 64, 32, 16):
        if n % t == 0 and t * w * 4 <= 4 * 1024 * 1024:
            return t
    raise ValueError((n, w))


def _sum_sibling(tag, cidx, g, r):
    _, N, W = g.shape
    tr = _row_tile(N, W)

    def body(c_ref, g_ref, r_ref, o_ref):
        del c_ref
        o_ref[...] = (g_ref[...].astype(F32) + r_ref[...].astype(F32)).astype(BF16)

    return _pallas(
        body, name=f"grad_sum_sibling_{tag}",
        grid_spec=pltpu.PrefetchScalarGridSpec(
            num_scalar_prefetch=1, grid=(N // tr,),
            in_specs=[pl.BlockSpec((None, tr, W), lambda i, c: (c[0], i, 0)),
                      pl.BlockSpec((tr, W), lambda i, c: (i, 0))],
            out_specs=pl.BlockSpec((tr, W), lambda i, c: (i, 0))),
        out_shape=jax.ShapeDtypeStruct((N, W), BF16),
        compiler_params=_cp(1),
    )(cidx, g, r)


def _sum_final(tag, idx, p, r2, l, L, prev):
    _, r, W = p.shape
    tr = _row_tile(r, W)

    def body(*refs):
        p_ref, r2_ref = refs[1:3]
        o_ref = refs[-1]
        acc = p_ref[...].astype(F32)
        for k in range(3):
            acc = acc + r2_ref[k].astype(F32)
        o_ref[...] = acc

    in_specs = [pl.BlockSpec((None, tr, W), lambda i, s: (s[1], i, 0)),
                pl.BlockSpec((3, tr, W), lambda i, s: (0, i, 0))]
    args = [idx, p, r2]
    aliases = {}
    if prev is not None:
        in_specs.append(ANY)
        args.append(prev)
        aliases = {3: 0}
    return _pallas(
        body, name=f"grad_sum_final_{tag}",
        grid_spec=pltpu.PrefetchScalarGridSpec(
            num_scalar_prefetch=1, grid=(r // tr,), in_specs=in_specs,
            out_specs=pl.BlockSpec((None, None, tr, W), lambda i, s: (l, s[0], i, 0))),
        out_shape=jax.ShapeDtypeStruct((L, 2, r, W), F32),
        input_output_aliases=aliases,
        compiler_params=_cp(1),
    )(*args)


def _adam_math(w, g, m, v):
    m = ADAM_B1 * m + (1.0 - ADAM_B1) * g
    v = ADAM_B2 * v + (1.0 - ADAM_B2) * (g * g)
    m_hat = m / (1.0 - ADAM_B1 ** ADAM_STEP)
    v_hat = v / (1.0 - ADAM_B2 ** ADAM_STEP)
    delta = -ADAM_LR * (m_hat / (jnp.sqrt(v_hat) + ADAM_EPS) + ADAM_WD * w)
    return delta, m, v


def _adam_big(name, w, m, v, gfull, row0):
    L, r, W = w.shape
    tr = _row_tile(r, W)
    assert row0 % tr == 0
    off = row0 // tr

    def body(w_ref, m_ref, v_ref, g_ref, go_ref, d_ref, mo_ref, vo_ref):
        g = g_ref[...]
        d, mn, vn = _adam_math(w_ref[...], g, m_ref[...], v_ref[...])
        go_ref[...] = g
        d_ref[...] = d
        mo_ref[...] = mn
        vo_ref[...] = vn

    blk = pl.BlockSpec((None, tr, W), lambda l, i: (l, i, 0))
    shp = jax.ShapeDtypeStruct(w.shape, F32)
    return _pallas(
        body, name=f"adam_{name}", grid=(L, r // tr),
        in_specs=[blk, blk, blk, pl.BlockSpec((None, tr, W), lambda l, i: (l, off + i, 0))],
        out_specs=[blk] * 4, out_shape=[shp] * 4,
        compiler_params=_cp(2),
    )(w, m, v, gfull)


def _adam_small(w, g, m, v):
    def body(w_ref, g_ref, m_ref, v_ref, d_ref, mo_ref, vo_ref):
        d, mn, vn = _adam_math(w_ref[...], g_ref[...], m_ref[...], v_ref[...])
        d_ref[...] = d
        mo_ref[...] = mn
        vo_ref[...] = vn

    spec = pl.BlockSpec(memory_space=pltpu.VMEM)
    shp = jax.ShapeDtypeStruct(w.shape, F32)
    return _pallas(body, name="adam_small", in_specs=[spec] * 4, out_specs=[spec] * 3, out_shape=[shp] * 3)(w, g, m, v)


_WEIGHTS = ['norm_ffn1_g', 'ffn1_w_gate', 'ffn1_w_up', 'ffn1_w_down', 'norm_mix_g', 'w_in', 'conv_dw', 'conv_b',
            'conv_ln_g', 'conv_ln_b', 'w_pa', 'short_dw', 'w_pb', 'pool_w', 'pool_scale', 'w_pc', 'w_o',
            'norm_ffn2_g', 'ffn2_w_gate', 'ffn2_w_up', 'ffn2_w_down', 'final_norm_g']
_BIG = ('ffn1_w_gate', 'ffn1_w_up', 'ffn1_w_down', 'w_in', 'w_pa', 'w_pb', 'w_pc', 'w_o',
        'ffn2_w_gate', 'ffn2_w_up', 'ffn2_w_down')
_TRANSPOSED = ('ffn1_w_gate', 'ffn1_w_up', 'ffn2_w_gate', 'ffn2_w_up')
_SMALL = tuple(n for n in _WEIGHTS if n not in _BIG)
_SMALL_REDUCED = ('norm_ffn1_g', 'norm_mix_g', 'conv_b', 'conv_ln_g', 'conv_ln_b', 'pool_w', 'pool_scale',
                  'norm_ffn2_g', 'final_norm_g', 'conv_dw', 'short_dw')


def _pack(arrs, rows_multiple=8):
    flat = jnp.concatenate([a.reshape(-1) for a in arrs])
    n = flat.shape[0]
    per = 128 * rows_multiple
    padded = -(-n // per) * per
    return jnp.pad(flat, (0, padded - n)).reshape(padded // 128, 128)


def _unpack(buf, shapes):
    flat = buf.reshape(-1)
    out, o = [], 0
    for s in shapes:
        k = 1
        for d in s:
            k *= d
        out.append(flat[o:o + k].reshape(s))
        o += k
    return out


def _halves(a):
    return a.reshape(2, a.shape[0] // 2, a.shape[1])


def _step(P, M, V, x, loss_target):
    tr = lambda a: jnp.transpose(a, (0, 2, 1))
    bf = lambda a: a.astype(BF16)
    L = P['w_in'].shape[0]
    cw, sw = P['conv_dw'].shape[-1], P['short_dw'].shape[-1]
    ffa = [jnp.stack([bf(tr(P[f'ffn{f}_w_gate'])), bf(tr(P[f'ffn{f}_w_up']))], axis=1) for f in (1, 2)]
    ffb = [bf(P[f'ffn{f}_w_down']) for f in (1, 2)]
    win = bf(P['w_in'])
    wp = jnp.concatenate([bf(P['w_pa']), bf(P['w_pb']), bf(P['w_pc'])], axis=1)
    wo = bf(P['w_o'])
    dws = jnp.zeros((L, 64, 128), F32)
    dws = dws.at[:, 0:CONV_W, 0:cw].set(P['conv_dw']).at[:, 32:32 + SHORT_W, 0:sw].set(P['short_dw'])
    shards = [dict(f1a=ffa[0][l], f1b=_halves(ffb[0][l]), f2a=ffa[1][l], f2b=_halves(ffb[1][l]),
                   mx=(_halves(win[l]), _halves(wp[l]), _halves(wo[l]), _halves(dws[l]))) for l in range(L)]

    xi, yi, ci = lax.axis_index("x"), lax.axis_index("y"), lax.axis_index("c")
    chip = 2 * xi + yi
    cidx = jnp.stack([ci]).astype(jnp.int32)
    idx = jnp.stack([ci, chip]).astype(jnp.int32)
    count = [0]

    def sum_block(gs, r1):
        t0 = count[0]
        count[0] += len(gs)
        parts = []
        for k, (g, r) in enumerate(zip(gs, r1)):
            W = g.shape[-1]
            p = _sum_sibling(t0 + k, cidx, g.reshape(2, -1, W), r.reshape(-1, W))
            parts.append(p.reshape(g.shape[1:]))
        return parts

    wa1, wb1 = _run_carried(_Carried("gather", [shards[0]["f1a"], shards[0]["f1b"]]), "first")
    small = {n: P[n] for n in _SMALL if n not in ('conv_dw', 'short_dw')}
    loss, dx, blocks, small_g = _fwd_bwd(x, loss_target, shards, small, (cw, sw), (wa1, _merge(wb1)), sum_block)

    finals = {}
    for t, (l, name, parts, r2) in enumerate(blocks):
        prev = finals.get(name, [None] * len(parts))
        finals[name] = [_sum_final(f"{t}_{k}", idx, p, r, l, L, pv) for k, (p, r, pv) in enumerate(zip(parts, r2, prev))]
    shared = _share_final(finals["f1"] + finals["f2"] + finals["mx"])
    f_f1, f_f2, f_in, f_p, f_o = [f.reshape(L, -1, f.shape[-1]) for f in shared]

    tot = _all_reduce_small(_pack([small_g[n] for n in _SMALL_REDUCED]))
    tot = dict(zip(_SMALL_REDUCED, _unpack(tot, [small_g[n].shape for n in _SMALL_REDUCED])))
    tot['conv_dw'] = lax.dynamic_slice_in_dim(tot['conv_dw'], chip * cw, cw, axis=2)
    tot['short_dw'] = lax.dynamic_slice_in_dim(tot['short_dw'], chip * sw, sw, axis=2)

    F4 = P['ffn1_w_down'].shape[1]
    dc, ds = P['w_pa'].shape[1], P['w_pb'].shape[1]
    src = {'ffn1_w_gate': (f_f1, 0), 'ffn1_w_up': (f_f1, F4), 'ffn1_w_down': (f_f1, 2 * F4),
           'ffn2_w_gate': (f_f2, 0), 'ffn2_w_up': (f_f2, F4), 'ffn2_w_down': (f_f2, 2 * F4),
           'w_in': (f_in, 0), 'w_pa': (f_p, 0), 'w_pb': (f_p, dc), 'w_pc': (f_p, dc + ds), 'w_o': (f_o, 0)}
    grads, deltas, new_m, new_v = {}, {}, {}, {}
    for n in _BIG:
        gfull, row0 = src[n]
        if n in _TRANSPOSED:
            outs = _adam_big(n, tr(P[n]), tr(M[n]), tr(V[n]), gfull, row0)
            grads[n], deltas[n], new_m[n], new_v[n] = [tr(o) for o in outs]
        else:
            grads[n], deltas[n], new_m[n], new_v[n] = _adam_big(n, P[n], M[n], V[n], gfull, row0)
    shapes = [P[n].shape for n in _SMALL]
    d_s, m_s, v_s = _adam_small(_pack([P[n] for n in _SMALL]), _pack([tot[n] for n in _SMALL]),
                                _pack([M[n] for n in _SMALL]), _pack([V[n] for n in _SMALL]))
    for n, d, mm, vv in zip(_SMALL, _unpack(d_s, shapes), _unpack(m_s, shapes), _unpack(v_s, shapes)):
        grads[n], deltas[n], new_m[n], new_v[n] = tot[n], d, mm, vv

    loss = lax.psum(loss, ("x", "y", "c"))
    return (loss, dx, *[grads[n] for n in _WEIGHTS], *[deltas[n] for n in _WEIGHTS],
            *[new_m[n] for n in _WEIGHTS], *[new_v[n] for n in _WEIGHTS])


def kernel(x, norm_ffn1_g, ffn1_w_gate, ffn1_w_up, ffn1_w_down, norm_mix_g, w_in, conv_dw, conv_b, conv_ln_g, conv_ln_b, w_pa, short_dw, w_pb, pool_w, pool_scale, w_pc, w_o, norm_ffn2_g, ffn2_w_gate, ffn2_w_up, ffn2_w_down, final_norm_g, loss_target, m_norm_ffn1_g, m_ffn1_w_gate, m_ffn1_w_up, m_ffn1_w_down, m_norm_mix_g, m_w_in, m_conv_dw, m_conv_b, m_conv_ln_g, m_conv_ln_b, m_w_pa, m_short_dw, m_w_pb, m_pool_w, m_pool_scale, m_w_pc, m_w_o, m_norm_ffn2_g, m_ffn2_w_gate, m_ffn2_w_up, m_ffn2_w_down, m_final_norm_g, v_norm_ffn1_g, v_ffn1_w_gate, v_ffn1_w_up, v_ffn1_w_down, v_norm_mix_g, v_w_in, v_conv_dw, v_conv_b, v_conv_ln_g, v_conv_ln_b, v_w_pa, v_short_dw, v_w_pb, v_pool_w, v_pool_scale, v_w_pc, v_w_o, v_norm_ffn2_g, v_ffn2_w_gate, v_ffn2_w_up, v_ffn2_w_down, v_final_norm_g):
    args = locals()
    P = {n: args[n] for n in _WEIGHTS}
    M = {n: args["m_" + n] for n in _WEIGHTS}
    V = {n: args["v_" + n] for n in _WEIGHTS}
    return _step(P, M, V, x, loss_target)
```

```python
import jax
import jax.numpy as jnp
from jax import lax
from jax.experimental import pallas as pl
from jax.experimental.pallas import tpu as pltpu

F32 = jnp.float32
BF16 = jnp.bfloat16
EPS = 1e-6
NS = 4
CONV_W = 31
SHORT_W = 3
POOL_WINDOWS = (2, 4, 8, 16)
HALO = 32
ADAM_LR, ADAM_B1, ADAM_B2, ADAM_EPS, ADAM_WD, ADAM_STEP = 0.001, 0.9, 0.999, 1e-08, 0.01, 10
MESH = pl.DeviceIdType.MESH
ANY = pl.BlockSpec(memory_space=pl.ANY)
VMEM_LIMIT = 56 * 1024 * 1024
FFN_BWD_SUBTILES = 2


def _pallas(body, **kw):
    return pl.pallas_call(body, **kw)


def _cp(n_axes):
    return pltpu.CompilerParams(dimension_semantics=("arbitrary",) * n_axes, vmem_limit_bytes=VMEM_LIMIT)


def _nn(a, b):
    return jnp.dot(a, b, preferred_element_type=F32)


def _nt(a, b):
    return lax.dot_general(a, b, (((1,), (1,)), ((), ())), preferred_element_type=F32)


def _tn(a, b):
    return lax.dot_general(a, b, (((0,), (0,)), ((), ())), preferred_element_type=F32)


def _sigmoid(v):
    return 1.0 / (1.0 + jnp.exp(-v))


def _rms_stats(x):
    rs = lax.rsqrt(jnp.mean(x * x, axis=-1, keepdims=True) + EPS)
    return x * rs, rs


def _rms_bwd(dh, x, g):
    xh, rs = _rms_stats(x)
    dhg = dh * g
    dx = rs * (dhg - xh * jnp.mean(dhg * xh, axis=-1, keepdims=True))
    return dx, jnp.sum(dh * xh, axis=0, keepdims=True)


def _tile(n, pref):
    t = min(n, pref)
    assert n % t == 0, (n, t)
    return t


def _place():
    x, y, c = lax.axis_index("x"), lax.axis_index("y"), lax.axis_index("c")
    chips = [(1 - x, y), (x, 1 - y), (1 - x, 1 - y)]
    return x, y, c, chips


def _gather_copies(ins, outs, sems):
    send_sems, recv_sems, local_sems = sems
    x, y, c, chips = _place()
    me = 2 * x + y
    sib = (x, y, 1 - c)

    def remote(ai, k, src, dst, to):
        return pltpu.make_async_remote_copy(src_ref=src, dst_ref=dst, send_sem=send_sems.at[ai, k],
                                            recv_sem=recv_sems.at[ai, k], device_id=to, device_id_type=MESH)

    local, first, landed, passed, from_sib = [], [], [], [], []
    for ai in range(len(ins)):
        local.append(pltpu.make_async_copy(ins[ai], outs[ai].at[me], local_sems.at[ai]))
        for k, chip in enumerate(chips):
            theirs = 2 * chip[0] + chip[1]
            first.append(remote(ai, k, ins[ai].at[c], outs[ai].at[me, c], (*chip, c)))
            blk = outs[ai].at[theirs, c]
            landed.append(remote(ai, k, blk, blk, (*chip, c)))
            passed.append(remote(ai, 3 + k, blk, blk, sib))
            blk2 = outs[ai].at[theirs, 1 - c]
            from_sib.append(remote(ai, 3 + k, blk2, blk2, sib))
    return local, first, landed, passed, from_sib


def _gather_start(ins, outs, sems):
    local, first, _, _, _ = _gather_copies(ins, outs, sems)
    for cp in local + first:
        cp.start()


def _gather_finish(ins, outs, sems):
    local, first, landed, passed, from_sib = _gather_copies(ins, outs, sems)
    for arrive, fwd in zip(landed, passed):
        arrive.wait_recv()
        fwd.start()
    for cp in from_sib:
        cp.wait_recv()
    for cp in first + passed:
        cp.wait_send()
    for cp in local:
        cp.wait()


def _scatter_copies(ins, outs, sems):
    send_sems, recv_sems = sems
    x, y, c, chips = _place()
    return [pltpu.make_async_remote_copy(
        src_ref=ins[ai].at[2 * chip[0] + chip[1]], dst_ref=outs[ai].at[k],
        send_sem=send_sems.at[ai, k], recv_sem=recv_sems.at[ai, k], device_id=(*chip, c), device_id_type=MESH)
        for ai in range(len(ins)) for k, chip in enumerate(chips)]


def _exchange_copies(ins, outs, sems):
    send_sems, recv_sems = sems
    x, y, c, _ = _place()
    return [pltpu.make_async_remote_copy(
        src_ref=ins[ai].at[1 - c], dst_ref=outs[ai], send_sem=send_sems.at[ai], recv_sem=recv_sems.at[ai],
        device_id=(x, y, 1 - c), device_id_type=MESH) for ai in range(len(ins))]


class _Carried:
    def __init__(self, kind="gather", arrs=()):
        self.kind, self.arrs, self.n = kind, tuple(arrs), len(arrs)
        self.specs = [ANY] * self.n
        if kind == "gather":
            self.out_shape = [jax.ShapeDtypeStruct((NS,) + a.shape, a.dtype) for a in self.arrs]
            sems = [(self.n, 6), (self.n, 6), (self.n,)]
        elif kind == "exchange":
            self.out_shape = [jax.ShapeDtypeStruct(a.shape[1:], a.dtype) for a in self.arrs]
            sems = [(self.n,), (self.n,)]
        else:
            self.out_shape = [jax.ShapeDtypeStruct((3,) + a.shape[1:], a.dtype) for a in self.arrs]
            sems = [(self.n, 3), (self.n, 3)]
        self.scratch = [pltpu.SemaphoreType.DMA(s) for s in sems] if self.n else []

    def split(self, refs, n_in, n_out):
        n = self.n
        a, b, c = n_in + n, n_in + n + n_out, n_in + 2 * n + n_out
        n_sem = len(self.scratch)
        own_scr = refs[c:len(refs) - n_sem]
        return refs[:n_in], refs[a:b], own_scr, (refs[n_in:a], refs[b:c], refs[len(refs) - n_sem:])

    def start(self, carried):
        ins, outs, sems = carried
        if self.kind == "gather":
            _gather_start(ins, outs, sems)
        else:
            for cp in (_exchange_copies if self.kind == "exchange" else _scatter_copies)(ins, outs, sems):
                cp.start()

    def finish(self, carried):
        ins, outs, sems = carried
        if self.kind == "gather":
            _gather_finish(ins, outs, sems)
        else:
            for cp in (_exchange_copies if self.kind == "exchange" else _scatter_copies)(ins, outs, sems):
                cp.wait()

    def when(self, cond, carried, what):
        if self.n:
            pl.when(cond)(lambda: what(carried))


_NOTHING = _Carried()


def _run_carried(car, tag):
    def body(*refs):
        _, _, _, carried = car.split(refs, 0, 0)
        car.start(carried)
        car.finish(carried)

    return _pallas(
        body, name=f"{car.kind}_{tag}",
        in_specs=car.specs, out_specs=car.specs, out_shape=car.out_shape, scratch_shapes=car.scratch,
        compiler_params=pltpu.CompilerParams(has_side_effects=True),
    )(*car.arrs)


def _ffn_weight_specs(F4, D):
    return [pl.BlockSpec((None, None, F4, D), lambda i, j: (j, 0, 0, 0)),
            pl.BlockSpec((None, None, F4, D), lambda i, j: (j, 1, 0, 0)),
            pl.BlockSpec((None, F4, D), lambda i, j: (j, 0, 0))]


def _ffn_fwd(x, g, wa, wb, l, f, car=_NOTHING):
    T, D = x.shape
    F4 = wb.shape[1]
    tm = _tile(T, 512)
    ni = T // tm

    def body(*refs):
        ((x_ref, g_ref, wg_ref, wu_ref, wd_ref), (xo_ref, h_ref, p_ref, q_ref, z_ref), (acc,),
         carried) = car.split(refs, 5, 5)
        i = pl.program_id(0)
        j = pl.program_id(1)
        car.when((i == 0) & (j == 0), carried, car.start)

        @pl.when(j == 0)
        def _():
            xh, _ = _rms_stats(x_ref[...])
            h_ref[...] = (xh * g_ref[...]).astype(BF16)
            acc[...] = jnp.zeros_like(acc)

        h = h_ref[...]
        a = _nt(h, wg_ref[...])
        b = _nt(h, wu_ref[...])
        sg = _sigmoid(a)
        silu = a * sg
        p_ref[...] = (b * (sg + silu * (1.0 - sg))).astype(BF16)
        q_ref[...] = silu.astype(BF16)
        z = (silu * b).astype(BF16)
        z_ref[...] = z
        acc[...] += _nn(z, wd_ref[...])

        @pl.when(j == NS - 1)
        def _():
            xo_ref[...] = x_ref[...] + 0.5 * acc[...]

        car.when((i == ni - 1) & (j == NS - 1), carried, car.finish)

    return _pallas(
        body, name=f"ffn_fwd_{l}_{f}", grid=(ni, NS),
        in_specs=[pl.BlockSpec((tm, D), lambda i, j: (i, 0)), pl.BlockSpec((1, D), lambda i, j: (0, 0))]
        + _ffn_weight_specs(F4, D) + car.specs,
        out_specs=[pl.BlockSpec((tm, D), lambda i, j: (i, 0)),
                   pl.BlockSpec((tm, D), lambda i, j: (i, 0))]
        + [pl.BlockSpec((None, tm, F4), lambda i, j: (j, i, 0))] * 3 + car.specs,
        out_shape=[jax.ShapeDtypeStruct((T, D), F32), jax.ShapeDtypeStruct((T, D), BF16)]
        + [jax.ShapeDtypeStruct((NS, T, F4), BF16)] * 3 + car.out_shape,
        scratch_shapes=[pltpu.VMEM((tm, D), F32)] + car.scratch,
        compiler_params=_cp(2),
    )(x, g, wa, wa, wb, *car.arrs)


def _ffn_bwd(x, g, dy, p, q, wa, wb, l, f, car=_NOTHING):
    T, D = x.shape
    F4 = wb.shape[1]
    tm = _tile(T, 512)
    ni = T // tm

    def body(*refs):
        ((x_ref, g_ref, dy_ref, p_ref, q_ref, wg_ref, wu_ref, wd_ref),
         (dx_ref, dg_ref, da_ref, db_ref, dyb_ref), (dh,), carried) = car.split(refs, 8, 5)
        i = pl.program_id(0)
        j = pl.program_id(1)
        car.when((i == 0) & (j == 0), carried, car.start)

        @pl.when(j == 0)
        def _():
            dyb_ref[...] = (0.5 * dy_ref[...]).astype(BF16)
            dh[...] = jnp.zeros_like(dh)

        @pl.when((i == 0) & (j == 0))
        def _():
            dg_ref[...] = jnp.zeros_like(dg_ref)

        for r in range(FFN_BWD_SUBTILES):
            rows = slice(r * (tm // FFN_BWD_SUBTILES), (r + 1) * (tm // FFN_BWD_SUBTILES))
            dz = _nt(dyb_ref[rows, :], wd_ref[...])
            da = (dz * p_ref[rows, :].astype(F32)).astype(BF16)
            db = (dz * q_ref[rows, :].astype(F32)).astype(BF16)
            da_ref[rows, :] = da
            db_ref[rows, :] = db
            dh[rows, :] += _nn(da, wg_ref[...]) + _nn(db, wu_ref[...])

        @pl.when(j == NS - 1)
        def _():
            dxn, dg = _rms_bwd(dh[...], x_ref[...], g_ref[...])
            dx_ref[...] = dy_ref[...] + dxn
            dg_ref[...] += dg

        car.when((i == ni - 1) & (j == NS - 1), carried, car.finish)

    tok = pl.BlockSpec((tm, D), lambda i, j: (i, 0))
    vec = pl.BlockSpec((1, D), lambda i, j: (0, 0))
    chunk = pl.BlockSpec((None, tm, F4), lambda i, j: (j, i, 0))
    return _pallas(
        body, name=f"ffn_bwd_{l}_{f}", grid=(ni, NS),
        in_specs=[tok, vec, tok, chunk, chunk] + _ffn_weight_specs(F4, D) + car.specs,
        out_specs=[tok, vec, chunk, chunk, tok] + car.specs,
        out_shape=[jax.ShapeDtypeStruct((T, D), F32), jax.ShapeDtypeStruct((1, D), F32),
                   jax.ShapeDtypeStruct((NS, T, F4), BF16), jax.ShapeDtypeStruct((NS, T, F4), BF16),
                   jax.ShapeDtypeStruct((T, D), BF16)] + car.out_shape,
        scratch_shapes=[pltpu.VMEM((tm, D), F32)] + car.scratch,
        compiler_params=_cp(2),
    )(x, g, dy, p, q, wa, wa, wb, *car.arrs)


def _ffn_bwd_first(dy, p, q, wb, l, f, car=_NOTHING):
    T, D = dy.shape
    F4 = wb.shape[1]
    tm = _tile(T, 512)
    ni = T // tm

    def body(*refs):
        (dy_ref, p_ref, q_ref, wd_ref), (da_ref, db_ref, dyb_ref), _, carried = car.split(refs, 4, 3)
        i = pl.program_id(0)
        j = pl.program_id(1)
        car.when((i == 0) & (j == 0), carried, car.start)

        @pl.when(j == 0)
        def _():
            dyb_ref[...] = (0.5 * dy_ref[...]).astype(BF16)

        for r in range(FFN_BWD_SUBTILES):
            rows = slice(r * (tm // FFN_BWD_SUBTILES), (r + 1) * (tm // FFN_BWD_SUBTILES))
            dz = _nt(dyb_ref[rows, :], wd_ref[...])
            da_ref[rows, :] = (dz * p_ref[rows, :].astype(F32)).astype(BF16)
            db_ref[rows, :] = (dz * q_ref[rows, :].astype(F32)).astype(BF16)

        car.when((i == ni - 1) & (j == NS - 1), carried, car.finish)

    tok = pl.BlockSpec((tm, D), lambda i, j: (i, 0))
    chunk = pl.BlockSpec((None, tm, F4), lambda i, j: (j, i, 0))
    return _pallas(
        body, name=f"ffn_bwd_first_{l}_{f}", grid=(ni, NS),
        in_specs=[tok, chunk, chunk, _ffn_weight_specs(F4, D)[2]] + car.specs,
        out_specs=[chunk, chunk, tok] + car.specs,
        out_shape=[jax.ShapeDtypeStruct((NS, T, F4), BF16), jax.ShapeDtypeStruct((NS, T, F4), BF16),
                   jax.ShapeDtypeStruct((T, D), BF16)] + car.out_shape,
        scratch_shapes=car.scratch,
        compiler_params=_cp(2),
    )(dy, p, q, wb, *car.arrs)


def _ffn_bwd_second(x, g, dy, da, db, wa, l, f, car=_NOTHING):
    T, D = x.shape
    F4 = da.shape[-1]
    tm = _tile(T, 512)
    ni = T // tm

    def body(*refs):
        (x_ref, g_ref, dy_ref, da_ref, db_ref, wg_ref, wu_ref), (dx_ref, dg_ref), (dh,), carried = car.split(refs, 7, 2)
        i = pl.program_id(0)
        j = pl.program_id(1)
        car.when((i == 0) & (j == 0), carried, car.start)

        @pl.when(j == 0)
        def _():
            dh[...] = jnp.zeros_like(dh)

        @pl.when((i == 0) & (j == 0))
        def _():
            dg_ref[...] = jnp.zeros_like(dg_ref)

        dh[...] += _nn(da_ref[...], wg_ref[...]) + _nn(db_ref[...], wu_ref[...])

        @pl.when(j == NS - 1)
        def _():
            dxn, dg = _rms_bwd(dh[...], x_ref[...], g_ref[...])
            dx_ref[...] = dy_ref[...] + dxn
            dg_ref[...] += dg

        car.when((i == ni - 1) & (j == NS - 1), carried, car.finish)

    tok = pl.BlockSpec((tm, D), lambda i, j: (i, 0))
    vec = pl.BlockSpec((1, D), lambda i, j: (0, 0))
    chunk = pl.BlockSpec((None, tm, F4), lambda i, j: (j, i, 0))
    return _pallas(
        body, name=f"ffn_bwd_second_{l}_{f}", grid=(ni, NS),
        in_specs=[tok, vec, tok, chunk, chunk] + _ffn_weight_specs(F4, D)[:2] + car.specs,
        out_specs=[tok, vec] + car.specs,
        out_shape=[jax.ShapeDtypeStruct((T, D), F32), jax.ShapeDtypeStruct((1, D), F32)] + car.out_shape,
        scratch_shapes=[pltpu.VMEM((tm, D), F32)] + car.scratch,
        compiler_params=_cp(2),
    )(x, g, dy, da, db, wa, wa, *car.arrs)


def _ffn_dw(h, dyb, da, db, z, l, f, car=_NOTHING):
    T, D = h.shape
    F4 = da.shape[-1]
    tk = _tile(T, 512)
    nt = T // tk
    R2 = 3 * F4 // 2

    def body(*refs):
        (h_ref, dyb_ref, da_ref, db_ref, z_ref), (g_ref,), (accg, accu, accd), carried = car.split(refs, 5, 1)
        t = pl.program_id(1)
        car.when((pl.program_id(0) == 0) & (t == 0), carried, car.start)

        @pl.when(t == 0)
        def _():
            accg[...] = jnp.zeros_like(accg)
            accu[...] = jnp.zeros_like(accu)
            accd[...] = jnp.zeros_like(accd)

        hv = h_ref[...]
        accg[...] += _tn(da_ref[...], hv)
        accu[...] += _tn(db_ref[...], hv)
        accd[...] += _tn(z_ref[...], dyb_ref[...])

        @pl.when(t == nt - 1)
        def _():
            g_ref[0, 0:F4, :] = accg[...].astype(BF16)
            g_ref[0, F4:R2, :] = accu[0:R2 - F4, :].astype(BF16)
            g_ref[1, 0:2 * F4 - R2, :] = accu[R2 - F4:F4, :].astype(BF16)
            g_ref[1, 2 * F4 - R2:R2, :] = accd[...].astype(BF16)

        car.when((pl.program_id(0) == NS - 1) & (t == nt - 1), carried, car.finish)

    tok = pl.BlockSpec((tk, D), lambda s, t: (t, 0))
    chunk = pl.BlockSpec((None, tk, F4), lambda s, t: (s, t, 0))
    return _pallas(
        body, name=f"ffn_dw_{l}_{f}", grid=(NS, nt),
        in_specs=[tok, tok, chunk, chunk, chunk] + car.specs,
        out_specs=[pl.BlockSpec((2, None, R2, D), lambda s, t: (0, s, 0, 0))] + car.specs,
        out_shape=[jax.ShapeDtypeStruct((2, NS, R2, D), BF16)] + car.out_shape,
        scratch_shapes=[pltpu.VMEM((F4, D), F32), pltpu.VMEM((F4, D), F32), pltpu.VMEM((F4, D), F32)] + car.scratch,
        compiler_params=_cp(2),
    )(h, dyb, da, db, z, *car.arrs)


def _mix_in(x, g, win, l, car=_NOTHING):
    T, D = x.shape
    C4 = win.shape[-1]
    tm = _tile(T, 512)
    ni = T // tm

    def body(*refs):
        (x_ref, g_ref, w_ref), (h_ref, u_ref), _, carried = car.split(refs, 3, 2)
        i = pl.program_id(0)
        j = pl.program_id(1)
        car.when((i == 0) & (j == 0), carried, car.start)

        @pl.when(j == 0)
        def _():
            xh, _ = _rms_stats(x_ref[...])
            h_ref[...] = (xh * g_ref[...]).astype(BF16)

        u_ref[...] = _nn(h_ref[...], w_ref[...]).astype(BF16)
        car.when((i == ni - 1) & (j == NS - 1), carried, car.finish)

    return _pallas(
        body, name=f"mix_in_{l}", grid=(ni, NS),
        in_specs=[pl.BlockSpec((tm, D), lambda i, j: (i, 0)), pl.BlockSpec((1, D), lambda i, j: (0, 0)),
                  pl.BlockSpec((None, D, C4), lambda i, j: (j, 0, 0))] + car.specs,
        out_specs=[pl.BlockSpec((tm, D), lambda i, j: (i, 0)), pl.BlockSpec((tm, C4), lambda i, j: (i, j))] + car.specs,
        out_shape=[jax.ShapeDtypeStruct((T, D), BF16), jax.ShapeDtypeStruct((T, NS * C4), BF16)] + car.out_shape,
        scratch_shapes=car.scratch,
        compiler_params=_cp(2),
    )(x, g, win, *car.arrs)


def _pool_lane_window(n):
    lane = lax.broadcasted_iota(jnp.int32, (1, n), 1) // (n // len(POOL_WINDOWS))
    w = jnp.full((1, n), float(POOL_WINDOWS[-1]), F32)
    for gi in range(len(POOL_WINDOWS) - 1):
        w = jnp.where(lane == gi, float(POOL_WINDOWS[gi]), w)
    return lane, w


def _pool_select(lane, sums):
    out = sums[-1]
    for gi in range(len(sums) - 1):
        out = jnp.where(lane == gi, sums[gi], out)
    return out


def _back(v, s):
    return v if s == 0 else pltpu.roll(v, s, 0)


def _fwd_shift(v, s):
    return v if s == 0 else pltpu.roll(v, v.shape[0] - s, 0)


def _mix_seq_fwd(u, cdw, cb, lg, lb, sdw, pwblk, ps, bl, l, car=_NOTHING):
    T = u.shape[0]
    S = T // bl
    ts = _tile(S, 256)
    nt = S // ts
    DC, DS = cdw.shape[-1], sdw.shape[-1]
    o_ag, o_bg, o_cg, o_bx, o_p, o_end = DC, 2 * DC, 2 * DC + DS, 2 * DC + 2 * DS, 2 * DC + 3 * DS, 2 * DC + 4 * DS

    def body(*refs):
        ((up_ref, uc_ref, cdw_ref, cb_ref, lg_ref, lb_ref, sdw_ref, pw_ref, ps_ref),
         (act_ref, cv_ref), _, carried) = car.split(refs, 9, 2)
        i = pl.program_id(1)
        car.when((pl.program_id(0) == 0) & (i == 0), carried, car.start)
        keep = jnp.where(i > 0, 1.0, 0.0).astype(F32)

        def ext(lo, hi):
            p = up_ref[ts - HALO:ts, lo:hi].astype(F32) * keep
            return jnp.concatenate([p, uc_ref[:, lo:hi].astype(F32)], axis=0)

        glu = ext(0, o_ag) * _sigmoid(ext(o_ag, o_bg))
        cv = jnp.zeros((ts, DC), F32) + cb_ref[...]
        for s in range(CONV_W):
            cv = cv + _back(glu, s)[HALO:, :] * cdw_ref[CONV_W - 1 - s:CONV_W - s, :]
        cv_ref[...] = cv
        mu = jnp.mean(cv, axis=-1, keepdims=True)
        xc = cv - mu
        lnv = xc * lax.rsqrt(jnp.mean(xc * xc, axis=-1, keepdims=True) + EPS) * lg_ref[...] + lb_ref[...]
        act_ref[:, 0:DC] = (lnv * _sigmoid(lnv)).astype(BF16)

        q = ext(o_cg, o_bx) * ext(o_bx, o_p)
        sc = jnp.zeros((ts, DS), F32)
        for s in range(SHORT_W):
            sc = sc + _back(q, s)[HALO:, :] * sdw_ref[SHORT_W - 1 - s:SHORT_W - s, :]
        act_ref[:, DC:DC + DS] = (uc_ref[:, o_bg:o_cg].astype(F32) * sc).astype(BF16)

        p = ext(o_p, o_end)
        lane, wl = _pool_lane_window(DS)
        sums, cur, sh = [], p, 1
        for _ in POOL_WINDOWS:
            cur = cur + _back(cur, sh)
            sums.append(cur[HALO:, :])
            sh *= 2
        pos = (i * ts + lax.broadcasted_iota(jnp.int32, (ts, 1), 0) + 1).astype(F32)
        pooled = _pool_select(lane, sums) / jnp.minimum(pos, wl) - p[HALO:, :]
        act_ref[:, DC + DS:DC + 2 * DS] = (_nn(pooled.astype(BF16), pw_ref[...]) * ps_ref[...]).astype(BF16)
        car.when((pl.program_id(0) == bl - 1) & (i == nt - 1), carried, car.finish)

    ucol = 2 * DC + 4 * DS
    full = lambda a: pl.BlockSpec(a.shape, lambda b, i: (0,) * a.ndim)
    return _pallas(
        body, name=f"mix_seq_fwd_{l}", grid=(bl, nt),
        in_specs=[pl.BlockSpec((ts, ucol), lambda b, i: (b * nt + jnp.maximum(i - 1, 0), 0)),
                  pl.BlockSpec((ts, ucol), lambda b, i: (b * nt + i, 0)),
                  full(cdw), full(cb), full(lg), full(lb), full(sdw), full(pwblk), full(ps)] + car.specs,
        out_specs=[pl.BlockSpec((ts, DC + 2 * DS), lambda b, i: (b * nt + i, 0)),
                   pl.BlockSpec((ts, DC), lambda b, i: (b * nt + i, 0))] + car.specs,
        out_shape=[jax.ShapeDtypeStruct((T, DC + 2 * DS), BF16), jax.ShapeDtypeStruct((T, DC), F32)] + car.out_shape,
        scratch_shapes=car.scratch,
        compiler_params=_cp(2),
    )(u, u, cdw, cb, lg, lb, sdw, pwblk, ps, *car.arrs)


def _mix_out(x, act, u, wp, wo, l, car=_NOTHING):
    T, D = x.shape
    tm = _tile(T, 512)
    ni = T // tm
    DA = act.shape[-1]
    DC, DS = DA // 2, DA // 4
    NB = D // NS
    gcol = (2 * DC + 4 * DS) // D

    def body(*refs):
        ((x_ref, act_ref, g0_ref, g1_ref, g2_ref, wp_ref, wo_ref),
         (xo_ref, y_ref, m_ref), _, carried) = car.split(refs, 7, 3)
        car.when(pl.program_id(0) == 0, carried, car.start)
        parts = [(0, DC), (DC, DC + DS), (DC + DS, DC + 2 * DS)]
        m = jnp.zeros((tm, D), F32)
        for k, (lo, hi) in enumerate(parts):
            av = act_ref[:, lo:hi]
            y = jnp.concatenate([_nn(av, wp_ref[s, lo:hi, :]) for s in range(NS)], axis=1)
            y_ref[:, k * D:(k + 1) * D] = y.astype(BF16)
            gl = (g0_ref, g1_ref, g2_ref)[k][...].astype(F32)
            m = m + _sigmoid(gl) * y
        mb = m.astype(BF16)
        m_ref[...] = mb
        out = _nn(mb[:, 0:NB], wo_ref[0])
        for s in range(1, NS):
            out = out + _nn(mb[:, s * NB:(s + 1) * NB], wo_ref[s])
        xo_ref[...] = x_ref[...] + out
        car.when(pl.program_id(0) == ni - 1, carried, car.finish)

    tok = lambda w: pl.BlockSpec((tm, w), lambda i: (i, 0))
    return _pallas(
        body, name=f"mix_out_{l}", grid=(ni,),
        in_specs=[tok(D), tok(DA),
                  pl.BlockSpec((tm, D), lambda i: (i, gcol)), pl.BlockSpec((tm, D), lambda i: (i, gcol + 1)),
                  pl.BlockSpec((tm, D), lambda i: (i, gcol + 2)),
                  pl.BlockSpec((NS, DA, NB), lambda i: (0, 0, 0)),
                  pl.BlockSpec((NS, NB, D), lambda i: (0, 0, 0))] + car.specs,
        out_specs=[tok(D), tok(3 * D), tok(D)] + car.specs,
        out_shape=[jax.ShapeDtypeStruct((T, D), F32), jax.ShapeDtypeStruct((T, 3 * D), BF16),
                   jax.ShapeDtypeStruct((T, D), BF16)] + car.out_shape,
        scratch_shapes=car.scratch,
        compiler_params=_cp(1),
    )(x, act, u, u, u, wp, wo, *car.arrs)


def _mix_out_bwd(dxn, y, u, act, m, wp, wo, l, car=_NOTHING):
    T, D = dxn.shape
    tm = _tile(T, 256)
    nt = T // tm
    DA = act.shape[-1]
    DC, DS = DA // 2, DA // 4
    NB = D // NS
    UC = u.shape[-1]
    g_lo = 2 * DC + 4 * DS
    gcol = g_lo // D
    parts = [(0, DC), (DC, DC + DS), (DC + DS, DC + 2 * DS)]

    def body(*refs):
        ((dx_ref, y_ref, g0_ref, g1_ref, g2_ref, act_ref, m_ref, wp_ref, wo_ref),
         (du_ref, dact_ref, gwo_ref, gwp_ref), (acc_wo, acc_wp), carried) = car.split(refs, 9, 4)
        i = pl.program_id(0)
        car.when(i == 0, carried, car.start)

        @pl.when(i == 0)
        def _():
            acc_wo[...] = jnp.zeros_like(acc_wo)
            acc_wp[...] = jnp.zeros_like(acc_wp)

        dxb = dx_ref[...].astype(BF16)
        dm = jnp.concatenate([_nt(dxb, wo_ref[s]) for s in range(NS)], axis=1)
        acc_wo[...] += _tn(m_ref[...], dxb)
        du_ref[:, 0:g_lo] = jnp.zeros((tm, g_lo), BF16)
        for k, (lo, hi) in enumerate(parts):
            sg = _sigmoid((g0_ref, g1_ref, g2_ref)[k][...].astype(F32))
            yk = y_ref[:, k * D:(k + 1) * D].astype(F32)
            du_ref[:, g_lo + k * D:g_lo + (k + 1) * D] = (dm * yk * sg * (1.0 - sg)).astype(BF16)
            dyk = (dm * sg).astype(BF16)
            dk = _nt(dyk[:, 0:NB], wp_ref[0, lo:hi, :])
            for s in range(1, NS):
                dk = dk + _nt(dyk[:, s * NB:(s + 1) * NB], wp_ref[s, lo:hi, :])
            dact_ref[:, lo:hi] = dk
            acc_wp[lo:hi, :] += _tn(act_ref[:, lo:hi], dyk)

        @pl.when(i == nt - 1)
        def _():
            for s in range(NS):
                for hf in range(2):
                    r0 = s * NB + hf * (NB // 2)
                    gwo_ref[hf, s] = acc_wo[r0:r0 + NB // 2, :].astype(BF16)
                    gwp_ref[hf, s] = acc_wp[hf * (DA // 2):(hf + 1) * (DA // 2), s * NB:(s + 1) * NB].astype(BF16)

        car.when(i == nt - 1, carried, car.finish)

    tok = lambda w: pl.BlockSpec((tm, w), lambda i: (i, 0))
    whole = lambda shp: pl.BlockSpec(shp, lambda i: (0,) * len(shp))
    return _pallas(
        body, name=f"mix_out_bwd_{l}", grid=(nt,),
        in_specs=[tok(D), tok(3 * D),
                  pl.BlockSpec((tm, D), lambda i: (i, gcol)), pl.BlockSpec((tm, D), lambda i: (i, gcol + 1)),
                  pl.BlockSpec((tm, D), lambda i: (i, gcol + 2)),
                  tok(DA), tok(D), whole((NS, DA, NB)), whole((NS, NB, D))] + car.specs,
        out_specs=[tok(UC), tok(DA), whole((2, NS, NB // 2, D)), whole((2, NS, DA // 2, NB))] + car.specs,
        out_shape=[jax.ShapeDtypeStruct((T, UC), BF16), jax.ShapeDtypeStruct((T, DA), F32),
                   jax.ShapeDtypeStruct((2, NS, NB // 2, D), BF16),
                   jax.ShapeDtypeStruct((2, NS, DA // 2, NB), BF16)] + car.out_shape,
        scratch_shapes=[pltpu.VMEM((D, D), F32), pltpu.VMEM((DA, D), F32)] + car.scratch,
        compiler_params=_cp(1),
    )(dxn, y, u, u, u, act, m, wp, wo, *car.arrs)


def _mix_seq_bwd(du, u, dact, cv, cdw, lg, lb, sdw, pwblk, ps, bl, l, car=_NOTHING):
    T = u.shape[0]
    S = T // bl
    ts = _tile(S, 256)
    nt = S // ts
    DC, DS = cdw.shape[-1], sdw.shape[-1]
    DA = DC + 2 * DS
    o_ag, o_bg, o_cg, o_bx, o_p, o_end = DC, 2 * DC, 2 * DC + DS, 2 * DC + 2 * DS, 2 * DC + 3 * DS, 2 * DC + 4 * DS
    n_f = ts + HALO

    def body(*refs):
        ((_, up_ref, uc_ref, un_ref, dac_ref, dan_ref, cvc_ref, cvn_ref,
          cdw_ref, lg_ref, lb_ref, sdw_ref, pw_ref, ps_ref),
         (du_ref, gcdw_ref, g512_ref, g256_ref, gpw_ref), _, carried) = car.split(refs, 14, 5)
        b = pl.program_id(0)
        i = pl.program_id(1)
        car.when((b == 0) & (i == 0), carried, car.start)
        keep_p = jnp.where(i > 0, 1.0, 0.0).astype(F32)
        keep_n = jnp.where(i < nt - 1, 1.0, 0.0).astype(F32)

        @pl.when((b == 0) & (i == 0))
        def _():
            gcdw_ref[...] = jnp.zeros_like(gcdw_ref)
            g512_ref[...] = jnp.zeros_like(g512_ref)
            g256_ref[...] = jnp.zeros_like(g256_ref)
            gpw_ref[...] = jnp.zeros_like(gpw_ref)

        def back(lo, hi):
            p = up_ref[ts - HALO:ts, lo:hi].astype(F32) * keep_p
            return jnp.concatenate([p, uc_ref[:, lo:hi].astype(F32)], axis=0)

        def fwd(cur, nxt, lo, hi, mask):
            n = nxt[0:HALO, lo:hi].astype(F32)
            if mask:
                n = n * keep_n
            return jnp.concatenate([cur[:, lo:hi].astype(F32), n], axis=0)

        cvx = fwd(cvc_ref, cvn_ref, 0, DC, False)
        dA = fwd(dac_ref, dan_ref, 0, DC, True)
        mu = jnp.mean(cvx, axis=-1, keepdims=True)
        xc = cvx - mu
        rs = lax.rsqrt(jnp.mean(xc * xc, axis=-1, keepdims=True) + EPS)
        xh = xc * rs
        lnv = xh * lg_ref[...] + lb_ref[...]
        sg = _sigmoid(lnv)
        dln = dA * (sg * (1.0 + lnv * (1.0 - sg)))
        dxh = dln * lg_ref[...]
        dcv = rs * (dxh - jnp.mean(dxh, axis=-1, keepdims=True) - xh * jnp.mean(dxh * xh, axis=-1, keepdims=True))
        g512_ref[0:1, :] += jnp.sum(dcv[0:ts], axis=0, keepdims=True)
        g512_ref[1:2, :] += jnp.sum((dln * xh)[0:ts], axis=0, keepdims=True)
        g512_ref[2:3, :] += jnp.sum(dln[0:ts], axis=0, keepdims=True)

        av = back(0, o_ag)
        sga = _sigmoid(back(o_ag, o_bg))
        glu = av * sga
        dcv_c = dcv[0:ts]
        dglu = jnp.zeros((ts, DC), F32)
        for s in range(CONV_W):
            k = CONV_W - 1 - s
            dglu = dglu + _fwd_shift(dcv, s)[0:ts, :] * cdw_ref[k:k + 1, :]
            gcdw_ref[k:k + 1, :] += jnp.sum(_back(glu, s)[HALO:, :] * dcv_c, axis=0, keepdims=True)
        sga_c = sga[HALO:, :]
        du_ref[:, 0:o_ag] = (dglu * sga_c).astype(BF16)
        du_ref[:, o_ag:o_bg] = (dglu * av[HALO:, :] * sga_c * (1.0 - sga_c)).astype(BF16)

        cg = back(o_cg, o_bx)
        bx = back(o_bx, o_p)
        q = cg * bx
        sc = jnp.zeros((ts, DS), F32)
        for s in range(SHORT_W):
            sc = sc + _back(q, s)[HALO:, :] * sdw_ref[SHORT_W - 1 - s:SHORT_W - s, :]
        dB = fwd(dac_ref, dan_ref, DC, DC + DS, True)
        ds = dB * fwd(uc_ref, un_ref, o_bg, o_cg, False)
        du_ref[:, o_bg:o_cg] = (dB[0:ts] * sc).astype(BF16)
        ds_c = ds[0:ts]
        dq = jnp.zeros((ts, DS), F32)
        for s in range(SHORT_W):
            k = SHORT_W - 1 - s
            dq = dq + _fwd_shift(ds, s)[0:ts, :] * sdw_ref[k:k + 1, :]
            g256_ref[k:k + 1, :] += jnp.sum(_back(q, s)[HALO:, :] * ds_c, axis=0, keepdims=True)
        du_ref[:, o_cg:o_bx] = (dq * bx[HALO:, :]).astype(BF16)
        du_ref[:, o_bx:o_p] = (dq * cg[HALO:, :]).astype(BF16)

        p = back(o_p, o_end)
        lane, wl = _pool_lane_window(DS)
        sums, cur, sh = [], p, 1
        for _ in POOL_WINDOWS:
            cur = cur + _back(cur, sh)
            sums.append(cur[HALO:, :])
            sh *= 2
        pos_c = (i * ts + lax.broadcasted_iota(jnp.int32, (ts, 1), 0) + 1).astype(F32)
        pooled = (_pool_select(lane, sums) / jnp.minimum(pos_c, wl) - p[HALO:, :]).astype(BF16)
        pwv = _nn(pooled, pw_ref[...])
        dC = fwd(dac_ref, dan_ref, DC + DS, DA, True)
        g256_ref[SHORT_W:SHORT_W + 1, :] += jnp.sum(dC[0:ts] * pwv, axis=0, keepdims=True)
        dpw = (dC * ps_ref[...]).astype(BF16)
        gpw_ref[...] += _tn(pooled, dpw[0:ts])
        dpl = _nt(dpw, pw_ref[...])
        pos_f = (i * ts + lax.broadcasted_iota(jnp.int32, (n_f, 1), 0) + 1).astype(F32)
        e = dpl / jnp.minimum(pos_f, wl)
        fsums, cur, sh = [], e, 1
        for _ in POOL_WINDOWS:
            cur = cur + _fwd_shift(cur, sh)
            fsums.append(cur[0:ts, :])
            sh *= 2
        du_ref[:, o_p:o_end] = (_pool_select(lane, fsums) - dpl[0:ts]).astype(BF16)
        car.when((b == bl - 1) & (i == nt - 1), carried, car.finish)

    full = lambda a: pl.BlockSpec(a.shape, lambda b, i: (0,) * a.ndim)
    row = lambda w, f: pl.BlockSpec((ts, w), lambda b, i: (b * nt + f(i), 0))
    prv = lambda i: jnp.maximum(i - 1, 0)
    nxt = lambda i: jnp.minimum(i + 1, nt - 1)
    cur = lambda i: i
    return _pallas(
        body, name=f"mix_seq_bwd_{l}", grid=(bl, nt),
        in_specs=[ANY, row(o_end, prv), row(o_end, cur), row(o_end, nxt),
                  row(DA, cur), row(DA, nxt), row(DC, cur), row(DC, nxt),
                  full(cdw), full(lg), full(lb), full(sdw), full(pwblk), full(ps)] + car.specs,
        out_specs=[row(o_end, cur), full(cdw),
                   pl.BlockSpec((8, DC), lambda b, i: (0, 0)), pl.BlockSpec((8, DS), lambda b, i: (0, 0)),
                   full(pwblk)] + car.specs,
        out_shape=[jax.ShapeDtypeStruct(du.shape, BF16), jax.ShapeDtypeStruct(cdw.shape, F32),
                   jax.ShapeDtypeStruct((8, DC), F32), jax.ShapeDtypeStruct((8, DS), F32),
                   jax.ShapeDtypeStruct(pwblk.shape, F32)] + car.out_shape,
        scratch_shapes=car.scratch,
        input_output_aliases={0: 0},
        compiler_params=_cp(2),
    )(du, u, u, u, dact, dact, cv, cv, cdw, lg, lb, sdw, pwblk, ps, *car.arrs)


def _mix_in_bwd(x, g, dxn, du, win, l):
    T, D = x.shape
    C4 = win.shape[-1]
    tm = _tile(T, 512)

    def body(x_ref, g_ref, dxn_ref, du_ref, w_ref, dx_ref, dg_ref, dh):
        i = pl.program_id(0)
        j = pl.program_id(1)

        @pl.when(j == 0)
        def _():
            dh[...] = jnp.zeros_like(dh)

        @pl.when((i == 0) & (j == 0))
        def _():
            dg_ref[...] = jnp.zeros_like(dg_ref)

        dh[...] += _nt(du_ref[...], w_ref[...])

        @pl.when(j == NS - 1)
        def _():
            dxr, dg = _rms_bwd(dh[...], x_ref[...], g_ref[...])
            dx_ref[...] = dxn_ref[...] + dxr
            dg_ref[...] += dg

    tok = pl.BlockSpec((tm, D), lambda i, j: (i, 0))
    vec = pl.BlockSpec((1, D), lambda i, j: (0, 0))
    return _pallas(
        body, name=f"mix_in_bwd_{l}", grid=(T // tm, NS),
        in_specs=[tok, vec, tok, pl.BlockSpec((tm, C4), lambda i, j: (i, j)),
                  pl.BlockSpec((None, D, C4), lambda i, j: (j, 0, 0))],
        out_specs=[tok, vec],
        out_shape=[jax.ShapeDtypeStruct((T, D), F32), jax.ShapeDtypeStruct((1, D), F32)],
        scratch_shapes=[pltpu.VMEM((tm, D), F32)],
        compiler_params=_cp(2),
    )(x, g, dxn, du, win)


def _mix_in_dw(h, du, l):
    T, D = h.shape
    C4 = du.shape[-1] // NS
    tk = _tile(T, 512)
    nt = T // tk

    def body(h_ref, du_ref, g_ref, acc):
        t = pl.program_id(1)

        @pl.when(t == 0)
        def _():
            acc[...] = jnp.zeros_like(acc)

        acc[...] += _tn(h_ref[...], du_ref[...])

        @pl.when(t == nt - 1)
        def _():
            g_ref[0] = acc[0:D // 2, :].astype(BF16)
            g_ref[1] = acc[D // 2:D, :].astype(BF16)

    return _pallas(
        body, name=f"mix_in_dw_{l}", grid=(NS, nt),
        in_specs=[pl.BlockSpec((tk, D), lambda s, t: (t, 0)), pl.BlockSpec((tk, C4), lambda s, t: (t, s))],
        out_specs=pl.BlockSpec((2, None, D // 2, C4), lambda s, t: (0, s, 0, 0)),
        out_shape=jax.ShapeDtypeStruct((2, NS, D // 2, C4), BF16),
        scratch_shapes=[pltpu.VMEM((D, C4), F32)],
        compiler_params=_cp(2),
    )(h, du)


def _loss_head(x, g, target):
    T, D = x.shape
    tm = _tile(T, 512)

    def body(x_ref, g_ref, t_ref, dx_ref, loss_ref, dg_ref):
        @pl.when(pl.program_id(0) == 0)
        def _():
            loss_ref[...] = jnp.zeros_like(loss_ref)
            dg_ref[...] = jnp.zeros_like(dg_ref)

        xv = x_ref[...]
        xh, rs = _rms_stats(xv)
        gv = g_ref[...]
        e = xh * gv - t_ref[...]
        loss_ref[...] += 0.5 * jnp.sum(jnp.mean(e * e, axis=-1, keepdims=True))
        dy = e * (1.0 / D)
        dyg = dy * gv
        dx_ref[...] = rs * (dyg - xh * jnp.mean(dyg * xh, axis=-1, keepdims=True))
        dg_ref[...] += jnp.sum(dy * xh, axis=0, keepdims=True)

    tok = pl.BlockSpec((tm, D), lambda i: (i, 0))
    vec = pl.BlockSpec((1, D), lambda i: (0, 0))
    return _pallas(
        body, name="loss_head", grid=(T // tm,),
        in_specs=[tok, vec, tok],
        out_specs=[tok, pl.BlockSpec((8, 128), lambda i: (0, 0)), vec],
        out_shape=[jax.ShapeDtypeStruct((T, D), F32), jax.ShapeDtypeStruct((8, 128), F32),
                   jax.ShapeDtypeStruct((1, D), F32)],
        compiler_params=_cp(1),
    )(x, g, target)


def _block_diag(pw):
    G, c, _ = pw.shape
    out = jnp.zeros((G * c, G * c), pw.dtype)
    for gi in range(G):
        out = lax.dynamic_update_slice(out, pw[gi], (gi * c, gi * c))
    return out


def _pad_rows(a, n):
    return jnp.pad(a, ((0, n - a.shape[0]), (0, 0)))


def _merge(g):
    return g.reshape(g.shape[0], g.shape[1] * g.shape[2], g.shape[3])


def _split_dws(dws_g, cw, sw):
    cdw = jnp.transpose(dws_g[:, 0:CONV_W, 0:cw], (1, 0, 2)).reshape(CONV_W, NS * cw)
    sdw = jnp.transpose(dws_g[:, 32:32 + SHORT_W, 0:sw], (1, 0, 2)).reshape(SHORT_W, NS * sw)
    return cdw, sdw


def _fwd_bwd(x3, target3, shards, small, dw_widths, first, sum_block):
    bl, S, D = x3.shape
    T = bl * S
    x = x3.reshape(T, D)
    target = target3.reshape(T, D)
    L = len(shards)
    cw, sw = dw_widths
    row = lambda v: v[None, :]
    gather = lambda arrs: _Carried("gather", arrs)
    exchange = lambda arrs: _Carried("exchange", arrs)
    scatter = lambda arrs: _Carried("scatter", arrs)

    saved = []
    wa1, wb1 = first
    for l in range(L):
        sh = shards[l]
        nxt = shards[l + 1] if l + 1 < L else None
        sp = dict(pwblk=_block_diag(small["pool_w"][l]).astype(BF16), cb=row(small["conv_b"][l]),
                  lg=row(small["conv_ln_g"][l]), lb=row(small["conv_ln_b"][l]), ps=row(small["pool_scale"][l]),
                  g1=row(small["norm_ffn1_g"][l]), gm=row(small["norm_mix_g"][l]), g2=row(small["norm_ffn2_g"][l]))
        x0 = x
        x1, h1, p1, q1, z1, win_g, wp_g, wo_g, dws_g = _ffn_fwd(x0, sp["g1"], wa1, wb1, l, 0, gather(sh["mx"]))
        win_g, wp_g, wo_g = _merge(win_g), _merge(wp_g), _merge(wo_g)
        cdw, sdw = _split_dws(_merge(dws_g), cw, sw)
        sp["cdw"], sp["sdw"] = _pad_rows(cdw, HALO), _pad_rows(sdw, 8)
        hm, u, wa2 = _mix_in(x1, sp["gm"], win_g, l, gather([sh["f2a"]]))
        act, cv, wb2 = _mix_seq_fwd(u, sp["cdw"], sp["cb"], sp["lg"], sp["lb"], sp["sdw"], sp["pwblk"], sp["ps"],
                                    bl, l, gather([sh["f2b"]]))
        wb2 = _merge(wb2)
        res_o = _mix_out(x1, act, u, wp_g, wo_g, l, gather([nxt["f1b"]]) if nxt else _NOTHING)
        x2, y, m = res_o[:3]
        res_f = _ffn_fwd(x2, sp["g2"], wa2, wb2, l, 1, gather([nxt["f1a"]]) if nxt else _NOTHING)
        x, h2, p2, q2, z2 = res_f[:5]
        saved.append(dict(sp=sp, x0=x0, x1=x1, x2=x2, h1=h1, p1=p1, q1=q1, z1=z1, hm=hm, u=u, act=act, cv=cv, y=y, m=m,
                          h2=h2, p2=p2, q2=q2, z2=z2, wa1=wa1, wb1=wb1, wa2=wa2, wb2=wb2, win=win_g, wp=wp_g, wo=wo_g))
        if nxt:
            wa1, wb1 = res_f[5], _merge(res_o[3])

    dx, loss_blk, dgf = _loss_head(x, row(small["final_norm_g"]), target)
    loss = loss_blk[0, 0]

    sg = {k: [None] * L for k in ("norm_ffn1_g", "norm_mix_g", "norm_ffn2_g", "conv_dw", "conv_b", "conv_ln_g",
                                  "conv_ln_b", "short_dw", "pool_w", "pool_scale")}
    blocks = []
    g_up, l_up = [], None
    G, c = small["pool_w"].shape[1:3]
    for l in reversed(range(L)):
        sv = saved[l]
        sp = sv["sp"]
        res = _ffn_bwd(sv["x2"], sp["g2"], dx, sv["p2"], sv["q2"], sv["wa2"], sv["wb2"], l, 1, exchange(g_up))
        dx, dg2, da, db, dyb = res[:5]
        p_up = sum_block(g_up, res[5:]) if g_up else []
        res = _ffn_dw(sv["h2"], dyb, da, db, sv["z2"], l, 1, scatter(p_up))
        g_f2 = [res[0]]
        if p_up:
            blocks.append((l_up, "f1", p_up, res[1:]))
        res = _mix_out_bwd(dx, sv["y"], sv["u"], sv["act"], sv["m"], sv["wp"], sv["wo"], l, exchange(g_f2))
        du, dact, g_o, g_p = res[:4]
        p_f2 = sum_block(g_f2, res[4:])
        res = _mix_seq_bwd(du, sv["u"], dact, sv["cv"], sp["cdw"], sp["lg"], sp["lb"],
                           sp["sdw"], sp["pwblk"], sp["ps"], bl, l, scatter(p_f2))
        du, gcdw, g512, g256, gpw = res[:5]
        blocks.append((l, "f2", p_f2, res[5:]))
        dx, dgm = _mix_in_bwd(sv["x1"], sp["gm"], dx, du, sv["win"], l)
        g_mx = [_mix_in_dw(sv["hm"], du, l), g_p, g_o]
        if l > 0:
            res = _ffn_bwd(sv["x0"], sp["g1"], dx, sv["p1"], sv["q1"], sv["wa1"], sv["wb1"], l, 0, exchange(g_mx))
            dx, dg1, da, db, dyb = res[:5]
            p_mx = sum_block(g_mx, res[5:])
            res = _ffn_dw(sv["h1"], dyb, da, db, sv["z1"], l, 0, scatter(p_mx))
            blocks.append((l, "mx", p_mx, res[1:]))
            g_up, l_up = [res[0]], l
        else:
            res = _ffn_bwd_first(dx, sv["p1"], sv["q1"], sv["wb1"], l, 0, exchange(g_mx))
            da, db, dyb = res[:3]
            p_mx = sum_block(g_mx, res[3:])
            res = _ffn_dw(sv["h1"], dyb, da, db, sv["z1"], l, 0, scatter(p_mx))
            blocks.append((l, "mx", p_mx, res[1:]))
            g_f1 = [res[0]]
            p_f1 = sum_block(g_f1, _run_carried(exchange(g_f1), "last"))
            res = _ffn_bwd_second(sv["x0"], sp["g1"], dx, da, db, sv["wa1"], l, 0, scatter(p_f1))
            dx, dg1 = res[:2]
            blocks.append((l, "f1", p_f1, res[2:]))
        sg["norm_ffn1_g"][l], sg["norm_mix_g"][l], sg["norm_ffn2_g"][l] = dg1[0], dgm[0], dg2[0]
        sg["conv_dw"][l] = gcdw[:CONV_W]
        sg["conv_b"][l], sg["conv_ln_g"][l], sg["conv_ln_b"][l] = g512[0], g512[1], g512[2]
        sg["short_dw"][l] = g256[:SHORT_W]
        sg["pool_scale"][l] = g256[SHORT_W]
        sg["pool_w"][l] = jnp.stack([gpw[gi * c:(gi + 1) * c, gi * c:(gi + 1) * c] for gi in range(G)])
    small_g = {k: jnp.stack(v) for k, v in sg.items()}
    small_g["final_norm_g"] = dgf[0]
    return loss, dx.reshape(bl, S, D), blocks, small_g


def _share_final(fs):
    n = len(fs)
    L = fs[0].shape[0]

    def body(*refs):
        outs = refs[n:2 * n]
        send_sems, recv_sems = refs[2 * n:]
        x, y, c, _ = _place()
        sib = (x, y, 1 - c)
        cps = []
        for ai in range(n):
            for l in range(L):
                cp = pltpu.make_async_remote_copy(src_ref=outs[ai].at[l, c], dst_ref=outs[ai].at[l, c],
                                                  send_sem=send_sems.at[ai, l], recv_sem=recv_sems.at[ai, l],
                                                  device_id=sib, device_id_type=MESH)
                cp.start()
                cps.append(cp)
        for ai in range(n):
            for l in range(L):
                blk = outs[ai].at[l, 1 - c]
                pltpu.make_async_remote_copy(src_ref=blk, dst_ref=blk, send_sem=send_sems.at[ai, l],
                                             recv_sem=recv_sems.at[ai, l], device_id=sib, device_id_type=MESH).wait_recv()
        for cp in cps:
            cp.wait_send()

    return _pallas(
        body, name="grad_share_final",
        in_specs=[ANY] * n, out_specs=[ANY] * n,
        out_shape=[jax.ShapeDtypeStruct(f.shape, f.dtype) for f in fs],
        scratch_shapes=[pltpu.SemaphoreType.DMA((n, L)), pltpu.SemaphoreType.DMA((n, L))],
        input_output_aliases={i: i for i in range(n)},
        compiler_params=pltpu.CompilerParams(has_side_effects=True),
    )(*fs)


def _all_reduce_small(v):
    R, W = v.shape

    def body(v_ref, out_ref, buf, send_sems, recv_sems):
        x, y, c, _ = _place()
        me = 4 * x + 2 * y + c
        buf[me] = v_ref[...]
        cps = []
        for k in range(1, 8):
            kx, ky, kc = (k >> 2) & 1, (k >> 1) & 1, k & 1
            to = (1 - x if kx else x, 1 - y if ky else y, 1 - c if kc else c)
            cp = pltpu.make_async_remote_copy(src_ref=v_ref, dst_ref=buf.at[me], send_sem=send_sems.at[k - 1],
                                              recv_sem=recv_sems.at[k - 1], device_id=to, device_id_type=MESH)
            cp.start()
            cps.append(cp)
        for cp in cps:
            cp.wait()
        acc = buf[0]
        for d in range(1, 8):
            acc = acc + buf[d]
        out_ref[...] = acc

    return _pallas(
        body, name="all_reduce_small",
        in_specs=[pl.BlockSpec(memory_space=pltpu.VMEM)], out_specs=pl.BlockSpec(memory_space=pltpu.VMEM),
        out_shape=jax.ShapeDtypeStruct((R, W), F32),
        scratch_shapes=[pltpu.VMEM((8, R, W), F32), pltpu.SemaphoreType.DMA((7,)), pltpu.SemaphoreType.DMA((7,))],
        compiler_params=pltpu.CompilerParams(has_side_effects=True, vmem_limit_bytes=VMEM_LIMIT),
    )(v)


def _row_tile(n, w, streams):
    for t in (1056, 1024, 704, 512, 352, 256, 128, 64, 32, 16):
        if n % t == 0 and 2 * streams * t * w * 4 <= VMEM_LIMIT // 2:
            return t
    raise ValueError((n, w))


def _sum_sibling(tag, cidx, g, r):
    _, N, W = g.shape
    tr = _row_tile(N, W, 3)

    def body(c_ref, g_ref, r_ref, o_ref):
        del c_ref
        o_ref[...] = (g_ref[...].astype(F32) + r_ref[...].astype(F32)).astype(BF16)

    return _pallas(
        body, name=f"grad_sum_sibling_{tag}",
        grid_spec=pltpu.PrefetchScalarGridSpec(
            num_scalar_prefetch=1, grid=(N // tr,),
            in_specs=[pl.BlockSpec((None, tr, W), lambda i, c: (c[0], i, 0)),
                      pl.BlockSpec((tr, W), lambda i, c: (i, 0))],
            out_specs=pl.BlockSpec((tr, W), lambda i, c: (i, 0))),
        out_shape=jax.ShapeDtypeStruct((N, W), BF16),
        compiler_params=_cp(1),
    )(cidx, g, r)


def _sum_final(tag, idx, p, r2, l, L, prev):
    _, r, W = p.shape
    tr = _row_tile(r, W, 5)

    def body(*refs):
        p_ref, r2_ref = refs[1:3]
        o_ref = refs[-1]
        acc = p_ref[...].astype(F32)
        for k in range(3):
            acc = acc + r2_ref[k].astype(F32)
        o_ref[...] = acc

    in_specs = [pl.BlockSpec((None, tr, W), lambda i, s: (s[1], i, 0)),
                pl.BlockSpec((3, tr, W), lambda i, s: (0, i, 0))]
    args = [idx, p, r2]
    aliases = {}
    if prev is not None:
        in_specs.append(ANY)
        args.append(prev)
        aliases = {3: 0}
    return _pallas(
        body, name=f"grad_sum_final_{tag}",
        grid_spec=pltpu.PrefetchScalarGridSpec(
            num_scalar_prefetch=1, grid=(r // tr,), in_specs=in_specs,
            out_specs=pl.BlockSpec((None, None, tr, W), lambda i, s: (l, s[0], i, 0))),
        out_shape=jax.ShapeDtypeStruct((L, 2, r, W), F32),
        input_output_aliases=aliases,
        compiler_params=_cp(1),
    )(*args)


def _adam_math(w, g, m, v):
    m = ADAM_B1 * m + (1.0 - ADAM_B1) * g
    v = ADAM_B2 * v + (1.0 - ADAM_B2) * (g * g)
    m_hat = m / (1.0 - ADAM_B1 ** ADAM_STEP)
    v_hat = v / (1.0 - ADAM_B2 ** ADAM_STEP)
    delta = -ADAM_LR * (m_hat / (jnp.sqrt(v_hat) + ADAM_EPS) + ADAM_WD * w)
    return delta, m, v


def _adam_big(name, w, m, v, gfull, row0):
    L, r, W = w.shape
    tr = _row_tile(r, W, 8)
    assert row0 % tr == 0
    off = row0 // tr

    def body(w_ref, m_ref, v_ref, g_ref, go_ref, d_ref, mo_ref, vo_ref):
        g = g_ref[...]
        d, mn, vn = _adam_math(w_ref[...], g, m_ref[...], v_ref[...])
        go_ref[...] = g
        d_ref[...] = d
        mo_ref[...] = mn
        vo_ref[...] = vn

    blk = pl.BlockSpec((None, tr, W), lambda l, i: (l, i, 0))
    shp = jax.ShapeDtypeStruct(w.shape, F32)
    return _pallas(
        body, name=f"adam_{name}", grid=(L, r // tr),
        in_specs=[blk, blk, blk, pl.BlockSpec((None, tr, W), lambda l, i: (l, off + i, 0))],
        out_specs=[blk] * 4, out_shape=[shp] * 4,
        compiler_params=_cp(2),
    )(w, m, v, gfull)


def _adam_small(ws, gs, ms, vs):
    n = len(ws)

    def body(*refs):
        for k in range(n):
            w_ref, g_ref, m_ref, v_ref = (refs[j * n + k] for j in range(4))
            d_ref, mo_ref, vo_ref = (refs[(4 + j) * n + k] for j in range(3))
            d, mn, vn = _adam_math(w_ref[...], g_ref[...], m_ref[...], v_ref[...])
            d_ref[...] = d
            mo_ref[...] = mn
            vo_ref[...] = vn

    spec = pl.BlockSpec(memory_space=pltpu.VMEM)
    shp = [jax.ShapeDtypeStruct(w.shape, F32) for w in ws]
    out = _pallas(body, name="adam_small", in_specs=[spec] * (4 * n), out_specs=[spec] * (3 * n),
                  out_shape=shp * 3)(*ws, *gs, *ms, *vs)
    return out[:n], out[n:2 * n], out[2 * n:]


_WEIGHTS = ['norm_ffn1_g', 'ffn1_w_gate', 'ffn1_w_up', 'ffn1_w_down', 'norm_mix_g', 'w_in', 'conv_dw', 'conv_b',
            'conv_ln_g', 'conv_ln_b', 'w_pa', 'short_dw', 'w_pb', 'pool_w', 'pool_scale', 'w_pc', 'w_o',
            'norm_ffn2_g', 'ffn2_w_gate', 'ffn2_w_up', 'ffn2_w_down', 'final_norm_g']
_BIG = ('ffn1_w_gate', 'ffn1_w_up', 'ffn1_w_down', 'w_in', 'w_pa', 'w_pb', 'w_pc', 'w_o',
        'ffn2_w_gate', 'ffn2_w_up', 'ffn2_w_down')
_TRANSPOSED = ('ffn1_w_gate', 'ffn1_w_up', 'ffn2_w_gate', 'ffn2_w_up')
_SMALL = tuple(n for n in _WEIGHTS if n not in _BIG)
_SMALL_REDUCED = ('norm_ffn1_g', 'norm_mix_g', 'conv_b', 'conv_ln_g', 'conv_ln_b', 'pool_w', 'pool_scale',
                  'norm_ffn2_g', 'final_norm_g', 'conv_dw', 'short_dw')


def _pack(arrs, rows_multiple=8):
    flat = jnp.concatenate([a.reshape(-1) for a in arrs])
    n = flat.shape[0]
    per = 128 * rows_multiple
    padded = -(-n // per) * per
    return jnp.pad(flat, (0, padded - n)).reshape(padded // 128, 128)


def _unpack(buf, shapes):
    flat = buf.reshape(-1)
    out, o = [], 0
    for s in shapes:
        k = 1
        for d in s:
            k *= d
        out.append(flat[o:o + k].reshape(s))
        o += k
    return out


def _halves(a):
    return a.reshape(2, a.shape[0] // 2, a.shape[1])


def _step(P, M, V, x, loss_target):
    tr = lambda a: jnp.transpose(a, (0, 2, 1))
    bf = lambda a: a.astype(BF16)
    L = P['w_in'].shape[0]
    cw, sw = P['conv_dw'].shape[-1], P['short_dw'].shape[-1]
    ffa = [jnp.stack([bf(tr(P[f'ffn{f}_w_gate'])), bf(tr(P[f'ffn{f}_w_up']))], axis=1) for f in (1, 2)]
    ffb = [bf(P[f'ffn{f}_w_down']) for f in (1, 2)]
    win = bf(P['w_in'])
    wp = jnp.concatenate([bf(P['w_pa']), bf(P['w_pb']), bf(P['w_pc'])], axis=1)
    wo = bf(P['w_o'])
    dws = jnp.zeros((L, 64, 128), F32)
    dws = dws.at[:, 0:CONV_W, 0:cw].set(P['conv_dw']).at[:, 32:32 + SHORT_W, 0:sw].set(P['short_dw'])
    shards = [dict(f1a=ffa[0][l], f1b=_halves(ffb[0][l]), f2a=ffa[1][l], f2b=_halves(ffb[1][l]),
                   mx=(_halves(win[l]), _halves(wp[l]), _halves(wo[l]), _halves(dws[l]))) for l in range(L)]

    xi, yi, ci = lax.axis_index("x"), lax.axis_index("y"), lax.axis_index("c")
    chip = 2 * xi + yi
    cidx = jnp.stack([ci]).astype(jnp.int32)
    idx = jnp.stack([ci, chip]).astype(jnp.int32)
    count = [0]

    def sum_block(gs, r1):
        t0 = count[0]
        count[0] += len(gs)
        parts = []
        for k, (g, r) in enumerate(zip(gs, r1)):
            W = g.shape[-1]
            p = _sum_sibling(t0 + k, cidx, g.reshape(2, -1, W), r.reshape(-1, W))
            parts.append(p.reshape(g.shape[1:]))
        return parts

    wa1, wb1 = _run_carried(_Carried("gather", [shards[0]["f1a"], shards[0]["f1b"]]), "first")
    small = {n: P[n] for n in _SMALL if n not in ('conv_dw', 'short_dw')}
    loss, dx, blocks, small_g = _fwd_bwd(x, loss_target, shards, small, (cw, sw), (wa1, _merge(wb1)), sum_block)

    finals = {}
    for t, (l, name, parts, r2) in enumerate(blocks):
        prev = finals.get(name, [None] * len(parts))
        finals[name] = [_sum_final(f"{t}_{k}", idx, p, r, l, L, pv) for k, (p, r, pv) in enumerate(zip(parts, r2, prev))]
    shared = _share_final(finals["f1"] + finals["f2"] + finals["mx"])
    f_f1, f_f2, f_in, f_p, f_o = [f.reshape(L, -1, f.shape[-1]) for f in shared]

    tot = _all_reduce_small(_pack([small_g[n] for n in _SMALL_REDUCED] + [loss.reshape(1)]))
    *tot, loss = _unpack(tot, [small_g[n].shape for n in _SMALL_REDUCED] + [()])
    tot = dict(zip(_SMALL_REDUCED, tot))
    tot['conv_dw'] = lax.dynamic_slice_in_dim(tot['conv_dw'], chip * cw, cw, axis=2)
    tot['short_dw'] = lax.dynamic_slice_in_dim(tot['short_dw'], chip * sw, sw, axis=2)

    F4 = P['ffn1_w_down'].shape[1]
    dc, ds = P['w_pa'].shape[1], P['w_pb'].shape[1]
    src = {'ffn1_w_gate': (f_f1, 0), 'ffn1_w_up': (f_f1, F4), 'ffn1_w_down': (f_f1, 2 * F4),
           'ffn2_w_gate': (f_f2, 0), 'ffn2_w_up': (f_f2, F4), 'ffn2_w_down': (f_f2, 2 * F4),
           'w_in': (f_in, 0), 'w_pa': (f_p, 0), 'w_pb': (f_p, dc), 'w_pc': (f_p, dc + ds), 'w_o': (f_o, 0)}
    grads, deltas, new_m, new_v = {}, {}, {}, {}
    for n in _BIG:
        gfull, row0 = src[n]
        if n in _TRANSPOSED:
            outs = _adam_big(n, tr(P[n]), tr(M[n]), tr(V[n]), gfull, row0)
            grads[n], deltas[n], new_m[n], new_v[n] = [tr(o) for o in outs]
        else:
            grads[n], deltas[n], new_m[n], new_v[n] = _adam_big(n, P[n], M[n], V[n], gfull, row0)
    as2d = lambda a: a.reshape(1, -1) if a.ndim == 1 else a
    d_s, m_s, v_s = _adam_small([as2d(P[n]) for n in _SMALL], [as2d(tot[n]) for n in _SMALL],
                                [as2d(M[n]) for n in _SMALL], [as2d(V[n]) for n in _SMALL])
    for n, d, mm, vv in zip(_SMALL, d_s, m_s, v_s):
        shp = P[n].shape
        grads[n], deltas[n], new_m[n], new_v[n] = tot[n], d.reshape(shp), mm.reshape(shp), vv.reshape(shp)

    return (loss, dx, *[grads[n] for n in _WEIGHTS], *[deltas[n] for n in _WEIGHTS],
            *[new_m[n] for n in _WEIGHTS], *[new_v[n] for n in _WEIGHTS])


def kernel(x, norm_ffn1_g, ffn1_w_gate, ffn1_w_up, ffn1_w_down, norm_mix_g, w_in, conv_dw, conv_b, conv_ln_g, conv_ln_b, w_pa, short_dw, w_pb, pool_w, pool_scale, w_pc, w_o, norm_ffn2_g, ffn2_w_gate, ffn2_w_up, ffn2_w_down, final_norm_g, loss_target, m_norm_ffn1_g, m_ffn1_w_gate, m_ffn1_w_up, m_ffn1_w_down, m_norm_mix_g, m_w_in, m_conv_dw, m_conv_b, m_conv_ln_g, m_conv_ln_b, m_w_pa, m_short_dw, m_w_pb, m_pool_w, m_pool_scale, m_w_pc, m_w_o, m_norm_ffn2_g, m_ffn2_w_gate, m_ffn2_w_up, m_ffn2_w_down, m_final_norm_g, v_norm_ffn1_g, v_ffn1_w_gate, v_ffn1_w_up, v_ffn1_w_down, v_norm_mix_g, v_w_in, v_conv_dw, v_conv_b, v_conv_ln_g, v_conv_ln_b, v_w_pa, v_short_dw, v_w_pb, v_pool_w, v_pool_scale, v_w_pc, v_w_o, v_norm_ffn2_g, v_ffn2_w_gate, v_ffn2_w_up, v_ffn2_w_down, v_final_norm_g):
    args = locals()
    P = {n: args[n] for n in _WEIGHTS}
    M = {n: args["m_" + n] for n in _WEIGHTS}
    V = {n: args["v_" + n] for n in _WEIGHTS}
    return _step(P, M, V, x, loss_target)
```

```python
import jax
import jax.numpy as jnp
from jax import lax
from jax.experimental import pallas as pl
from jax.experimental.pallas import tpu as pltpu

F32 = jnp.float32
BF16 = jnp.bfloat16
EPS = 1e-6
NS = 4
CONV_W = 31
SHORT_W = 3
POOL_WINDOWS = (2, 4, 8, 16)
HALO = 32
ADAM_LR, ADAM_B1, ADAM_B2, ADAM_EPS, ADAM_WD, ADAM_STEP = 0.001, 0.9, 0.999, 1e-08, 0.01, 10
MESH = pl.DeviceIdType.MESH
ANY = pl.BlockSpec(memory_space=pl.ANY)
VMEM_LIMIT = 56 * 1024 * 1024
FFN_BWD_SUBTILES = 2


def _pallas(body, **kw):
    return pl.pallas_call(body, **kw)


def _cp(n_axes):
    return pltpu.CompilerParams(dimension_semantics=("arbitrary",) * n_axes, vmem_limit_bytes=VMEM_LIMIT)


def _nn(a, b):
    return jnp.dot(a, b, preferred_element_type=F32)


def _nt(a, b):
    return lax.dot_general(a, b, (((1,), (1,)), ((), ())), preferred_element_type=F32)


def _tn(a, b):
    return lax.dot_general(a, b, (((0,), (0,)), ((), ())), preferred_element_type=F32)


def _sigmoid(v):
    return 1.0 / (1.0 + jnp.exp(-v))


def _rms_stats(x):
    rs = lax.rsqrt(jnp.mean(x * x, axis=-1, keepdims=True) + EPS)
    return x * rs, rs


def _rms_bwd(dh, x, g):
    xh, rs = _rms_stats(x)
    dhg = dh * g
    dx = rs * (dhg - xh * jnp.mean(dhg * xh, axis=-1, keepdims=True))
    return dx, jnp.sum(dh * xh, axis=0, keepdims=True)


def _tile(n, pref):
    t = min(n, pref)
    assert n % t == 0, (n, t)
    return t


def _place():
    x, y, c = lax.axis_index("x"), lax.axis_index("y"), lax.axis_index("c")
    chips = [(1 - x, y), (x, 1 - y), (1 - x, 1 - y)]
    return x, y, c, chips


def _gather_copies(ins, outs, sems):
    send_sems, recv_sems, local_sems = sems
    x, y, c, _ = _place()
    me, at_x, at_y, diag = 2 * x + y, 2 * (1 - x) + y, 2 * x + (1 - y), 2 * (1 - x) + (1 - y)
    to_x, to_y, sib = (1 - x, y, c), (x, 1 - y, c), (x, y, 1 - c)

    def remote(ai, k, blk, to, src=None):
        return pltpu.make_async_remote_copy(src_ref=blk if src is None else src, dst_ref=blk,
                                            send_sem=send_sems.at[ai, k], recv_sem=recv_sems.at[ai, k],
                                            device_id=to, device_id_type=MESH)

    g = dict(local=[], first=[], landed=[], relay=[], relayed=[], passed=[], passed_diag=[], from_sib=[])
    for ai in range(len(ins)):
        o = outs[ai]
        h = ins[ai].shape[1] // 2
        lo, hi = pl.ds(0, h), pl.ds(h, h)
        g["local"].append(pltpu.make_async_copy(ins[ai], o.at[me], local_sems.at[ai]))
        g["first"] += [remote(ai, 0, o.at[me, c], to_x, src=ins[ai].at[c]),
                       remote(ai, 1, o.at[me, c], to_y, src=ins[ai].at[c])]
        g["landed"] += [remote(ai, 0, o.at[at_x, c], to_x), remote(ai, 1, o.at[at_y, c], to_y)]
        g["relay"] += [remote(ai, 2, o.at[at_x, c, lo], to_y), remote(ai, 3, o.at[at_y, c, hi], to_x)]
        g["relayed"] += [remote(ai, 2, o.at[diag, c, lo], to_y), remote(ai, 3, o.at[diag, c, hi], to_x)]
        g["passed"] += [remote(ai, 4, o.at[at_x, c], sib), remote(ai, 5, o.at[at_y, c], sib)]
        g["passed_diag"].append(remote(ai, 6, o.at[diag, c], sib))
        g["from_sib"] += [remote(ai, 4, o.at[at_x, 1 - c], sib), remote(ai, 5, o.at[at_y, 1 - c], sib),
                          remote(ai, 6, o.at[diag, 1 - c], sib)]
    return g


def _gather_start(ins, outs, sems):
    g = _gather_copies(ins, outs, sems)
    for cp in g["local"] + g["first"]:
        cp.start()


def _gather_middle(ins, outs, sems):
    g = _gather_copies(ins, outs, sems)
    for arrive, fwd, on in zip(g["landed"], g["passed"], g["relay"]):
        arrive.wait_recv()
        fwd.start()
        on.start()


def _gather_finish(ins, outs, sems):
    g = _gather_copies(ins, outs, sems)
    n = len(g["passed_diag"])
    for ai in range(n):
        g["relayed"][2 * ai].wait_recv()
        g["relayed"][2 * ai + 1].wait_recv()
        g["passed_diag"][ai].start()
    for cp in g["from_sib"]:
        cp.wait_recv()
    for cp in g["first"] + g["relay"] + g["passed"] + g["passed_diag"]:
        cp.wait_send()
    for cp in g["local"]:
        cp.wait()


def _scatter_copies(ins, outs, sems):
    send_sems, recv_sems = sems
    x, y, c, chips = _place()
    return [pltpu.make_async_remote_copy(
        src_ref=ins[ai].at[2 * chip[0] + chip[1]], dst_ref=outs[ai].at[k],
        send_sem=send_sems.at[ai, k], recv_sem=recv_sems.at[ai, k], device_id=(*chip, c), device_id_type=MESH)
        for ai in range(len(ins)) for k, chip in enumerate(chips)]


def _exchange_copies(ins, outs, sems):
    send_sems, recv_sems = sems
    x, y, c, _ = _place()
    return [pltpu.make_async_remote_copy(
        src_ref=ins[ai].at[1 - c], dst_ref=outs[ai], send_sem=send_sems.at[ai], recv_sem=recv_sems.at[ai],
        device_id=(x, y, 1 - c), device_id_type=MESH) for ai in range(len(ins))]


class _Carried:
    def __init__(self, kind="gather", arrs=()):
        self.kind, self.arrs, self.n = kind, tuple(arrs), len(arrs)
        self.specs = [ANY] * self.n
        if kind == "gather":
            self.out_shape = [jax.ShapeDtypeStruct((NS,) + a.shape, a.dtype) for a in self.arrs]
            sems = [(self.n, 7), (self.n, 7), (self.n,)]
        elif kind == "exchange":
            self.out_shape = [jax.ShapeDtypeStruct(a.shape[1:], a.dtype) for a in self.arrs]
            sems = [(self.n,), (self.n,)]
        else:
            self.out_shape = [jax.ShapeDtypeStruct((3,) + a.shape[1:], a.dtype) for a in self.arrs]
            sems = [(self.n, 3), (self.n, 3)]
        self.scratch = [pltpu.SemaphoreType.DMA(s) for s in sems] if self.n else []

    def split(self, refs, n_in, n_out):
        n = self.n
        a, b, c = n_in + n, n_in + n + n_out, n_in + 2 * n + n_out
        n_sem = len(self.scratch)
        own_scr = refs[c:len(refs) - n_sem]
        return refs[:n_in], refs[a:b], own_scr, (refs[n_in:a], refs[b:c], refs[len(refs) - n_sem:])

    def start(self, carried):
        ins, outs, sems = carried
        if self.kind == "gather":
            _gather_start(ins, outs, sems)
        else:
            for cp in (_exchange_copies if self.kind == "exchange" else _scatter_copies)(ins, outs, sems):
                cp.start()

    def middle(self, carried):
        if self.kind == "gather":
            _gather_middle(*carried)

    def finish(self, carried):
        ins, outs, sems = carried
        if self.kind == "gather":
            _gather_finish(ins, outs, sems)
        else:
            for cp in (_exchange_copies if self.kind == "exchange" else _scatter_copies)(ins, outs, sems):
                cp.wait()

    def when(self, cond, carried, what):
        if self.n:
            pl.when(cond)(lambda: what(carried))


_NOTHING = _Carried()


def _run_carried(car, tag):
    def body(*refs):
        _, _, _, carried = car.split(refs, 0, 0)
        car.start(carried)
        car.middle(carried)
        car.finish(carried)

    return _pallas(
        body, name=f"{car.kind}_{tag}",
        in_specs=car.specs, out_specs=car.specs, out_shape=car.out_shape, scratch_shapes=car.scratch,
        compiler_params=pltpu.CompilerParams(has_side_effects=True),
    )(*car.arrs)


def _ffn_weight_specs(F4, D):
    return [pl.BlockSpec((None, None, F4, D), lambda i, j: (j, 0, 0, 0)),
            pl.BlockSpec((None, None, F4, D), lambda i, j: (j, 1, 0, 0)),
            pl.BlockSpec((None, F4, D), lambda i, j: (j, 0, 0))]


def _ffn_fwd(x, g, wa, wb, l, f, car=_NOTHING):
    T, D = x.shape
    F4 = wb.shape[1]
    tm = _tile(T, 512)
    ni = T // tm

    def body(*refs):
        ((x_ref, g_ref, wg_ref, wu_ref, wd_ref), (xo_ref, h_ref, p_ref, q_ref, z_ref), (acc,),
         carried) = car.split(refs, 5, 5)
        i = pl.program_id(0)
        j = pl.program_id(1)
        car.when((i == 0) & (j == 0), carried, car.start)
        car.when((i == ni // 2) & (j == 0), carried, car.middle)

        @pl.when(j == 0)
        def _():
            xh, _ = _rms_stats(x_ref[...])
            h_ref[...] = (xh * g_ref[...]).astype(BF16)
            acc[...] = jnp.zeros_like(acc)

        h = h_ref[...]
        a = _nt(h, wg_ref[...])
        b = _nt(h, wu_ref[...])
        sg = _sigmoid(a)
        silu = a * sg
        p_ref[...] = (b * (sg + silu * (1.0 - sg))).astype(BF16)
        q_ref[...] = silu.astype(BF16)
        z = (silu * b).astype(BF16)
        z_ref[...] = z
        acc[...] += _nn(z, wd_ref[...])

        @pl.when(j == NS - 1)
        def _():
            xo_ref[...] = x_ref[...] + 0.5 * acc[...]

        car.when((i == ni - 1) & (j == NS - 1), carried, car.finish)

    return _pallas(
        body, name=f"ffn_fwd_{l}_{f}", grid=(ni, NS),
        in_specs=[pl.BlockSpec((tm, D), lambda i, j: (i, 0)), pl.BlockSpec((1, D), lambda i, j: (0, 0))]
        + _ffn_weight_specs(F4, D) + car.specs,
        out_specs=[pl.BlockSpec((tm, D), lambda i, j: (i, 0)),
                   pl.BlockSpec((tm, D), lambda i, j: (i, 0))]
        + [pl.BlockSpec((None, tm, F4), lambda i, j: (j, i, 0))] * 3 + car.specs,
        out_shape=[jax.ShapeDtypeStruct((T, D), F32), jax.ShapeDtypeStruct((T, D), BF16)]
        + [jax.ShapeDtypeStruct((NS, T, F4), BF16)] * 3 + car.out_shape,
        scratch_shapes=[pltpu.VMEM((tm, D), F32)] + car.scratch,
        compiler_params=_cp(2),
    )(x, g, wa, wa, wb, *car.arrs)


def _ffn_bwd(x, g, dy, p, q, wa, wb, l, f, car=_NOTHING):
    T, D = x.shape
    F4 = wb.shape[1]
    tm = _tile(T, 512)
    ni = T // tm

    def body(*refs):
        ((x_ref, g_ref, dy_ref, p_ref, q_ref, wg_ref, wu_ref, wd_ref),
         (dx_ref, dg_ref, da_ref, db_ref, dyb_ref), (dh,), carried) = car.split(refs, 8, 5)
        i = pl.program_id(0)
        j = pl.program_id(1)
        car.when((i == 0) & (j == 0), carried, car.start)

        @pl.when(j == 0)
        def _():
            dyb_ref[...] = (0.5 * dy_ref[...]).astype(BF16)
            dh[...] = jnp.zeros_like(dh)

        @pl.when((i == 0) & (j == 0))
        def _():
            dg_ref[...] = jnp.zeros_like(dg_ref)

        for r in range(FFN_BWD_SUBTILES):
            rows = slice(r * (tm // FFN_BWD_SUBTILES), (r + 1) * (tm // FFN_BWD_SUBTILES))
            dz = _nt(dyb_ref[rows, :], wd_ref[...])
            da = (dz * p_ref[rows, :].astype(F32)).astype(BF16)
            db = (dz * q_ref[rows, :].astype(F32)).astype(BF16)
            da_ref[rows, :] = da
            db_ref[rows, :] = db
            dh[rows, :] += _nn(da, wg_ref[...]) + _nn(db, wu_ref[...])

        @pl.when(j == NS - 1)
        def _():
            dxn, dg = _rms_bwd(dh[...], x_ref[...], g_ref[...])
            dx_ref[...] = dy_ref[...] + dxn
            dg_ref[...] += dg

        car.when((i == ni - 1) & (j == NS - 1), carried, car.finish)

    tok = pl.BlockSpec((tm, D), lambda i, j: (i, 0))
    vec = pl.BlockSpec((1, D), lambda i, j: (0, 0))
    chunk = pl.BlockSpec((None, tm, F4), lambda i, j: (j, i, 0))
    return _pallas(
        body, name=f"ffn_bwd_{l}_{f}", grid=(ni, NS),
        in_specs=[tok, vec, tok, chunk, chunk] + _ffn_weight_specs(F4, D) + car.specs,
        out_specs=[tok, vec, chunk, chunk, tok] + car.specs,
        out_shape=[jax.ShapeDtypeStruct((T, D), F32), jax.ShapeDtypeStruct((1, D), F32),
                   jax.ShapeDtypeStruct((NS, T, F4), BF16), jax.ShapeDtypeStruct((NS, T, F4), BF16),
                   jax.ShapeDtypeStruct((T, D), BF16)] + car.out_shape,
        scratch_shapes=[pltpu.VMEM((tm, D), F32)] + car.scratch,
        compiler_params=_cp(2),
    )(x, g, dy, p, q, wa, wa, wb, *car.arrs)


def _ffn_bwd_first(dy, p, q, wb, l, f, car=_NOTHING):
    T, D = dy.shape
    F4 = wb.shape[1]
    tm = _tile(T, 512)
    ni = T // tm

    def body(*refs):
        (dy_ref, p_ref, q_ref, wd_ref), (da_ref, db_ref, dyb_ref), _, carried = car.split(refs, 4, 3)
        i = pl.program_id(0)
        j = pl.program_id(1)
        car.when((i == 0) & (j == 0), carried, car.start)

        @pl.when(j == 0)
        def _():
            dyb_ref[...] = (0.5 * dy_ref[...]).astype(BF16)

        for r in range(FFN_BWD_SUBTILES):
            rows = slice(r * (tm // FFN_BWD_SUBTILES), (r + 1) * (tm // FFN_BWD_SUBTILES))
            dz = _nt(dyb_ref[rows, :], wd_ref[...])
            da_ref[rows, :] = (dz * p_ref[rows, :].astype(F32)).astype(BF16)
            db_ref[rows, :] = (dz * q_ref[rows, :].astype(F32)).astype(BF16)

        car.when((i == ni - 1) & (j == NS - 1), carried, car.finish)

    tok = pl.BlockSpec((tm, D), lambda i, j: (i, 0))
    chunk = pl.BlockSpec((None, tm, F4), lambda i, j: (j, i, 0))
    return _pallas(
        body, name=f"ffn_bwd_first_{l}_{f}", grid=(ni, NS),
        in_specs=[tok, chunk, chunk, _ffn_weight_specs(F4, D)[2]] + car.specs,
        out_specs=[chunk, chunk, tok] + car.specs,
        out_shape=[jax.ShapeDtypeStruct((NS, T, F4), BF16), jax.ShapeDtypeStruct((NS, T, F4), BF16),
                   jax.ShapeDtypeStruct((T, D), BF16)] + car.out_shape,
        scratch_shapes=car.scratch,
        compiler_params=_cp(2),
    )(dy, p, q, wb, *car.arrs)


def _ffn_bwd_second(x, g, dy, da, db, wa, l, f, car=_NOTHING):
    T, D = x.shape
    F4 = da.shape[-1]
    tm = _tile(T, 512)
    ni = T // tm

    def body(*refs):
        (x_ref, g_ref, dy_ref, da_ref, db_ref, wg_ref, wu_ref), (dx_ref, dg_ref), (dh,), carried = car.split(refs, 7, 2)
        i = pl.program_id(0)
        j = pl.program_id(1)
        car.when((i == 0) & (j == 0), carried, car.start)

        @pl.when(j == 0)
        def _():
            dh[...] = jnp.zeros_like(dh)

        @pl.when((i == 0) & (j == 0))
        def _():
            dg_ref[...] = jnp.zeros_like(dg_ref)

        dh[...] += _nn(da_ref[...], wg_ref[...]) + _nn(db_ref[...], wu_ref[...])

        @pl.when(j == NS - 1)
        def _():
            dxn, dg = _rms_bwd(dh[...], x_ref[...], g_ref[...])
            dx_ref[...] = dy_ref[...] + dxn
            dg_ref[...] += dg

        car.when((i == ni - 1) & (j == NS - 1), carried, car.finish)

    tok = pl.BlockSpec((tm, D), lambda i, j: (i, 0))
    vec = pl.BlockSpec((1, D), lambda i, j: (0, 0))
    chunk = pl.BlockSpec((None, tm, F4), lambda i, j: (j, i, 0))
    return _pallas(
        body, name=f"ffn_bwd_second_{l}_{f}", grid=(ni, NS),
        in_specs=[tok, vec, tok, chunk, chunk] + _ffn_weight_specs(F4, D)[:2] + car.specs,
        out_specs=[tok, vec] + car.specs,
        out_shape=[jax.ShapeDtypeStruct((T, D), F32), jax.ShapeDtypeStruct((1, D), F32)] + car.out_shape,
        scratch_shapes=[pltpu.VMEM((tm, D), F32)] + car.scratch,
        compiler_params=_cp(2),
    )(x, g, dy, da, db, wa, wa, *car.arrs)


def _ffn_dw(h, dyb, da, db, z, l, f, car=_NOTHING):
    T, D = h.shape
    F4 = da.shape[-1]
    tk = _tile(T, 512)
    nt = T // tk
    R2 = 3 * F4 // 2

    def body(*refs):
        (h_ref, dyb_ref, da_ref, db_ref, z_ref), (g_ref,), (accg, accu, accd), carried = car.split(refs, 5, 1)
        t = pl.program_id(1)
        car.when((pl.program_id(0) == 0) & (t == 0), carried, car.start)

        @pl.when(t == 0)
        def _():
            accg[...] = jnp.zeros_like(accg)
            accu[...] = jnp.zeros_like(accu)
            accd[...] = jnp.zeros_like(accd)

        hv = h_ref[...]
        accg[...] += _tn(da_ref[...], hv)
        accu[...] += _tn(db_ref[...], hv)
        accd[...] += _tn(z_ref[...], dyb_ref[...])

        @pl.when(t == nt - 1)
        def _():
            g_ref[0, 0:F4, :] = accg[...].astype(BF16)
            g_ref[0, F4:R2, :] = accu[0:R2 - F4, :].astype(BF16)
            g_ref[1, 0:2 * F4 - R2, :] = accu[R2 - F4:F4, :].astype(BF16)
            g_ref[1, 2 * F4 - R2:R2, :] = accd[...].astype(BF16)

        car.when((pl.program_id(0) == NS - 1) & (t == nt - 1), carried, car.finish)

    tok = pl.BlockSpec((tk, D), lambda s, t: (t, 0))
    chunk = pl.BlockSpec((None, tk, F4), lambda s, t: (s, t, 0))
    return _pallas(
        body, name=f"ffn_dw_{l}_{f}", grid=(NS, nt),
        in_specs=[tok, tok, chunk, chunk, chunk] + car.specs,
        out_specs=[pl.BlockSpec((2, None, R2, D), lambda s, t: (0, s, 0, 0))] + car.specs,
        out_shape=[jax.ShapeDtypeStruct((2, NS, R2, D), BF16)] + car.out_shape,
        scratch_shapes=[pltpu.VMEM((F4, D), F32), pltpu.VMEM((F4, D), F32), pltpu.VMEM((F4, D), F32)] + car.scratch,
        compiler_params=_cp(2),
    )(h, dyb, da, db, z, *car.arrs)


def _mix_in(x, g, win, l, car=_NOTHING):
    T, D = x.shape
    C4 = win.shape[-1]
    tm = _tile(T, 512)
    ni = T // tm

    def body(*refs):
        (x_ref, g_ref, w_ref), (h_ref, u_ref), _, carried = car.split(refs, 3, 2)
        i = pl.program_id(0)
        j = pl.program_id(1)
        car.when((i == 0) & (j == 0), carried, car.start)
        car.when((i == ni // 2) & (j == 0), carried, car.middle)

        @pl.when(j == 0)
        def _():
            xh, _ = _rms_stats(x_ref[...])
            h_ref[...] = (xh * g_ref[...]).astype(BF16)

        u_ref[...] = _nn(h_ref[...], w_ref[...]).astype(BF16)
        car.when((i == ni - 1) & (j == NS - 1), carried, car.finish)

    return _pallas(
        body, name=f"mix_in_{l}", grid=(ni, NS),
        in_specs=[pl.BlockSpec((tm, D), lambda i, j: (i, 0)), pl.BlockSpec((1, D), lambda i, j: (0, 0)),
                  pl.BlockSpec((None, D, C4), lambda i, j: (j, 0, 0))] + car.specs,
        out_specs=[pl.BlockSpec((tm, D), lambda i, j: (i, 0)), pl.BlockSpec((tm, C4), lambda i, j: (i, j))] + car.specs,
        out_shape=[jax.ShapeDtypeStruct((T, D), BF16), jax.ShapeDtypeStruct((T, NS * C4), BF16)] + car.out_shape,
        scratch_shapes=car.scratch,
        compiler_params=_cp(2),
    )(x, g, win, *car.arrs)


def _pool_lane_window(n):
    lane = lax.broadcasted_iota(jnp.int32, (1, n), 1) // (n // len(POOL_WINDOWS))
    w = jnp.full((1, n), float(POOL_WINDOWS[-1]), F32)
    for gi in range(len(POOL_WINDOWS) - 1):
        w = jnp.where(lane == gi, float(POOL_WINDOWS[gi]), w)
    return lane, w


def _pool_select(lane, sums):
    out = sums[-1]
    for gi in range(len(sums) - 1):
        out = jnp.where(lane == gi, sums[gi], out)
    return out


def _back(v, s):
    return v if s == 0 else pltpu.roll(v, s, 0)


def _fwd_shift(v, s):
    return v if s == 0 else pltpu.roll(v, v.shape[0] - s, 0)


def _mix_seq_fwd(u, cdw, cb, lg, lb, sdw, pwblk, ps, bl, l, car=_NOTHING):
    T = u.shape[0]
    S = T // bl
    ts = _tile(S, 256)
    nt = S // ts
    DC, DS = cdw.shape[-1], sdw.shape[-1]
    o_ag, o_bg, o_cg, o_bx, o_p, o_end = DC, 2 * DC, 2 * DC + DS, 2 * DC + 2 * DS, 2 * DC + 3 * DS, 2 * DC + 4 * DS

    def body(*refs):
        ((up_ref, uc_ref, cdw_ref, cb_ref, lg_ref, lb_ref, sdw_ref, pw_ref, ps_ref),
         (act_ref, cv_ref), _, carried) = car.split(refs, 9, 2)
        i = pl.program_id(1)
        car.when((pl.program_id(0) == 0) & (i == 0), carried, car.start)
        car.when((pl.program_id(0) == bl // 2) & (i == 0), carried, car.middle)
        keep = jnp.where(i > 0, 1.0, 0.0).astype(F32)

        def ext(lo, hi):
            p = up_ref[ts - HALO:ts, lo:hi].astype(F32) * keep
            return jnp.concatenate([p, uc_ref[:, lo:hi].astype(F32)], axis=0)

        glu = ext(0, o_ag) * _sigmoid(ext(o_ag, o_bg))
        cv = jnp.zeros((ts, DC), F32) + cb_ref[...]
        for s in range(CONV_W):
            cv = cv + _back(glu, s)[HALO:, :] * cdw_ref[CONV_W - 1 - s:CONV_W - s, :]
        cv_ref[...] = cv
        mu = jnp.mean(cv, axis=-1, keepdims=True)
        xc = cv - mu
        lnv = xc * lax.rsqrt(jnp.mean(xc * xc, axis=-1, keepdims=True) + EPS) * lg_ref[...] + lb_ref[...]
        act_ref[:, 0:DC] = (lnv * _sigmoid(lnv)).astype(BF16)

        q = ext(o_cg, o_bx) * ext(o_bx, o_p)
        sc = jnp.zeros((ts, DS), F32)
        for s in range(SHORT_W):
            sc = sc + _back(q, s)[HALO:, :] * sdw_ref[SHORT_W - 1 - s:SHORT_W - s, :]
        act_ref[:, DC:DC + DS] = (uc_ref[:, o_bg:o_cg].astype(F32) * sc).astype(BF16)

        p = ext(o_p, o_end)
        lane, wl = _pool_lane_window(DS)
        sums, cur, sh = [], p, 1
        for _ in POOL_WINDOWS:
            cur = cur + _back(cur, sh)
            sums.append(cur[HALO:, :])
            sh *= 2
        pos = (i * ts + lax.broadcasted_iota(jnp.int32, (ts, 1), 0) + 1).astype(F32)
        pooled = _pool_select(lane, sums) / jnp.minimum(pos, wl) - p[HALO:, :]
        act_ref[:, DC + DS:DC + 2 * DS] = (_nn(pooled.astype(BF16), pw_ref[...]) * ps_ref[...]).astype(BF16)
        car.when((pl.program_id(0) == bl - 1) & (i == nt - 1), carried, car.finish)

    ucol = 2 * DC + 4 * DS
    full = lambda a: pl.BlockSpec(a.shape, lambda b, i: (0,) * a.ndim)
    return _pallas(
        body, name=f"mix_seq_fwd_{l}", grid=(bl, nt),
        in_specs=[pl.BlockSpec((ts, ucol), lambda b, i: (b * nt + jnp.maximum(i - 1, 0), 0)),
                  pl.BlockSpec((ts, ucol), lambda b, i: (b * nt + i, 0)),
                  full(cdw), full(cb), full(lg), full(lb), full(sdw), full(pwblk), full(ps)] + car.specs,
        out_specs=[pl.BlockSpec((ts, DC + 2 * DS), lambda b, i: (b * nt + i, 0)),
                   pl.BlockSpec((ts, DC), lambda b, i: (b * nt + i, 0))] + car.specs,
        out_shape=[jax.ShapeDtypeStruct((T, DC + 2 * DS), BF16), jax.ShapeDtypeStruct((T, DC), F32)] + car.out_shape,
        scratch_shapes=car.scratch,
        compiler_params=_cp(2),
    )(u, u, cdw, cb, lg, lb, sdw, pwblk, ps, *car.arrs)


def _mix_out(x, act, u, wp, wo, l, car=_NOTHING):
    T, D = x.shape
    tm = _tile(T, 512)
    ni = T // tm
    DA = act.shape[-1]
    DC, DS = DA // 2, DA // 4
    NB = D // NS
    gcol = (2 * DC + 4 * DS) // D

    def body(*refs):
        ((x_ref, act_ref, g0_ref, g1_ref, g2_ref, wp_ref, wo_ref),
         (xo_ref, y_ref, m_ref), _, carried) = car.split(refs, 7, 3)
        car.when(pl.program_id(0) == 0, carried, car.start)
        car.when(pl.program_id(0) == ni // 2, carried, car.middle)
        parts = [(0, DC), (DC, DC + DS), (DC + DS, DC + 2 * DS)]
        m = jnp.zeros((tm, D), F32)
        for k, (lo, hi) in enumerate(parts):
            av = act_ref[:, lo:hi]
            y = jnp.concatenate([_nn(av, wp_ref[s, lo:hi, :]) for s in range(NS)], axis=1)
            y_ref[:, k * D:(k + 1) * D] = y.astype(BF16)
            gl = (g0_ref, g1_ref, g2_ref)[k][...].astype(F32)
            m = m + _sigmoid(gl) * y
        mb = m.astype(BF16)
        m_ref[...] = mb
        out = _nn(mb[:, 0:NB], wo_ref[0])
        for s in range(1, NS):
            out = out + _nn(mb[:, s * NB:(s + 1) * NB], wo_ref[s])
        xo_ref[...] = x_ref[...] + out
        car.when(pl.program_id(0) == ni - 1, carried, car.finish)

    tok = lambda w: pl.BlockSpec((tm, w), lambda i: (i, 0))
    return _pallas(
        body, name=f"mix_out_{l}", grid=(ni,),
        in_specs=[tok(D), tok(DA),
                  pl.BlockSpec((tm, D), lambda i: (i, gcol)), pl.BlockSpec((tm, D), lambda i: (i, gcol + 1)),
                  pl.BlockSpec((tm, D), lambda i: (i, gcol + 2)),
                  pl.BlockSpec((NS, DA, NB), lambda i: (0, 0, 0)),
                  pl.BlockSpec((NS, NB, D), lambda i: (0, 0, 0))] + car.specs,
        out_specs=[tok(D), tok(3 * D), tok(D)] + car.specs,
        out_shape=[jax.ShapeDtypeStruct((T, D), F32), jax.ShapeDtypeStruct((T, 3 * D), BF16),
                   jax.ShapeDtypeStruct((T, D), BF16)] + car.out_shape,
        scratch_shapes=car.scratch,
        compiler_params=_cp(1),
    )(x, act, u, u, u, wp, wo, *car.arrs)


def _mix_out_bwd(dxn, y, u, act, m, wp, wo, l, car=_NOTHING):
    T, D = dxn.shape
    tm = _tile(T, 256)
    nt = T // tm
    DA = act.shape[-1]
    DC, DS = DA // 2, DA // 4
    NB = D // NS
    UC = u.shape[-1]
    g_lo = 2 * DC + 4 * DS
    gcol = g_lo // D
    parts = [(0, DC), (DC, DC + DS), (DC + DS, DC + 2 * DS)]

    def body(*refs):
        ((dx_ref, y_ref, g0_ref, g1_ref, g2_ref, act_ref, m_ref, wp_ref, wo_ref),
         (du_ref, dact_ref, gwo_ref, gwp_ref), (acc_wo, acc_wp), carried) = car.split(refs, 9, 4)
        i = pl.program_id(0)
        car.when(i == 0, carried, car.start)

        @pl.when(i == 0)
        def _():
            acc_wo[...] = jnp.zeros_like(acc_wo)
            acc_wp[...] = jnp.zeros_like(acc_wp)

        dxb = dx_ref[...].astype(BF16)
        dm = jnp.concatenate([_nt(dxb, wo_ref[s]) for s in range(NS)], axis=1)
        acc_wo[...] += _tn(m_ref[...], dxb)
        du_ref[:, 0:g_lo] = jnp.zeros((tm, g_lo), BF16)
        for k, (lo, hi) in enumerate(parts):
            sg = _sigmoid((g0_ref, g1_ref, g2_ref)[k][...].astype(F32))
            yk = y_ref[:, k * D:(k + 1) * D].astype(F32)
            du_ref[:, g_lo + k * D:g_lo + (k + 1) * D] = (dm * yk * sg * (1.0 - sg)).astype(BF16)
            dyk = (dm * sg).astype(BF16)
            dk = _nt(dyk[:, 0:NB], wp_ref[0, lo:hi, :])
            for s in range(1, NS):
                dk = dk + _nt(dyk[:, s * NB:(s + 1) * NB], wp_ref[s, lo:hi, :])
            dact_ref[:, lo:hi] = dk
            acc_wp[lo:hi, :] += _tn(act_ref[:, lo:hi], dyk)

        @pl.when(i == nt - 1)
        def _():
            for s in range(NS):
                for hf in range(2):
                    r0 = s * NB + hf * (NB // 2)
                    gwo_ref[hf, s] = acc_wo[r0:r0 + NB // 2, :].astype(BF16)
                    gwp_ref[hf, s] = acc_wp[hf * (DA // 2):(hf + 1) * (DA // 2), s * NB:(s + 1) * NB].astype(BF16)

        car.when(i == nt - 1, carried, car.finish)

    tok = lambda w: pl.BlockSpec((tm, w), lambda i: (i, 0))
    whole = lambda shp: pl.BlockSpec(shp, lambda i: (0,) * len(shp))
    return _pallas(
        body, name=f"mix_out_bwd_{l}", grid=(nt,),
        in_specs=[tok(D), tok(3 * D),
                  pl.BlockSpec((tm, D), lambda i: (i, gcol)), pl.BlockSpec((tm, D), lambda i: (i, gcol + 1)),
                  pl.BlockSpec((tm, D), lambda i: (i, gcol + 2)),
                  tok(DA), tok(D), whole((NS, DA, NB)), whole((NS, NB, D))] + car.specs,
        out_specs=[tok(UC), tok(DA), whole((2, NS, NB // 2, D)), whole((2, NS, DA // 2, NB))] + car.specs,
        out_shape=[jax.ShapeDtypeStruct((T, UC), BF16), jax.ShapeDtypeStruct((T, DA), F32),
                   jax.ShapeDtypeStruct((2, NS, NB // 2, D), BF16),
                   jax.ShapeDtypeStruct((2, NS, DA // 2, NB), BF16)] + car.out_shape,
        scratch_shapes=[pltpu.VMEM((D, D), F32), pltpu.VMEM((DA, D), F32)] + car.scratch,
        compiler_params=_cp(1),
    )(dxn, y, u, u, u, act, m, wp, wo, *car.arrs)


def _mix_seq_bwd(du, u, dact, cv, cdw, lg, lb, sdw, pwblk, ps, bl, l, car=_NOTHING):
    T = u.shape[0]
    S = T // bl
    ts = _tile(S, 256)
    nt = S // ts
    DC, DS = cdw.shape[-1], sdw.shape[-1]
    DA = DC + 2 * DS
    o_ag, o_bg, o_cg, o_bx, o_p, o_end = DC, 2 * DC, 2 * DC + DS, 2 * DC + 2 * DS, 2 * DC + 3 * DS, 2 * DC + 4 * DS
    n_f = ts + HALO

    def body(*refs):
        ((_, up_ref, uc_ref, un_ref, dac_ref, dan_ref, cvc_ref, cvn_ref,
          cdw_ref, lg_ref, lb_ref, sdw_ref, pw_ref, ps_ref),
         (du_ref, gcdw_ref, g512_ref, g256_ref, gpw_ref), _, carried) = car.split(refs, 14, 5)
        b = pl.program_id(0)
        i = pl.program_id(1)
        car.when((b == 0) & (i == 0), carried, car.start)
        keep_p = jnp.where(i > 0, 1.0, 0.0).astype(F32)
        keep_n = jnp.where(i < nt - 1, 1.0, 0.0).astype(F32)

        @pl.when((b == 0) & (i == 0))
        def _():
            gcdw_ref[...] = jnp.zeros_like(gcdw_ref)
            g512_ref[...] = jnp.zeros_like(g512_ref)
            g256_ref[...] = jnp.zeros_like(g256_ref)
            gpw_ref[...] = jnp.zeros_like(gpw_ref)

        def back(lo, hi):
            p = up_ref[ts - HALO:ts, lo:hi].astype(F32) * keep_p
            return jnp.concatenate([p, uc_ref[:, lo:hi].astype(F32)], axis=0)

        def fwd(cur, nxt, lo, hi, mask):
            n = nxt[0:HALO, lo:hi].astype(F32)
            if mask:
                n = n * keep_n
            return jnp.concatenate([cur[:, lo:hi].astype(F32), n], axis=0)

        cvx = fwd(cvc_ref, cvn_ref, 0, DC, False)
        dA = fwd(dac_ref, dan_ref, 0, DC, True)
        mu = jnp.mean(cvx, axis=-1, keepdims=True)
        xc = cvx - mu
        rs = lax.rsqrt(jnp.mean(xc * xc, axis=-1, keepdims=True) + EPS)
        xh = xc * rs
        lnv = xh * lg_ref[...] + lb_ref[...]
        sg = _sigmoid(lnv)
        dln = dA * (sg * (1.0 + lnv * (1.0 - sg)))
        dxh = dln * lg_ref[...]
        dcv = rs * (dxh - jnp.mean(dxh, axis=-1, keepdims=True) - xh * jnp.mean(dxh * xh, axis=-1, keepdims=True))
        g512_ref[0:1, :] += jnp.sum(dcv[0:ts], axis=0, keepdims=True)
        g512_ref[1:2, :] += jnp.sum((dln * xh)[0:ts], axis=0, keepdims=True)
        g512_ref[2:3, :] += jnp.sum(dln[0:ts], axis=0, keepdims=True)

        av = back(0, o_ag)
        sga = _sigmoid(back(o_ag, o_bg))
        glu = av * sga
        dcv_c = dcv[0:ts]
        dglu = jnp.zeros((ts, DC), F32)
        for s in range(CONV_W):
            k = CONV_W - 1 - s
            dglu = dglu + _fwd_shift(dcv, s)[0:ts, :] * cdw_ref[k:k + 1, :]
            gcdw_ref[k:k + 1, :] += jnp.sum(_back(glu, s)[HALO:, :] * dcv_c, axis=0, keepdims=True)
        sga_c = sga[HALO:, :]
        du_ref[:, 0:o_ag] = (dglu * sga_c).astype(BF16)
        du_ref[:, o_ag:o_bg] = (dglu * av[HALO:, :] * sga_c * (1.0 - sga_c)).astype(BF16)

        cg = back(o_cg, o_bx)
        bx = back(o_bx, o_p)
        q = cg * bx
        sc = jnp.zeros((ts, DS), F32)
        for s in range(SHORT_W):
            sc = sc + _back(q, s)[HALO:, :] * sdw_ref[SHORT_W - 1 - s:SHORT_W - s, :]
        dB = fwd(dac_ref, dan_ref, DC, DC + DS, True)
        ds = dB * fwd(uc_ref, un_ref, o_bg, o_cg, False)
        du_ref[:, o_bg:o_cg] = (dB[0:ts] * sc).astype(BF16)
        ds_c = ds[0:ts]
        dq = jnp.zeros((ts, DS), F32)
        for s in range(SHORT_W):
            k = SHORT_W - 1 - s
            dq = dq + _fwd_shift(ds, s)[0:ts, :] * sdw_ref[k:k + 1, :]
            g256_ref[k:k + 1, :] += jnp.sum(_back(q, s)[HALO:, :] * ds_c, axis=0, keepdims=True)
        du_ref[:, o_cg:o_bx] = (dq * bx[HALO:, :]).astype(BF16)
        du_ref[:, o_bx:o_p] = (dq * cg[HALO:, :]).astype(BF16)

        p = back(o_p, o_end)
        lane, wl = _pool_lane_window(DS)
        sums, cur, sh = [], p, 1
        for _ in POOL_WINDOWS:
            cur = cur + _back(cur, sh)
            sums.append(cur[HALO:, :])
            sh *= 2
        pos_c = (i * ts + lax.broadcasted_iota(jnp.int32, (ts, 1), 0) + 1).astype(F32)
        pooled = (_pool_select(lane, sums) / jnp.minimum(pos_c, wl) - p[HALO:, :]).astype(BF16)
        pwv = _nn(pooled, pw_ref[...])
        dC = fwd(dac_ref, dan_ref, DC + DS, DA, True)
        g256_ref[SHORT_W:SHORT_W + 1, :] += jnp.sum(dC[0:ts] * pwv, axis=0, keepdims=True)
        dpw = (dC * ps_ref[...]).astype(BF16)
        gpw_ref[...] += _tn(pooled, dpw[0:ts])
        dpl = _nt(dpw, pw_ref[...])
        pos_f = (i * ts + lax.broadcasted_iota(jnp.int32, (n_f, 1), 0) + 1).astype(F32)
        e = dpl / jnp.minimum(pos_f, wl)
        fsums, cur, sh = [], e, 1
        for _ in POOL_WINDOWS:
            cur = cur + _fwd_shift(cur, sh)
            fsums.append(cur[0:ts, :])
            sh *= 2
        du_ref[:, o_p:o_end] = (_pool_select(lane, fsums) - dpl[0:ts]).astype(BF16)
        car.when((b == bl - 1) & (i == nt - 1), carried, car.finish)

    full = lambda a: pl.BlockSpec(a.shape, lambda b, i: (0,) * a.ndim)
    row = lambda w, f: pl.BlockSpec((ts, w), lambda b, i: (b * nt + f(i), 0))
    prv = lambda i: jnp.maximum(i - 1, 0)
    nxt = lambda i: jnp.minimum(i + 1, nt - 1)
    cur = lambda i: i
    return _pallas(
        body, name=f"mix_seq_bwd_{l}", grid=(bl, nt),
        in_specs=[ANY, row(o_end, prv), row(o_end, cur), row(o_end, nxt),
                  row(DA, cur), row(DA, nxt), row(DC, cur), row(DC, nxt),
                  full(cdw), full(lg), full(lb), full(sdw), full(pwblk), full(ps)] + car.specs,
        out_specs=[row(o_end, cur), full(cdw),
                   pl.BlockSpec((8, DC), lambda b, i: (0, 0)), pl.BlockSpec((8, DS), lambda b, i: (0, 0)),
                   full(pwblk)] + car.specs,
        out_shape=[jax.ShapeDtypeStruct(du.shape, BF16), jax.ShapeDtypeStruct(cdw.shape, F32),
                   jax.ShapeDtypeStruct((8, DC), F32), jax.ShapeDtypeStruct((8, DS), F32),
                   jax.ShapeDtypeStruct(pwblk.shape, F32)] + car.out_shape,
        scratch_shapes=car.scratch,
        input_output_aliases={0: 0},
        compiler_params=_cp(2),
    )(du, u, u, u, dact, dact, cv, cv, cdw, lg, lb, sdw, pwblk, ps, *car.arrs)


def _mix_in_bwd(x, g, dxn, du, win, l):
    T, D = x.shape
    C4 = win.shape[-1]
    tm = _tile(T, 512)

    def body(x_ref, g_ref, dxn_ref, du_ref, w_ref, dx_ref, dg_ref, dh):
        i = pl.program_id(0)
        j = pl.program_id(1)

        @pl.when(j == 0)
        def _():
            dh[...] = jnp.zeros_like(dh)

        @pl.when((i == 0) & (j == 0))
        def _():
            dg_ref[...] = jnp.zeros_like(dg_ref)

        dh[...] += _nt(du_ref[...], w_ref[...])

        @pl.when(j == NS - 1)
        def _():
            dxr, dg = _rms_bwd(dh[...], x_ref[...], g_ref[...])
            dx_ref[...] = dxn_ref[...] + dxr
            dg_ref[...] += dg

    tok = pl.BlockSpec((tm, D), lambda i, j: (i, 0))
    vec = pl.BlockSpec((1, D), lambda i, j: (0, 0))
    return _pallas(
        body, name=f"mix_in_bwd_{l}", grid=(T // tm, NS),
        in_specs=[tok, vec, tok, pl.BlockSpec((tm, C4), lambda i, j: (i, j)),
                  pl.BlockSpec((None, D, C4), lambda i, j: (j, 0, 0))],
        out_specs=[tok, vec],
        out_shape=[jax.ShapeDtypeStruct((T, D), F32), jax.ShapeDtypeStruct((1, D), F32)],
        scratch_shapes=[pltpu.VMEM((tm, D), F32)],
        compiler_params=_cp(2),
    )(x, g, dxn, du, win)


def _mix_in_dw(h, du, l):
    T, D = h.shape
    C4 = du.shape[-1] // NS
    tk = _tile(T, 512)
    nt = T // tk

    def body(h_ref, du_ref, g_ref, acc):
        t = pl.program_id(1)

        @pl.when(t == 0)
        def _():
            acc[...] = jnp.zeros_like(acc)

        acc[...] += _tn(h_ref[...], du_ref[...])

        @pl.when(t == nt - 1)
        def _():
            g_ref[0] = acc[0:D // 2, :].astype(BF16)
            g_ref[1] = acc[D // 2:D, :].astype(BF16)

    return _pallas(
        body, name=f"mix_in_dw_{l}", grid=(NS, nt),
        in_specs=[pl.BlockSpec((tk, D), lambda s, t: (t, 0)), pl.BlockSpec((tk, C4), lambda s, t: (t, s))],
        out_specs=pl.BlockSpec((2, None, D // 2, C4), lambda s, t: (0, s, 0, 0)),
        out_shape=jax.ShapeDtypeStruct((2, NS, D // 2, C4), BF16),
        scratch_shapes=[pltpu.VMEM((D, C4), F32)],
        compiler_params=_cp(2),
    )(h, du)


def _loss_head(x, g, target):
    T, D = x.shape
    tm = _tile(T, 512)

    def body(x_ref, g_ref, t_ref, dx_ref, loss_ref, dg_ref):
        @pl.when(pl.program_id(0) == 0)
        def _():
            loss_ref[...] = jnp.zeros_like(loss_ref)
            dg_ref[...] = jnp.zeros_like(dg_ref)

        xv = x_ref[...]
        xh, rs = _rms_stats(xv)
        gv = g_ref[...]
        e = xh * gv - t_ref[...]
        loss_ref[...] += 0.5 * jnp.sum(jnp.mean(e * e, axis=-1, keepdims=True))
        dy = e * (1.0 / D)
        dyg = dy * gv
        dx_ref[...] = rs * (dyg - xh * jnp.mean(dyg * xh, axis=-1, keepdims=True))
        dg_ref[...] += jnp.sum(dy * xh, axis=0, keepdims=True)

    tok = pl.BlockSpec((tm, D), lambda i: (i, 0))
    vec = pl.BlockSpec((1, D), lambda i: (0, 0))
    return _pallas(
        body, name="loss_head", grid=(T // tm,),
        in_specs=[tok, vec, tok],
        out_specs=[tok, pl.BlockSpec((8, 128), lambda i: (0, 0)), vec],
        out_shape=[jax.ShapeDtypeStruct((T, D), F32), jax.ShapeDtypeStruct((8, 128), F32),
                   jax.ShapeDtypeStruct((1, D), F32)],
        compiler_params=_cp(1),
    )(x, g, target)


def _block_diag(pw):
    G, c, _ = pw.shape
    out = jnp.zeros((G * c, G * c), pw.dtype)
    for gi in range(G):
        out = lax.dynamic_update_slice(out, pw[gi], (gi * c, gi * c))
    return out


def _pad_rows(a, n):
    return jnp.pad(a, ((0, n - a.shape[0]), (0, 0)))


def _merge(g):
    return g.reshape(g.shape[0], g.shape[1] * g.shape[2], g.shape[3])


def _split_dws(dws_g, cw, sw):
    cdw = jnp.transpose(dws_g[:, 0:CONV_W, 0:cw], (1, 0, 2)).reshape(CONV_W, NS * cw)
    sdw = jnp.transpose(dws_g[:, 32:32 + SHORT_W, 0:sw], (1, 0, 2)).reshape(SHORT_W, NS * sw)
    return cdw, sdw


def _fwd_bwd(x3, target3, shards, small, dw_widths, first, sum_block):
    bl, S, D = x3.shape
    T = bl * S
    x = x3.reshape(T, D)
    target = target3.reshape(T, D)
    L = len(shards)
    cw, sw = dw_widths
    row = lambda v: v[None, :]
    gather = lambda arrs: _Carried("gather", arrs)
    exchange = lambda arrs: _Carried("exchange", arrs)
    scatter = lambda arrs: _Carried("scatter", arrs)

    saved = []
    wa1, wb1 = first
    for l in range(L):
        sh = shards[l]
        nxt = shards[l + 1] if l + 1 < L else None
        sp = dict(pwblk=_block_diag(small["pool_w"][l]).astype(BF16), cb=row(small["conv_b"][l]),
                  lg=row(small["conv_ln_g"][l]), lb=row(small["conv_ln_b"][l]), ps=row(small["pool_scale"][l]),
                  g1=row(small["norm_ffn1_g"][l]), gm=row(small["norm_mix_g"][l]), g2=row(small["norm_ffn2_g"][l]))
        x0 = x
        x1, h1, p1, q1, z1, win_g, wp_g, wo_g, dws_g = _ffn_fwd(x0, sp["g1"], wa1, wb1, l, 0, gather(sh["mx"]))
        win_g, wp_g, wo_g = _merge(win_g), _merge(wp_g), _merge(wo_g)
        cdw, sdw = _split_dws(_merge(dws_g), cw, sw)
        sp["cdw"], sp["sdw"] = _pad_rows(cdw, HALO), _pad_rows(sdw, 8)
        hm, u, wa2 = _mix_in(x1, sp["gm"], win_g, l, gather([sh["f2a"]]))
        act, cv, wb2 = _mix_seq_fwd(u, sp["cdw"], sp["cb"], sp["lg"], sp["lb"], sp["sdw"], sp["pwblk"], sp["ps"],
                                    bl, l, gather([sh["f2b"]]))
        wb2 = _merge(wb2)
        res_o = _mix_out(x1, act, u, wp_g, wo_g, l, gather([nxt["f1b"]]) if nxt else _NOTHING)
        x2, y, m = res_o[:3]
        res_f = _ffn_fwd(x2, sp["g2"], wa2, wb2, l, 1, gather([nxt["f1a"]]) if nxt else _NOTHING)
        x, h2, p2, q2, z2 = res_f[:5]
        saved.append(dict(sp=sp, x0=x0, x1=x1, x2=x2, h1=h1, p1=p1, q1=q1, z1=z1, hm=hm, u=u, act=act, cv=cv, y=y, m=m,
                          h2=h2, p2=p2, q2=q2, z2=z2, wa1=wa1, wb1=wb1, wa2=wa2, wb2=wb2, win=win_g, wp=wp_g, wo=wo_g))
        if nxt:
            wa1, wb1 = res_f[5], _merge(res_o[3])

    dx, loss_blk, dgf = _loss_head(x, row(small["final_norm_g"]), target)
    loss = loss_blk[0, 0]

    sg = {k: [None] * L for k in ("norm_ffn1_g", "norm_mix_g", "norm_ffn2_g", "conv_dw", "conv_b", "conv_ln_g",
                                  "conv_ln_b", "short_dw", "pool_w", "pool_scale")}
    blocks = []
    g_up, l_up = [], None
    G, c = small["pool_w"].shape[1:3]
    for l in reversed(range(L)):
        sv = saved[l]
        sp = sv["sp"]
        res = _ffn_bwd(sv["x2"], sp["g2"], dx, sv["p2"], sv["q2"], sv["wa2"], sv["wb2"], l, 1, exchange(g_up))
        dx, dg2, da, db, dyb = res[:5]
        p_up = sum_block(g_up, res[5:]) if g_up else []
        res = _ffn_dw(sv["h2"], dyb, da, db, sv["z2"], l, 1, scatter(p_up))
        g_f2 = [res[0]]
        if p_up:
            blocks.append((l_up, "f1", p_up, res[1:]))
        res = _mix_out_bwd(dx, sv["y"], sv["u"], sv["act"], sv["m"], sv["wp"], sv["wo"], l, exchange(g_f2))
        du, dact, g_o, g_p = res[:4]
        p_f2 = sum_block(g_f2, res[4:])
        res = _mix_seq_bwd(du, sv["u"], dact, sv["cv"], sp["cdw"], sp["lg"], sp["lb"],
                           sp["sdw"], sp["pwblk"], sp["ps"], bl, l, scatter(p_f2))
        du, gcdw, g512, g256, gpw = res[:5]
        blocks.append((l, "f2", p_f2, res[5:]))
        dx, dgm = _mix_in_bwd(sv["x1"], sp["gm"], dx, du, sv["win"], l)
        g_mx = [_mix_in_dw(sv["hm"], du, l), g_p, g_o]
        if l > 0:
            res = _ffn_bwd(sv["x0"], sp["g1"], dx, sv["p1"], sv["q1"], sv["wa1"], sv["wb1"], l, 0, exchange(g_mx))
            dx, dg1, da, db, dyb = res[:5]
            p_mx = sum_block(g_mx, res[5:])
            res = _ffn_dw(sv["h1"], dyb, da, db, sv["z1"], l, 0, scatter(p_mx))
            blocks.append((l, "mx", p_mx, res[1:]))
            g_up, l_up = [res[0]], l
        else:
            res = _ffn_bwd_first(dx, sv["p1"], sv["q1"], sv["wb1"], l, 0, exchange(g_mx))
            da, db, dyb = res[:3]
            p_mx = sum_block(g_mx, res[3:])
            res = _ffn_dw(sv["h1"], dyb, da, db, sv["z1"], l, 0, scatter(p_mx))
            blocks.append((l, "mx", p_mx, res[1:]))
            g_f1 = [res[0]]
            p_f1 = sum_block(g_f1, _run_carried(exchange(g_f1), "last"))
            res = _ffn_bwd_second(sv["x0"], sp["g1"], dx, da, db, sv["wa1"], l, 0, scatter(p_f1))
            dx, dg1 = res[:2]
            blocks.append((l, "f1", p_f1, res[2:]))
        sg["norm_ffn1_g"][l], sg["norm_mix_g"][l], sg["norm_ffn2_g"][l] = dg1[0], dgm[0], dg2[0]
        sg["conv_dw"][l] = gcdw[:CONV_W]
        sg["conv_b"][l], sg["conv_ln_g"][l], sg["conv_ln_b"][l] = g512[0], g512[1], g512[2]
        sg["short_dw"][l] = g256[:SHORT_W]
        sg["pool_scale"][l] = g256[SHORT_W]
        sg["pool_w"][l] = jnp.stack([gpw[gi * c:(gi + 1) * c, gi * c:(gi + 1) * c] for gi in range(G)])
    small_g = {k: jnp.stack(v) for k, v in sg.items()}
    small_g["final_norm_g"] = dgf[0]
    return loss, dx.reshape(bl, S, D), blocks, small_g


def _share_final(fs):
    n = len(fs)
    L = fs[0].shape[0]

    def body(*refs):
        outs = refs[n:2 * n]
        send_sems, recv_sems = refs[2 * n:]
        x, y, c, _ = _place()
        sib = (x, y, 1 - c)
        cps = []
        for ai in range(n):
            for l in range(L):
                cp = pltpu.make_async_remote_copy(src_ref=outs[ai].at[l, c], dst_ref=outs[ai].at[l, c],
                                                  send_sem=send_sems.at[ai, l], recv_sem=recv_sems.at[ai, l],
                                                  device_id=sib, device_id_type=MESH)
                cp.start()
                cps.append(cp)
        for ai in range(n):
            for l in range(L):
                blk = outs[ai].at[l, 1 - c]
                pltpu.make_async_remote_copy(src_ref=blk, dst_ref=blk, send_sem=send_sems.at[ai, l],
                                             recv_sem=recv_sems.at[ai, l], device_id=sib, device_id_type=MESH).wait_recv()
        for cp in cps:
            cp.wait_send()

    return _pallas(
        body, name="grad_share_final",
        in_specs=[ANY] * n, out_specs=[ANY] * n,
        out_shape=[jax.ShapeDtypeStruct(f.shape, f.dtype) for f in fs],
        scratch_shapes=[pltpu.SemaphoreType.DMA((n, L)), pltpu.SemaphoreType.DMA((n, L))],
        input_output_aliases={i: i for i in range(n)},
        compiler_params=pltpu.CompilerParams(has_side_effects=True),
    )(*fs)


def _all_reduce_small(v):
    R, W = v.shape

    def body(v_ref, out_ref, buf, send_sems, recv_sems):
        x, y, c, _ = _place()
        me = 4 * x + 2 * y + c
        buf[me] = v_ref[...]
        cps = []
        for k in range(1, 8):
            kx, ky, kc = (k >> 2) & 1, (k >> 1) & 1, k & 1
            to = (1 - x if kx else x, 1 - y if ky else y, 1 - c if kc else c)
            cp = pltpu.make_async_remote_copy(src_ref=v_ref, dst_ref=buf.at[me], send_sem=send_sems.at[k - 1],
                                              recv_sem=recv_sems.at[k - 1], device_id=to, device_id_type=MESH)
            cp.start()
            cps.append(cp)
        for cp in cps:
            cp.wait()
        acc = buf[0]
        for d in range(1, 8):
            acc = acc + buf[d]
        out_ref[...] = acc

    return _pallas(
        body, name="all_reduce_small",
        in_specs=[pl.BlockSpec(memory_space=pltpu.VMEM)], out_specs=pl.BlockSpec(memory_space=pltpu.VMEM),
        out_shape=jax.ShapeDtypeStruct((R, W), F32),
        scratch_shapes=[pltpu.VMEM((8, R, W), F32), pltpu.SemaphoreType.DMA((7,)), pltpu.SemaphoreType.DMA((7,))],
        compiler_params=pltpu.CompilerParams(has_side_effects=True, vmem_limit_bytes=VMEM_LIMIT),
    )(v)


def _row_tile(n, w, streams):
    for t in (1056, 1024, 704, 512, 352, 256, 128, 64, 32, 16):
        if n % t == 0 and 2 * streams * t * w * 4 <= VMEM_LIMIT // 2:
            return t
    raise ValueError((n, w))


def _sum_sibling(tag, cidx, g, r):
    _, N, W = g.shape
    tr = _row_tile(N, W, 3)

    def body(c_ref, g_ref, r_ref, o_ref):
        del c_ref
        o_ref[...] = (g_ref[...].astype(F32) + r_ref[...].astype(F32)).astype(BF16)

    return _pallas(
        body, name=f"grad_sum_sibling_{tag}",
        grid_spec=pltpu.PrefetchScalarGridSpec(
            num_scalar_prefetch=1, grid=(N // tr,),
            in_specs=[pl.BlockSpec((None, tr, W), lambda i, c: (c[0], i, 0)),
                      pl.BlockSpec((tr, W), lambda i, c: (i, 0))],
            out_specs=pl.BlockSpec((tr, W), lambda i, c: (i, 0))),
        out_shape=jax.ShapeDtypeStruct((N, W), BF16),
        compiler_params=_cp(1),
    )(cidx, g, r)


def _sum_final(tag, idx, p, r2, l, L, prev):
    _, r, W = p.shape
    tr = _row_tile(r, W, 5)

    def body(*refs):
        p_ref, r2_ref = refs[1:3]
        o_ref = refs[-1]
        acc = p_ref[...].astype(F32)
        for k in range(3):
            acc = acc + r2_ref[k].astype(F32)
        o_ref[...] = acc

    in_specs = [pl.BlockSpec((None, tr, W), lambda i, s: (s[1], i, 0)),
                pl.BlockSpec((3, tr, W), lambda i, s: (0, i, 0))]
    args = [idx, p, r2]
    aliases = {}
    if prev is not None:
        in_specs.append(ANY)
        args.append(prev)
        aliases = {3: 0}
    return _pallas(
        body, name=f"grad_sum_final_{tag}",
        grid_spec=pltpu.PrefetchScalarGridSpec(
            num_scalar_prefetch=1, grid=(r // tr,), in_specs=in_specs,
            out_specs=pl.BlockSpec((None, None, tr, W), lambda i, s: (l, s[0], i, 0))),
        out_shape=jax.ShapeDtypeStruct((L, 2, r, W), F32),
        input_output_aliases=aliases,
        compiler_params=_cp(1),
    )(*args)


def _adam_math(w, g, m, v):
    m = ADAM_B1 * m + (1.0 - ADAM_B1) * g
    v = ADAM_B2 * v + (1.0 - ADAM_B2) * (g * g)
    m_hat = m / (1.0 - ADAM_B1 ** ADAM_STEP)
    v_hat = v / (1.0 - ADAM_B2 ** ADAM_STEP)
    delta = -ADAM_LR * (m_hat / (jnp.sqrt(v_hat) + ADAM_EPS) + ADAM_WD * w)
    return delta, m, v


def _adam_big(name, w, m, v, gfull, row0):
    L, r, W = w.shape
    tr = _row_tile(r, W, 8)
    assert row0 % tr == 0
    off = row0 // tr

    def body(w_ref, m_ref, v_ref, g_ref, go_ref, d_ref, mo_ref, vo_ref):
        g = g_ref[...]
        d, mn, vn = _adam_math(w_ref[...], g, m_ref[...], v_ref[...])
        go_ref[...] = g
        d_ref[...] = d
        mo_ref[...] = mn
        vo_ref[...] = vn

    blk = pl.BlockSpec((None, tr, W), lambda l, i: (l, i, 0))
    shp = jax.ShapeDtypeStruct(w.shape, F32)
    return _pallas(
        body, name=f"adam_{name}", grid=(L, r // tr),
        in_specs=[blk, blk, blk, pl.BlockSpec((None, tr, W), lambda l, i: (l, off + i, 0))],
        out_specs=[blk] * 4, out_shape=[shp] * 4,
        compiler_params=_cp(2),
    )(w, m, v, gfull)


def _adam_small(ws, gs, ms, vs):
    n = len(ws)

    def body(*refs):
        for k in range(n):
            w_ref, g_ref, m_ref, v_ref = (refs[j * n + k] for j in range(4))
            d_ref, mo_ref, vo_ref = (refs[(4 + j) * n + k] for j in range(3))
            d, mn, vn = _adam_math(w_ref[...], g_ref[...], m_ref[...], v_ref[...])
            d_ref[...] = d
            mo_ref[...] = mn
            vo_ref[...] = vn

    spec = pl.BlockSpec(memory_space=pltpu.VMEM)
    shp = [jax.ShapeDtypeStruct(w.shape, F32) for w in ws]
    out = _pallas(body, name="adam_small", in_specs=[spec] * (4 * n), out_specs=[spec] * (3 * n),
                  out_shape=shp * 3)(*ws, *gs, *ms, *vs)
    return out[:n], out[n:2 * n], out[2 * n:]


_WEIGHTS = ['norm_ffn1_g', 'ffn1_w_gate', 'ffn1_w_up', 'ffn1_w_down', 'norm_mix_g', 'w_in', 'conv_dw', 'conv_b',
            'conv_ln_g', 'conv_ln_b', 'w_pa', 'short_dw', 'w_pb', 'pool_w', 'pool_scale', 'w_pc', 'w_o',
            'norm_ffn2_g', 'ffn2_w_gate', 'ffn2_w_up', 'ffn2_w_down', 'final_norm_g']
_BIG = ('ffn1_w_gate', 'ffn1_w_up', 'ffn1_w_down', 'w_in', 'w_pa', 'w_pb', 'w_pc', 'w_o',
        'ffn2_w_gate', 'ffn2_w_up', 'ffn2_w_down')
_TRANSPOSED = ('ffn1_w_gate', 'ffn1_w_up', 'ffn2_w_gate', 'ffn2_w_up')
_SMALL = tuple(n for n in _WEIGHTS if n not in _BIG)
_SMALL_REDUCED = ('norm_ffn1_g', 'norm_mix_g', 'conv_b', 'conv_ln_g', 'conv_ln_b', 'pool_w', 'pool_scale',
                  'norm_ffn2_g', 'final_norm_g', 'conv_dw', 'short_dw')


def _pack(arrs, rows_multiple=8):
    flat = jnp.concatenate([a.reshape(-1) for a in arrs])
    n = flat.shape[0]
    per = 128 * rows_multiple
    padded = -(-n // per) * per
    return jnp.pad(flat, (0, padded - n)).reshape(padded // 128, 128)


def _unpack(buf, shapes):
    flat = buf.reshape(-1)
    out, o = [], 0
    for s in shapes:
        k = 1
        for d in s:
            k *= d
        out.append(flat[o:o + k].reshape(s))
        o += k
    return out


def _halves(a):
    return a.reshape(2, a.shape[0] // 2, a.shape[1])


def _step(P, M, V, x, loss_target):
    tr = lambda a: jnp.transpose(a, (0, 2, 1))
    bf = lambda a: a.astype(BF16)
    L = P['w_in'].shape[0]
    cw, sw = P['conv_dw'].shape[-1], P['short_dw'].shape[-1]
    ffa = [jnp.stack([bf(tr(P[f'ffn{f}_w_gate'])), bf(tr(P[f'ffn{f}_w_up']))], axis=1) for f in (1, 2)]
    ffb = [bf(P[f'ffn{f}_w_down']) for f in (1, 2)]
    win = bf(P['w_in'])
    wp = jnp.concatenate([bf(P['w_pa']), bf(P['w_pb']), bf(P['w_pc'])], axis=1)
    wo = bf(P['w_o'])
    dws = jnp.zeros((L, 64, 128), F32)
    dws = dws.at[:, 0:CONV_W, 0:cw].set(P['conv_dw']).at[:, 32:32 + SHORT_W, 0:sw].set(P['short_dw'])
    shards = [dict(f1a=ffa[0][l], f1b=_halves(ffb[0][l]), f2a=ffa[1][l], f2b=_halves(ffb[1][l]),
                   mx=(_halves(win[l]), _halves(wp[l]), _halves(wo[l]), _halves(dws[l]))) for l in range(L)]

    xi, yi, ci = lax.axis_index("x"), lax.axis_index("y"), lax.axis_index("c")
    chip = 2 * xi + yi
    cidx = jnp.stack([ci]).astype(jnp.int32)
    idx = jnp.stack([ci, chip]).astype(jnp.int32)
    count = [0]

    def sum_block(gs, r1):
        t0 = count[0]
        count[0] += len(gs)
        parts = []
        for k, (g, r) in enumerate(zip(gs, r1)):
            W = g.shape[-1]
            p = _sum_sibling(t0 + k, cidx, g.reshape(2, -1, W), r.reshape(-1, W))
            parts.append(p.reshape(g.shape[1:]))
        return parts

    wa1, wb1 = _run_carried(_Carried("gather", [shards[0]["f1a"], shards[0]["f1b"]]), "first")
    small = {n: P[n] for n in _SMALL if n not in ('conv_dw', 'short_dw')}
    loss, dx, blocks, small_g = _fwd_bwd(x, loss_target, shards, small, (cw, sw), (wa1, _merge(wb1)), sum_block)

    finals = {}
    for t, (l, name, parts, r2) in enumerate(blocks):
        prev = finals.get(name, [None] * len(parts))
        finals[name] = [_sum_final(f"{t}_{k}", idx, p, r, l, L, pv) for k, (p, r, pv) in enumerate(zip(parts, r2, prev))]
    shared = _share_final(finals["f1"] + finals["f2"] + finals["mx"])
    f_f1, f_f2, f_in, f_p, f_o = [f.reshape(L, -1, f.shape[-1]) for f in shared]

    tot = _all_reduce_small(_pack([small_g[n] for n in _SMALL_REDUCED] + [loss.reshape(1)]))
    *tot, loss = _unpack(tot, [small_g[n].shape for n in _SMALL_REDUCED] + [()])
    tot = dict(zip(_SMALL_REDUCED, tot))
    tot['conv_dw'] = lax.dynamic_slice_in_dim(tot['conv_dw'], chip * cw, cw, axis=2)
    tot['short_dw'] = lax.dynamic_slice_in_dim(tot['short_dw'], chip * sw, sw, axis=2)

    F4 = P['ffn1_w_down'].shape[1]
    dc, ds = P['w_pa'].shape[1], P['w_pb'].shape[1]
    src = {'ffn1_w_gate': (f_f1, 0), 'ffn1_w_up': (f_f1, F4), 'ffn1_w_down': (f_f1, 2 * F4),
           'ffn2_w_gate': (f_f2, 0), 'ffn2_w_up': (f_f2, F4), 'ffn2_w_down': (f_f2, 2 * F4),
           'w_in': (f_in, 0), 'w_pa': (f_p, 0), 'w_pb': (f_p, dc), 'w_pc': (f_p, dc + ds), 'w_o': (f_o, 0)}
    grads, deltas, new_m, new_v = {}, {}, {}, {}
    for n in _BIG:
        gfull, row0 = src[n]
        if n in _TRANSPOSED:
            outs = _adam_big(n, tr(P[n]), tr(M[n]), tr(V[n]), gfull, row0)
            grads[n], deltas[n], new_m[n], new_v[n] = [tr(o) for o in outs]
        else:
            grads[n], deltas[n], new_m[n], new_v[n] = _adam_big(n, P[n], M[n], V[n], gfull, row0)
    as2d = lambda a: a.reshape(1, -1) if a.ndim == 1 else a
    d_s, m_s, v_s = _adam_small([as2d(P[n]) for n in _SMALL], [as2d(tot[n]) for n in _SMALL],
                                [as2d(M[n]) for n in _SMALL], [as2d(V[n]) for n in _SMALL])
    for n, d, mm, vv in zip(_SMALL, d_s, m_s, v_s):
        shp = P[n].shape
        grads[n], deltas[n], new_m[n], new_v[n] = tot[n], d.reshape(shp), mm.reshape(shp), vv.reshape(shp)

    return (loss, dx, *[grads[n] for n in _WEIGHTS], *[deltas[n] for n in _WEIGHTS],
            *[new_m[n] for n in _WEIGHTS], *[new_v[n] for n in _WEIGHTS])


def kernel(x, norm_ffn1_g, ffn1_w_gate, ffn1_w_up, ffn1_w_down, norm_mix_g, w_in, conv_dw, conv_b, conv_ln_g, conv_ln_b, w_pa, short_dw, w_pb, pool_w, pool_scale, w_pc, w_o, norm_ffn2_g, ffn2_w_gate, ffn2_w_up, ffn2_w_down, final_norm_g, loss_target, m_norm_ffn1_g, m_ffn1_w_gate, m_ffn1_w_up, m_ffn1_w_down, m_norm_mix_g, m_w_in, m_conv_dw, m_conv_b, m_conv_ln_g, m_conv_ln_b, m_w_pa, m_short_dw, m_w_pb, m_pool_w, m_pool_scale, m_w_pc, m_w_o, m_norm_ffn2_g, m_ffn2_w_gate, m_ffn2_w_up, m_ffn2_w_down, m_final_norm_g, v_norm_ffn1_g, v_ffn1_w_gate, v_ffn1_w_up, v_ffn1_w_down, v_norm_mix_g, v_w_in, v_conv_dw, v_conv_b, v_conv_ln_g, v_conv_ln_b, v_w_pa, v_short_dw, v_w_pb, v_pool_w, v_pool_scale, v_w_pc, v_w_o, v_norm_ffn2_g, v_ffn2_w_gate, v_ffn2_w_up, v_ffn2_w_down, v_final_norm_g):
    args = locals()
    P = {n: args[n] for n in _WEIGHTS}
    M = {n: args["m_" + n] for n in _WEIGHTS}
    V = {n: args["v_" + n] for n in _WEIGHTS}
    return _step(P, M, V, x, loss_target)
```

```python
import jax
import jax.numpy as jnp
from jax import lax
from jax.experimental import pallas as pl
from jax.experimental.pallas import tpu as pltpu

F32 = jnp.float32
BF16 = jnp.bfloat16
EPS = 1e-6
NS = 4
CONV_W = 31
SHORT_W = 3
POOL_WINDOWS = (2, 4, 8, 16)
HALO = 32
ADAM_LR, ADAM_B1, ADAM_B2, ADAM_EPS, ADAM_WD, ADAM_STEP = 0.001, 0.9, 0.999, 1e-08, 0.01, 10
MESH = pl.DeviceIdType.MESH
ANY = pl.BlockSpec(memory_space=pl.ANY)
VMEM_LIMIT = 56 * 1024 * 1024
FFN_BWD_SUBTILES = 2


def _pallas(body, **kw):
    return pl.pallas_call(body, **kw)


def _cp(n_axes):
    return pltpu.CompilerParams(dimension_semantics=("arbitrary",) * n_axes, vmem_limit_bytes=VMEM_LIMIT)


def _nn(a, b):
    return jnp.dot(a, b, preferred_element_type=F32)


def _nt(a, b):
    return lax.dot_general(a, b, (((1,), (1,)), ((), ())), preferred_element_type=F32)


def _tn(a, b):
    return lax.dot_general(a, b, (((0,), (0,)), ((), ())), preferred_element_type=F32)


def _sigmoid(v):
    return 1.0 / (1.0 + jnp.exp(-v))


def _rms_stats(x):
    rs = lax.rsqrt(jnp.mean(x * x, axis=-1, keepdims=True) + EPS)
    return x * rs, rs


def _rms_bwd(dh, x, g):
    xh, rs = _rms_stats(x)
    dhg = dh * g
    dx = rs * (dhg - xh * jnp.mean(dhg * xh, axis=-1, keepdims=True))
    return dx, jnp.sum(dh * xh, axis=0, keepdims=True)


def _tile(n, pref):
    t = min(n, pref)
    assert n % t == 0, (n, t)
    return t


def _place():
    x, y, c = lax.axis_index("x"), lax.axis_index("y"), lax.axis_index("c")
    chips = [(1 - x, y), (x, 1 - y), (1 - x, 1 - y)]
    return x, y, c, chips


def _gather_copies(ins, outs, sems):
    send_sems, recv_sems, local_sems = sems
    x, y, c, _ = _place()
    me, at_x, at_y, diag = 2 * x + y, 2 * (1 - x) + y, 2 * x + (1 - y), 2 * (1 - x) + (1 - y)
    to_x, to_y, sib = (1 - x, y, c), (x, 1 - y, c), (x, y, 1 - c)

    def remote(ai, k, blk, to, src=None):
        return pltpu.make_async_remote_copy(src_ref=blk if src is None else src, dst_ref=blk,
                                            send_sem=send_sems.at[ai, k], recv_sem=recv_sems.at[ai, k],
                                            device_id=to, device_id_type=MESH)

    g = dict(local=[], first=[], landed=[], relay=[], relayed=[], passed=[], passed_diag=[], from_sib=[])
    for ai in range(len(ins)):
        o = outs[ai]
        h = ins[ai].shape[1] // 2
        lo, hi = pl.ds(0, h), pl.ds(h, h)
        g["local"].append(pltpu.make_async_copy(ins[ai], o.at[me], local_sems.at[ai]))
        g["first"] += [remote(ai, 0, o.at[me, c], to_x, src=ins[ai].at[c]),
                       remote(ai, 1, o.at[me, c], to_y, src=ins[ai].at[c])]
        g["landed"] += [remote(ai, 0, o.at[at_x, c], to_x), remote(ai, 1, o.at[at_y, c], to_y)]
        g["relay"] += [remote(ai, 2, o.at[at_x, c, lo], to_y), remote(ai, 3, o.at[at_y, c, hi], to_x)]
        g["relayed"] += [remote(ai, 2, o.at[diag, c, lo], to_y), remote(ai, 3, o.at[diag, c, hi], to_x)]
        g["passed"] += [remote(ai, 4, o.at[at_x, c], sib), remote(ai, 5, o.at[at_y, c], sib)]
        g["passed_diag"].append(remote(ai, 6, o.at[diag, c], sib))
        g["from_sib"] += [remote(ai, 4, o.at[at_x, 1 - c], sib), remote(ai, 5, o.at[at_y, 1 - c], sib),
                          remote(ai, 6, o.at[diag, 1 - c], sib)]
    return g


def _gather_start(ins, outs, sems):
    g = _gather_copies(ins, outs, sems)
    for cp in g["local"] + g["first"]:
        cp.start()


def _gather_middle(ins, outs, sems):
    g = _gather_copies(ins, outs, sems)
    for arrive, fwd, on in zip(g["landed"], g["passed"], g["relay"]):
        arrive.wait_recv()
        fwd.start()
        on.start()


def _gather_finish(ins, outs, sems):
    g = _gather_copies(ins, outs, sems)
    n = len(g["passed_diag"])
    for ai in range(n):
        g["relayed"][2 * ai].wait_recv()
        g["relayed"][2 * ai + 1].wait_recv()
        g["passed_diag"][ai].start()
    for cp in g["from_sib"]:
        cp.wait_recv()
    for cp in g["first"] + g["relay"] + g["passed"] + g["passed_diag"]:
        cp.wait_send()
    for cp in g["local"]:
        cp.wait()


def _scatter_copies(ins, outs, sems):
    send_sems, recv_sems = sems
    x, y, c, chips = _place()
    return [pltpu.make_async_remote_copy(
        src_ref=ins[ai].at[2 * chip[0] + chip[1]], dst_ref=outs[ai].at[k],
        send_sem=send_sems.at[ai, k], recv_sem=recv_sems.at[ai, k], device_id=(*chip, c), device_id_type=MESH)
        for ai in range(len(ins)) for k, chip in enumerate(chips)]


def _exchange_copies(ins, outs, sems):
    send_sems, recv_sems = sems
    x, y, c, _ = _place()
    return [pltpu.make_async_remote_copy(
        src_ref=ins[ai].at[1 - c], dst_ref=outs[ai], send_sem=send_sems.at[ai], recv_sem=recv_sems.at[ai],
        device_id=(x, y, 1 - c), device_id_type=MESH) for ai in range(len(ins))]


class _Carried:
    def __init__(self, kind="gather", arrs=()):
        self.kind, self.arrs, self.n = kind, tuple(arrs), len(arrs)
        self.specs = [ANY] * self.n
        if kind == "gather":
            self.out_shape = [jax.ShapeDtypeStruct((NS,) + a.shape, a.dtype) for a in self.arrs]
            sems = [(self.n, 7), (self.n, 7), (self.n,)]
        elif kind == "exchange":
            self.out_shape = [jax.ShapeDtypeStruct(a.shape[1:], a.dtype) for a in self.arrs]
            sems = [(self.n,), (self.n,)]
        else:
            self.out_shape = [jax.ShapeDtypeStruct((3,) + a.shape[1:], a.dtype) for a in self.arrs]
            sems = [(self.n, 3), (self.n, 3)]
        self.scratch = [pltpu.SemaphoreType.DMA(s) for s in sems] if self.n else []

    def split(self, refs, n_in, n_out):
        n = self.n
        a, b, c = n_in + n, n_in + n + n_out, n_in + 2 * n + n_out
        n_sem = len(self.scratch)
        own_scr = refs[c:len(refs) - n_sem]
        return refs[:n_in], refs[a:b], own_scr, (refs[n_in:a], refs[b:c], refs[len(refs) - n_sem:])

    def start(self, carried):
        ins, outs, sems = carried
        if self.kind == "gather":
            _gather_start(ins, outs, sems)
        else:
            for cp in (_exchange_copies if self.kind == "exchange" else _scatter_copies)(ins, outs, sems):
                cp.start()

    def middle(self, carried):
        if self.kind == "gather":
            _gather_middle(*carried)

    def finish(self, carried):
        ins, outs, sems = carried
        if self.kind == "gather":
            _gather_finish(ins, outs, sems)
        else:
            for cp in (_exchange_copies if self.kind == "exchange" else _scatter_copies)(ins, outs, sems):
                cp.wait()

    def when(self, cond, carried, what):
        if self.n:
            pl.when(cond)(lambda: what(carried))


_NOTHING = _Carried()


def _run_carried(car, tag):
    def body(*refs):
        _, _, _, carried = car.split(refs, 0, 0)
        car.start(carried)
        car.middle(carried)
        car.finish(carried)

    return _pallas(
        body, name=f"{car.kind}_{tag}",
        in_specs=car.specs, out_specs=car.specs, out_shape=car.out_shape, scratch_shapes=car.scratch,
        compiler_params=pltpu.CompilerParams(has_side_effects=True),
    )(*car.arrs)


def _ffn_weight_specs(F4, D):
    return [pl.BlockSpec((None, None, F4, D), lambda i, j: (j, 0, 0, 0)),
            pl.BlockSpec((None, None, F4, D), lambda i, j: (j, 1, 0, 0)),
            pl.BlockSpec((None, F4, D), lambda i, j: (j, 0, 0))]


def _ffn_fwd(x, g, wa, wb, l, f, car=_NOTHING):
    T, D = x.shape
    F4 = wb.shape[1]
    tm = _tile(T, 1024)
    ni = T // tm

    def body(*refs):
        ((x_ref, g_ref, wg_ref, wu_ref, wd_ref), (xo_ref, h_ref, p_ref, q_ref, z_ref), (acc,),
         carried) = car.split(refs, 5, 5)
        i = pl.program_id(0)
        j = pl.program_id(1)
        car.when((i == 0) & (j == 0), carried, car.start)
        car.when((i == ni // 2) & (j == 0), carried, car.middle)

        @pl.when(j == 0)
        def _():
            xh, _ = _rms_stats(x_ref[...])
            h_ref[...] = (xh * g_ref[...]).astype(BF16)
            acc[...] = jnp.zeros_like(acc)

        h = h_ref[...]
        a = _nt(h, wg_ref[...])
        b = _nt(h, wu_ref[...])
        sg = _sigmoid(a)
        silu = a * sg
        p_ref[...] = (b * (sg + silu * (1.0 - sg))).astype(BF16)
        q_ref[...] = silu.astype(BF16)
        z = (silu * b).astype(BF16)
        z_ref[...] = z
        acc[...] += _nn(z, wd_ref[...])

        @pl.when(j == NS - 1)
        def _():
            xo_ref[...] = x_ref[...] + 0.5 * acc[...]

        car.when((i == ni - 1) & (j == NS - 1), carried, car.finish)

    return _pallas(
        body, name=f"ffn_fwd_{l}_{f}", grid=(ni, NS),
        in_specs=[pl.BlockSpec((tm, D), lambda i, j: (i, 0)), pl.BlockSpec((1, D), lambda i, j: (0, 0))]
        + _ffn_weight_specs(F4, D) + car.specs,
        out_specs=[pl.BlockSpec((tm, D), lambda i, j: (i, 0)),
                   pl.BlockSpec((tm, D), lambda i, j: (i, 0))]
        + [pl.BlockSpec((None, tm, F4), lambda i, j: (j, i, 0))] * 3 + car.specs,
        out_shape=[jax.ShapeDtypeStruct((T, D), F32), jax.ShapeDtypeStruct((T, D), BF16)]
        + [jax.ShapeDtypeStruct((NS, T, F4), BF16)] * 3 + car.out_shape,
        scratch_shapes=[pltpu.VMEM((tm, D), F32)] + car.scratch,
        compiler_params=_cp(2),
    )(x, g, wa, wa, wb, *car.arrs)


def _ffn_bwd(x, g, dy, p, q, wa, wb, l, f, car=_NOTHING):
    T, D = x.shape
    F4 = wb.shape[1]
    tm = _tile(T, 512)
    ni = T // tm

    def body(*refs):
        ((x_ref, g_ref, dy_ref, p_ref, q_ref, wg_ref, wu_ref, wd_ref),
         (dx_ref, dg_ref, da_ref, db_ref, dyb_ref), (dh,), carried) = car.split(refs, 8, 5)
        i = pl.program_id(0)
        j = pl.program_id(1)
        car.when((i == 0) & (j == 0), carried, car.start)

        @pl.when(j == 0)
        def _():
            dyb_ref[...] = (0.5 * dy_ref[...]).astype(BF16)
            dh[...] = jnp.zeros_like(dh)

        @pl.when((i == 0) & (j == 0))
        def _():
            dg_ref[...] = jnp.zeros_like(dg_ref)

        for r in range(FFN_BWD_SUBTILES):
            rows = slice(r * (tm // FFN_BWD_SUBTILES), (r + 1) * (tm // FFN_BWD_SUBTILES))
            dz = _nt(dyb_ref[rows, :], wd_ref[...])
            da = (dz * p_ref[rows, :].astype(F32)).astype(BF16)
            db = (dz * q_ref[rows, :].astype(F32)).astype(BF16)
            da_ref[rows, :] = da
            db_ref[rows, :] = db
            dh[rows, :] += _nn(da, wg_ref[...]) + _nn(db, wu_ref[...])

        @pl.when(j == NS - 1)
        def _():
            dxn, dg = _rms_bwd(dh[...], x_ref[...], g_ref[...])
            dx_ref[...] = dy_ref[...] + dxn
            dg_ref[...] += dg

        car.when((i == ni - 1) & (j == NS - 1), carried, car.finish)

    tok = pl.BlockSpec((tm, D), lambda i, j: (i, 0))
    vec = pl.BlockSpec((1, D), lambda i, j: (0, 0))
    chunk = pl.BlockSpec((None, tm, F4), lambda i, j: (j, i, 0))
    return _pallas(
        body, name=f"ffn_bwd_{l}_{f}", grid=(ni, NS),
        in_specs=[tok, vec, tok, chunk, chunk] + _ffn_weight_specs(F4, D) + car.specs,
        out_specs=[tok, vec, chunk, chunk, tok] + car.specs,
        out_shape=[jax.ShapeDtypeStruct((T, D), F32), jax.ShapeDtypeStruct((1, D), F32),
                   jax.ShapeDtypeStruct((NS, T, F4), BF16), jax.ShapeDtypeStruct((NS, T, F4), BF16),
                   jax.ShapeDtypeStruct((T, D), BF16)] + car.out_shape,
        scratch_shapes=[pltpu.VMEM((tm, D), F32)] + car.scratch,
        compiler_params=_cp(2),
    )(x, g, dy, p, q, wa, wa, wb, *car.arrs)


def _ffn_bwd_first(dy, p, q, wb, l, f, car=_NOTHING):
    T, D = dy.shape
    F4 = wb.shape[1]
    tm = _tile(T, 1024)
    ni = T // tm

    def body(*refs):
        (dy_ref, p_ref, q_ref, wd_ref), (da_ref, db_ref, dyb_ref), _, carried = car.split(refs, 4, 3)
        i = pl.program_id(0)
        j = pl.program_id(1)
        car.when((i == 0) & (j == 0), carried, car.start)

        @pl.when(j == 0)
        def _():
            dyb_ref[...] = (0.5 * dy_ref[...]).astype(BF16)

        for r in range(FFN_BWD_SUBTILES):
            rows = slice(r * (tm // FFN_BWD_SUBTILES), (r + 1) * (tm // FFN_BWD_SUBTILES))
            dz = _nt(dyb_ref[rows, :], wd_ref[...])
            da_ref[rows, :] = (dz * p_ref[rows, :].astype(F32)).astype(BF16)
            db_ref[rows, :] = (dz * q_ref[rows, :].astype(F32)).astype(BF16)

        car.when((i == ni - 1) & (j == NS - 1), carried, car.finish)

    tok = pl.BlockSpec((tm, D), lambda i, j: (i, 0))
    chunk = pl.BlockSpec((None, tm, F4), lambda i, j: (j, i, 0))
    return _pallas(
        body, name=f"ffn_bwd_first_{l}_{f}", grid=(ni, NS),
        in_specs=[tok, chunk, chunk, _ffn_weight_specs(F4, D)[2]] + car.specs,
        out_specs=[chunk, chunk, tok] + car.specs,
        out_shape=[jax.ShapeDtypeStruct((NS, T, F4), BF16), jax.ShapeDtypeStruct((NS, T, F4), BF16),
                   jax.ShapeDtypeStruct((T, D), BF16)] + car.out_shape,
        scratch_shapes=car.scratch,
        compiler_params=_cp(2),
    )(dy, p, q, wb, *car.arrs)


def _ffn_bwd_second(x, g, dy, da, db, wa, l, f, car=_NOTHING):
    T, D = x.shape
    F4 = da.shape[-1]
    tm = _tile(T, 512)
    ni = T // tm

    def body(*refs):
        (x_ref, g_ref, dy_ref, da_ref, db_ref, wg_ref, wu_ref), (dx_ref, dg_ref), (dh,), carried = car.split(refs, 7, 2)
        i = pl.program_id(0)
        j = pl.program_id(1)
        car.when((i == 0) & (j == 0), carried, car.start)

        @pl.when(j == 0)
        def _():
            dh[...] = jnp.zeros_like(dh)

        @pl.when((i == 0) & (j == 0))
        def _():
            dg_ref[...] = jnp.zeros_like(dg_ref)

        dh[...] += _nn(da_ref[...], wg_ref[...]) + _nn(db_ref[...], wu_ref[...])

        @pl.when(j == NS - 1)
        def _():
            dxn, dg = _rms_bwd(dh[...], x_ref[...], g_ref[...])
            dx_ref[...] = dy_ref[...] + dxn
            dg_ref[...] += dg

        car.when((i == ni - 1) & (j == NS - 1), carried, car.finish)

    tok = pl.BlockSpec((tm, D), lambda i, j: (i, 0))
    vec = pl.BlockSpec((1, D), lambda i, j: (0, 0))
    chunk = pl.BlockSpec((None, tm, F4), lambda i, j: (j, i, 0))
    return _pallas(
        body, name=f"ffn_bwd_second_{l}_{f}", grid=(ni, NS),
        in_specs=[tok, vec, tok, chunk, chunk] + _ffn_weight_specs(F4, D)[:2] + car.specs,
        out_specs=[tok, vec] + car.specs,
        out_shape=[jax.ShapeDtypeStruct((T, D), F32), jax.ShapeDtypeStruct((1, D), F32)] + car.out_shape,
        scratch_shapes=[pltpu.VMEM((tm, D), F32)] + car.scratch,
        compiler_params=_cp(2),
    )(x, g, dy, da, db, wa, wa, *car.arrs)


def _ffn_dw(h, dyb, da, db, z, l, f, car=_NOTHING):
    T, D = h.shape
    F4 = da.shape[-1]
    tk = _tile(T, 1024)
    nt = T // tk
    R2 = 3 * F4 // 2

    def body(*refs):
        (h_ref, dyb_ref, da_ref, db_ref, z_ref), (g_ref,), (accg, accu, accd), carried = car.split(refs, 5, 1)
        t = pl.program_id(1)
        car.when((pl.program_id(0) == 0) & (t == 0), carried, car.start)

        @pl.when(t == 0)
        def _():
            accg[...] = jnp.zeros_like(accg)
            accu[...] = jnp.zeros_like(accu)
            accd[...] = jnp.zeros_like(accd)

        hv = h_ref[...]
        accg[...] += _tn(da_ref[...], hv)
        accu[...] += _tn(db_ref[...], hv)
        accd[...] += _tn(z_ref[...], dyb_ref[...])

        @pl.when(t == nt - 1)
        def _():
            g_ref[0, 0:F4, :] = accg[...].astype(BF16)
            g_ref[0, F4:R2, :] = accu[0:R2 - F4, :].astype(BF16)
            g_ref[1, 0:2 * F4 - R2, :] = accu[R2 - F4:F4, :].astype(BF16)
            g_ref[1, 2 * F4 - R2:R2, :] = accd[...].astype(BF16)

        car.when((pl.program_id(0) == NS - 1) & (t == nt - 1), carried, car.finish)

    tok = pl.BlockSpec((tk, D), lambda s, t: (t, 0))
    chunk = pl.BlockSpec((None, tk, F4), lambda s, t: (s, t, 0))
    return _pallas(
        body, name=f"ffn_dw_{l}_{f}", grid=(NS, nt),
        in_specs=[tok, tok, chunk, chunk, chunk] + car.specs,
        out_specs=[pl.BlockSpec((2, None, R2, D), lambda s, t: (0, s, 0, 0))] + car.specs,
        out_shape=[jax.ShapeDtypeStruct((2, NS, R2, D), BF16)] + car.out_shape,
        scratch_shapes=[pltpu.VMEM((F4, D), F32), pltpu.VMEM((F4, D), F32), pltpu.VMEM((F4, D), F32)] + car.scratch,
        compiler_params=_cp(2),
    )(h, dyb, da, db, z, *car.arrs)


def _mix_in(x, g, win, l, car=_NOTHING):
    T, D = x.shape
    C4 = win.shape[-1]
    tm = _tile(T, 1024)
    ni = T // tm

    def body(*refs):
        (x_ref, g_ref, w_ref), (h_ref, u_ref), _, carried = car.split(refs, 3, 2)
        i = pl.program_id(0)
        j = pl.program_id(1)
        car.when((i == 0) & (j == 0), carried, car.start)
        car.when((i == ni // 2) & (j == 0), carried, car.middle)

        @pl.when(j == 0)
        def _():
            xh, _ = _rms_stats(x_ref[...])
            h_ref[...] = (xh * g_ref[...]).astype(BF16)

        u_ref[...] = _nn(h_ref[...], w_ref[...]).astype(BF16)
        car.when((i == ni - 1) & (j == NS - 1), carried, car.finish)

    return _pallas(
        body, name=f"mix_in_{l}", grid=(ni, NS),
        in_specs=[pl.BlockSpec((tm, D), lambda i, j: (i, 0)), pl.BlockSpec((1, D), lambda i, j: (0, 0)),
                  pl.BlockSpec((None, D, C4), lambda i, j: (j, 0, 0))] + car.specs,
        out_specs=[pl.BlockSpec((tm, D), lambda i, j: (i, 0)), pl.BlockSpec((tm, C4), lambda i, j: (i, j))] + car.specs,
        out_shape=[jax.ShapeDtypeStruct((T, D), BF16), jax.ShapeDtypeStruct((T, NS * C4), BF16)] + car.out_shape,
        scratch_shapes=car.scratch,
        compiler_params=_cp(2),
    )(x, g, win, *car.arrs)


def _pool_lane_window(n):
    lane = lax.broadcasted_iota(jnp.int32, (1, n), 1) // (n // len(POOL_WINDOWS))
    w = jnp.full((1, n), float(POOL_WINDOWS[-1]), F32)
    for gi in range(len(POOL_WINDOWS) - 1):
        w = jnp.where(lane == gi, float(POOL_WINDOWS[gi]), w)
    return lane, w


def _pool_select(lane, sums):
    out = sums[-1]
    for gi in range(len(sums) - 1):
        out = jnp.where(lane == gi, sums[gi], out)
    return out


def _back(v, s):
    return v if s == 0 else pltpu.roll(v, s, 0)


def _fwd_shift(v, s):
    return v if s == 0 else pltpu.roll(v, v.shape[0] - s, 0)


def _mix_seq_fwd(u, cdw, cb, lg, lb, sdw, pwblk, ps, bl, l, car=_NOTHING):
    T = u.shape[0]
    S = T // bl
    ts = _tile(S, 256)
    nt = S // ts
    DC, DS = cdw.shape[-1], sdw.shape[-1]
    o_ag, o_bg, o_cg, o_bx, o_p, o_end = DC, 2 * DC, 2 * DC + DS, 2 * DC + 2 * DS, 2 * DC + 3 * DS, 2 * DC + 4 * DS

    def body(*refs):
        ((up_ref, uc_ref, cdw_ref, cb_ref, lg_ref, lb_ref, sdw_ref, pw_ref, ps_ref),
         (act_ref, cv_ref), _, carried) = car.split(refs, 9, 2)
        i = pl.program_id(1)
        car.when((pl.program_id(0) == 0) & (i == 0), carried, car.start)
        car.when((pl.program_id(0) == bl // 2) & (i == 0), carried, car.middle)
        keep = jnp.where(i > 0, 1.0, 0.0).astype(F32)

        def ext(lo, hi):
            p = up_ref[ts - HALO:ts, lo:hi].astype(F32) * keep
            return jnp.concatenate([p, uc_ref[:, lo:hi].astype(F32)], axis=0)

        glu = ext(0, o_ag) * _sigmoid(ext(o_ag, o_bg))
        cv = jnp.zeros((ts, DC), F32) + cb_ref[...]
        for s in range(CONV_W):
            cv = cv + _back(glu, s)[HALO:, :] * cdw_ref[CONV_W - 1 - s:CONV_W - s, :]
        cv_ref[...] = cv
        mu = jnp.mean(cv, axis=-1, keepdims=True)
        xc = cv - mu
        lnv = xc * lax.rsqrt(jnp.mean(xc * xc, axis=-1, keepdims=True) + EPS) * lg_ref[...] + lb_ref[...]
        act_ref[:, 0:DC] = (lnv * _sigmoid(lnv)).astype(BF16)

        q = ext(o_cg, o_bx) * ext(o_bx, o_p)
        sc = jnp.zeros((ts, DS), F32)
        for s in range(SHORT_W):
            sc = sc + _back(q, s)[HALO:, :] * sdw_ref[SHORT_W - 1 - s:SHORT_W - s, :]
        act_ref[:, DC:DC + DS] = (uc_ref[:, o_bg:o_cg].astype(F32) * sc).astype(BF16)

        p = ext(o_p, o_end)
        lane, wl = _pool_lane_window(DS)
        sums, cur, sh = [], p, 1
        for _ in POOL_WINDOWS:
            cur = cur + _back(cur, sh)
            sums.append(cur[HALO:, :])
            sh *= 2
        pos = (i * ts + lax.broadcasted_iota(jnp.int32, (ts, 1), 0) + 1).astype(F32)
        pooled = _pool_select(lane, sums) / jnp.minimum(pos, wl) - p[HALO:, :]
        act_ref[:, DC + DS:DC + 2 * DS] = (_nn(pooled.astype(BF16), pw_ref[...]) * ps_ref[...]).astype(BF16)
        car.when((pl.program_id(0) == bl - 1) & (i == nt - 1), carried, car.finish)

    ucol = 2 * DC + 4 * DS
    full = lambda a: pl.BlockSpec(a.shape, lambda b, i: (0,) * a.ndim)
    return _pallas(
        body, name=f"mix_seq_fwd_{l}", grid=(bl, nt),
        in_specs=[pl.BlockSpec((ts, ucol), lambda b, i: (b * nt + jnp.maximum(i - 1, 0), 0)),
                  pl.BlockSpec((ts, ucol), lambda b, i: (b * nt + i, 0)),
                  full(cdw), full(cb), full(lg), full(lb), full(sdw), full(pwblk), full(ps)] + car.specs,
        out_specs=[pl.BlockSpec((ts, DC + 2 * DS), lambda b, i: (b * nt + i, 0)),
                   pl.BlockSpec((ts, DC), lambda b, i: (b * nt + i, 0))] + car.specs,
        out_shape=[jax.ShapeDtypeStruct((T, DC + 2 * DS), BF16), jax.ShapeDtypeStruct((T, DC), F32)] + car.out_shape,
        scratch_shapes=car.scratch,
        compiler_params=_cp(2),
    )(u, u, cdw, cb, lg, lb, sdw, pwblk, ps, *car.arrs)


def _mix_out(x, act, u, wp, wo, l, car=_NOTHING):
    T, D = x.shape
    tm = _tile(T, 512)
    ni = T // tm
    DA = act.shape[-1]
    DC, DS = DA // 2, DA // 4
    NB = D // NS
    gcol = (2 * DC + 4 * DS) // D

    def body(*refs):
        ((x_ref, act_ref, g0_ref, g1_ref, g2_ref, wp_ref, wo_ref),
         (xo_ref, y_ref, m_ref), _, carried) = car.split(refs, 7, 3)
        car.when(pl.program_id(0) == 0, carried, car.start)
        car.when(pl.program_id(0) == ni // 2, carried, car.middle)
        parts = [(0, DC), (DC, DC + DS), (DC + DS, DC + 2 * DS)]
        m = jnp.zeros((tm, D), F32)
        for k, (lo, hi) in enumerate(parts):
            av = act_ref[:, lo:hi]
            y = jnp.concatenate([_nn(av, wp_ref[s, lo:hi, :]) for s in range(NS)], axis=1)
            y_ref[:, k * D:(k + 1) * D] = y.astype(BF16)
            gl = (g0_ref, g1_ref, g2_ref)[k][...].astype(F32)
            m = m + _sigmoid(gl) * y
        mb = m.astype(BF16)
        m_ref[...] = mb
        out = _nn(mb[:, 0:NB], wo_ref[0])
        for s in range(1, NS):
            out = out + _nn(mb[:, s * NB:(s + 1) * NB], wo_ref[s])
        xo_ref[...] = x_ref[...] + out
        car.when(pl.program_id(0) == ni - 1, carried, car.finish)

    tok = lambda w: pl.BlockSpec((tm, w), lambda i: (i, 0))
    return _pallas(
        body, name=f"mix_out_{l}", grid=(ni,),
        in_specs=[tok(D), tok(DA),
                  pl.BlockSpec((tm, D), lambda i: (i, gcol)), pl.BlockSpec((tm, D), lambda i: (i, gcol + 1)),
                  pl.BlockSpec((tm, D), lambda i: (i, gcol + 2)),
                  pl.BlockSpec((NS, DA, NB), lambda i: (0, 0, 0)),
                  pl.BlockSpec((NS, NB, D), lambda i: (0, 0, 0))] + car.specs,
        out_specs=[tok(D), tok(3 * D), tok(D)] + car.specs,
        out_shape=[jax.ShapeDtypeStruct((T, D), F32), jax.ShapeDtypeStruct((T, 3 * D), BF16),
                   jax.ShapeDtypeStruct((T, D), BF16)] + car.out_shape,
        scratch_shapes=car.scratch,
        compiler_params=_cp(1),
    )(x, act, u, u, u, wp, wo, *car.arrs)


def _mix_out_bwd(dxn, y, u, act, m, wp, wo, l, car=_NOTHING):
    T, D = dxn.shape
    tm = _tile(T, 256)
    nt = T // tm
    DA = act.shape[-1]
    DC, DS = DA // 2, DA // 4
    NB = D // NS
    UC = u.shape[-1]
    g_lo = 2 * DC + 4 * DS
    gcol = g_lo // D
    parts = [(0, DC), (DC, DC + DS), (DC + DS, DC + 2 * DS)]

    def body(*refs):
        ((dx_ref, y_ref, g0_ref, g1_ref, g2_ref, act_ref, m_ref, wp_ref, wo_ref),
         (du_ref, dact_ref, gwo_ref, gwp_ref), (acc_wo, acc_wp), carried) = car.split(refs, 9, 4)
        i = pl.program_id(0)
        car.when(i == 0, carried, car.start)

        @pl.when(i == 0)
        def _():
            acc_wo[...] = jnp.zeros_like(acc_wo)
            acc_wp[...] = jnp.zeros_like(acc_wp)

        dxb = dx_ref[...].astype(BF16)
        dm = jnp.concatenate([_nt(dxb, wo_ref[s]) for s in range(NS)], axis=1)
        acc_wo[...] += _tn(m_ref[...], dxb)
        du_ref[:, 0:g_lo] = jnp.zeros((tm, g_lo), BF16)
        for k, (lo, hi) in enumerate(parts):
            sg = _sigmoid((g0_ref, g1_ref, g2_ref)[k][...].astype(F32))
            yk = y_ref[:, k * D:(k + 1) * D].astype(F32)
            du_ref[:, g_lo + k * D:g_lo + (k + 1) * D] = (dm * yk * sg * (1.0 - sg)).astype(BF16)
            dyk = (dm * sg).astype(BF16)
            dk = _nt(dyk[:, 0:NB], wp_ref[0, lo:hi, :])
            for s in range(1, NS):
                dk = dk + _nt(dyk[:, s * NB:(s + 1) * NB], wp_ref[s, lo:hi, :])
            dact_ref[:, lo:hi] = dk
            acc_wp[lo:hi, :] += _tn(act_ref[:, lo:hi], dyk)

        @pl.when(i == nt - 1)
        def _():
            for s in range(NS):
                for hf in range(2):
                    r0 = s * NB + hf * (NB // 2)
                    gwo_ref[hf, s] = acc_wo[r0:r0 + NB // 2, :].astype(BF16)
                    gwp_ref[hf, s] = acc_wp[hf * (DA // 2):(hf + 1) * (DA // 2), s * NB:(s + 1) * NB].astype(BF16)

        car.when(i == nt - 1, carried, car.finish)

    tok = lambda w: pl.BlockSpec((tm, w), lambda i: (i, 0))
    whole = lambda shp: pl.BlockSpec(shp, lambda i: (0,) * len(shp))
    return _pallas(
        body, name=f"mix_out_bwd_{l}", grid=(nt,),
        in_specs=[tok(D), tok(3 * D),
                  pl.BlockSpec((tm, D), lambda i: (i, gcol)), pl.BlockSpec((tm, D), lambda i: (i, gcol + 1)),
                  pl.BlockSpec((tm, D), lambda i: (i, gcol + 2)),
                  tok(DA), tok(D), whole((NS, DA, NB)), whole((NS, NB, D))] + car.specs,
        out_specs=[tok(UC), tok(DA), whole((2, NS, NB // 2, D)), whole((2, NS, DA // 2, NB))] + car.specs,
        out_shape=[jax.ShapeDtypeStruct((T, UC), BF16), jax.ShapeDtypeStruct((T, DA), F32),
                   jax.ShapeDtypeStruct((2, NS, NB // 2, D), BF16),
                   jax.ShapeDtypeStruct((2, NS, DA // 2, NB), BF16)] + car.out_shape,
        scratch_shapes=[pltpu.VMEM((D, D), F32), pltpu.VMEM((DA, D), F32)] + car.scratch,
        compiler_params=_cp(1),
    )(dxn, y, u, u, u, act, m, wp, wo, *car.arrs)


def _mix_seq_bwd(du, u, dact, cv, cdw, lg, lb, sdw, pwblk, ps, bl, l, car=_NOTHING):
    T = u.shape[0]
    S = T // bl
    ts = _tile(S, 256)
    nt = S // ts
    DC, DS = cdw.shape[-1], sdw.shape[-1]
    DA = DC + 2 * DS
    o_ag, o_bg, o_cg, o_bx, o_p, o_end = DC, 2 * DC, 2 * DC + DS, 2 * DC + 2 * DS, 2 * DC + 3 * DS, 2 * DC + 4 * DS
    n_f = ts + HALO

    def body(*refs):
        ((_, up_ref, uc_ref, un_ref, dac_ref, dan_ref, cvc_ref, cvn_ref,
          cdw_ref, lg_ref, lb_ref, sdw_ref, pw_ref, ps_ref),
         (du_ref, gcdw_ref, g512_ref, g256_ref, gpw_ref), _, carried) = car.split(refs, 14, 5)
        b = pl.program_id(0)
        i = pl.program_id(1)
        car.when((b == 0) & (i == 0), carried, car.start)
        keep_p = jnp.where(i > 0, 1.0, 0.0).astype(F32)
        keep_n = jnp.where(i < nt - 1, 1.0, 0.0).astype(F32)

        @pl.when((b == 0) & (i == 0))
        def _():
            gcdw_ref[...] = jnp.zeros_like(gcdw_ref)
            g512_ref[...] = jnp.zeros_like(g512_ref)
            g256_ref[...] = jnp.zeros_like(g256_ref)
            gpw_ref[...] = jnp.zeros_like(gpw_ref)

        def back(lo, hi):
            p = up_ref[ts - HALO:ts, lo:hi].astype(F32) * keep_p
            return jnp.concatenate([p, uc_ref[:, lo:hi].astype(F32)], axis=0)

        def fwd(cur, nxt, lo, hi, mask):
            n = nxt[0:HALO, lo:hi].astype(F32)
            if mask:
                n = n * keep_n
            return jnp.concatenate([cur[:, lo:hi].astype(F32), n], axis=0)

        cvx = fwd(cvc_ref, cvn_ref, 0, DC, False)
        dA = fwd(dac_ref, dan_ref, 0, DC, True)
        mu = jnp.mean(cvx, axis=-1, keepdims=True)
        xc = cvx - mu
        rs = lax.rsqrt(jnp.mean(xc * xc, axis=-1, keepdims=True) + EPS)
        xh = xc * rs
        lnv = xh * lg_ref[...] + lb_ref[...]
        sg = _sigmoid(lnv)
        dln = dA * (sg * (1.0 + lnv * (1.0 - sg)))
        dxh = dln * lg_ref[...]
        dcv = rs * (dxh - jnp.mean(dxh, axis=-1, keepdims=True) - xh * jnp.mean(dxh * xh, axis=-1, keepdims=True))
        g512_ref[0:1, :] += jnp.sum(dcv[0:ts], axis=0, keepdims=True)
        g512_ref[1:2, :] += jnp.sum((dln * xh)[0:ts], axis=0, keepdims=True)
        g512_ref[2:3, :] += jnp.sum(dln[0:ts], axis=0, keepdims=True)

        av = back(0, o_ag)
        sga = _sigmoid(back(o_ag, o_bg))
        glu = av * sga
        dcv_c = dcv[0:ts]
        dglu = jnp.zeros((ts, DC), F32)
        for s in range(CONV_W):
            k = CONV_W - 1 - s
            dglu = dglu + _fwd_shift(dcv, s)[0:ts, :] * cdw_ref[k:k + 1, :]
            gcdw_ref[k:k + 1, :] += jnp.sum(_back(glu, s)[HALO:, :] * dcv_c, axis=0, keepdims=True)
        sga_c = sga[HALO:, :]
        du_ref[:, 0:o_ag] = (dglu * sga_c).astype(BF16)
        du_ref[:, o_ag:o_bg] = (dglu * av[HALO:, :] * sga_c * (1.0 - sga_c)).astype(BF16)

        cg = back(o_cg, o_bx)
        bx = back(o_bx, o_p)
        q = cg * bx
        sc = jnp.zeros((ts, DS), F32)
        for s in range(SHORT_W):
            sc = sc + _back(q, s)[HALO:, :] * sdw_ref[SHORT_W - 1 - s:SHORT_W - s, :]
        dB = fwd(dac_ref, dan_ref, DC, DC + DS, True)
        ds = dB * fwd(uc_ref, un_ref, o_bg, o_cg, False)
        du_ref[:, o_bg:o_cg] = (dB[0:ts] * sc).astype(BF16)
        ds_c = ds[0:ts]
        dq = jnp.zeros((ts, DS), F32)
        for s in range(SHORT_W):
            k = SHORT_W - 1 - s
            dq = dq + _fwd_shift(ds, s)[0:ts, :] * sdw_ref[k:k + 1, :]
            g256_ref[k:k + 1, :] += jnp.sum(_back(q, s)[HALO:, :] * ds_c, axis=0, keepdims=True)
        du_ref[:, o_cg:o_bx] = (dq * bx[HALO:, :]).astype(BF16)
        du_ref[:, o_bx:o_p] = (dq * cg[HALO:, :]).astype(BF16)

        p = back(o_p, o_end)
        lane, wl = _pool_lane_window(DS)
        sums, cur, sh = [], p, 1
        for _ in POOL_WINDOWS:
            cur = cur + _back(cur, sh)
            sums.append(cur[HALO:, :])
            sh *= 2
        pos_c = (i * ts + lax.broadcasted_iota(jnp.int32, (ts, 1), 0) + 1).astype(F32)
        pooled = (_pool_select(lane, sums) / jnp.minimum(pos_c, wl) - p[HALO:, :]).astype(BF16)
        pwv = _nn(pooled, pw_ref[...])
        dC = fwd(dac_ref, dan_ref, DC + DS, DA, True)
        g256_ref[SHORT_W:SHORT_W + 1, :] += jnp.sum(dC[0:ts] * pwv, axis=0, keepdims=True)
        dpw = (dC * ps_ref[...]).astype(BF16)
        gpw_ref[...] += _tn(pooled, dpw[0:ts])
        dpl = _nt(dpw, pw_ref[...])
        pos_f = (i * ts + lax.broadcasted_iota(jnp.int32, (n_f, 1), 0) + 1).astype(F32)
        e = dpl / jnp.minimum(pos_f, wl)
        fsums, cur, sh = [], e, 1
        for _ in POOL_WINDOWS:
            cur = cur + _fwd_shift(cur, sh)
            fsums.append(cur[0:ts, :])
            sh *= 2
        du_ref[:, o_p:o_end] = (_pool_select(lane, fsums) - dpl[0:ts]).astype(BF16)
        car.when((b == bl - 1) & (i == nt - 1), carried, car.finish)

    full = lambda a: pl.BlockSpec(a.shape, lambda b, i: (0,) * a.ndim)
    row = lambda w, f: pl.BlockSpec((ts, w), lambda b, i: (b * nt + f(i), 0))
    prv = lambda i: jnp.maximum(i - 1, 0)
    nxt = lambda i: jnp.minimum(i + 1, nt - 1)
    cur = lambda i: i
    return _pallas(
        body, name=f"mix_seq_bwd_{l}", grid=(bl, nt),
        in_specs=[ANY, row(o_end, prv), row(o_end, cur), row(o_end, nxt),
                  row(DA, cur), row(DA, nxt), row(DC, cur), row(DC, nxt),
                  full(cdw), full(lg), full(lb), full(sdw), full(pwblk), full(ps)] + car.specs,
        out_specs=[row(o_end, cur), full(cdw),
                   pl.BlockSpec((8, DC), lambda b, i: (0, 0)), pl.BlockSpec((8, DS), lambda b, i: (0, 0)),
                   full(pwblk)] + car.specs,
        out_shape=[jax.ShapeDtypeStruct(du.shape, BF16), jax.ShapeDtypeStruct(cdw.shape, F32),
                   jax.ShapeDtypeStruct((8, DC), F32), jax.ShapeDtypeStruct((8, DS), F32),
                   jax.ShapeDtypeStruct(pwblk.shape, F32)] + car.out_shape,
        scratch_shapes=car.scratch,
        input_output_aliases={0: 0},
        compiler_params=_cp(2),
    )(du, u, u, u, dact, dact, cv, cv, cdw, lg, lb, sdw, pwblk, ps, *car.arrs)


def _mix_in_bwd(x, g, dxn, du, win, l):
    T, D = x.shape
    C4 = win.shape[-1]
    tm = _tile(T, 1024)

    def body(x_ref, g_ref, dxn_ref, du_ref, w_ref, dx_ref, dg_ref, dh):
        i = pl.program_id(0)
        j = pl.program_id(1)

        @pl.when(j == 0)
        def _():
            dh[...] = jnp.zeros_like(dh)

        @pl.when((i == 0) & (j == 0))
        def _():
            dg_ref[...] = jnp.zeros_like(dg_ref)

        dh[...] += _nt(du_ref[...], w_ref[...])

        @pl.when(j == NS - 1)
        def _():
            dxr, dg = _rms_bwd(dh[...], x_ref[...], g_ref[...])
            dx_ref[...] = dxn_ref[...] + dxr
            dg_ref[...] += dg

    tok = pl.BlockSpec((tm, D), lambda i, j: (i, 0))
    vec = pl.BlockSpec((1, D), lambda i, j: (0, 0))
    return _pallas(
        body, name=f"mix_in_bwd_{l}", grid=(T // tm, NS),
        in_specs=[tok, vec, tok, pl.BlockSpec((tm, C4), lambda i, j: (i, j)),
                  pl.BlockSpec((None, D, C4), lambda i, j: (j, 0, 0))],
        out_specs=[tok, vec],
        out_shape=[jax.ShapeDtypeStruct((T, D), F32), jax.ShapeDtypeStruct((1, D), F32)],
        scratch_shapes=[pltpu.VMEM((tm, D), F32)],
        compiler_params=_cp(2),
    )(x, g, dxn, du, win)


def _mix_in_dw(h, du, l):
    T, D = h.shape
    C4 = du.shape[-1] // NS
    tk = _tile(T, 1024)
    nt = T // tk

    def body(h_ref, du_ref, g_ref, acc):
        t = pl.program_id(1)

        @pl.when(t == 0)
        def _():
            acc[...] = jnp.zeros_like(acc)

        acc[...] += _tn(h_ref[...], du_ref[...])

        @pl.when(t == nt - 1)
        def _():
            g_ref[0] = acc[0:D // 2, :].astype(BF16)
            g_ref[1] = acc[D // 2:D, :].astype(BF16)

    return _pallas(
        body, name=f"mix_in_dw_{l}", grid=(NS, nt),
        in_specs=[pl.BlockSpec((tk, D), lambda s, t: (t, 0)), pl.BlockSpec((tk, C4), lambda s, t: (t, s))],
        out_specs=pl.BlockSpec((2, None, D // 2, C4), lambda s, t: (0, s, 0, 0)),
        out_shape=jax.ShapeDtypeStruct((2, NS, D // 2, C4), BF16),
        scratch_shapes=[pltpu.VMEM((D, C4), F32)],
        compiler_params=_cp(2),
    )(h, du)


def _loss_head(x, g, target):
    T, D = x.shape
    tm = _tile(T, 512)

    def body(x_ref, g_ref, t_ref, dx_ref, loss_ref, dg_ref):
        @pl.when(pl.program_id(0) == 0)
        def _():
            loss_ref[...] = jnp.zeros_like(loss_ref)
            dg_ref[...] = jnp.zeros_like(dg_ref)

        xv = x_ref[...]
        xh, rs = _rms_stats(xv)
        gv = g_ref[...]
        e = xh * gv - t_ref[...]
        loss_ref[...] += 0.5 * jnp.sum(jnp.mean(e * e, axis=-1, keepdims=True))
        dy = e * (1.0 / D)
        dyg = dy * gv
        dx_ref[...] = rs * (dyg - xh * jnp.mean(dyg * xh, axis=-1, keepdims=True))
        dg_ref[...] += jnp.sum(dy * xh, axis=0, keepdims=True)

    tok = pl.BlockSpec((tm, D), lambda i: (i, 0))
    vec = pl.BlockSpec((1, D), lambda i: (0, 0))
    return _pallas(
        body, name="loss_head", grid=(T // tm,),
        in_specs=[tok, vec, tok],
        out_specs=[tok, pl.BlockSpec((8, 128), lambda i: (0, 0)), vec],
        out_shape=[jax.ShapeDtypeStruct((T, D), F32), jax.ShapeDtypeStruct((8, 128), F32),
                   jax.ShapeDtypeStruct((1, D), F32)],
        compiler_params=_cp(1),
    )(x, g, target)


def _block_diag(pw):
    G, c, _ = pw.shape
    out = jnp.zeros((G * c, G * c), pw.dtype)
    for gi in range(G):
        out = lax.dynamic_update_slice(out, pw[gi], (gi * c, gi * c))
    return out


def _pad_rows(a, n):
    return jnp.pad(a, ((0, n - a.shape[0]), (0, 0)))


def _merge(g):
    return g.reshape(g.shape[0], g.shape[1] * g.shape[2], g.shape[3])


def _split_dws(dws_g, cw, sw):
    cdw = jnp.transpose(dws_g[:, 0:CONV_W, 0:cw], (1, 0, 2)).reshape(CONV_W, NS * cw)
    sdw = jnp.transpose(dws_g[:, 32:32 + SHORT_W, 0:sw], (1, 0, 2)).reshape(SHORT_W, NS * sw)
    return cdw, sdw


def _fwd_bwd(x3, target3, shards, small, dw_widths, first, sum_block):
    bl, S, D = x3.shape
    T = bl * S
    x = x3.reshape(T, D)
    target = target3.reshape(T, D)
    L = len(shards)
    cw, sw = dw_widths
    row = lambda v: v[None, :]
    gather = lambda arrs: _Carried("gather", arrs)
    exchange = lambda arrs: _Carried("exchange", arrs)
    scatter = lambda arrs: _Carried("scatter", arrs)

    saved = []
    wa1, wb1 = first
    for l in range(L):
        sh = shards[l]
        nxt = shards[l + 1] if l + 1 < L else None
        sp = dict(pwblk=_block_diag(small["pool_w"][l]).astype(BF16), cb=row(small["conv_b"][l]),
                  lg=row(small["conv_ln_g"][l]), lb=row(small["conv_ln_b"][l]), ps=row(small["pool_scale"][l]),
                  g1=row(small["norm_ffn1_g"][l]), gm=row(small["norm_mix_g"][l]), g2=row(small["norm_ffn2_g"][l]))
        x0 = x
        x1, h1, p1, q1, z1, win_g, wp_g, wo_g, dws_g = _ffn_fwd(x0, sp["g1"], wa1, wb1, l, 0, gather(sh["mx"]))
        win_g, wp_g, wo_g = _merge(win_g), _merge(wp_g), _merge(wo_g)
        cdw, sdw = _split_dws(_merge(dws_g), cw, sw)
        sp["cdw"], sp["sdw"] = _pad_rows(cdw, HALO), _pad_rows(sdw, 8)
        hm, u, wa2 = _mix_in(x1, sp["gm"], win_g, l, gather([sh["f2a"]]))
        act, cv, wb2 = _mix_seq_fwd(u, sp["cdw"], sp["cb"], sp["lg"], sp["lb"], sp["sdw"], sp["pwblk"], sp["ps"],
                                    bl, l, gather([sh["f2b"]]))
        wb2 = _merge(wb2)
        res_o = _mix_out(x1, act, u, wp_g, wo_g, l, gather([nxt["f1b"]]) if nxt else _NOTHING)
        x2, y, m = res_o[:3]
        res_f = _ffn_fwd(x2, sp["g2"], wa2, wb2, l, 1, gather([nxt["f1a"]]) if nxt else _NOTHING)
        x, h2, p2, q2, z2 = res_f[:5]
        saved.append(dict(sp=sp, x0=x0, x1=x1, x2=x2, h1=h1, p1=p1, q1=q1, z1=z1, hm=hm, u=u, act=act, cv=cv, y=y, m=m,
                          h2=h2, p2=p2, q2=q2, z2=z2, wa1=wa1, wb1=wb1, wa2=wa2, wb2=wb2, win=win_g, wp=wp_g, wo=wo_g))
        if nxt:
            wa1, wb1 = res_f[5], _merge(res_o[3])

    dx, loss_blk, dgf = _loss_head(x, row(small["final_norm_g"]), target)
    loss = loss_blk[0, 0]

    sg = {k: [None] * L for k in ("norm_ffn1_g", "norm_mix_g", "norm_ffn2_g", "conv_dw", "conv_b", "conv_ln_g",
                                  "conv_ln_b", "short_dw", "pool_w", "pool_scale")}
    blocks = []
    g_up, l_up = [], None
    G, c = small["pool_w"].shape[1:3]
    for l in reversed(range(L)):
        sv = saved[l]
        sp = sv["sp"]
        res = _ffn_bwd(sv["x2"], sp["g2"], dx, sv["p2"], sv["q2"], sv["wa2"], sv["wb2"], l, 1, exchange(g_up))
        dx, dg2, da, db, dyb = res[:5]
        p_up = sum_block(g_up, res[5:]) if g_up else []
        res = _ffn_dw(sv["h2"], dyb, da, db, sv["z2"], l, 1, scatter(p_up))
        g_f2 = [res[0]]
        if p_up:
            blocks.append((l_up, "f1", p_up, res[1:]))
        res = _mix_out_bwd(dx, sv["y"], sv["u"], sv["act"], sv["m"], sv["wp"], sv["wo"], l, exchange(g_f2))
        du, dact, g_o, g_p = res[:4]
        p_f2 = sum_block(g_f2, res[4:])
        res = _mix_seq_bwd(du, sv["u"], dact, sv["cv"], sp["cdw"], sp["lg"], sp["lb"],
                           sp["sdw"], sp["pwblk"], sp["ps"], bl, l, scatter(p_f2))
        du, gcdw, g512, g256, gpw = res[:5]
        blocks.append((l, "f2", p_f2, res[5:]))
        dx, dgm = _mix_in_bwd(sv["x1"], sp["gm"], dx, du, sv["win"], l)
        g_mx = [_mix_in_dw(sv["hm"], du, l), g_p, g_o]
        if l > 0:
            res = _ffn_bwd(sv["x0"], sp["g1"], dx, sv["p1"], sv["q1"], sv["wa1"], sv["wb1"], l, 0, exchange(g_mx))
            dx, dg1, da, db, dyb = res[:5]
            p_mx = sum_block(g_mx, res[5:])
            res = _ffn_dw(sv["h1"], dyb, da, db, sv["z1"], l, 0, scatter(p_mx))
            blocks.append((l, "mx", p_mx, res[1:]))
            g_up, l_up = [res[0]], l
        else:
            res = _ffn_bwd_first(dx, sv["p1"], sv["q1"], sv["wb1"], l, 0, exchange(g_mx))
            da, db, dyb = res[:3]
            p_mx = sum_block(g_mx, res[3:])
            res = _ffn_dw(sv["h1"], dyb, da, db, sv["z1"], l, 0, scatter(p_mx))
            blocks.append((l, "mx", p_mx, res[1:]))
            g_f1 = [res[0]]
            p_f1 = sum_block(g_f1, _run_carried(exchange(g_f1), "last"))
            res = _ffn_bwd_second(sv["x0"], sp["g1"], dx, da, db, sv["wa1"], l, 0, scatter(p_f1))
            dx, dg1 = res[:2]
            blocks.append((l, "f1", p_f1, res[2:]))
        sg["norm_ffn1_g"][l], sg["norm_mix_g"][l], sg["norm_ffn2_g"][l] = dg1[0], dgm[0], dg2[0]
        sg["conv_dw"][l] = gcdw[:CONV_W]
        sg["conv_b"][l], sg["conv_ln_g"][l], sg["conv_ln_b"][l] = g512[0], g512[1], g512[2]
        sg["short_dw"][l] = g256[:SHORT_W]
        sg["pool_scale"][l] = g256[SHORT_W]
        sg["pool_w"][l] = jnp.stack([gpw[gi * c:(gi + 1) * c, gi * c:(gi + 1) * c] for gi in range(G)])
    small_g = {k: jnp.stack(v) for k, v in sg.items()}
    small_g["final_norm_g"] = dgf[0]
    return loss, dx.reshape(bl, S, D), blocks, small_g


def _share_final(fs):
    n = len(fs)
    L = fs[0].shape[0]

    def body(*refs):
        outs = refs[n:2 * n]
        send_sems, recv_sems = refs[2 * n:]
        x, y, c, _ = _place()
        sib = (x, y, 1 - c)
        cps = []
        for ai in range(n):
            for l in range(L):
                cp = pltpu.make_async_remote_copy(src_ref=outs[ai].at[l, c], dst_ref=outs[ai].at[l, c],
                                                  send_sem=send_sems.at[ai, l], recv_sem=recv_sems.at[ai, l],
                                                  device_id=sib, device_id_type=MESH)
                cp.start()
                cps.append(cp)
        for ai in range(n):
            for l in range(L):
                blk = outs[ai].at[l, 1 - c]
                pltpu.make_async_remote_copy(src_ref=blk, dst_ref=blk, send_sem=send_sems.at[ai, l],
                                             recv_sem=recv_sems.at[ai, l], device_id=sib, device_id_type=MESH).wait_recv()
        for cp in cps:
            cp.wait_send()

    return _pallas(
        body, name="grad_share_final",
        in_specs=[ANY] * n, out_specs=[ANY] * n,
        out_shape=[jax.ShapeDtypeStruct(f.shape, f.dtype) for f in fs],
        scratch_shapes=[pltpu.SemaphoreType.DMA((n, L)), pltpu.SemaphoreType.DMA((n, L))],
        input_output_aliases={i: i for i in range(n)},
        compiler_params=pltpu.CompilerParams(has_side_effects=True),
    )(*fs)


def _all_reduce_small(v):
    R, W = v.shape

    def body(v_ref, out_ref, buf, send_sems, recv_sems):
        x, y, c, _ = _place()
        me = 4 * x + 2 * y + c
        buf[me] = v_ref[...]
        cps = []
        for k in range(1, 8):
            kx, ky, kc = (k >> 2) & 1, (k >> 1) & 1, k & 1
            to = (1 - x if kx else x, 1 - y if ky else y, 1 - c if kc else c)
            cp = pltpu.make_async_remote_copy(src_ref=v_ref, dst_ref=buf.at[me], send_sem=send_sems.at[k - 1],
                                              recv_sem=recv_sems.at[k - 1], device_id=to, device_id_type=MESH)
            cp.start()
            cps.append(cp)
        for cp in cps:
            cp.wait()
        acc = buf[0]
        for d in range(1, 8):
            acc = acc + buf[d]
        out_ref[...] = acc

    return _pallas(
        body, name="all_reduce_small",
        in_specs=[pl.BlockSpec(memory_space=pltpu.VMEM)], out_specs=pl.BlockSpec(memory_space=pltpu.VMEM),
        out_shape=jax.ShapeDtypeStruct((R, W), F32),
        scratch_shapes=[pltpu.VMEM((8, R, W), F32), pltpu.SemaphoreType.DMA((7,)), pltpu.SemaphoreType.DMA((7,))],
        compiler_params=pltpu.CompilerParams(has_side_effects=True, vmem_limit_bytes=VMEM_LIMIT),
    )(v)


def _row_tile(n, w, streams):
    for t in (1056, 1024, 704, 512, 352, 256, 128, 64, 32, 16):
        if n % t == 0 and 2 * streams * t * w * 4 <= VMEM_LIMIT // 2:
            return t
    raise ValueError((n, w))


def _sum_sibling(tag, cidx, g, r):
    _, N, W = g.shape
    tr = _row_tile(N, W, 3)

    def body(c_ref, g_ref, r_ref, o_ref):
        del c_ref
        o_ref[...] = (g_ref[...].astype(F32) + r_ref[...].astype(F32)).astype(BF16)

    return _pallas(
        body, name=f"grad_sum_sibling_{tag}",
        grid_spec=pltpu.PrefetchScalarGridSpec(
            num_scalar_prefetch=1, grid=(N // tr,),
            in_specs=[pl.BlockSpec((None, tr, W), lambda i, c: (c[0], i, 0)),
                      pl.BlockSpec((tr, W), lambda i, c: (i, 0))],
            out_specs=pl.BlockSpec((tr, W), lambda i, c: (i, 0))),
        out_shape=jax.ShapeDtypeStruct((N, W), BF16),
        compiler_params=_cp(1),
    )(cidx, g, r)


def _sum_final(tag, idx, p, r2, l, L, prev):
    _, r, W = p.shape
    tr = _row_tile(r, W, 5)

    def body(*refs):
        p_ref, r2_ref = refs[1:3]
        o_ref = refs[-1]
        acc = p_ref[...].astype(F32)
        for k in range(3):
            acc = acc + r2_ref[k].astype(F32)
        o_ref[...] = acc

    in_specs = [pl.BlockSpec((None, tr, W), lambda i, s: (s[1], i, 0)),
                pl.BlockSpec((3, tr, W), lambda i, s: (0, i, 0))]
    args = [idx, p, r2]
    aliases = {}
    if prev is not None:
        in_specs.append(ANY)
        args.append(prev)
        aliases = {3: 0}
    return _pallas(
        body, name=f"grad_sum_final_{tag}",
        grid_spec=pltpu.PrefetchScalarGridSpec(
            num_scalar_prefetch=1, grid=(r // tr,), in_specs=in_specs,
            out_specs=pl.BlockSpec((None, None, tr, W), lambda i, s: (l, s[0], i, 0))),
        out_shape=jax.ShapeDtypeStruct((L, 2, r, W), F32),
        input_output_aliases=aliases,
        compiler_params=_cp(1),
    )(*args)


def _adam_math(w, g, m, v):
    m = ADAM_B1 * m + (1.0 - ADAM_B1) * g
    v = ADAM_B2 * v + (1.0 - ADAM_B2) * (g * g)
    m_hat = m / (1.0 - ADAM_B1 ** ADAM_STEP)
    v_hat = v / (1.0 - ADAM_B2 ** ADAM_STEP)
    delta = -ADAM_LR * (m_hat / (jnp.sqrt(v_hat) + ADAM_EPS) + ADAM_WD * w)
    return delta, m, v


def _adam_big(name, w, m, v, gfull, row0):
    L, r, W = w.shape
    tr = _row_tile(r, W, 8)
    assert row0 % tr == 0
    off = row0 // tr

    def body(w_ref, m_ref, v_ref, g_ref, go_ref, d_ref, mo_ref, vo_ref):
        g = g_ref[...]
        d, mn, vn = _adam_math(w_ref[...], g, m_ref[...], v_ref[...])
        go_ref[...] = g
        d_ref[...] = d
        mo_ref[...] = mn
        vo_ref[...] = vn

    blk = pl.BlockSpec((None, tr, W), lambda l, i: (l, i, 0))
    shp = jax.ShapeDtypeStruct(w.shape, F32)
    return _pallas(
        body, name=f"adam_{name}", grid=(L, r // tr),
        in_specs=[blk, blk, blk, pl.BlockSpec((None, tr, W), lambda l, i: (l, off + i, 0))],
        out_specs=[blk] * 4, out_shape=[shp] * 4,
        compiler_params=_cp(2),
    )(w, m, v, gfull)


def _adam_small(ws, gs, ms, vs):
    n = len(ws)

    def body(*refs):
        for k in range(n):
            w_ref, g_ref, m_ref, v_ref = (refs[j * n + k] for j in range(4))
            d_ref, mo_ref, vo_ref = (refs[(4 + j) * n + k] for j in range(3))
            d, mn, vn = _adam_math(w_ref[...], g_ref[...], m_ref[...], v_ref[...])
            d_ref[...] = d
            mo_ref[...] = mn
            vo_ref[...] = vn

    spec = pl.BlockSpec(memory_space=pltpu.VMEM)
    shp = [jax.ShapeDtypeStruct(w.shape, F32) for w in ws]
    out = _pallas(body, name="adam_small", in_specs=[spec] * (4 * n), out_specs=[spec] * (3 * n),
                  out_shape=shp * 3)(*ws, *gs, *ms, *vs)
    return out[:n], out[n:2 * n], out[2 * n:]


_WEIGHTS = ['norm_ffn1_g', 'ffn1_w_gate', 'ffn1_w_up', 'ffn1_w_down', 'norm_mix_g', 'w_in', 'conv_dw', 'conv_b',
            'conv_ln_g', 'conv_ln_b', 'w_pa', 'short_dw', 'w_pb', 'pool_w', 'pool_scale', 'w_pc', 'w_o',
            'norm_ffn2_g', 'ffn2_w_gate', 'ffn2_w_up', 'ffn2_w_down', 'final_norm_g']
_BIG = ('ffn1_w_gate', 'ffn1_w_up', 'ffn1_w_down', 'w_in', 'w_pa', 'w_pb', 'w_pc', 'w_o',
        'ffn2_w_gate', 'ffn2_w_up', 'ffn2_w_down')
_TRANSPOSED = ('ffn1_w_gate', 'ffn1_w_up', 'ffn2_w_gate', 'ffn2_w_up')
_SMALL = tuple(n for n in _WEIGHTS if n not in _BIG)
_SMALL_REDUCED = ('norm_ffn1_g', 'norm_mix_g', 'conv_b', 'conv_ln_g', 'conv_ln_b', 'pool_w', 'pool_scale',
                  'norm_ffn2_g', 'final_norm_g', 'conv_dw', 'short_dw')


def _pack(arrs, rows_multiple=8):
    flat = jnp.concatenate([a.reshape(-1) for a in arrs])
    n = flat.shape[0]
    per = 128 * rows_multiple
    padded = -(-n // per) * per
    return jnp.pad(flat, (0, padded - n)).reshape(padded // 128, 128)


def _unpack(buf, shapes):
    flat = buf.reshape(-1)
    out, o = [], 0
    for s in shapes:
        k = 1
        for d in s:
            k *= d
        out.append(flat[o:o + k].reshape(s))
        o += k
    return out


def _halves(a):
    return a.reshape(2, a.shape[0] // 2, a.shape[1])


def _step(P, M, V, x, loss_target):
    tr = lambda a: jnp.transpose(a, (0, 2, 1))
    bf = lambda a: a.astype(BF16)
    L = P['w_in'].shape[0]
    cw, sw = P['conv_dw'].shape[-1], P['short_dw'].shape[-1]
    ffa = [jnp.stack([bf(tr(P[f'ffn{f}_w_gate'])), bf(tr(P[f'ffn{f}_w_up']))], axis=1) for f in (1, 2)]
    ffb = [bf(P[f'ffn{f}_w_down']) for f in (1, 2)]
    win = bf(P['w_in'])
    wp = jnp.concatenate([bf(P['w_pa']), bf(P['w_pb']), bf(P['w_pc'])], axis=1)
    wo = bf(P['w_o'])
    dws = jnp.zeros((L, 64, 128), F32)
    dws = dws.at[:, 0:CONV_W, 0:cw].set(P['conv_dw']).at[:, 32:32 + SHORT_W, 0:sw].set(P['short_dw'])
    shards = [dict(f1a=ffa[0][l], f1b=_halves(ffb[0][l]), f2a=ffa[1][l], f2b=_halves(ffb[1][l]),
                   mx=(_halves(win[l]), _halves(wp[l]), _halves(wo[l]), _halves(dws[l]))) for l in range(L)]

    xi, yi, ci = lax.axis_index("x"), lax.axis_index("y"), lax.axis_index("c")
    chip = 2 * xi + yi
    cidx = jnp.stack([ci]).astype(jnp.int32)
    idx = jnp.stack([ci, chip]).astype(jnp.int32)
    count = [0]

    def sum_block(gs, r1):
        t0 = count[0]
        count[0] += len(gs)
        parts = []
        for k, (g, r) in enumerate(zip(gs, r1)):
            W = g.shape[-1]
            p = _sum_sibling(t0 + k, cidx, g.reshape(2, -1, W), r.reshape(-1, W))
            parts.append(p.reshape(g.shape[1:]))
        return parts

    wa1, wb1 = _run_carried(_Carried("gather", [shards[0]["f1a"], shards[0]["f1b"]]), "first")
    small = {n: P[n] for n in _SMALL if n not in ('conv_dw', 'short_dw')}
    loss, dx, blocks, small_g = _fwd_bwd(x, loss_target, shards, small, (cw, sw), (wa1, _merge(wb1)), sum_block)

    finals = {}
    for t, (l, name, parts, r2) in enumerate(blocks):
        prev = finals.get(name, [None] * len(parts))
        finals[name] = [_sum_final(f"{t}_{k}", idx, p, r, l, L, pv) for k, (p, r, pv) in enumerate(zip(parts, r2, prev))]
    shared = _share_final(finals["f1"] + finals["f2"] + finals["mx"])
    f_f1, f_f2, f_in, f_p, f_o = [f.reshape(L, -1, f.shape[-1]) for f in shared]

    tot = _all_reduce_small(_pack([small_g[n] for n in _SMALL_REDUCED] + [loss.reshape(1)]))
    *tot, loss = _unpack(tot, [small_g[n].shape for n in _SMALL_REDUCED] + [()])
    tot = dict(zip(_SMALL_REDUCED, tot))
    tot['conv_dw'] = lax.dynamic_slice_in_dim(tot['conv_dw'], chip * cw, cw, axis=2)
    tot['short_dw'] = lax.dynamic_slice_in_dim(tot['short_dw'], chip * sw, sw, axis=2)

    F4 = P['ffn1_w_down'].shape[1]
    dc, ds = P['w_pa'].shape[1], P['w_pb'].shape[1]
    src = {'ffn1_w_gate': (f_f1, 0), 'ffn1_w_up': (f_f1, F4), 'ffn1_w_down': (f_f1, 2 * F4),
           'ffn2_w_gate': (f_f2, 0), 'ffn2_w_up': (f_f2, F4), 'ffn2_w_down': (f_f2, 2 * F4),
           'w_in': (f_in, 0), 'w_pa': (f_p, 0), 'w_pb': (f_p, dc), 'w_pc': (f_p, dc + ds), 'w_o': (f_o, 0)}
    grads, deltas, new_m, new_v = {}, {}, {}, {}
    for n in _BIG:
        gfull, row0 = src[n]
        if n in _TRANSPOSED:
            outs = _adam_big(n, tr(P[n]), tr(M[n]), tr(V[n]), gfull, row0)
            grads[n], deltas[n], new_m[n], new_v[n] = [tr(o) for o in outs]
        else:
            grads[n], deltas[n], new_m[n], new_v[n] = _adam_big(n, P[n], M[n], V[n], gfull, row0)
    as2d = lambda a: a.reshape(1, -1) if a.ndim == 1 else a
    d_s, m_s, v_s = _adam_small([as2d(P[n]) for n in _SMALL], [as2d(tot[n]) for n in _SMALL],
                                [as2d(M[n]) for n in _SMALL], [as2d(V[n]) for n in _SMALL])
    for n, d, mm, vv in zip(_SMALL, d_s, m_s, v_s):
        shp = P[n].shape
        grads[n], deltas[n], new_m[n], new_v[n] = tot[n], d.reshape(shp), mm.reshape(shp), vv.reshape(shp)

    return (loss, dx, *[grads[n] for n in _WEIGHTS], *[deltas[n] for n in _WEIGHTS],
            *[new_m[n] for n in _WEIGHTS], *[new_v[n] for n in _WEIGHTS])


def kernel(x, norm_ffn1_g, ffn1_w_gate, ffn1_w_up, ffn1_w_down, norm_mix_g, w_in, conv_dw, conv_b, conv_ln_g, conv_ln_b, w_pa, short_dw, w_pb, pool_w, pool_scale, w_pc, w_o, norm_ffn2_g, ffn2_w_gate, ffn2_w_up, ffn2_w_down, final_norm_g, loss_target, m_norm_ffn1_g, m_ffn1_w_gate, m_ffn1_w_up, m_ffn1_w_down, m_norm_mix_g, m_w_in, m_conv_dw, m_conv_b, m_conv_ln_g, m_conv_ln_b, m_w_pa, m_short_dw, m_w_pb, m_pool_w, m_pool_scale, m_w_pc, m_w_o, m_norm_ffn2_g, m_ffn2_w_gate, m_ffn2_w_up, m_ffn2_w_down, m_final_norm_g, v_norm_ffn1_g, v_ffn1_w_gate, v_ffn1_w_up, v_ffn1_w_down, v_norm_mix_g, v_w_in, v_conv_dw, v_conv_b, v_conv_ln_g, v_conv_ln_b, v_w_pa, v_short_dw, v_w_pb, v_pool_w, v_pool_scale, v_w_pc, v_w_o, v_norm_ffn2_g, v_ffn2_w_gate, v_ffn2_w_up, v_ffn2_w_down, v_final_norm_g):
    args = locals()
    P = {n: args[n] for n in _WEIGHTS}
    M = {n: args["m_" + n] for n in _WEIGHTS}
    V = {n: args["v_" + n] for n in _WEIGHTS}
    return _step(P, M, V, x, loss_target)
```

```python
import jax
import jax.numpy as jnp
from jax import lax
from jax.experimental import pallas as pl
from jax.experimental.pallas import tpu as pltpu

F32 = jnp.float32
BF16 = jnp.bfloat16
EPS = 1e-6
NS = 4
CONV_W = 31
SHORT_W = 3
POOL_WINDOWS = (2, 4, 8, 16)
HALO = 32
ADAM_LR, ADAM_B1, ADAM_B2, ADAM_EPS, ADAM_WD, ADAM_STEP = 0.001, 0.9, 0.999, 1e-08, 0.01, 10
MESH = pl.DeviceIdType.MESH
ANY = pl.BlockSpec(memory_space=pl.ANY)
VMEM_LIMIT = 56 * 1024 * 1024
FFN_BWD_SUBTILES = 2
FFN_BWD_VMEM_LIMIT = 60 * 1024 * 1024


def _pallas(body, **kw):
    return pl.pallas_call(body, **kw)


def _cp(n_axes, vmem_limit=VMEM_LIMIT):
    return pltpu.CompilerParams(dimension_semantics=("arbitrary",) * n_axes, vmem_limit_bytes=vmem_limit)


def _nn(a, b):
    return jnp.dot(a, b, preferred_element_type=F32)


def _nt(a, b):
    return lax.dot_general(a, b, (((1,), (1,)), ((), ())), preferred_element_type=F32)


def _tn(a, b):
    return lax.dot_general(a, b, (((0,), (0,)), ((), ())), preferred_element_type=F32)


def _sigmoid(v):
    return 1.0 / (1.0 + jnp.exp(-v))


def _rms_stats(x):
    rs = lax.rsqrt(jnp.mean(x * x, axis=-1, keepdims=True) + EPS)
    return x * rs, rs


def _rms_bwd(dh, x, g):
    xh, rs = _rms_stats(x)
    dhg = dh * g
    dx = rs * (dhg - xh * jnp.mean(dhg * xh, axis=-1, keepdims=True))
    return dx, jnp.sum(dh * xh, axis=0, keepdims=True)


def _tile(n, pref):
    t = min(n, pref)
    assert n % t == 0, (n, t)
    return t


def _place():
    x, y, c = lax.axis_index("x"), lax.axis_index("y"), lax.axis_index("c")
    chips = [(1 - x, y), (x, 1 - y), (1 - x, 1 - y)]
    return x, y, c, chips


def _gather_copies(ins, outs, sems):
    send_sems, recv_sems, local_sems = sems
    x, y, c, _ = _place()
    me, at_x, at_y, diag = 2 * x + y, 2 * (1 - x) + y, 2 * x + (1 - y), 2 * (1 - x) + (1 - y)
    to_x, to_y, sib = (1 - x, y, c), (x, 1 - y, c), (x, y, 1 - c)

    def remote(ai, k, blk, to, src=None):
        return pltpu.make_async_remote_copy(src_ref=blk if src is None else src, dst_ref=blk,
                                            send_sem=send_sems.at[ai, k], recv_sem=recv_sems.at[ai, k],
                                            device_id=to, device_id_type=MESH)

    g = dict(local=[], first=[], landed=[], relay=[], relayed=[], passed=[], passed_diag=[], from_sib=[])
    for ai in range(len(ins)):
        o = outs[ai]
        h = ins[ai].shape[1] // 2
        lo, hi = pl.ds(0, h), pl.ds(h, h)
        g["local"].append(pltpu.make_async_copy(ins[ai], o.at[me], local_sems.at[ai]))
        g["first"] += [remote(ai, 0, o.at[me, c], to_x, src=ins[ai].at[c]),
                       remote(ai, 1, o.at[me, c], to_y, src=ins[ai].at[c])]
        g["landed"] += [remote(ai, 0, o.at[at_x, c], to_x), remote(ai, 1, o.at[at_y, c], to_y)]
        g["relay"] += [remote(ai, 2, o.at[at_x, c, lo], to_y), remote(ai, 3, o.at[at_y, c, hi], to_x)]
        g["relayed"] += [remote(ai, 2, o.at[diag, c, lo], to_y), remote(ai, 3, o.at[diag, c, hi], to_x)]
        g["passed"] += [remote(ai, 4, o.at[at_x, c], sib), remote(ai, 5, o.at[at_y, c], sib)]
        g["passed_diag"].append(remote(ai, 6, o.at[diag, c], sib))
        g["from_sib"] += [remote(ai, 4, o.at[at_x, 1 - c], sib), remote(ai, 5, o.at[at_y, 1 - c], sib),
                          remote(ai, 6, o.at[diag, 1 - c], sib)]
    return g


def _gather_start(ins, outs, sems):
    g = _gather_copies(ins, outs, sems)
    for cp in g["local"] + g["first"]:
        cp.start()


def _gather_middle(ins, outs, sems):
    g = _gather_copies(ins, outs, sems)
    for arrive, fwd, on in zip(g["landed"], g["passed"], g["relay"]):
        arrive.wait_recv()
        fwd.start()
        on.start()


def _gather_finish(ins, outs, sems):
    g = _gather_copies(ins, outs, sems)
    n = len(g["passed_diag"])
    for ai in range(n):
        g["relayed"][2 * ai].wait_recv()
        g["relayed"][2 * ai + 1].wait_recv()
        g["passed_diag"][ai].start()
    for cp in g["from_sib"]:
        cp.wait_recv()
    for cp in g["first"] + g["relay"] + g["passed"] + g["passed_diag"]:
        cp.wait_send()
    for cp in g["local"]:
        cp.wait()


def _scatter_copies(ins, outs, sems):
    send_sems, recv_sems = sems
    x, y, c, chips = _place()
    return [pltpu.make_async_remote_copy(
        src_ref=ins[ai].at[2 * chip[0] + chip[1]], dst_ref=outs[ai].at[k],
        send_sem=send_sems.at[ai, k], recv_sem=recv_sems.at[ai, k], device_id=(*chip, c), device_id_type=MESH)
        for ai in range(len(ins)) for k, chip in enumerate(chips)]


def _exchange_copies(ins, outs, sems):
    send_sems, recv_sems = sems
    x, y, c, _ = _place()
    return [pltpu.make_async_remote_copy(
        src_ref=ins[ai].at[1 - c], dst_ref=outs[ai], send_sem=send_sems.at[ai], recv_sem=recv_sems.at[ai],
        device_id=(x, y, 1 - c), device_id_type=MESH) for ai in range(len(ins))]


class _Carried:
    def __init__(self, kind="gather", arrs=()):
        self.kind, self.arrs, self.n = kind, tuple(arrs), len(arrs)
        self.specs = [ANY] * self.n
        if kind == "gather":
            self.out_shape = [jax.ShapeDtypeStruct((NS,) + a.shape, a.dtype) for a in self.arrs]
            sems = [(self.n, 7), (self.n, 7), (self.n,)]
        elif kind == "exchange":
            self.out_shape = [jax.ShapeDtypeStruct(a.shape[1:], a.dtype) for a in self.arrs]
            sems = [(self.n,), (self.n,)]
        else:
            self.out_shape = [jax.ShapeDtypeStruct((3,) + a.shape[1:], a.dtype) for a in self.arrs]
            sems = [(self.n, 3), (self.n, 3)]
        self.scratch = [pltpu.SemaphoreType.DMA(s) for s in sems] if self.n else []

    def split(self, refs, n_in, n_out):
        n = self.n
        a, b, c = n_in + n, n_in + n + n_out, n_in + 2 * n + n_out
        n_sem = len(self.scratch)
        own_scr = refs[c:len(refs) - n_sem]
        return refs[:n_in], refs[a:b], own_scr, (refs[n_in:a], refs[b:c], refs[len(refs) - n_sem:])

    def start(self, carried):
        ins, outs, sems = carried
        if self.kind == "gather":
            _gather_start(ins, outs, sems)
        else:
            for cp in (_exchange_copies if self.kind == "exchange" else _scatter_copies)(ins, outs, sems):
                cp.start()

    def middle(self, carried):
        if self.kind == "gather":
            _gather_middle(*carried)

    def finish(self, carried):
        ins, outs, sems = carried
        if self.kind == "gather":
            _gather_finish(ins, outs, sems)
        else:
            for cp in (_exchange_copies if self.kind == "exchange" else _scatter_copies)(ins, outs, sems):
                cp.wait()

    def when(self, cond, carried, what):
        if self.n:
            pl.when(cond)(lambda: what(carried))


_NOTHING = _Carried()


def _run_carried(car, tag):
    def body(*refs):
        _, _, _, carried = car.split(refs, 0, 0)
        car.start(carried)
        car.middle(carried)
        car.finish(carried)

    return _pallas(
        body, name=f"{car.kind}_{tag}",
        in_specs=car.specs, out_specs=car.specs, out_shape=car.out_shape, scratch_shapes=car.scratch,
        compiler_params=pltpu.CompilerParams(has_side_effects=True),
    )(*car.arrs)


def _ffn_weight_specs(F4, D):
    return [pl.BlockSpec((None, None, F4, D), lambda i, j: (j, 0, 0, 0)),
            pl.BlockSpec((None, None, F4, D), lambda i, j: (j, 1, 0, 0)),
            pl.BlockSpec((None, F4, D), lambda i, j: (j, 0, 0))]


def _ffn_fwd(x, g, wa, wb, l, f, car=_NOTHING):
    T, D = x.shape
    F4 = wb.shape[1]
    tm = _tile(T, 1024)
    ni = T // tm

    def body(*refs):
        ((x_ref, g_ref, wg_ref, wu_ref, wd_ref), (xo_ref, h_ref, p_ref, q_ref, z_ref), (acc,),
         carried) = car.split(refs, 5, 5)
        i = pl.program_id(0)
        j = pl.program_id(1)
        car.when((i == 0) & (j == 0), carried, car.start)
        car.when((i == ni // 2) & (j == 0), carried, car.middle)

        @pl.when(j == 0)
        def _():
            xh, _ = _rms_stats(x_ref[...])
            h_ref[...] = (xh * g_ref[...]).astype(BF16)
            acc[...] = jnp.zeros_like(acc)

        h = h_ref[...]
        a = _nt(h, wg_ref[...])
        b = _nt(h, wu_ref[...])
        sg = _sigmoid(a)
        silu = a * sg
        p_ref[...] = (b * (sg + silu * (1.0 - sg))).astype(BF16)
        q_ref[...] = silu.astype(BF16)
        z = (silu * b).astype(BF16)
        z_ref[...] = z
        acc[...] += _nn(z, wd_ref[...])

        @pl.when(j == NS - 1)
        def _():
            xo_ref[...] = x_ref[...] + 0.5 * acc[...]

        car.when((i == ni - 1) & (j == NS - 1), carried, car.finish)

    return _pallas(
        body, name=f"ffn_fwd_{l}_{f}", grid=(ni, NS),
        in_specs=[pl.BlockSpec((tm, D), lambda i, j: (i, 0)), pl.BlockSpec((1, D), lambda i, j: (0, 0))]
        + _ffn_weight_specs(F4, D) + car.specs,
        out_specs=[pl.BlockSpec((tm, D), lambda i, j: (i, 0)),
                   pl.BlockSpec((tm, D), lambda i, j: (i, 0))]
        + [pl.BlockSpec((None, tm, F4), lambda i, j: (j, i, 0))] * 3 + car.specs,
        out_shape=[jax.ShapeDtypeStruct((T, D), F32), jax.ShapeDtypeStruct((T, D), BF16)]
        + [jax.ShapeDtypeStruct((NS, T, F4), BF16)] * 3 + car.out_shape,
        scratch_shapes=[pltpu.VMEM((tm, D), F32)] + car.scratch,
        compiler_params=_cp(2),
    )(x, g, wa, wa, wb, *car.arrs)


def _ffn_bwd(x, g, dy, p, q, wa, wb, l, f, car=_NOTHING):
    T, D = x.shape
    F4 = wb.shape[1]
    tm = _tile(T, 1024)
    ni = T // tm

    def body(*refs):
        ((x_ref, g_ref, dy_ref, p_ref, q_ref, wg_ref, wu_ref, wd_ref),
         (dx_ref, dg_ref, da_ref, db_ref, dyb_ref), _, carried) = car.split(refs, 8, 5)
        dh = dx_ref
        i = pl.program_id(0)
        j = pl.program_id(1)
        car.when((i == 0) & (j == 0), carried, car.start)

        @pl.when(j == 0)
        def _():
            dyb_ref[...] = (0.5 * dy_ref[...]).astype(BF16)
            dh[...] = jnp.zeros_like(dh)

        @pl.when((i == 0) & (j == 0))
        def _():
            dg_ref[...] = jnp.zeros_like(dg_ref)

        for r in range(FFN_BWD_SUBTILES):
            rows = slice(r * (tm // FFN_BWD_SUBTILES), (r + 1) * (tm // FFN_BWD_SUBTILES))
            dz = _nt(dyb_ref[rows, :], wd_ref[...])
            da = (dz * p_ref[rows, :].astype(F32)).astype(BF16)
            db = (dz * q_ref[rows, :].astype(F32)).astype(BF16)
            da_ref[rows, :] = da
            db_ref[rows, :] = db
            dh[rows, :] += _nn(da, wg_ref[...]) + _nn(db, wu_ref[...])

        @pl.when(j == NS - 1)
        def _():
            dxn, dg = _rms_bwd(dh[...], x_ref[...], g_ref[...])
            dx_ref[...] = dy_ref[...] + dxn
            dg_ref[...] += dg

        car.when((i == ni - 1) & (j == NS - 1), carried, car.finish)

    tok = pl.BlockSpec((tm, D), lambda i, j: (i, 0))
    vec = pl.BlockSpec((1, D), lambda i, j: (0, 0))
    chunk = pl.BlockSpec((None, tm, F4), lambda i, j: (j, i, 0))
    return _pallas(
        body, name=f"ffn_bwd_{l}_{f}", grid=(ni, NS),
        in_specs=[tok, vec, tok, chunk, chunk] + _ffn_weight_specs(F4, D) + car.specs,
        out_specs=[tok, vec, chunk, chunk, tok] + car.specs,
        out_shape=[jax.ShapeDtypeStruct((T, D), F32), jax.ShapeDtypeStruct((1, D), F32),
                   jax.ShapeDtypeStruct((NS, T, F4), BF16), jax.ShapeDtypeStruct((NS, T, F4), BF16),
                   jax.ShapeDtypeStruct((T, D), BF16)] + car.out_shape,
        scratch_shapes=car.scratch,
        compiler_params=_cp(2, FFN_BWD_VMEM_LIMIT),
    )(x, g, dy, p, q, wa, wa, wb, *car.arrs)


def _ffn_bwd_first(dy, p, q, wb, l, f, car=_NOTHING):
    T, D = dy.shape
    F4 = wb.shape[1]
    tm = _tile(T, 1024)
    ni = T // tm

    def body(*refs):
        (dy_ref, p_ref, q_ref, wd_ref), (da_ref, db_ref, dyb_ref), _, carried = car.split(refs, 4, 3)
        i = pl.program_id(0)
        j = pl.program_id(1)
        car.when((i == 0) & (j == 0), carried, car.start)

        @pl.when(j == 0)
        def _():
            dyb_ref[...] = (0.5 * dy_ref[...]).astype(BF16)

        for r in range(FFN_BWD_SUBTILES):
            rows = slice(r * (tm // FFN_BWD_SUBTILES), (r + 1) * (tm // FFN_BWD_SUBTILES))
            dz = _nt(dyb_ref[rows, :], wd_ref[...])
            da_ref[rows, :] = (dz * p_ref[rows, :].astype(F32)).astype(BF16)
            db_ref[rows, :] = (dz * q_ref[rows, :].astype(F32)).astype(BF16)

        car.when((i == ni - 1) & (j == NS - 1), carried, car.finish)

    tok = pl.BlockSpec((tm, D), lambda i, j: (i, 0))
    chunk = pl.BlockSpec((None, tm, F4), lambda i, j: (j, i, 0))
    return _pallas(
        body, name=f"ffn_bwd_first_{l}_{f}", grid=(ni, NS),
        in_specs=[tok, chunk, chunk, _ffn_weight_specs(F4, D)[2]] + car.specs,
        out_specs=[chunk, chunk, tok] + car.specs,
        out_shape=[jax.ShapeDtypeStruct((NS, T, F4), BF16), jax.ShapeDtypeStruct((NS, T, F4), BF16),
                   jax.ShapeDtypeStruct((T, D), BF16)] + car.out_shape,
        scratch_shapes=car.scratch,
        compiler_params=_cp(2),
    )(dy, p, q, wb, *car.arrs)


def _ffn_bwd_second(x, g, dy, da, db, wa, l, f, car=_NOTHING):
    T, D = x.shape
    F4 = da.shape[-1]
    tm = _tile(T, 512)
    ni = T // tm

    def body(*refs):
        (x_ref, g_ref, dy_ref, da_ref, db_ref, wg_ref, wu_ref), (dx_ref, dg_ref), (dh,), carried = car.split(refs, 7, 2)
        i = pl.program_id(0)
        j = pl.program_id(1)
        car.when((i == 0) & (j == 0), carried, car.start)

        @pl.when(j == 0)
        def _():
            dh[...] = jnp.zeros_like(dh)

        @pl.when((i == 0) & (j == 0))
        def _():
            dg_ref[...] = jnp.zeros_like(dg_ref)

        dh[...] += _nn(da_ref[...], wg_ref[...]) + _nn(db_ref[...], wu_ref[...])

        @pl.when(j == NS - 1)
        def _():
            dxn, dg = _rms_bwd(dh[...], x_ref[...], g_ref[...])
            dx_ref[...] = dy_ref[...] + dxn
            dg_ref[...] += dg

        car.when((i == ni - 1) & (j == NS - 1), carried, car.finish)

    tok = pl.BlockSpec((tm, D), lambda i, j: (i, 0))
    vec = pl.BlockSpec((1, D), lambda i, j: (0, 0))
    chunk = pl.BlockSpec((None, tm, F4), lambda i, j: (j, i, 0))
    return _pallas(
        body, name=f"ffn_bwd_second_{l}_{f}", grid=(ni, NS),
        in_specs=[tok, vec, tok, chunk, chunk] + _ffn_weight_specs(F4, D)[:2] + car.specs,
        out_specs=[tok, vec] + car.specs,
        out_shape=[jax.ShapeDtypeStruct((T, D), F32), jax.ShapeDtypeStruct((1, D), F32)] + car.out_shape,
        scratch_shapes=[pltpu.VMEM((tm, D), F32)] + car.scratch,
        compiler_params=_cp(2),
    )(x, g, dy, da, db, wa, wa, *car.arrs)


def _ffn_dw(h, dyb, da, db, z, l, f, car=_NOTHING):
    T, D = h.shape
    F4 = da.shape[-1]
    tk = _tile(T, 1024)
    nt = T // tk
    R2 = 3 * F4 // 2

    def body(*refs):
        (h_ref, dyb_ref, da_ref, db_ref, z_ref), (g_ref,), (accg, accu, accd), carried = car.split(refs, 5, 1)
        t = pl.program_id(1)
        car.when((pl.program_id(0) == 0) & (t == 0), carried, car.start)

        @pl.when(t == 0)
        def _():
            accg[...] = jnp.zeros_like(accg)
            accu[...] = jnp.zeros_like(accu)
            accd[...] = jnp.zeros_like(accd)

        hv = h_ref[...]
        accg[...] += _tn(da_ref[...], hv)
        accu[...] += _tn(db_ref[...], hv)
        accd[...] += _tn(z_ref[...], dyb_ref[...])

        @pl.when(t == nt - 1)
        def _():
            g_ref[0, 0:F4, :] = accg[...].astype(BF16)
            g_ref[0, F4:R2, :] = accu[0:R2 - F4, :].astype(BF16)
            g_ref[1, 0:2 * F4 - R2, :] = accu[R2 - F4:F4, :].astype(BF16)
            g_ref[1, 2 * F4 - R2:R2, :] = accd[...].astype(BF16)

        car.when((pl.program_id(0) == NS - 1) & (t == nt - 1), carried, car.finish)

    tok = pl.BlockSpec((tk, D), lambda s, t: (t, 0))
    chunk = pl.BlockSpec((None, tk, F4), lambda s, t: (s, t, 0))
    return _pallas(
        body, name=f"ffn_dw_{l}_{f}", grid=(NS, nt),
        in_specs=[tok, tok, chunk, chunk, chunk] + car.specs,
        out_specs=[pl.BlockSpec((2, None, R2, D), lambda s, t: (0, s, 0, 0))] + car.specs,
        out_shape=[jax.ShapeDtypeStruct((2, NS, R2, D), BF16)] + car.out_shape,
        scratch_shapes=[pltpu.VMEM((F4, D), F32), pltpu.VMEM((F4, D), F32), pltpu.VMEM((F4, D), F32)] + car.scratch,
        compiler_params=_cp(2),
    )(h, dyb, da, db, z, *car.arrs)


def _mix_in(x, g, win, l, car=_NOTHING):
    T, D = x.shape
    C4 = win.shape[-1]
    tm = _tile(T, 1024)
    ni = T // tm

    def body(*refs):
        (x_ref, g_ref, w_ref), (h_ref, u_ref), _, carried = car.split(refs, 3, 2)
        i = pl.program_id(0)
        j = pl.program_id(1)
        car.when((i == 0) & (j == 0), carried, car.start)
        car.when((i == ni // 2) & (j == 0), carried, car.middle)

        @pl.when(j == 0)
        def _():
            xh, _ = _rms_stats(x_ref[...])
            h_ref[...] = (xh * g_ref[...]).astype(BF16)

        u_ref[...] = _nn(h_ref[...], w_ref[...]).astype(BF16)
        car.when((i == ni - 1) & (j == NS - 1), carried, car.finish)

    return _pallas(
        body, name=f"mix_in_{l}", grid=(ni, NS),
        in_specs=[pl.BlockSpec((tm, D), lambda i, j: (i, 0)), pl.BlockSpec((1, D), lambda i, j: (0, 0)),
                  pl.BlockSpec((None, D, C4), lambda i, j: (j, 0, 0))] + car.specs,
        out_specs=[pl.BlockSpec((tm, D), lambda i, j: (i, 0)), pl.BlockSpec((tm, C4), lambda i, j: (i, j))] + car.specs,
        out_shape=[jax.ShapeDtypeStruct((T, D), BF16), jax.ShapeDtypeStruct((T, NS * C4), BF16)] + car.out_shape,
        scratch_shapes=car.scratch,
        compiler_params=_cp(2),
    )(x, g, win, *car.arrs)


def _pool_lane_window(n):
    lane = lax.broadcasted_iota(jnp.int32, (1, n), 1) // (n // len(POOL_WINDOWS))
    w = jnp.full((1, n), float(POOL_WINDOWS[-1]), F32)
    for gi in range(len(POOL_WINDOWS) - 1):
        w = jnp.where(lane == gi, float(POOL_WINDOWS[gi]), w)
    return lane, w


def _pool_select(lane, sums):
    out = sums[-1]
    for gi in range(len(sums) - 1):
        out = jnp.where(lane == gi, sums[gi], out)
    return out


def _back(v, s):
    return v if s == 0 else pltpu.roll(v, s, 0)


def _fwd_shift(v, s):
    return v if s == 0 else pltpu.roll(v, v.shape[0] - s, 0)


def _mix_seq_fwd(u, cdw, cb, lg, lb, sdw, pwblk, ps, bl, l, car=_NOTHING):
    T = u.shape[0]
    S = T // bl
    ts = _tile(S, 256)
    nt = S // ts
    DC, DS = cdw.shape[-1], sdw.shape[-1]
    o_ag, o_bg, o_cg, o_bx, o_p, o_end = DC, 2 * DC, 2 * DC + DS, 2 * DC + 2 * DS, 2 * DC + 3 * DS, 2 * DC + 4 * DS

    def body(*refs):
        ((up_ref, uc_ref, cdw_ref, cb_ref, lg_ref, lb_ref, sdw_ref, pw_ref, ps_ref),
         (act_ref, cv_ref), _, carried) = car.split(refs, 9, 2)
        i = pl.program_id(1)
        car.when((pl.program_id(0) == 0) & (i == 0), carried, car.start)
        car.when((pl.program_id(0) == bl // 2) & (i == 0), carried, car.middle)
        keep = jnp.where(i > 0, 1.0, 0.0).astype(F32)

        def ext(lo, hi):
            p = up_ref[ts - HALO:ts, lo:hi].astype(F32) * keep
            return jnp.concatenate([p, uc_ref[:, lo:hi].astype(F32)], axis=0)

        glu = ext(0, o_ag) * _sigmoid(ext(o_ag, o_bg))
        cv = jnp.zeros((ts, DC), F32) + cb_ref[...]
        for s in range(CONV_W):
            cv = cv + _back(glu, s)[HALO:, :] * cdw_ref[CONV_W - 1 - s:CONV_W - s, :]
        cv_ref[...] = cv
        mu = jnp.mean(cv, axis=-1, keepdims=True)
        xc = cv - mu
        lnv = xc * lax.rsqrt(jnp.mean(xc * xc, axis=-1, keepdims=True) + EPS) * lg_ref[...] + lb_ref[...]
        act_ref[:, 0:DC] = (lnv * _sigmoid(lnv)).astype(BF16)

        q = ext(o_cg, o_bx) * ext(o_bx, o_p)
        sc = jnp.zeros((ts, DS), F32)
        for s in range(SHORT_W):
            sc = sc + _back(q, s)[HALO:, :] * sdw_ref[SHORT_W - 1 - s:SHORT_W - s, :]
        act_ref[:, DC:DC + DS] = (uc_ref[:, o_bg:o_cg].astype(F32) * sc).astype(BF16)

        p = ext(o_p, o_end)
        lane, wl = _pool_lane_window(DS)
        sums, cur, sh = [], p, 1
        for _ in POOL_WINDOWS:
            cur = cur + _back(cur, sh)
            sums.append(cur[HALO:, :])
            sh *= 2
        pos = (i * ts + lax.broadcasted_iota(jnp.int32, (ts, 1), 0) + 1).astype(F32)
        pooled = _pool_select(lane, sums) / jnp.minimum(pos, wl) - p[HALO:, :]
        act_ref[:, DC + DS:DC + 2 * DS] = (_nn(pooled.astype(BF16), pw_ref[...]) * ps_ref[...]).astype(BF16)
        car.when((pl.program_id(0) == bl - 1) & (i == nt - 1), carried, car.finish)

    ucol = 2 * DC + 4 * DS
    full = lambda a: pl.BlockSpec(a.shape, lambda b, i: (0,) * a.ndim)
    return _pallas(
        body, name=f"mix_seq_fwd_{l}", grid=(bl, nt),
        in_specs=[pl.BlockSpec((ts, ucol), lambda b, i: (b * nt + jnp.maximum(i - 1, 0), 0)),
                  pl.BlockSpec((ts, ucol), lambda b, i: (b * nt + i, 0)),
                  full(cdw), full(cb), full(lg), full(lb), full(sdw), full(pwblk), full(ps)] + car.specs,
        out_specs=[pl.BlockSpec((ts, DC + 2 * DS), lambda b, i: (b * nt + i, 0)),
                   pl.BlockSpec((ts, DC), lambda b, i: (b * nt + i, 0))] + car.specs,
        out_shape=[jax.ShapeDtypeStruct((T, DC + 2 * DS), BF16), jax.ShapeDtypeStruct((T, DC), F32)] + car.out_shape,
        scratch_shapes=car.scratch,
        compiler_params=_cp(2),
    )(u, u, cdw, cb, lg, lb, sdw, pwblk, ps, *car.arrs)


def _mix_out(x, act, u, wp, wo, l, car=_NOTHING):
    T, D = x.shape
    tm = _tile(T, 512)
    ni = T // tm
    DA = act.shape[-1]
    DC, DS = DA // 2, DA // 4
    NB = D // NS
    gcol = (2 * DC + 4 * DS) // D

    def body(*refs):
        ((x_ref, act_ref, g0_ref, g1_ref, g2_ref, wp_ref, wo_ref),
         (xo_ref, y_ref, m_ref), _, carried) = car.split(refs, 7, 3)
        car.when(pl.program_id(0) == 0, carried, car.start)
        car.when(pl.program_id(0) == ni // 2, carried, car.middle)
        parts = [(0, DC), (DC, DC + DS), (DC + DS, DC + 2 * DS)]
        m = jnp.zeros((tm, D), F32)
        for k, (lo, hi) in enumerate(parts):
            av = act_ref[:, lo:hi]
            y = jnp.concatenate([_nn(av, wp_ref[s, lo:hi, :]) for s in range(NS)], axis=1)
            y_ref[:, k * D:(k + 1) * D] = y.astype(BF16)
            gl = (g0_ref, g1_ref, g2_ref)[k][...].astype(F32)
            m = m + _sigmoid(gl) * y
        mb = m.astype(BF16)
        m_ref[...] = mb
        out = _nn(mb[:, 0:NB], wo_ref[0])
        for s in range(1, NS):
            out = out + _nn(mb[:, s * NB:(s + 1) * NB], wo_ref[s])
        xo_ref[...] = x_ref[...] + out
        car.when(pl.program_id(0) == ni - 1, carried, car.finish)

    tok = lambda w: pl.BlockSpec((tm, w), lambda i: (i, 0))
    return _pallas(
        body, name=f"mix_out_{l}", grid=(ni,),
        in_specs=[tok(D), tok(DA),
                  pl.BlockSpec((tm, D), lambda i: (i, gcol)), pl.BlockSpec((tm, D), lambda i: (i, gcol + 1)),
                  pl.BlockSpec((tm, D), lambda i: (i, gcol + 2)),
                  pl.BlockSpec((NS, DA, NB), lambda i: (0, 0, 0)),
                  pl.BlockSpec((NS, NB, D), lambda i: (0, 0, 0))] + car.specs,
        out_specs=[tok(D), tok(3 * D), tok(D)] + car.specs,
        out_shape=[jax.ShapeDtypeStruct((T, D), F32), jax.ShapeDtypeStruct((T, 3 * D), BF16),
                   jax.ShapeDtypeStruct((T, D), BF16)] + car.out_shape,
        scratch_shapes=car.scratch,
        compiler_params=_cp(1),
    )(x, act, u, u, u, wp, wo, *car.arrs)


def _mix_out_bwd(dxn, y, u, act, m, wp, wo, l, car=_NOTHING):
    T, D = dxn.shape
    tm = _tile(T, 256)
    nt = T // tm
    DA = act.shape[-1]
    DC, DS = DA // 2, DA // 4
    NB = D // NS
    UC = u.shape[-1]
    g_lo = 2 * DC + 4 * DS
    gcol = g_lo // D
    parts = [(0, DC), (DC, DC + DS), (DC + DS, DC + 2 * DS)]

    def body(*refs):
        ((dx_ref, y_ref, g0_ref, g1_ref, g2_ref, act_ref, m_ref, wp_ref, wo_ref),
         (du_ref, dact_ref, gwo_ref, gwp_ref), (acc_wo, acc_wp), carried) = car.split(refs, 9, 4)
        i = pl.program_id(0)
        car.when(i == 0, carried, car.start)

        @pl.when(i == 0)
        def _():
            acc_wo[...] = jnp.zeros_like(acc_wo)
            acc_wp[...] = jnp.zeros_like(acc_wp)

        dxb = dx_ref[...].astype(BF16)
        dm = jnp.concatenate([_nt(dxb, wo_ref[s]) for s in range(NS)], axis=1)
        acc_wo[...] += _tn(m_ref[...], dxb)
        du_ref[:, 0:g_lo] = jnp.zeros((tm, g_lo), BF16)
        for k, (lo, hi) in enumerate(parts):
            sg = _sigmoid((g0_ref, g1_ref, g2_ref)[k][...].astype(F32))
            yk = y_ref[:, k * D:(k + 1) * D].astype(F32)
            du_ref[:, g_lo + k * D:g_lo + (k + 1) * D] = (dm * yk * sg * (1.0 - sg)).astype(BF16)
            dyk = (dm * sg).astype(BF16)
            dk = _nt(dyk[:, 0:NB], wp_ref[0, lo:hi, :])
            for s in range(1, NS):
                dk = dk + _nt(dyk[:, s * NB:(s + 1) * NB], wp_ref[s, lo:hi, :])
            dact_ref[:, lo:hi] = dk
            acc_wp[lo:hi, :] += _tn(act_ref[:, lo:hi], dyk)

        @pl.when(i == nt - 1)
        def _():
            for s in range(NS):
                for hf in range(2):
                    r0 = s * NB + hf * (NB // 2)
                    gwo_ref[hf, s] = acc_wo[r0:r0 + NB // 2, :].astype(BF16)
                    gwp_ref[hf, s] = acc_wp[hf * (DA // 2):(hf + 1) * (DA // 2), s * NB:(s + 1) * NB].astype(BF16)

        car.when(i == nt - 1, carried, car.finish)

    tok = lambda w: pl.BlockSpec((tm, w), lambda i: (i, 0))
    whole = lambda shp: pl.BlockSpec(shp, lambda i: (0,) * len(shp))
    return _pallas(
        body, name=f"mix_out_bwd_{l}", grid=(nt,),
        in_specs=[tok(D), tok(3 * D),
                  pl.BlockSpec((tm, D), lambda i: (i, gcol)), pl.BlockSpec((tm, D), lambda i: (i, gcol + 1)),
                  pl.BlockSpec((tm, D), lambda i: (i, gcol + 2)),
                  tok(DA), tok(D), whole((NS, DA, NB)), whole((NS, NB, D))] + car.specs,
        out_specs=[tok(UC), tok(DA), whole((2, NS, NB // 2, D)), whole((2, NS, DA // 2, NB))] + car.specs,
        out_shape=[jax.ShapeDtypeStruct((T, UC), BF16), jax.ShapeDtypeStruct((T, DA), F32),
                   jax.ShapeDtypeStruct((2, NS, NB // 2, D), BF16),
                   jax.ShapeDtypeStruct((2, NS, DA // 2, NB), BF16)] + car.out_shape,
        scratch_shapes=[pltpu.VMEM((D, D), F32), pltpu.VMEM((DA, D), F32)] + car.scratch,
        compiler_params=_cp(1),
    )(dxn, y, u, u, u, act, m, wp, wo, *car.arrs)


def _mix_seq_bwd(du, u, dact, cv, cdw, lg, lb, sdw, pwblk, ps, bl, l, car=_NOTHING):
    T = u.shape[0]
    S = T // bl
    ts = _tile(S, 256)
    nt = S // ts
    DC, DS = cdw.shape[-1], sdw.shape[-1]
    DA = DC + 2 * DS
    o_ag, o_bg, o_cg, o_bx, o_p, o_end = DC, 2 * DC, 2 * DC + DS, 2 * DC + 2 * DS, 2 * DC + 3 * DS, 2 * DC + 4 * DS
    n_f = ts + HALO

    def body(*refs):
        ((_, up_ref, uc_ref, un_ref, dac_ref, dan_ref, cvc_ref, cvn_ref,
          cdw_ref, lg_ref, lb_ref, sdw_ref, pw_ref, ps_ref),
         (du_ref, gcdw_ref, g512_ref, g256_ref, gpw_ref), _, carried) = car.split(refs, 14, 5)
        b = pl.program_id(0)
        i = pl.program_id(1)
        car.when((b == 0) & (i == 0), carried, car.start)
        keep_p = jnp.where(i > 0, 1.0, 0.0).astype(F32)
        keep_n = jnp.where(i < nt - 1, 1.0, 0.0).astype(F32)

        @pl.when((b == 0) & (i == 0))
        def _():
            gcdw_ref[...] = jnp.zeros_like(gcdw_ref)
            g512_ref[...] = jnp.zeros_like(g512_ref)
            g256_ref[...] = jnp.zeros_like(g256_ref)
            gpw_ref[...] = jnp.zeros_like(gpw_ref)

        def back(lo, hi):
            p = up_ref[ts - HALO:ts, lo:hi].astype(F32) * keep_p
            return jnp.concatenate([p, uc_ref[:, lo:hi].astype(F32)], axis=0)

        def fwd(cur, nxt, lo, hi, mask):
            n = nxt[0:HALO, lo:hi].astype(F32)
            if mask:
                n = n * keep_n
            return jnp.concatenate([cur[:, lo:hi].astype(F32), n], axis=0)

        cvx = fwd(cvc_ref, cvn_ref, 0, DC, False)
        dA = fwd(dac_ref, dan_ref, 0, DC, True)
        mu = jnp.mean(cvx, axis=-1, keepdims=True)
        xc = cvx - mu
        rs = lax.rsqrt(jnp.mean(xc * xc, axis=-1, keepdims=True) + EPS)
        xh = xc * rs
        lnv = xh * lg_ref[...] + lb_ref[...]
        sg = _sigmoid(lnv)
        dln = dA * (sg * (1.0 + lnv * (1.0 - sg)))
        dxh = dln * lg_ref[...]
        dcv = rs * (dxh - jnp.mean(dxh, axis=-1, keepdims=True) - xh * jnp.mean(dxh * xh, axis=-1, keepdims=True))
        g512_ref[0:1, :] += jnp.sum(dcv[0:ts], axis=0, keepdims=True)
        g512_ref[1:2, :] += jnp.sum((dln * xh)[0:ts], axis=0, keepdims=True)
        g512_ref[2:3, :] += jnp.sum(dln[0:ts], axis=0, keepdims=True)

        av = back(0, o_ag)
        sga = _sigmoid(back(o_ag, o_bg))
        glu = av * sga
        dcv_c = dcv[0:ts]
        dglu = jnp.zeros((ts, DC), F32)
        for s in range(CONV_W):
            k = CONV_W - 1 - s
            dglu = dglu + _fwd_shift(dcv, s)[0:ts, :] * cdw_ref[k:k + 1, :]
            gcdw_ref[k:k + 1, :] += jnp.sum(_back(glu, s)[HALO:, :] * dcv_c, axis=0, keepdims=True)
        sga_c = sga[HALO:, :]
        du_ref[:, 0:o_ag] = (dglu * sga_c).astype(BF16)
        du_ref[:, o_ag:o_bg] = (dglu * av[HALO:, :] * sga_c * (1.0 - sga_c)).astype(BF16)

        cg = back(o_cg, o_bx)
        bx = back(o_bx, o_p)
        q = cg * bx
        sc = jnp.zeros((ts, DS), F32)
        for s in range(SHORT_W):
            sc = sc + _back(q, s)[HALO:, :] * sdw_ref[SHORT_W - 1 - s:SHORT_W - s, :]
        dB = fwd(dac_ref, dan_ref, DC, DC + DS, True)
        ds = dB * fwd(uc_ref, un_ref, o_bg, o_cg, False)
        du_ref[:, o_bg:o_cg] = (dB[0:ts] * sc).astype(BF16)
        ds_c = ds[0:ts]
        dq = jnp.zeros((ts, DS), F32)
        for s in range(SHORT_W):
            k = SHORT_W - 1 - s
            dq = dq + _fwd_shift(ds, s)[0:ts, :] * sdw_ref[k:k + 1, :]
            g256_ref[k:k + 1, :] += jnp.sum(_back(q, s)[HALO:, :] * ds_c, axis=0, keepdims=True)
        du_ref[:, o_cg:o_bx] = (dq * bx[HALO:, :]).astype(BF16)
        du_ref[:, o_bx:o_p] = (dq * cg[HALO:, :]).astype(BF16)

        p = back(o_p, o_end)
        lane, wl = _pool_lane_window(DS)
        sums, cur, sh = [], p, 1
        for _ in POOL_WINDOWS:
            cur = cur + _back(cur, sh)
            sums.append(cur[HALO:, :])
            sh *= 2
        pos_c = (i * ts + lax.broadcasted_iota(jnp.int32, (ts, 1), 0) + 1).astype(F32)
        pooled = (_pool_select(lane, sums) / jnp.minimum(pos_c, wl) - p[HALO:, :]).astype(BF16)
        pwv = _nn(pooled, pw_ref[...])
        dC = fwd(dac_ref, dan_ref, DC + DS, DA, True)
        g256_ref[SHORT_W:SHORT_W + 1, :] += jnp.sum(dC[0:ts] * pwv, axis=0, keepdims=True)
        dpw = (dC * ps_ref[...]).astype(BF16)
        gpw_ref[...] += _tn(pooled, dpw[0:ts])
        dpl = _nt(dpw, pw_ref[...])
        pos_f = (i * ts + lax.broadcasted_iota(jnp.int32, (n_f, 1), 0) + 1).astype(F32)
        e = dpl / jnp.minimum(pos_f, wl)
        fsums, cur, sh = [], e, 1
        for _ in POOL_WINDOWS:
            cur = cur + _fwd_shift(cur, sh)
            fsums.append(cur[0:ts, :])
            sh *= 2
        du_ref[:, o_p:o_end] = (_pool_select(lane, fsums) - dpl[0:ts]).astype(BF16)
        car.when((b == bl - 1) & (i == nt - 1), carried, car.finish)

    full = lambda a: pl.BlockSpec(a.shape, lambda b, i: (0,) * a.ndim)
    row = lambda w, f: pl.BlockSpec((ts, w), lambda b, i: (b * nt + f(i), 0))
    prv = lambda i: jnp.maximum(i - 1, 0)
    nxt = lambda i: jnp.minimum(i + 1, nt - 1)
    cur = lambda i: i
    return _pallas(
        body, name=f"mix_seq_bwd_{l}", grid=(bl, nt),
        in_specs=[ANY, row(o_end, prv), row(o_end, cur), row(o_end, nxt),
                  row(DA, cur), row(DA, nxt), row(DC, cur), row(DC, nxt),
                  full(cdw), full(lg), full(lb), full(sdw), full(pwblk), full(ps)] + car.specs,
        out_specs=[row(o_end, cur), full(cdw),
                   pl.BlockSpec((8, DC), lambda b, i: (0, 0)), pl.BlockSpec((8, DS), lambda b, i: (0, 0)),
                   full(pwblk)] + car.specs,
        out_shape=[jax.ShapeDtypeStruct(du.shape, BF16), jax.ShapeDtypeStruct(cdw.shape, F32),
                   jax.ShapeDtypeStruct((8, DC), F32), jax.ShapeDtypeStruct((8, DS), F32),
                   jax.ShapeDtypeStruct(pwblk.shape, F32)] + car.out_shape,
        scratch_shapes=car.scratch,
        input_output_aliases={0: 0},
        compiler_params=_cp(2),
    )(du, u, u, u, dact, dact, cv, cv, cdw, lg, lb, sdw, pwblk, ps, *car.arrs)


def _mix_in_bwd(x, g, dxn, du, win, l):
    T, D = x.shape
    C4 = win.shape[-1]
    tm = _tile(T, 1024)

    def body(x_ref, g_ref, dxn_ref, du_ref, w_ref, dx_ref, dg_ref, dh):
        i = pl.program_id(0)
        j = pl.program_id(1)

        @pl.when(j == 0)
        def _():
            dh[...] = jnp.zeros_like(dh)

        @pl.when((i == 0) & (j == 0))
        def _():
            dg_ref[...] = jnp.zeros_like(dg_ref)

        dh[...] += _nt(du_ref[...], w_ref[...])

        @pl.when(j == NS - 1)
        def _():
            dxr, dg = _rms_bwd(dh[...], x_ref[...], g_ref[...])
            dx_ref[...] = dxn_ref[...] + dxr
            dg_ref[...] += dg

    tok = pl.BlockSpec((tm, D), lambda i, j: (i, 0))
    vec = pl.BlockSpec((1, D), lambda i, j: (0, 0))
    return _pallas(
        body, name=f"mix_in_bwd_{l}", grid=(T // tm, NS),
        in_specs=[tok, vec, tok, pl.BlockSpec((tm, C4), lambda i, j: (i, j)),
                  pl.BlockSpec((None, D, C4), lambda i, j: (j, 0, 0))],
        out_specs=[tok, vec],
        out_shape=[jax.ShapeDtypeStruct((T, D), F32), jax.ShapeDtypeStruct((1, D), F32)],
        scratch_shapes=[pltpu.VMEM((tm, D), F32)],
        compiler_params=_cp(2),
    )(x, g, dxn, du, win)


def _mix_in_dw(h, du, l):
    T, D = h.shape
    C4 = du.shape[-1] // NS
    tk = _tile(T, 1024)
    nt = T // tk

    def body(h_ref, du_ref, g_ref, acc):
        t = pl.program_id(1)

        @pl.when(t == 0)
        def _():
            acc[...] = jnp.zeros_like(acc)

        acc[...] += _tn(h_ref[...], du_ref[...])

        @pl.when(t == nt - 1)
        def _():
            g_ref[0] = acc[0:D // 2, :].astype(BF16)
            g_ref[1] = acc[D // 2:D, :].astype(BF16)

    return _pallas(
        body, name=f"mix_in_dw_{l}", grid=(NS, nt),
        in_specs=[pl.BlockSpec((tk, D), lambda s, t: (t, 0)), pl.BlockSpec((tk, C4), lambda s, t: (t, s))],
        out_specs=pl.BlockSpec((2, None, D // 2, C4), lambda s, t: (0, s, 0, 0)),
        out_shape=jax.ShapeDtypeStruct((2, NS, D // 2, C4), BF16),
        scratch_shapes=[pltpu.VMEM((D, C4), F32)],
        compiler_params=_cp(2),
    )(h, du)


def _loss_head(x, g, target):
    T, D = x.shape
    tm = _tile(T, 512)

    def body(x_ref, g_ref, t_ref, dx_ref, loss_ref, dg_ref):
        @pl.when(pl.program_id(0) == 0)
        def _():
            loss_ref[...] = jnp.zeros_like(loss_ref)
            dg_ref[...] = jnp.zeros_like(dg_ref)

        xv = x_ref[...]
        xh, rs = _rms_stats(xv)
        gv = g_ref[...]
        e = xh * gv - t_ref[...]
        loss_ref[...] += 0.5 * jnp.sum(jnp.mean(e * e, axis=-1, keepdims=True))
        dy = e * (1.0 / D)
        dyg = dy * gv
        dx_ref[...] = rs * (dyg - xh * jnp.mean(dyg * xh, axis=-1, keepdims=True))
        dg_ref[...] += jnp.sum(dy * xh, axis=0, keepdims=True)

    tok = pl.BlockSpec((tm, D), lambda i: (i, 0))
    vec = pl.BlockSpec((1, D), lambda i: (0, 0))
    return _pallas(
        body, name="loss_head", grid=(T // tm,),
        in_specs=[tok, vec, tok],
        out_specs=[tok, pl.BlockSpec((8, 128), lambda i: (0, 0)), vec],
        out_shape=[jax.ShapeDtypeStruct((T, D), F32), jax.ShapeDtypeStruct((8, 128), F32),
                   jax.ShapeDtypeStruct((1, D), F32)],
        compiler_params=_cp(1),
    )(x, g, target)


def _block_diag(pw):
    G, c, _ = pw.shape
    out = jnp.zeros((G * c, G * c), pw.dtype)
    for gi in range(G):
        out = lax.dynamic_update_slice(out, pw[gi], (gi * c, gi * c))
    return out


def _pad_rows(a, n):
    return jnp.pad(a, ((0, n - a.shape[0]), (0, 0)))


def _merge(g):
    return g.reshape(g.shape[0], g.shape[1] * g.shape[2], g.shape[3])


def _split_dws(dws_g, cw, sw):
    cdw = jnp.transpose(dws_g[:, 0:CONV_W, 0:cw], (1, 0, 2)).reshape(CONV_W, NS * cw)
    sdw = jnp.transpose(dws_g[:, 32:32 + SHORT_W, 0:sw], (1, 0, 2)).reshape(SHORT_W, NS * sw)
    return cdw, sdw


def _fwd_bwd(x3, target3, shards, small, dw_widths, first, sum_block):
    bl, S, D = x3.shape
    T = bl * S
    x = x3.reshape(T, D)
    target = target3.reshape(T, D)
    L = len(shards)
    cw, sw = dw_widths
    row = lambda v: v[None, :]
    gather = lambda arrs: _Carried("gather", arrs)
    exchange = lambda arrs: _Carried("exchange", arrs)
    scatter = lambda arrs: _Carried("scatter", arrs)

    saved = []
    wa1, wb1 = first
    for l in range(L):
        sh = shards[l]
        nxt = shards[l + 1] if l + 1 < L else None
        sp = dict(pwblk=_block_diag(small["pool_w"][l]).astype(BF16), cb=row(small["conv_b"][l]),
                  lg=row(small["conv_ln_g"][l]), lb=row(small["conv_ln_b"][l]), ps=row(small["pool_scale"][l]),
                  g1=row(small["norm_ffn1_g"][l]), gm=row(small["norm_mix_g"][l]), g2=row(small["norm_ffn2_g"][l]))
        x0 = x
        x1, h1, p1, q1, z1, win_g, wp_g, wo_g, dws_g = _ffn_fwd(x0, sp["g1"], wa1, wb1, l, 0, gather(sh["mx"]))
        win_g, wp_g, wo_g = _merge(win_g), _merge(wp_g), _merge(wo_g)
        cdw, sdw = _split_dws(_merge(dws_g), cw, sw)
        sp["cdw"], sp["sdw"] = _pad_rows(cdw, HALO), _pad_rows(sdw, 8)
        hm, u, wa2 = _mix_in(x1, sp["gm"], win_g, l, gather([sh["f2a"]]))
        act, cv, wb2 = _mix_seq_fwd(u, sp["cdw"], sp["cb"], sp["lg"], sp["lb"], sp["sdw"], sp["pwblk"], sp["ps"],
                                    bl, l, gather([sh["f2b"]]))
        wb2 = _merge(wb2)
        res_o = _mix_out(x1, act, u, wp_g, wo_g, l, gather([nxt["f1b"]]) if nxt else _NOTHING)
        x2, y, m = res_o[:3]
        res_f = _ffn_fwd(x2, sp["g2"], wa2, wb2, l, 1, gather([nxt["f1a"]]) if nxt else _NOTHING)
        x, h2, p2, q2, z2 = res_f[:5]
        saved.append(dict(sp=sp, x0=x0, x1=x1, x2=x2, h1=h1, p1=p1, q1=q1, z1=z1, hm=hm, u=u, act=act, cv=cv, y=y, m=m,
                          h2=h2, p2=p2, q2=q2, z2=z2, wa1=wa1, wb1=wb1, wa2=wa2, wb2=wb2, win=win_g, wp=wp_g, wo=wo_g))
        if nxt:
            wa1, wb1 = res_f[5], _merge(res_o[3])

    dx, loss_blk, dgf = _loss_head(x, row(small["final_norm_g"]), target)
    loss = loss_blk[0, 0]

    sg = {k: [None] * L for k in ("norm_ffn1_g", "norm_mix_g", "norm_ffn2_g", "conv_dw", "conv_b", "conv_ln_g",
                                  "conv_ln_b", "short_dw", "pool_w", "pool_scale")}
    blocks = []
    g_up, l_up = [], None
    G, c = small["pool_w"].shape[1:3]
    for l in reversed(range(L)):
        sv = saved[l]
        sp = sv["sp"]
        res = _ffn_bwd(sv["x2"], sp["g2"], dx, sv["p2"], sv["q2"], sv["wa2"], sv["wb2"], l, 1, exchange(g_up))
        dx, dg2, da, db, dyb = res[:5]
        p_up = sum_block(g_up, res[5:]) if g_up else []
        res = _ffn_dw(sv["h2"], dyb, da, db, sv["z2"], l, 1, scatter(p_up))
        g_f2 = [res[0]]
        if p_up:
            blocks.append((l_up, "f1", p_up, res[1:]))
        res = _mix_out_bwd(dx, sv["y"], sv["u"], sv["act"], sv["m"], sv["wp"], sv["wo"], l, exchange(g_f2))
        du, dact, g_o, g_p = res[:4]
        p_f2 = sum_block(g_f2, res[4:])
        res = _mix_seq_bwd(du, sv["u"], dact, sv["cv"], sp["cdw"], sp["lg"], sp["lb"],
                           sp["sdw"], sp["pwblk"], sp["ps"], bl, l, scatter(p_f2))
        du, gcdw, g512, g256, gpw = res[:5]
        blocks.append((l, "f2", p_f2, res[5:]))
        dx, dgm = _mix_in_bwd(sv["x1"], sp["gm"], dx, du, sv["win"], l)
        g_mx = [_mix_in_dw(sv["hm"], du, l), g_p, g_o]
        if l > 0:
            res = _ffn_bwd(sv["x0"], sp["g1"], dx, sv["p1"], sv["q1"], sv["wa1"], sv["wb1"], l, 0, exchange(g_mx))
            dx, dg1, da, db, dyb = res[:5]
            p_mx = sum_block(g_mx, res[5:])
            res = _ffn_dw(sv["h1"], dyb, da, db, sv["z1"], l, 0, scatter(p_mx))
            blocks.append((l, "mx", p_mx, res[1:]))
            g_up, l_up = [res[0]], l
        else:
            res = _ffn_bwd_first(dx, sv["p1"], sv["q1"], sv["wb1"], l, 0, exchange(g_mx))
            da, db, dyb = res[:3]
            p_mx = sum_block(g_mx, res[3:])
            res = _ffn_dw(sv["h1"], dyb, da, db, sv["z1"], l, 0, scatter(p_mx))
            blocks.append((l, "mx", p_mx, res[1:]))
            g_f1 = [res[0]]
            p_f1 = sum_block(g_f1, _run_carried(exchange(g_f1), "last"))
            res = _ffn_bwd_second(sv["x0"], sp["g1"], dx, da, db, sv["wa1"], l, 0, scatter(p_f1))
            dx, dg1 = res[:2]
            blocks.append((l, "f1", p_f1, res[2:]))
        sg["norm_ffn1_g"][l], sg["norm_mix_g"][l], sg["norm_ffn2_g"][l] = dg1[0], dgm[0], dg2[0]
        sg["conv_dw"][l] = gcdw[:CONV_W]
        sg["conv_b"][l], sg["conv_ln_g"][l], sg["conv_ln_b"][l] = g512[0], g512[1], g512[2]
        sg["short_dw"][l] = g256[:SHORT_W]
        sg["pool_scale"][l] = g256[SHORT_W]
        sg["pool_w"][l] = jnp.stack([gpw[gi * c:(gi + 1) * c, gi * c:(gi + 1) * c] for gi in range(G)])
    small_g = {k: jnp.stack(v) for k, v in sg.items()}
    small_g["final_norm_g"] = dgf[0]
    return loss, dx.reshape(bl, S, D), blocks, small_g


def _share_final_and_reduce_small(fs, v):
    n = len(fs)
    L = fs[0].shape[0]
    R, W = v.shape

    def body(*refs):
        v_ref = refs[0]
        out_ref = refs[1 + n]
        outs = refs[2 + n:2 + 2 * n]
        buf, send_sems, recv_sems, share_send, share_recv = refs[2 + 2 * n:]
        x, y, c, _ = _place()
        sib = (x, y, 1 - c)
        me = 4 * x + 2 * y + c

        def share(ai, l, half):
            blk = outs[ai].at[l, half]
            return pltpu.make_async_remote_copy(src_ref=blk, dst_ref=blk, send_sem=share_send.at[ai, l],
                                                recv_sem=share_recv.at[ai, l], device_id=sib, device_id_type=MESH)

        for ai in range(n):
            for l in range(L):
                share(ai, l, c).start()

        buf[me] = v_ref[...]
        cps = []
        for k in range(1, 8):
            kx, ky, kc = (k >> 2) & 1, (k >> 1) & 1, k & 1
            to = (1 - x if kx else x, 1 - y if ky else y, 1 - c if kc else c)
            cp = pltpu.make_async_remote_copy(src_ref=v_ref, dst_ref=buf.at[me], send_sem=send_sems.at[k - 1],
                                              recv_sem=recv_sems.at[k - 1], device_id=to, device_id_type=MESH)
            cp.start()
            cps.append(cp)
        for cp in cps:
            cp.wait()
        acc = buf[0]
        for d in range(1, 8):
            acc = acc + buf[d]
        out_ref[...] = acc

        for ai in range(n):
            for l in range(L):
                share(ai, l, 1 - c).wait_recv()
                share(ai, l, c).wait_send()

    vmem = pl.BlockSpec(memory_space=pltpu.VMEM)
    out = _pallas(
        body, name="share_final_and_reduce_small",
        in_specs=[vmem] + [ANY] * n, out_specs=[vmem] + [ANY] * n,
        out_shape=[jax.ShapeDtypeStruct((R, W), F32)] + [jax.ShapeDtypeStruct(f.shape, f.dtype) for f in fs],
        scratch_shapes=[pltpu.VMEM((8, R, W), F32), pltpu.SemaphoreType.DMA((7,)), pltpu.SemaphoreType.DMA((7,)),
                        pltpu.SemaphoreType.DMA((n, L)), pltpu.SemaphoreType.DMA((n, L))],
        input_output_aliases={1 + i: 1 + i for i in range(n)},
        compiler_params=pltpu.CompilerParams(has_side_effects=True, vmem_limit_bytes=VMEM_LIMIT),
    )(v, *fs)
    return out[1:], out[0]


def _row_tile(n, w, streams):
    for t in (1056, 1024, 704, 512, 352, 256, 128, 64, 32, 16):
        if n % t == 0 and 2 * streams * t * w * 4 <= VMEM_LIMIT // 2:
            return t
    raise ValueError((n, w))


def _sum_sibling(tag, cidx, g, r):
    _, N, W = g.shape
    tr = _row_tile(N, W, 3)

    def body(c_ref, g_ref, r_ref, o_ref):
        del c_ref
        o_ref[...] = (g_ref[...].astype(F32) + r_ref[...].astype(F32)).astype(BF16)

    return _pallas(
        body, name=f"grad_sum_sibling_{tag}",
        grid_spec=pltpu.PrefetchScalarGridSpec(
            num_scalar_prefetch=1, grid=(N // tr,),
            in_specs=[pl.BlockSpec((None, tr, W), lambda i, c: (c[0], i, 0)),
                      pl.BlockSpec((tr, W), lambda i, c: (i, 0))],
            out_specs=pl.BlockSpec((tr, W), lambda i, c: (i, 0))),
        out_shape=jax.ShapeDtypeStruct((N, W), BF16),
        compiler_params=_cp(1),
    )(cidx, g, r)


def _sum_final(tag, idx, p, r2, l, L, prev):
    _, r, W = p.shape
    tr = _row_tile(r, W, 5)

    def body(*refs):
        p_ref, r2_ref = refs[1:3]
        o_ref = refs[-1]
        acc = p_ref[...].astype(F32)
        for k in range(3):
            acc = acc + r2_ref[k].astype(F32)
        o_ref[...] = acc

    in_specs = [pl.BlockSpec((None, tr, W), lambda i, s: (s[1], i, 0)),
                pl.BlockSpec((3, tr, W), lambda i, s: (0, i, 0))]
    args = [idx, p, r2]
    aliases = {}
    if prev is not None:
        in_specs.append(ANY)
        args.append(prev)
        aliases = {3: 0}
    return _pallas(
        body, name=f"grad_sum_final_{tag}",
        grid_spec=pltpu.PrefetchScalarGridSpec(
            num_scalar_prefetch=1, grid=(r // tr,), in_specs=in_specs,
            out_specs=pl.BlockSpec((None, None, tr, W), lambda i, s: (l, s[0], i, 0))),
        out_shape=jax.ShapeDtypeStruct((L, 2, r, W), F32),
        input_output_aliases=aliases,
        compiler_params=_cp(1),
    )(*args)


def _adam_math(w, g, m, v):
    m = ADAM_B1 * m + (1.0 - ADAM_B1) * g
    v = ADAM_B2 * v + (1.0 - ADAM_B2) * (g * g)
    m_hat = m / (1.0 - ADAM_B1 ** ADAM_STEP)
    v_hat = v / (1.0 - ADAM_B2 ** ADAM_STEP)
    delta = -ADAM_LR * (m_hat / (jnp.sqrt(v_hat) + ADAM_EPS) + ADAM_WD * w)
    return delta, m, v


def _adam_big(name, w, m, v, gfull, row0):
    L, r, W = w.shape
    tr = _row_tile(r, W, 8)
    assert row0 % tr == 0
    off = row0 // tr

    def body(w_ref, m_ref, v_ref, g_ref, go_ref, d_ref, mo_ref, vo_ref):
        g = g_ref[...]
        d, mn, vn = _adam_math(w_ref[...], g, m_ref[...], v_ref[...])
        go_ref[...] = g
        d_ref[...] = d
        mo_ref[...] = mn
        vo_ref[...] = vn

    blk = pl.BlockSpec((None, tr, W), lambda l, i: (l, i, 0))
    shp = jax.ShapeDtypeStruct(w.shape, F32)
    return _pallas(
        body, name=f"adam_{name}", grid=(L, r // tr),
        in_specs=[blk, blk, blk, pl.BlockSpec((None, tr, W), lambda l, i: (l, off + i, 0))],
        out_specs=[blk] * 4, out_shape=[shp] * 4,
        compiler_params=_cp(2),
    )(w, m, v, gfull)


def _adam_small(ws, gs, ms, vs):
    n = len(ws)

    def body(*refs):
        for k in range(n):
            w_ref, g_ref, m_ref, v_ref = (refs[j * n + k] for j in range(4))
            d_ref, mo_ref, vo_ref = (refs[(4 + j) * n + k] for j in range(3))
            d, mn, vn = _adam_math(w_ref[...], g_ref[...], m_ref[...], v_ref[...])
            d_ref[...] = d
            mo_ref[...] = mn
            vo_ref[...] = vn

    spec = pl.BlockSpec(memory_space=pltpu.VMEM)
    shp = [jax.ShapeDtypeStruct(w.shape, F32) for w in ws]
    out = _pallas(body, name="adam_small", in_specs=[spec] * (4 * n), out_specs=[spec] * (3 * n),
                  out_shape=shp * 3)(*ws, *gs, *ms, *vs)
    return out[:n], out[n:2 * n], out[2 * n:]


_WEIGHTS = ['norm_ffn1_g', 'ffn1_w_gate', 'ffn1_w_up', 'ffn1_w_down', 'norm_mix_g', 'w_in', 'conv_dw', 'conv_b',
            'conv_ln_g', 'conv_ln_b', 'w_pa', 'short_dw', 'w_pb', 'pool_w', 'pool_scale', 'w_pc', 'w_o',
            'norm_ffn2_g', 'ffn2_w_gate', 'ffn2_w_up', 'ffn2_w_down', 'final_norm_g']
_BIG = ('ffn1_w_gate', 'ffn1_w_up', 'ffn1_w_down', 'w_in', 'w_pa', 'w_pb', 'w_pc', 'w_o',
        'ffn2_w_gate', 'ffn2_w_up', 'ffn2_w_down')
_TRANSPOSED = ('ffn1_w_gate', 'ffn1_w_up', 'ffn2_w_gate', 'ffn2_w_up')
_SMALL = tuple(n for n in _WEIGHTS if n not in _BIG)
_SMALL_REDUCED = ('norm_ffn1_g', 'norm_mix_g', 'conv_b', 'conv_ln_g', 'conv_ln_b', 'pool_w', 'pool_scale',
                  'norm_ffn2_g', 'final_norm_g', 'conv_dw', 'short_dw')


def _pack(arrs, rows_multiple=8):
    flat = jnp.concatenate([a.reshape(-1) for a in arrs])
    n = flat.shape[0]
    per = 128 * rows_multiple
    padded = -(-n // per) * per
    return jnp.pad(flat, (0, padded - n)).reshape(padded // 128, 128)


def _unpack(buf, shapes):
    flat = buf.reshape(-1)
    out, o = [], 0
    for s in shapes:
        k = 1
        for d in s:
            k *= d
        out.append(flat[o:o + k].reshape(s))
        o += k
    return out


def _halves(a):
    return a.reshape(2, a.shape[0] // 2, a.shape[1])


def _step(P, M, V, x, loss_target):
    tr = lambda a: jnp.transpose(a, (0, 2, 1))
    bf = lambda a: a.astype(BF16)
    L = P['w_in'].shape[0]
    cw, sw = P['conv_dw'].shape[-1], P['short_dw'].shape[-1]
    ffa = [jnp.stack([bf(tr(P[f'ffn{f}_w_gate'])), bf(tr(P[f'ffn{f}_w_up']))], axis=1) for f in (1, 2)]
    ffb = [bf(P[f'ffn{f}_w_down']) for f in (1, 2)]
    win = bf(P['w_in'])
    wp = jnp.concatenate([bf(P['w_pa']), bf(P['w_pb']), bf(P['w_pc'])], axis=1)
    wo = bf(P['w_o'])
    dws = jnp.zeros((L, 64, 128), F32)
    dws = dws.at[:, 0:CONV_W, 0:cw].set(P['conv_dw']).at[:, 32:32 + SHORT_W, 0:sw].set(P['short_dw'])
    shards = [dict(f1a=ffa[0][l], f1b=_halves(ffb[0][l]), f2a=ffa[1][l], f2b=_halves(ffb[1][l]),
                   mx=(_halves(win[l]), _halves(wp[l]), _halves(wo[l]), _halves(dws[l]))) for l in range(L)]

    xi, yi, ci = lax.axis_index("x"), lax.axis_index("y"), lax.axis_index("c")
    chip = 2 * xi + yi
    cidx = jnp.stack([ci]).astype(jnp.int32)
    idx = jnp.stack([ci, chip]).astype(jnp.int32)
    count = [0]

    def sum_block(gs, r1):
        t0 = count[0]
        count[0] += len(gs)
        parts = []
        for k, (g, r) in enumerate(zip(gs, r1)):
            W = g.shape[-1]
            p = _sum_sibling(t0 + k, cidx, g.reshape(2, -1, W), r.reshape(-1, W))
            parts.append(p.reshape(g.shape[1:]))
        return parts

    wa1, wb1 = _run_carried(_Carried("gather", [shards[0]["f1a"], shards[0]["f1b"]]), "first")
    small = {n: P[n] for n in _SMALL if n not in ('conv_dw', 'short_dw')}
    loss, dx, blocks, small_g = _fwd_bwd(x, loss_target, shards, small, (cw, sw), (wa1, _merge(wb1)), sum_block)

    finals = {}
    for t, (l, name, parts, r2) in enumerate(blocks):
        prev = finals.get(name, [None] * len(parts))
        finals[name] = [_sum_final(f"{t}_{k}", idx, p, r, l, L, pv) for k, (p, r, pv) in enumerate(zip(parts, r2, prev))]
    shared, tot = _share_final_and_reduce_small(finals["f1"] + finals["f2"] + finals["mx"],
                                                _pack([small_g[n] for n in _SMALL_REDUCED] + [loss.reshape(1)]))
    f_f1, f_f2, f_in, f_p, f_o = [f.reshape(L, -1, f.shape[-1]) for f in shared]
    *tot, loss = _unpack(tot, [small_g[n].shape for n in _SMALL_REDUCED] + [()])
    tot = dict(zip(_SMALL_REDUCED, tot))
    tot['conv_dw'] = lax.dynamic_slice_in_dim(tot['conv_dw'], chip * cw, cw, axis=2)
    tot['short_dw'] = lax.dynamic_slice_in_dim(tot['short_dw'], chip * sw, sw, axis=2)

    F4 = P['ffn1_w_down'].shape[1]
    dc, ds = P['w_pa'].shape[1], P['w_pb'].shape[1]
    src = {'ffn1_w_gate': (f_f1, 0), 'ffn1_w_up': (f_f1, F4), 'ffn1_w_down': (f_f1, 2 * F4),
           'ffn2_w_gate': (f_f2, 0), 'ffn2_w_up': (f_f2, F4), 'ffn2_w_down': (f_f2, 2 * F4),
           'w_in': (f_in, 0), 'w_pa': (f_p, 0), 'w_pb': (f_p, dc), 'w_pc': (f_p, dc + ds), 'w_o': (f_o, 0)}
    grads, deltas, new_m, new_v = {}, {}, {}, {}
    for n in _BIG:
        gfull, row0 = src[n]
        if n in _TRANSPOSED:
            outs = _adam_big(n, tr(P[n]), tr(M[n]), tr(V[n]), gfull, row0)
            grads[n], deltas[n], new_m[n], new_v[n] = [tr(o) for o in outs]
        else:
            grads[n], deltas[n], new_m[n], new_v[n] = _adam_big(n, P[n], M[n], V[n], gfull, row0)
    as2d = lambda a: a.reshape(1, -1) if a.ndim == 1 else a
    d_s, m_s, v_s = _adam_small([as2d(P[n]) for n in _SMALL], [as2d(tot[n]) for n in _SMALL],
                                [as2d(M[n]) for n in _SMALL], [as2d(V[n]) for n in _SMALL])
    for n, d, mm, vv in zip(_SMALL, d_s, m_s, v_s):
        shp = P[n].shape
        grads[n], deltas[n], new_m[n], new_v[n] = tot[n], d.reshape(shp), mm.reshape(shp), vv.reshape(shp)

    return (loss, dx, *[grads[n] for n in _WEIGHTS], *[deltas[n] for n in _WEIGHTS],
            *[new_m[n] for n in _WEIGHTS], *[new_v[n] for n in _WEIGHTS])


def kernel(x, norm_ffn1_g, ffn1_w_gate, ffn1_w_up, ffn1_w_down, norm_mix_g, w_in, conv_dw, conv_b, conv_ln_g, conv_ln_b, w_pa, short_dw, w_pb, pool_w, pool_scale, w_pc, w_o, norm_ffn2_g, ffn2_w_gate, ffn2_w_up, ffn2_w_down, final_norm_g, loss_target, m_norm_ffn1_g, m_ffn1_w_gate, m_ffn1_w_up, m_ffn1_w_down, m_norm_mix_g, m_w_in, m_conv_dw, m_conv_b, m_conv_ln_g, m_conv_ln_b, m_w_pa, m_short_dw, m_w_pb, m_pool_w, m_pool_scale, m_w_pc, m_w_o, m_norm_ffn2_g, m_ffn2_w_gate, m_ffn2_w_up, m_ffn2_w_down, m_final_norm_g, v_norm_ffn1_g, v_ffn1_w_gate, v_ffn1_w_up, v_ffn1_w_down, v_norm_mix_g, v_w_in, v_conv_dw, v_conv_b, v_conv_ln_g, v_conv_ln_b, v_w_pa, v_short_dw, v_w_pb, v_pool_w, v_pool_scale, v_w_pc, v_w_o, v_norm_ffn2_g, v_ffn2_w_gate, v_ffn2_w_up, v_ffn2_w_down, v_final_norm_g):
    args = locals()
    P = {n: args[n] for n in _WEIGHTS}
    M = {n: args["m_" + n] for n in _WEIGHTS}
    V = {n: args["v_" + n] for n in _WEIGHTS}
    return _step(P, M, V, x, loss_target)
```

```python
import jax
import jax.numpy as jnp
from jax import lax
from jax.experimental import pallas as pl
from jax.experimental.pallas import tpu as pltpu

F32 = jnp.float32
BF16 = jnp.bfloat16
EPS = 1e-6
NS = 4
CONV_W = 31
SHORT_W = 3
POOL_WINDOWS = (2, 4, 8, 16)
HALO = 32
ADAM_LR, ADAM_B1, ADAM_B2, ADAM_EPS, ADAM_WD, ADAM_STEP = 0.001, 0.9, 0.999, 1e-08, 0.01, 10
MESH = pl.DeviceIdType.MESH
ANY = pl.BlockSpec(memory_space=pl.ANY)
VMEM_LIMIT = 56 * 1024 * 1024
FFN_BWD_SUBTILES = 2
BIG_BLOCKS_VMEM_LIMIT = 60 * 1024 * 1024


def _pallas(body, **kw):
    return pl.pallas_call(body, **kw)


def _cp(n_axes, vmem_limit=VMEM_LIMIT):
    return pltpu.CompilerParams(dimension_semantics=("arbitrary",) * n_axes, vmem_limit_bytes=vmem_limit)


def _nn(a, b):
    return jnp.dot(a, b, preferred_element_type=F32)


def _nt(a, b):
    return lax.dot_general(a, b, (((1,), (1,)), ((), ())), preferred_element_type=F32)


def _tn(a, b):
    return lax.dot_general(a, b, (((0,), (0,)), ((), ())), preferred_element_type=F32)


def _sigmoid(v):
    return 1.0 / (1.0 + jnp.exp(-v))


def _rms_stats(x):
    rs = lax.rsqrt(jnp.mean(x * x, axis=-1, keepdims=True) + EPS)
    return x * rs, rs


def _rms_bwd(dh, x, g):
    xh, rs = _rms_stats(x)
    dhg = dh * g
    dx = rs * (dhg - xh * jnp.mean(dhg * xh, axis=-1, keepdims=True))
    return dx, jnp.sum(dh * xh, axis=0, keepdims=True)


def _tile(n, pref):
    t = min(n, pref)
    assert n % t == 0, (n, t)
    return t


def _place():
    x, y, c = lax.axis_index("x"), lax.axis_index("y"), lax.axis_index("c")
    chips = [(1 - x, y), (x, 1 - y), (1 - x, 1 - y)]
    return x, y, c, chips


def _gather_copies(ins, outs, sems):
    send_sems, recv_sems, local_sems = sems
    x, y, c, _ = _place()
    me, at_x, at_y, diag = 2 * x + y, 2 * (1 - x) + y, 2 * x + (1 - y), 2 * (1 - x) + (1 - y)
    to_x, to_y, sib = (1 - x, y, c), (x, 1 - y, c), (x, y, 1 - c)

    def remote(ai, k, blk, to, src=None):
        return pltpu.make_async_remote_copy(src_ref=blk if src is None else src, dst_ref=blk,
                                            send_sem=send_sems.at[ai, k], recv_sem=recv_sems.at[ai, k],
                                            device_id=to, device_id_type=MESH)

    g = dict(local=[], first=[], landed=[], relay=[], relayed=[], passed=[], passed_diag=[], from_sib=[])
    for ai in range(len(ins)):
        o = outs[ai]
        h = ins[ai].shape[1] // 2
        lo, hi = pl.ds(0, h), pl.ds(h, h)
        g["local"].append(pltpu.make_async_copy(ins[ai], o.at[me], local_sems.at[ai]))
        g["first"] += [remote(ai, 0, o.at[me, c], to_x, src=ins[ai].at[c]),
                       remote(ai, 1, o.at[me, c], to_y, src=ins[ai].at[c])]
        g["landed"] += [remote(ai, 0, o.at[at_x, c], to_x), remote(ai, 1, o.at[at_y, c], to_y)]
        g["relay"] += [remote(ai, 2, o.at[at_x, c, lo], to_y), remote(ai, 3, o.at[at_y, c, hi], to_x)]
        g["relayed"] += [remote(ai, 2, o.at[diag, c, lo], to_y), remote(ai, 3, o.at[diag, c, hi], to_x)]
        g["passed"] += [remote(ai, 4, o.at[at_x, c], sib), remote(ai, 5, o.at[at_y, c], sib)]
        g["passed_diag"].append(remote(ai, 6, o.at[diag, c], sib))
        g["from_sib"] += [remote(ai, 4, o.at[at_x, 1 - c], sib), remote(ai, 5, o.at[at_y, 1 - c], sib),
                          remote(ai, 6, o.at[diag, 1 - c], sib)]
    return g


def _gather_start(ins, outs, sems):
    g = _gather_copies(ins, outs, sems)
    for cp in g["local"] + g["first"]:
        cp.start()


def _gather_middle(ins, outs, sems):
    g = _gather_copies(ins, outs, sems)
    for arrive, fwd, on in zip(g["landed"], g["passed"], g["relay"]):
        arrive.wait_recv()
        fwd.start()
        on.start()


def _gather_finish(ins, outs, sems):
    g = _gather_copies(ins, outs, sems)
    n = len(g["passed_diag"])
    for ai in range(n):
        g["relayed"][2 * ai].wait_recv()
        g["relayed"][2 * ai + 1].wait_recv()
        g["passed_diag"][ai].start()
    for cp in g["from_sib"]:
        cp.wait_recv()
    for cp in g["first"] + g["relay"] + g["passed"] + g["passed_diag"]:
        cp.wait_send()
    for cp in g["local"]:
        cp.wait()


def _scatter_copies(ins, outs, sems):
    send_sems, recv_sems = sems
    x, y, c, chips = _place()
    return [pltpu.make_async_remote_copy(
        src_ref=ins[ai].at[2 * chip[0] + chip[1]], dst_ref=outs[ai].at[k],
        send_sem=send_sems.at[ai, k], recv_sem=recv_sems.at[ai, k], device_id=(*chip, c), device_id_type=MESH)
        for ai in range(len(ins)) for k, chip in enumerate(chips)]


def _exchange_copies(ins, outs, sems):
    send_sems, recv_sems = sems
    x, y, c, _ = _place()
    return [pltpu.make_async_remote_copy(
        src_ref=ins[ai].at[1 - c], dst_ref=outs[ai], send_sem=send_sems.at[ai], recv_sem=recv_sems.at[ai],
        device_id=(x, y, 1 - c), device_id_type=MESH) for ai in range(len(ins))]


class _Carried:
    def __init__(self, kind="gather", arrs=()):
        self.kind, self.arrs, self.n = kind, tuple(arrs), len(arrs)
        self.specs = [ANY] * self.n
        if kind == "gather":
            self.out_shape = [jax.ShapeDtypeStruct((NS,) + a.shape, a.dtype) for a in self.arrs]
            sems = [(self.n, 7), (self.n, 7), (self.n,)]
        elif kind == "exchange":
            self.out_shape = [jax.ShapeDtypeStruct(a.shape[1:], a.dtype) for a in self.arrs]
            sems = [(self.n,), (self.n,)]
        else:
            self.out_shape = [jax.ShapeDtypeStruct((3,) + a.shape[1:], a.dtype) for a in self.arrs]
            sems = [(self.n, 3), (self.n, 3)]
        self.scratch = [pltpu.SemaphoreType.DMA(s) for s in sems] if self.n else []

    def split(self, refs, n_in, n_out):
        n = self.n
        a, b, c = n_in + n, n_in + n + n_out, n_in + 2 * n + n_out
        n_sem = len(self.scratch)
        own_scr = refs[c:len(refs) - n_sem]
        return refs[:n_in], refs[a:b], own_scr, (refs[n_in:a], refs[b:c], refs[len(refs) - n_sem:])

    def start(self, carried):
        ins, outs, sems = carried
        if self.kind == "gather":
            _gather_start(ins, outs, sems)
        else:
            for cp in (_exchange_copies if self.kind == "exchange" else _scatter_copies)(ins, outs, sems):
                cp.start()

    def middle(self, carried):
        if self.kind == "gather":
            _gather_middle(*carried)

    def finish(self, carried):
        ins, outs, sems = carried
        if self.kind == "gather":
            _gather_finish(ins, outs, sems)
        else:
            for cp in (_exchange_copies if self.kind == "exchange" else _scatter_copies)(ins, outs, sems):
                cp.wait()

    def when(self, cond, carried, what):
        if self.n:
            pl.when(cond)(lambda: what(carried))


_NOTHING = _Carried()


def _run_carried(car, tag):
    def body(*refs):
        _, _, _, carried = car.split(refs, 0, 0)
        car.start(carried)
        car.middle(carried)
        car.finish(carried)

    return _pallas(
        body, name=f"{car.kind}_{tag}",
        in_specs=car.specs, out_specs=car.specs, out_shape=car.out_shape, scratch_shapes=car.scratch,
        compiler_params=pltpu.CompilerParams(has_side_effects=True),
    )(*car.arrs)


def _ffn_weight_specs(F4, D):
    return [pl.BlockSpec((None, None, F4, D), lambda i, j: (j, 0, 0, 0)),
            pl.BlockSpec((None, None, F4, D), lambda i, j: (j, 1, 0, 0)),
            pl.BlockSpec((None, F4, D), lambda i, j: (j, 0, 0))]


def _ffn_fwd(x, g, wa, wb, l, f, car=_NOTHING):
    T, D = x.shape
    F4 = wb.shape[1]
    tm = _tile(T, 1024)
    ni = T // tm

    def body(*refs):
        ((x_ref, g_ref, wg_ref, wu_ref, wd_ref), (xo_ref, h_ref, p_ref, q_ref, z_ref), (acc,),
         carried) = car.split(refs, 5, 5)
        i = pl.program_id(0)
        j = pl.program_id(1)
        car.when((i == 0) & (j == 0), carried, car.start)
        car.when((i == ni // 2) & (j == 0), carried, car.middle)

        @pl.when(j == 0)
        def _():
            xh, _ = _rms_stats(x_ref[...])
            h_ref[...] = (xh * g_ref[...]).astype(BF16)
            acc[...] = jnp.zeros_like(acc)

        h = h_ref[...]
        a = _nt(h, wg_ref[...])
        b = _nt(h, wu_ref[...])
        sg = _sigmoid(a)
        silu = a * sg
        p_ref[...] = (b * (sg + silu * (1.0 - sg))).astype(BF16)
        q_ref[...] = silu.astype(BF16)
        z = (silu * b).astype(BF16)
        z_ref[...] = z
        acc[...] += _nn(z, wd_ref[...])

        @pl.when(j == NS - 1)
        def _():
            xo_ref[...] = x_ref[...] + 0.5 * acc[...]

        car.when((i == ni - 1) & (j == NS - 1), carried, car.finish)

    return _pallas(
        body, name=f"ffn_fwd_{l}_{f}", grid=(ni, NS),
        in_specs=[pl.BlockSpec((tm, D), lambda i, j: (i, 0)), pl.BlockSpec((1, D), lambda i, j: (0, 0))]
        + _ffn_weight_specs(F4, D) + car.specs,
        out_specs=[pl.BlockSpec((tm, D), lambda i, j: (i, 0)),
                   pl.BlockSpec((tm, D), lambda i, j: (i, 0))]
        + [pl.BlockSpec((None, tm, F4), lambda i, j: (j, i, 0))] * 3 + car.specs,
        out_shape=[jax.ShapeDtypeStruct((T, D), F32), jax.ShapeDtypeStruct((T, D), BF16)]
        + [jax.ShapeDtypeStruct((NS, T, F4), BF16)] * 3 + car.out_shape,
        scratch_shapes=[pltpu.VMEM((tm, D), F32)] + car.scratch,
        compiler_params=_cp(2),
    )(x, g, wa, wa, wb, *car.arrs)


def _ffn_bwd(x, g, dy, p, q, wa, wb, l, f, car=_NOTHING):
    T, D = x.shape
    F4 = wb.shape[1]
    tm = _tile(T, 1024)
    ni = T // tm

    def body(*refs):
        ((x_ref, g_ref, dy_ref, p_ref, q_ref, wg_ref, wu_ref, wd_ref),
         (dx_ref, dg_ref, da_ref, db_ref, dyb_ref), _, carried) = car.split(refs, 8, 5)
        dh = dx_ref
        i = pl.program_id(0)
        j = pl.program_id(1)
        car.when((i == 0) & (j == 0), carried, car.start)

        @pl.when(j == 0)
        def _():
            dyb_ref[...] = (0.5 * dy_ref[...]).astype(BF16)
            dh[...] = jnp.zeros_like(dh)

        @pl.when((i == 0) & (j == 0))
        def _():
            dg_ref[...] = jnp.zeros_like(dg_ref)

        for r in range(FFN_BWD_SUBTILES):
            rows = slice(r * (tm // FFN_BWD_SUBTILES), (r + 1) * (tm // FFN_BWD_SUBTILES))
            dz = _nt(dyb_ref[rows, :], wd_ref[...])
            da = (dz * p_ref[rows, :].astype(F32)).astype(BF16)
            db = (dz * q_ref[rows, :].astype(F32)).astype(BF16)
            da_ref[rows, :] = da
            db_ref[rows, :] = db
            dh[rows, :] += _nn(da, wg_ref[...]) + _nn(db, wu_ref[...])

        @pl.when(j == NS - 1)
        def _():
            dxn, dg = _rms_bwd(dh[...], x_ref[...], g_ref[...])
            dx_ref[...] = dy_ref[...] + dxn
            dg_ref[...] += dg

        car.when((i == ni - 1) & (j == NS - 1), carried, car.finish)

    tok = pl.BlockSpec((tm, D), lambda i, j: (i, 0))
    vec = pl.BlockSpec((1, D), lambda i, j: (0, 0))
    chunk = pl.BlockSpec((None, tm, F4), lambda i, j: (j, i, 0))
    return _pallas(
        body, name=f"ffn_bwd_{l}_{f}", grid=(ni, NS),
        in_specs=[tok, vec, tok, chunk, chunk] + _ffn_weight_specs(F4, D) + car.specs,
        out_specs=[tok, vec, chunk, chunk, tok] + car.specs,
        out_shape=[jax.ShapeDtypeStruct((T, D), F32), jax.ShapeDtypeStruct((1, D), F32),
                   jax.ShapeDtypeStruct((NS, T, F4), BF16), jax.ShapeDtypeStruct((NS, T, F4), BF16),
                   jax.ShapeDtypeStruct((T, D), BF16)] + car.out_shape,
        scratch_shapes=car.scratch,
        compiler_params=_cp(2, BIG_BLOCKS_VMEM_LIMIT),
    )(x, g, dy, p, q, wa, wa, wb, *car.arrs)


def _ffn_bwd_first(dy, p, q, wb, l, f, car=_NOTHING):
    T, D = dy.shape
    F4 = wb.shape[1]
    tm = _tile(T, 1024)
    ni = T // tm

    def body(*refs):
        (dy_ref, p_ref, q_ref, wd_ref), (da_ref, db_ref, dyb_ref), _, carried = car.split(refs, 4, 3)
        i = pl.program_id(0)
        j = pl.program_id(1)
        car.when((i == 0) & (j == 0), carried, car.start)

        @pl.when(j == 0)
        def _():
            dyb_ref[...] = (0.5 * dy_ref[...]).astype(BF16)

        for r in range(FFN_BWD_SUBTILES):
            rows = slice(r * (tm // FFN_BWD_SUBTILES), (r + 1) * (tm // FFN_BWD_SUBTILES))
            dz = _nt(dyb_ref[rows, :], wd_ref[...])
            da_ref[rows, :] = (dz * p_ref[rows, :].astype(F32)).astype(BF16)
            db_ref[rows, :] = (dz * q_ref[rows, :].astype(F32)).astype(BF16)

        car.when((i == ni - 1) & (j == NS - 1), carried, car.finish)

    tok = pl.BlockSpec((tm, D), lambda i, j: (i, 0))
    chunk = pl.BlockSpec((None, tm, F4), lambda i, j: (j, i, 0))
    return _pallas(
        body, name=f"ffn_bwd_first_{l}_{f}", grid=(ni, NS),
        in_specs=[tok, chunk, chunk, _ffn_weight_specs(F4, D)[2]] + car.specs,
        out_specs=[chunk, chunk, tok] + car.specs,
        out_shape=[jax.ShapeDtypeStruct((NS, T, F4), BF16), jax.ShapeDtypeStruct((NS, T, F4), BF16),
                   jax.ShapeDtypeStruct((T, D), BF16)] + car.out_shape,
        scratch_shapes=car.scratch,
        compiler_params=_cp(2),
    )(dy, p, q, wb, *car.arrs)


def _ffn_bwd_second(x, g, dy, da, db, wa, l, f, car=_NOTHING):
    T, D = x.shape
    F4 = da.shape[-1]
    tm = _tile(T, 512)
    ni = T // tm

    def body(*refs):
        (x_ref, g_ref, dy_ref, da_ref, db_ref, wg_ref, wu_ref), (dx_ref, dg_ref), (dh,), carried = car.split(refs, 7, 2)
        i = pl.program_id(0)
        j = pl.program_id(1)
        car.when((i == 0) & (j == 0), carried, car.start)

        @pl.when(j == 0)
        def _():
            dh[...] = jnp.zeros_like(dh)

        @pl.when((i == 0) & (j == 0))
        def _():
            dg_ref[...] = jnp.zeros_like(dg_ref)

        dh[...] += _nn(da_ref[...], wg_ref[...]) + _nn(db_ref[...], wu_ref[...])

        @pl.when(j == NS - 1)
        def _():
            dxn, dg = _rms_bwd(dh[...], x_ref[...], g_ref[...])
            dx_ref[...] = dy_ref[...] + dxn
            dg_ref[...] += dg

        car.when((i == ni - 1) & (j == NS - 1), carried, car.finish)

    tok = pl.BlockSpec((tm, D), lambda i, j: (i, 0))
    vec = pl.BlockSpec((1, D), lambda i, j: (0, 0))
    chunk = pl.BlockSpec((None, tm, F4), lambda i, j: (j, i, 0))
    return _pallas(
        body, name=f"ffn_bwd_second_{l}_{f}", grid=(ni, NS),
        in_specs=[tok, vec, tok, chunk, chunk] + _ffn_weight_specs(F4, D)[:2] + car.specs,
        out_specs=[tok, vec] + car.specs,
        out_shape=[jax.ShapeDtypeStruct((T, D), F32), jax.ShapeDtypeStruct((1, D), F32)] + car.out_shape,
        scratch_shapes=[pltpu.VMEM((tm, D), F32)] + car.scratch,
        compiler_params=_cp(2),
    )(x, g, dy, da, db, wa, wa, *car.arrs)


def _ffn_dw(h, dyb, da, db, z, l, f, car=_NOTHING):
    T, D = h.shape
    F4 = da.shape[-1]
    tk = _tile(T, 1024)
    nt = T // tk
    R2 = 3 * F4 // 2

    def body(*refs):
        (h_ref, dyb_ref, da_ref, db_ref, z_ref), (g_ref,), (accg, accu, accd), carried = car.split(refs, 5, 1)
        t = pl.program_id(1)
        car.when((pl.program_id(0) == 0) & (t == 0), carried, car.start)

        @pl.when(t == 0)
        def _():
            accg[...] = jnp.zeros_like(accg)
            accu[...] = jnp.zeros_like(accu)
            accd[...] = jnp.zeros_like(accd)

        hv = h_ref[...]
        accg[...] += _tn(da_ref[...], hv)
        accu[...] += _tn(db_ref[...], hv)
        accd[...] += _tn(z_ref[...], dyb_ref[...])

        @pl.when(t == nt - 1)
        def _():
            g_ref[0, 0:F4, :] = accg[...].astype(BF16)
            g_ref[0, F4:R2, :] = accu[0:R2 - F4, :].astype(BF16)
            g_ref[1, 0:2 * F4 - R2, :] = accu[R2 - F4:F4, :].astype(BF16)
            g_ref[1, 2 * F4 - R2:R2, :] = accd[...].astype(BF16)

        car.when((pl.program_id(0) == NS - 1) & (t == nt - 1), carried, car.finish)

    tok = pl.BlockSpec((tk, D), lambda s, t: (t, 0))
    chunk = pl.BlockSpec((None, tk, F4), lambda s, t: (s, t, 0))
    return _pallas(
        body, name=f"ffn_dw_{l}_{f}", grid=(NS, nt),
        in_specs=[tok, tok, chunk, chunk, chunk] + car.specs,
        out_specs=[pl.BlockSpec((2, None, R2, D), lambda s, t: (0, s, 0, 0))] + car.specs,
        out_shape=[jax.ShapeDtypeStruct((2, NS, R2, D), BF16)] + car.out_shape,
        scratch_shapes=[pltpu.VMEM((F4, D), F32), pltpu.VMEM((F4, D), F32), pltpu.VMEM((F4, D), F32)] + car.scratch,
        compiler_params=_cp(2),
    )(h, dyb, da, db, z, *car.arrs)


def _mix_in(x, g, win, l, car=_NOTHING):
    T, D = x.shape
    C4 = win.shape[-1]
    tm = _tile(T, 1024)
    ni = T // tm

    def body(*refs):
        (x_ref, g_ref, w_ref), (h_ref, u_ref), _, carried = car.split(refs, 3, 2)
        i = pl.program_id(0)
        j = pl.program_id(1)
        car.when((i == 0) & (j == 0), carried, car.start)
        car.when((i == ni // 2) & (j == 0), carried, car.middle)

        @pl.when(j == 0)
        def _():
            xh, _ = _rms_stats(x_ref[...])
            h_ref[...] = (xh * g_ref[...]).astype(BF16)

        u_ref[...] = _nn(h_ref[...], w_ref[...]).astype(BF16)
        car.when((i == ni - 1) & (j == NS - 1), carried, car.finish)

    return _pallas(
        body, name=f"mix_in_{l}", grid=(ni, NS),
        in_specs=[pl.BlockSpec((tm, D), lambda i, j: (i, 0)), pl.BlockSpec((1, D), lambda i, j: (0, 0)),
                  pl.BlockSpec((None, D, C4), lambda i, j: (j, 0, 0))] + car.specs,
        out_specs=[pl.BlockSpec((tm, D), lambda i, j: (i, 0)), pl.BlockSpec((tm, C4), lambda i, j: (i, j))] + car.specs,
        out_shape=[jax.ShapeDtypeStruct((T, D), BF16), jax.ShapeDtypeStruct((T, NS * C4), BF16)] + car.out_shape,
        scratch_shapes=car.scratch,
        compiler_params=_cp(2),
    )(x, g, win, *car.arrs)


def _pool_lane_window(n):
    lane = lax.broadcasted_iota(jnp.int32, (1, n), 1) // (n // len(POOL_WINDOWS))
    w = jnp.full((1, n), float(POOL_WINDOWS[-1]), F32)
    for gi in range(len(POOL_WINDOWS) - 1):
        w = jnp.where(lane == gi, float(POOL_WINDOWS[gi]), w)
    return lane, w


def _pool_select(lane, sums):
    out = sums[-1]
    for gi in range(len(sums) - 1):
        out = jnp.where(lane == gi, sums[gi], out)
    return out


def _back(v, s):
    return v if s == 0 else pltpu.roll(v, s, 0)


def _fwd_shift(v, s):
    return v if s == 0 else pltpu.roll(v, v.shape[0] - s, 0)


def _mix_seq_fwd(u, cdw, cb, lg, lb, sdw, pwblk, ps, bl, l, car=_NOTHING):
    T = u.shape[0]
    S = T // bl
    ts = _tile(S, 256)
    nt = S // ts
    DC, DS = cdw.shape[-1], sdw.shape[-1]
    o_ag, o_bg, o_cg, o_bx, o_p, o_end = DC, 2 * DC, 2 * DC + DS, 2 * DC + 2 * DS, 2 * DC + 3 * DS, 2 * DC + 4 * DS

    def body(*refs):
        ((up_ref, uc_ref, cdw_ref, cb_ref, lg_ref, lb_ref, sdw_ref, pw_ref, ps_ref),
         (act_ref, cv_ref), _, carried) = car.split(refs, 9, 2)
        i = pl.program_id(1)
        car.when((pl.program_id(0) == 0) & (i == 0), carried, car.start)
        car.when((pl.program_id(0) == bl // 2) & (i == 0), carried, car.middle)
        keep = jnp.where(i > 0, 1.0, 0.0).astype(F32)

        def ext(lo, hi):
            p = up_ref[ts - HALO:ts, lo:hi].astype(F32) * keep
            return jnp.concatenate([p, uc_ref[:, lo:hi].astype(F32)], axis=0)

        glu = ext(0, o_ag) * _sigmoid(ext(o_ag, o_bg))
        cv = jnp.zeros((ts, DC), F32) + cb_ref[...]
        for s in range(CONV_W):
            cv = cv + _back(glu, s)[HALO:, :] * cdw_ref[CONV_W - 1 - s:CONV_W - s, :]
        cv_ref[...] = cv
        mu = jnp.mean(cv, axis=-1, keepdims=True)
        xc = cv - mu
        lnv = xc * lax.rsqrt(jnp.mean(xc * xc, axis=-1, keepdims=True) + EPS) * lg_ref[...] + lb_ref[...]
        act_ref[:, 0:DC] = (lnv * _sigmoid(lnv)).astype(BF16)

        q = ext(o_cg, o_bx) * ext(o_bx, o_p)
        sc = jnp.zeros((ts, DS), F32)
        for s in range(SHORT_W):
            sc = sc + _back(q, s)[HALO:, :] * sdw_ref[SHORT_W - 1 - s:SHORT_W - s, :]
        act_ref[:, DC:DC + DS] = (uc_ref[:, o_bg:o_cg].astype(F32) * sc).astype(BF16)

        p = ext(o_p, o_end)
        lane, wl = _pool_lane_window(DS)
        sums, cur, sh = [], p, 1
        for _ in POOL_WINDOWS:
            cur = cur + _back(cur, sh)
            sums.append(cur[HALO:, :])
            sh *= 2
        pos = (i * ts + lax.broadcasted_iota(jnp.int32, (ts, 1), 0) + 1).astype(F32)
        pooled = _pool_select(lane, sums) / jnp.minimum(pos, wl) - p[HALO:, :]
        act_ref[:, DC + DS:DC + 2 * DS] = (_nn(pooled.astype(BF16), pw_ref[...]) * ps_ref[...]).astype(BF16)
        car.when((pl.program_id(0) == bl - 1) & (i == nt - 1), carried, car.finish)

    ucol = 2 * DC + 4 * DS
    full = lambda a: pl.BlockSpec(a.shape, lambda b, i: (0,) * a.ndim)
    return _pallas(
        body, name=f"mix_seq_fwd_{l}", grid=(bl, nt),
        in_specs=[pl.BlockSpec((ts, ucol), lambda b, i: (b * nt + jnp.maximum(i - 1, 0), 0)),
                  pl.BlockSpec((ts, ucol), lambda b, i: (b * nt + i, 0)),
                  full(cdw), full(cb), full(lg), full(lb), full(sdw), full(pwblk), full(ps)] + car.specs,
        out_specs=[pl.BlockSpec((ts, DC + 2 * DS), lambda b, i: (b * nt + i, 0)),
                   pl.BlockSpec((ts, DC), lambda b, i: (b * nt + i, 0))] + car.specs,
        out_shape=[jax.ShapeDtypeStruct((T, DC + 2 * DS), BF16), jax.ShapeDtypeStruct((T, DC), F32)] + car.out_shape,
        scratch_shapes=car.scratch,
        compiler_params=_cp(2),
    )(u, u, cdw, cb, lg, lb, sdw, pwblk, ps, *car.arrs)


def _mix_out(x, act, u, wp, wo, l, car=_NOTHING):
    T, D = x.shape
    tm = _tile(T, 512)
    ni = T // tm
    DA = act.shape[-1]
    DC, DS = DA // 2, DA // 4
    NB = D // NS
    gcol = (2 * DC + 4 * DS) // D

    def body(*refs):
        ((x_ref, act_ref, g0_ref, g1_ref, g2_ref, wp_ref, wo_ref),
         (xo_ref, y_ref, m_ref), _, carried) = car.split(refs, 7, 3)
        car.when(pl.program_id(0) == 0, carried, car.start)
        car.when(pl.program_id(0) == ni // 2, carried, car.middle)
        parts = [(0, DC), (DC, DC + DS), (DC + DS, DC + 2 * DS)]
        m = jnp.zeros((tm, D), F32)
        for k, (lo, hi) in enumerate(parts):
            av = act_ref[:, lo:hi]
            y = jnp.concatenate([_nn(av, wp_ref[s, lo:hi, :]) for s in range(NS)], axis=1)
            y_ref[:, k * D:(k + 1) * D] = y.astype(BF16)
            gl = (g0_ref, g1_ref, g2_ref)[k][...].astype(F32)
            m = m + _sigmoid(gl) * y
        mb = m.astype(BF16)
        m_ref[...] = mb
        out = _nn(mb[:, 0:NB], wo_ref[0])
        for s in range(1, NS):
            out = out + _nn(mb[:, s * NB:(s + 1) * NB], wo_ref[s])
        xo_ref[...] = x_ref[...] + out
        car.when(pl.program_id(0) == ni - 1, carried, car.finish)

    tok = lambda w: pl.BlockSpec((tm, w), lambda i: (i, 0))
    return _pallas(
        body, name=f"mix_out_{l}", grid=(ni,),
        in_specs=[tok(D), tok(DA),
                  pl.BlockSpec((tm, D), lambda i: (i, gcol)), pl.BlockSpec((tm, D), lambda i: (i, gcol + 1)),
                  pl.BlockSpec((tm, D), lambda i: (i, gcol + 2)),
                  pl.BlockSpec((NS, DA, NB), lambda i: (0, 0, 0)),
                  pl.BlockSpec((NS, NB, D), lambda i: (0, 0, 0))] + car.specs,
        out_specs=[tok(D), tok(3 * D), tok(D)] + car.specs,
        out_shape=[jax.ShapeDtypeStruct((T, D), F32), jax.ShapeDtypeStruct((T, 3 * D), BF16),
                   jax.ShapeDtypeStruct((T, D), BF16)] + car.out_shape,
        scratch_shapes=car.scratch,
        compiler_params=_cp(1),
    )(x, act, u, u, u, wp, wo, *car.arrs)


def _mix_out_bwd(dxn, y, u, act, m, wp, wo, l, car=_NOTHING):
    T, D = dxn.shape
    tm = _tile(T, 512)
    nt = T // tm
    DA = act.shape[-1]
    DC, DS = DA // 2, DA // 4
    NB = D // NS
    UC = u.shape[-1]
    g_lo = 2 * DC + 4 * DS
    gcol = g_lo // D
    parts = [(0, DC), (DC, DC + DS), (DC + DS, DC + 2 * DS)]

    def body(*refs):
        ((dx_ref, y_ref, g0_ref, g1_ref, g2_ref, act_ref, m_ref, wp_ref, wo_ref),
         (du_ref, dact_ref, gwo_ref, gwp_ref), (acc_wo, acc_wp), carried) = car.split(refs, 9, 4)
        i = pl.program_id(0)
        car.when(i == 0, carried, car.start)

        @pl.when(i == 0)
        def _():
            acc_wo[...] = jnp.zeros_like(acc_wo)
            acc_wp[...] = jnp.zeros_like(acc_wp)

        dxb = dx_ref[...].astype(BF16)
        dm = jnp.concatenate([_nt(dxb, wo_ref[s]) for s in range(NS)], axis=1)
        acc_wo[...] += _tn(m_ref[...], dxb)
        du_ref[:, 0:g_lo] = jnp.zeros((tm, g_lo), BF16)
        for k, (lo, hi) in enumerate(parts):
            sg = _sigmoid((g0_ref, g1_ref, g2_ref)[k][...].astype(F32))
            yk = y_ref[:, k * D:(k + 1) * D].astype(F32)
            du_ref[:, g_lo + k * D:g_lo + (k + 1) * D] = (dm * yk * sg * (1.0 - sg)).astype(BF16)
            dyk = (dm * sg).astype(BF16)
            dk = _nt(dyk[:, 0:NB], wp_ref[0, lo:hi, :])
            for s in range(1, NS):
                dk = dk + _nt(dyk[:, s * NB:(s + 1) * NB], wp_ref[s, lo:hi, :])
            dact_ref[:, lo:hi] = dk
            acc_wp[lo:hi, :] += _tn(act_ref[:, lo:hi], dyk)

        @pl.when(i == nt - 1)
        def _():
            for s in range(NS):
                for hf in range(2):
                    r0 = s * NB + hf * (NB // 2)
                    gwo_ref[hf, s] = acc_wo[r0:r0 + NB // 2, :].astype(BF16)
                    gwp_ref[hf, s] = acc_wp[hf * (DA // 2):(hf + 1) * (DA // 2), s * NB:(s + 1) * NB].astype(BF16)

        car.when(i == nt - 1, carried, car.finish)

    tok = lambda w: pl.BlockSpec((tm, w), lambda i: (i, 0))
    whole = lambda shp: pl.BlockSpec(shp, lambda i: (0,) * len(shp))
    return _pallas(
        body, name=f"mix_out_bwd_{l}", grid=(nt,),
        in_specs=[tok(D), tok(3 * D),
                  pl.BlockSpec((tm, D), lambda i: (i, gcol)), pl.BlockSpec((tm, D), lambda i: (i, gcol + 1)),
                  pl.BlockSpec((tm, D), lambda i: (i, gcol + 2)),
                  tok(DA), tok(D), whole((NS, DA, NB)), whole((NS, NB, D))] + car.specs,
        out_specs=[tok(UC), tok(DA), whole((2, NS, NB // 2, D)), whole((2, NS, DA // 2, NB))] + car.specs,
        out_shape=[jax.ShapeDtypeStruct((T, UC), BF16), jax.ShapeDtypeStruct((T, DA), F32),
                   jax.ShapeDtypeStruct((2, NS, NB // 2, D), BF16),
                   jax.ShapeDtypeStruct((2, NS, DA // 2, NB), BF16)] + car.out_shape,
        scratch_shapes=[pltpu.VMEM((D, D), F32), pltpu.VMEM((DA, D), F32)] + car.scratch,
        compiler_params=_cp(1, BIG_BLOCKS_VMEM_LIMIT),
    )(dxn, y, u, u, u, act, m, wp, wo, *car.arrs)


def _mix_seq_bwd(du, u, dact, cv, cdw, lg, lb, sdw, pwblk, ps, bl, l, car=_NOTHING):
    T = u.shape[0]
    S = T // bl
    ts = _tile(S, 256)
    nt = S // ts
    DC, DS = cdw.shape[-1], sdw.shape[-1]
    DA = DC + 2 * DS
    o_ag, o_bg, o_cg, o_bx, o_p, o_end = DC, 2 * DC, 2 * DC + DS, 2 * DC + 2 * DS, 2 * DC + 3 * DS, 2 * DC + 4 * DS
    n_f = ts + HALO

    def body(*refs):
        ((_, up_ref, uc_ref, un_ref, dac_ref, dan_ref, cvc_ref, cvn_ref,
          cdw_ref, lg_ref, lb_ref, sdw_ref, pw_ref, ps_ref),
         (du_ref, gcdw_ref, g512_ref, g256_ref, gpw_ref), _, carried) = car.split(refs, 14, 5)
        b = pl.program_id(0)
        i = pl.program_id(1)
        car.when((b == 0) & (i == 0), carried, car.start)
        keep_p = jnp.where(i > 0, 1.0, 0.0).astype(F32)
        keep_n = jnp.where(i < nt - 1, 1.0, 0.0).astype(F32)

        @pl.when((b == 0) & (i == 0))
        def _():
            gcdw_ref[...] = jnp.zeros_like(gcdw_ref)
            g512_ref[...] = jnp.zeros_like(g512_ref)
            g256_ref[...] = jnp.zeros_like(g256_ref)
            gpw_ref[...] = jnp.zeros_like(gpw_ref)

        def back(lo, hi):
            p = up_ref[ts - HALO:ts, lo:hi].astype(F32) * keep_p
            return jnp.concatenate([p, uc_ref[:, lo:hi].astype(F32)], axis=0)

        def fwd(cur, nxt, lo, hi, mask):
            n = nxt[0:HALO, lo:hi].astype(F32)
            if mask:
                n = n * keep_n
            return jnp.concatenate([cur[:, lo:hi].astype(F32), n], axis=0)

        cvx = fwd(cvc_ref, cvn_ref, 0, DC, False)
        dA = fwd(dac_ref, dan_ref, 0, DC, True)
        mu = jnp.mean(cvx, axis=-1, keepdims=True)
        xc = cvx - mu
        rs = lax.rsqrt(jnp.mean(xc * xc, axis=-1, keepdims=True) + EPS)
        xh = xc * rs
        lnv = xh * lg_ref[...] + lb_ref[...]
        sg = _sigmoid(lnv)
        dln = dA * (sg * (1.0 + lnv * (1.0 - sg)))
        dxh = dln * lg_ref[...]
        dcv = rs * (dxh - jnp.mean(dxh, axis=-1, keepdims=True) - xh * jnp.mean(dxh * xh, axis=-1, keepdims=True))
        g512_ref[0:1, :] += jnp.sum(dcv[0:ts], axis=0, keepdims=True)
        g512_ref[1:2, :] += jnp.sum((dln * xh)[0:ts], axis=0, keepdims=True)
        g512_ref[2:3, :] += jnp.sum(dln[0:ts], axis=0, keepdims=True)

        av = back(0, o_ag)
        sga = _sigmoid(back(o_ag, o_bg))
        glu = av * sga
        dcv_c = dcv[0:ts]
        dglu = jnp.zeros((ts, DC), F32)
        for s in range(CONV_W):
            k = CONV_W - 1 - s
            dglu = dglu + _fwd_shift(dcv, s)[0:ts, :] * cdw_ref[k:k + 1, :]
            gcdw_ref[k:k + 1, :] += jnp.sum(_back(glu, s)[HALO:, :] * dcv_c, axis=0, keepdims=True)
        sga_c = sga[HALO:, :]
        du_ref[:, 0:o_ag] = (dglu * sga_c).astype(BF16)
        du_ref[:, o_ag:o_bg] = (dglu * av[HALO:, :] * sga_c * (1.0 - sga_c)).astype(BF16)

        cg = back(o_cg, o_bx)
        bx = back(o_bx, o_p)
        q = cg * bx
        sc = jnp.zeros((ts, DS), F32)
        for s in range(SHORT_W):
            sc = sc + _back(q, s)[HALO:, :] * sdw_ref[SHORT_W - 1 - s:SHORT_W - s, :]
        dB = fwd(dac_ref, dan_ref, DC, DC + DS, True)
        ds = dB * fwd(uc_ref, un_ref, o_bg, o_cg, False)
        du_ref[:, o_bg:o_cg] = (dB[0:ts] * sc).astype(BF16)
        ds_c = ds[0:ts]
        dq = jnp.zeros((ts, DS), F32)
        for s in range(SHORT_W):
            k = SHORT_W - 1 - s
            dq = dq + _fwd_shift(ds, s)[0:ts, :] * sdw_ref[k:k + 1, :]
            g256_ref[k:k + 1, :] += jnp.sum(_back(q, s)[HALO:, :] * ds_c, axis=0, keepdims=True)
        du_ref[:, o_cg:o_bx] = (dq * bx[HALO:, :]).astype(BF16)
        du_ref[:, o_bx:o_p] = (dq * cg[HALO:, :]).astype(BF16)

        p = back(o_p, o_end)
        lane, wl = _pool_lane_window(DS)
        sums, cur, sh = [], p, 1
        for _ in POOL_WINDOWS:
            cur = cur + _back(cur, sh)
            sums.append(cur[HALO:, :])
            sh *= 2
        pos_c = (i * ts + lax.broadcasted_iota(jnp.int32, (ts, 1), 0) + 1).astype(F32)
        pooled = (_pool_select(lane, sums) / jnp.minimum(pos_c, wl) - p[HALO:, :]).astype(BF16)
        pwv = _nn(pooled, pw_ref[...])
        dC = fwd(dac_ref, dan_ref, DC + DS, DA, True)
        g256_ref[SHORT_W:SHORT_W + 1, :] += jnp.sum(dC[0:ts] * pwv, axis=0, keepdims=True)
        dpw = (dC * ps_ref[...]).astype(BF16)
        gpw_ref[...] += _tn(pooled, dpw[0:ts])
        dpl = _nt(dpw, pw_ref[...])
        pos_f = (i * ts + lax.broadcasted_iota(jnp.int32, (n_f, 1), 0) + 1).astype(F32)
        e = dpl / jnp.minimum(pos_f, wl)
        fsums, cur, sh = [], e, 1
        for _ in POOL_WINDOWS:
            cur = cur + _fwd_shift(cur, sh)
            fsums.append(cur[0:ts, :])
            sh *= 2
        du_ref[:, o_p:o_end] = (_pool_select(lane, fsums) - dpl[0:ts]).astype(BF16)
        car.when((b == bl - 1) & (i == nt - 1), carried, car.finish)

    full = lambda a: pl.BlockSpec(a.shape, lambda b, i: (0,) * a.ndim)
    row = lambda w, f: pl.BlockSpec((ts, w), lambda b, i: (b * nt + f(i), 0))
    prv = lambda i: jnp.maximum(i - 1, 0)
    nxt = lambda i: jnp.minimum(i + 1, nt - 1)
    cur = lambda i: i
    return _pallas(
        body, name=f"mix_seq_bwd_{l}", grid=(bl, nt),
        in_specs=[ANY, row(o_end, prv), row(o_end, cur), row(o_end, nxt),
                  row(DA, cur), row(DA, nxt), row(DC, cur), row(DC, nxt),
                  full(cdw), full(lg), full(lb), full(sdw), full(pwblk), full(ps)] + car.specs,
        out_specs=[row(o_end, cur), full(cdw),
                   pl.BlockSpec((8, DC), lambda b, i: (0, 0)), pl.BlockSpec((8, DS), lambda b, i: (0, 0)),
                   full(pwblk)] + car.specs,
        out_shape=[jax.ShapeDtypeStruct(du.shape, BF16), jax.ShapeDtypeStruct(cdw.shape, F32),
                   jax.ShapeDtypeStruct((8, DC), F32), jax.ShapeDtypeStruct((8, DS), F32),
                   jax.ShapeDtypeStruct(pwblk.shape, F32)] + car.out_shape,
        scratch_shapes=car.scratch,
        input_output_aliases={0: 0},
        compiler_params=_cp(2),
    )(du, u, u, u, dact, dact, cv, cv, cdw, lg, lb, sdw, pwblk, ps, *car.arrs)


def _mix_in_bwd(x, g, dxn, du, win, l):
    T, D = x.shape
    C4 = win.shape[-1]
    tm = _tile(T, 1024)

    def body(x_ref, g_ref, dxn_ref, du_ref, w_ref, dx_ref, dg_ref, dh):
        i = pl.program_id(0)
        j = pl.program_id(1)

        @pl.when(j == 0)
        def _():
            dh[...] = jnp.zeros_like(dh)

        @pl.when((i == 0) & (j == 0))
        def _():
            dg_ref[...] = jnp.zeros_like(dg_ref)

        dh[...] += _nt(du_ref[...], w_ref[...])

        @pl.when(j == NS - 1)
        def _():
            dxr, dg = _rms_bwd(dh[...], x_ref[...], g_ref[...])
            dx_ref[...] = dxn_ref[...] + dxr
            dg_ref[...] += dg

    tok = pl.BlockSpec((tm, D), lambda i, j: (i, 0))
    vec = pl.BlockSpec((1, D), lambda i, j: (0, 0))
    return _pallas(
        body, name=f"mix_in_bwd_{l}", grid=(T // tm, NS),
        in_specs=[tok, vec, tok, pl.BlockSpec((tm, C4), lambda i, j: (i, j)),
                  pl.BlockSpec((None, D, C4), lambda i, j: (j, 0, 0))],
        out_specs=[tok, vec],
        out_shape=[jax.ShapeDtypeStruct((T, D), F32), jax.ShapeDtypeStruct((1, D), F32)],
        scratch_shapes=[pltpu.VMEM((tm, D), F32)],
        compiler_params=_cp(2),
    )(x, g, dxn, du, win)


def _mix_in_dw(h, du, l):
    T, D = h.shape
    C4 = du.shape[-1] // NS
    tk = _tile(T, 1024)
    nt = T // tk

    def body(h_ref, du_ref, g_ref, acc):
        t = pl.program_id(1)

        @pl.when(t == 0)
        def _():
            acc[...] = jnp.zeros_like(acc)

        acc[...] += _tn(h_ref[...], du_ref[...])

        @pl.when(t == nt - 1)
        def _():
            g_ref[0] = acc[0:D // 2, :].astype(BF16)
            g_ref[1] = acc[D // 2:D, :].astype(BF16)

    return _pallas(
        body, name=f"mix_in_dw_{l}", grid=(NS, nt),
        in_specs=[pl.BlockSpec((tk, D), lambda s, t: (t, 0)), pl.BlockSpec((tk, C4), lambda s, t: (t, s))],
        out_specs=pl.BlockSpec((2, None, D // 2, C4), lambda s, t: (0, s, 0, 0)),
        out_shape=jax.ShapeDtypeStruct((2, NS, D // 2, C4), BF16),
        scratch_shapes=[pltpu.VMEM((D, C4), F32)],
        compiler_params=_cp(2),
    )(h, du)


def _loss_head(x, g, target):
    T, D = x.shape
    tm = _tile(T, 512)

    def body(x_ref, g_ref, t_ref, dx_ref, loss_ref, dg_ref):
        @pl.when(pl.program_id(0) == 0)
        def _():
            loss_ref[...] = jnp.zeros_like(loss_ref)
            dg_ref[...] = jnp.zeros_like(dg_ref)

        xv = x_ref[...]
        xh, rs = _rms_stats(xv)
        gv = g_ref[...]
        e = xh * gv - t_ref[...]
        loss_ref[...] += 0.5 * jnp.sum(jnp.mean(e * e, axis=-1, keepdims=True))
        dy = e * (1.0 / D)
        dyg = dy * gv
        dx_ref[...] = rs * (dyg - xh * jnp.mean(dyg * xh, axis=-1, keepdims=True))
        dg_ref[...] += jnp.sum(dy * xh, axis=0, keepdims=True)

    tok = pl.BlockSpec((tm, D), lambda i: (i, 0))
    vec = pl.BlockSpec((1, D), lambda i: (0, 0))
    return _pallas(
        body, name="loss_head", grid=(T // tm,),
        in_specs=[tok, vec, tok],
        out_specs=[tok, pl.BlockSpec((8, 128), lambda i: (0, 0)), vec],
        out_shape=[jax.ShapeDtypeStruct((T, D), F32), jax.ShapeDtypeStruct((8, 128), F32),
                   jax.ShapeDtypeStruct((1, D), F32)],
        compiler_params=_cp(1),
    )(x, g, target)


def _block_diag(pw):
    G, c, _ = pw.shape
    out = jnp.zeros((G * c, G * c), pw.dtype)
    for gi in range(G):
        out = lax.dynamic_update_slice(out, pw[gi], (gi * c, gi * c))
    return out


def _pad_rows(a, n):
    return jnp.pad(a, ((0, n - a.shape[0]), (0, 0)))


def _merge(g):
    return g.reshape(g.shape[0], g.shape[1] * g.shape[2], g.shape[3])


def _split_dws(dws_g, cw, sw):
    cdw = jnp.transpose(dws_g[:, 0:CONV_W, 0:cw], (1, 0, 2)).reshape(CONV_W, NS * cw)
    sdw = jnp.transpose(dws_g[:, 32:32 + SHORT_W, 0:sw], (1, 0, 2)).reshape(SHORT_W, NS * sw)
    return cdw, sdw


def _fwd_bwd(x3, target3, shards, small, dw_widths, first, sum_block):
    bl, S, D = x3.shape
    T = bl * S
    x = x3.reshape(T, D)
    target = target3.reshape(T, D)
    L = len(shards)
    cw, sw = dw_widths
    row = lambda v: v[None, :]
    gather = lambda arrs: _Carried("gather", arrs)
    exchange = lambda arrs: _Carried("exchange", arrs)
    scatter = lambda arrs: _Carried("scatter", arrs)

    saved = []
    wa1, wb1 = first
    for l in range(L):
        sh = shards[l]
        nxt = shards[l + 1] if l + 1 < L else None
        sp = dict(pwblk=_block_diag(small["pool_w"][l]).astype(BF16), cb=row(small["conv_b"][l]),
                  lg=row(small["conv_ln_g"][l]), lb=row(small["conv_ln_b"][l]), ps=row(small["pool_scale"][l]),
                  g1=row(small["norm_ffn1_g"][l]), gm=row(small["norm_mix_g"][l]), g2=row(small["norm_ffn2_g"][l]))
        x0 = x
        x1, h1, p1, q1, z1, win_g, wp_g, wo_g, dws_g = _ffn_fwd(x0, sp["g1"], wa1, wb1, l, 0, gather(sh["mx"]))
        win_g, wp_g, wo_g = _merge(win_g), _merge(wp_g), _merge(wo_g)
        cdw, sdw = _split_dws(_merge(dws_g), cw, sw)
        sp["cdw"], sp["sdw"] = _pad_rows(cdw, HALO), _pad_rows(sdw, 8)
        hm, u, wa2 = _mix_in(x1, sp["gm"], win_g, l, gather([sh["f2a"]]))
        act, cv, wb2 = _mix_seq_fwd(u, sp["cdw"], sp["cb"], sp["lg"], sp["lb"], sp["sdw"], sp["pwblk"], sp["ps"],
                                    bl, l, gather([sh["f2b"]]))
        wb2 = _merge(wb2)
        res_o = _mix_out(x1, act, u, wp_g, wo_g, l, gather([nxt["f1b"]]) if nxt else _NOTHING)
        x2, y, m = res_o[:3]
        res_f = _ffn_fwd(x2, sp["g2"], wa2, wb2, l, 1, gather([nxt["f1a"]]) if nxt else _NOTHING)
        x, h2, p2, q2, z2 = res_f[:5]
        saved.append(dict(sp=sp, x0=x0, x1=x1, x2=x2, h1=h1, p1=p1, q1=q1, z1=z1, hm=hm, u=u, act=act, cv=cv, y=y, m=m,
                          h2=h2, p2=p2, q2=q2, z2=z2, wa1=wa1, wb1=wb1, wa2=wa2, wb2=wb2, win=win_g, wp=wp_g, wo=wo_g))
        if nxt:
            wa1, wb1 = res_f[5], _merge(res_o[3])

    dx, loss_blk, dgf = _loss_head(x, row(small["final_norm_g"]), target)
    loss = loss_blk[0, 0]

    sg = {k: [None] * L for k in ("norm_ffn1_g", "norm_mix_g", "norm_ffn2_g", "conv_dw", "conv_b", "conv_ln_g",
                                  "conv_ln_b", "short_dw", "pool_w", "pool_scale")}
    blocks = []
    g_up, l_up = [], None
    G, c = small["pool_w"].shape[1:3]
    for l in reversed(range(L)):
        sv = saved[l]
        sp = sv["sp"]
        res = _ffn_bwd(sv["x2"], sp["g2"], dx, sv["p2"], sv["q2"], sv["wa2"], sv["wb2"], l, 1, exchange(g_up))
        dx, dg2, da, db, dyb = res[:5]
        p_up = sum_block(g_up, res[5:]) if g_up else []
        res = _ffn_dw(sv["h2"], dyb, da, db, sv["z2"], l, 1, scatter(p_up))
        g_f2 = [res[0]]
        if p_up:
            blocks.append((l_up, "f1", p_up, res[1:]))
        res = _mix_out_bwd(dx, sv["y"], sv["u"], sv["act"], sv["m"], sv["wp"], sv["wo"], l, exchange(g_f2))
        du, dact, g_o, g_p = res[:4]
        p_f2 = sum_block(g_f2, res[4:])
        res = _mix_seq_bwd(du, sv["u"], dact, sv["cv"], sp["cdw"], sp["lg"], sp["lb"],
                           sp["sdw"], sp["pwblk"], sp["ps"], bl, l, scatter(p_f2))
        du, gcdw, g512, g256, gpw = res[:5]
        blocks.append((l, "f2", p_f2, res[5:]))
        dx, dgm = _mix_in_bwd(sv["x1"], sp["gm"], dx, du, sv["win"], l)
        g_mx = [_mix_in_dw(sv["hm"], du, l), g_p, g_o]
        if l > 0:
            res = _ffn_bwd(sv["x0"], sp["g1"], dx, sv["p1"], sv["q1"], sv["wa1"], sv["wb1"], l, 0, exchange(g_mx))
            dx, dg1, da, db, dyb = res[:5]
            p_mx = sum_block(g_mx, res[5:])
            res = _ffn_dw(sv["h1"], dyb, da, db, sv["z1"], l, 0, scatter(p_mx))
            blocks.append((l, "mx", p_mx, res[1:]))
            g_up, l_up = [res[0]], l
        else:
            res = _ffn_bwd_first(dx, sv["p1"], sv["q1"], sv["wb1"], l, 0, exchange(g_mx))
            da, db, dyb = res[:3]
            p_mx = sum_block(g_mx, res[3:])
            res = _ffn_dw(sv["h1"], dyb, da, db, sv["z1"], l, 0, scatter(p_mx))
            blocks.append((l, "mx", p_mx, res[1:]))
            g_f1 = [res[0]]
            p_f1 = sum_block(g_f1, _run_carried(exchange(g_f1), "last"))
            res = _ffn_bwd_second(sv["x0"], sp["g1"], dx, da, db, sv["wa1"], l, 0, scatter(p_f1))
            dx, dg1 = res[:2]
            blocks.append((l, "f1", p_f1, res[2:]))
        sg["norm_ffn1_g"][l], sg["norm_mix_g"][l], sg["norm_ffn2_g"][l] = dg1[0], dgm[0], dg2[0]
        sg["conv_dw"][l] = gcdw[:CONV_W]
        sg["conv_b"][l], sg["conv_ln_g"][l], sg["conv_ln_b"][l] = g512[0], g512[1], g512[2]
        sg["short_dw"][l] = g256[:SHORT_W]
        sg["pool_scale"][l] = g256[SHORT_W]
        sg["pool_w"][l] = jnp.stack([gpw[gi * c:(gi + 1) * c, gi * c:(gi + 1) * c] for gi in range(G)])
    small_g = {k: jnp.stack(v) for k, v in sg.items()}
    small_g["final_norm_g"] = dgf[0]
    return loss, dx.reshape(bl, S, D), blocks, small_g


def _share_final_and_reduce_small(fs, v):
    n = len(fs)
    L = fs[0].shape[0]
    R, W = v.shape

    def body(*refs):
        v_ref = refs[0]
        out_ref = refs[1 + n]
        outs = refs[2 + n:2 + 2 * n]
        buf, send_sems, recv_sems, share_send, share_recv = refs[2 + 2 * n:]
        x, y, c, _ = _place()
        sib = (x, y, 1 - c)
        me = 4 * x + 2 * y + c

        def share(ai, l, half):
            blk = outs[ai].at[l, half]
            return pltpu.make_async_remote_copy(src_ref=blk, dst_ref=blk, send_sem=share_send.at[ai, l],
                                                recv_sem=share_recv.at[ai, l], device_id=sib, device_id_type=MESH)

        for ai in range(n):
            for l in range(L):
                share(ai, l, c).start()

        buf[me] = v_ref[...]
        cps = []
        for k in range(1, 8):
            kx, ky, kc = (k >> 2) & 1, (k >> 1) & 1, k & 1
            to = (1 - x if kx else x, 1 - y if ky else y, 1 - c if kc else c)
            cp = pltpu.make_async_remote_copy(src_ref=v_ref, dst_ref=buf.at[me], send_sem=send_sems.at[k - 1],
                                              recv_sem=recv_sems.at[k - 1], device_id=to, device_id_type=MESH)
            cp.start()
            cps.append(cp)
        for cp in cps:
            cp.wait()
        acc = buf[0]
        for d in range(1, 8):
            acc = acc + buf[d]
        out_ref[...] = acc

        for ai in range(n):
            for l in range(L):
                share(ai, l, 1 - c).wait_recv()
                share(ai, l, c).wait_send()

    vmem = pl.BlockSpec(memory_space=pltpu.VMEM)
    out = _pallas(
        body, name="share_final_and_reduce_small",
        in_specs=[vmem] + [ANY] * n, out_specs=[vmem] + [ANY] * n,
        out_shape=[jax.ShapeDtypeStruct((R, W), F32)] + [jax.ShapeDtypeStruct(f.shape, f.dtype) for f in fs],
        scratch_shapes=[pltpu.VMEM((8, R, W), F32), pltpu.SemaphoreType.DMA((7,)), pltpu.SemaphoreType.DMA((7,)),
                        pltpu.SemaphoreType.DMA((n, L)), pltpu.SemaphoreType.DMA((n, L))],
        input_output_aliases={1 + i: 1 + i for i in range(n)},
        compiler_params=pltpu.CompilerParams(has_side_effects=True, vmem_limit_bytes=VMEM_LIMIT),
    )(v, *fs)
    return out[1:], out[0]


def _row_tile(n, w, streams):
    for t in (1056, 1024, 704, 512, 352, 256, 128, 64, 32, 16):
        if n % t == 0 and 2 * streams * t * w * 4 <= VMEM_LIMIT // 2:
            return t
    raise ValueError((n, w))


def _sum_sibling(tag, cidx, g, r):
    _, N, W = g.shape
    tr = _row_tile(N, W, 3)

    def body(c_ref, g_ref, r_ref, o_ref):
        del c_ref
        o_ref[...] = (g_ref[...].astype(F32) + r_ref[...].astype(F32)).astype(BF16)

    return _pallas(
        body, name=f"grad_sum_sibling_{tag}",
        grid_spec=pltpu.PrefetchScalarGridSpec(
            num_scalar_prefetch=1, grid=(N // tr,),
            in_specs=[pl.BlockSpec((None, tr, W), lambda i, c: (c[0], i, 0)),
                      pl.BlockSpec((tr, W), lambda i, c: (i, 0))],
            out_specs=pl.BlockSpec((tr, W), lambda i, c: (i, 0))),
        out_shape=jax.ShapeDtypeStruct((N, W), BF16),
        compiler_params=_cp(1),
    )(cidx, g, r)


def _sum_final(tag, idx, p, r2, l, L, prev):
    _, r, W = p.shape
    tr = _row_tile(r, W, 5)

    def body(*refs):
        p_ref, r2_ref = refs[1:3]
        o_ref = refs[-1]
        acc = p_ref[...].astype(F32)
        for k in range(3):
            acc = acc + r2_ref[k].astype(F32)
        o_ref[...] = acc

    in_specs = [pl.BlockSpec((None, tr, W), lambda i, s: (s[1], i, 0)),
                pl.BlockSpec((3, tr, W), lambda i, s: (0, i, 0))]
    args = [idx, p, r2]
    aliases = {}
    if prev is not None:
        in_specs.append(ANY)
        args.append(prev)
        aliases = {3: 0}
    return _pallas(
        body, name=f"grad_sum_final_{tag}",
        grid_spec=pltpu.PrefetchScalarGridSpec(
            num_scalar_prefetch=1, grid=(r // tr,), in_specs=in_specs,
            out_specs=pl.BlockSpec((None, None, tr, W), lambda i, s: (l, s[0], i, 0))),
        out_shape=jax.ShapeDtypeStruct((L, 2, r, W), F32),
        input_output_aliases=aliases,
        compiler_params=_cp(1),
    )(*args)


def _adam_math(w, g, m, v):
    m = ADAM_B1 * m + (1.0 - ADAM_B1) * g
    v = ADAM_B2 * v + (1.0 - ADAM_B2) * (g * g)
    m_hat = m / (1.0 - ADAM_B1 ** ADAM_STEP)
    v_hat = v / (1.0 - ADAM_B2 ** ADAM_STEP)
    delta = -ADAM_LR * (m_hat / (jnp.sqrt(v_hat) + ADAM_EPS) + ADAM_WD * w)
    return delta, m, v


def _adam_big(name, w, m, v, gfull, row0):
    L, r, W = w.shape
    tr = _row_tile(r, W, 8)
    assert row0 % tr == 0
    off = row0 // tr

    def body(w_ref, m_ref, v_ref, g_ref, go_ref, d_ref, mo_ref, vo_ref):
        g = g_ref[...]
        d, mn, vn = _adam_math(w_ref[...], g, m_ref[...], v_ref[...])
        go_ref[...] = g
        d_ref[...] = d
        mo_ref[...] = mn
        vo_ref[...] = vn

    blk = pl.BlockSpec((None, tr, W), lambda l, i: (l, i, 0))
    shp = jax.ShapeDtypeStruct(w.shape, F32)
    return _pallas(
        body, name=f"adam_{name}", grid=(L, r // tr),
        in_specs=[blk, blk, blk, pl.BlockSpec((None, tr, W), lambda l, i: (l, off + i, 0))],
        out_specs=[blk] * 4, out_shape=[shp] * 4,
        compiler_params=_cp(2),
    )(w, m, v, gfull)


def _adam_small(ws, gs, ms, vs):
    n = len(ws)

    def body(*refs):
        for k in range(n):
            w_ref, g_ref, m_ref, v_ref = (refs[j * n + k] for j in range(4))
            d_ref, mo_ref, vo_ref = (refs[(4 + j) * n + k] for j in range(3))
            d, mn, vn = _adam_math(w_ref[...], g_ref[...], m_ref[...], v_ref[...])
            d_ref[...] = d
            mo_ref[...] = mn
            vo_ref[...] = vn

    spec = pl.BlockSpec(memory_space=pltpu.VMEM)
    shp = [jax.ShapeDtypeStruct(w.shape, F32) for w in ws]
    out = _pallas(body, name="adam_small", in_specs=[spec] * (4 * n), out_specs=[spec] * (3 * n),
                  out_shape=shp * 3)(*ws, *gs, *ms, *vs)
    return out[:n], out[n:2 * n], out[2 * n:]


_WEIGHTS = ['norm_ffn1_g', 'ffn1_w_gate', 'ffn1_w_up', 'ffn1_w_down', 'norm_mix_g', 'w_in', 'conv_dw', 'conv_b',
            'conv_ln_g', 'conv_ln_b', 'w_pa', 'short_dw', 'w_pb', 'pool_w', 'pool_scale', 'w_pc', 'w_o',
            'norm_ffn2_g', 'ffn2_w_gate', 'ffn2_w_up', 'ffn2_w_down', 'final_norm_g']
_BIG = ('ffn1_w_gate', 'ffn1_w_up', 'ffn1_w_down', 'w_in', 'w_pa', 'w_pb', 'w_pc', 'w_o',
        'ffn2_w_gate', 'ffn2_w_up', 'ffn2_w_down')
_TRANSPOSED = ('ffn1_w_gate', 'ffn1_w_up', 'ffn2_w_gate', 'ffn2_w_up')
_SMALL = tuple(n for n in _WEIGHTS if n not in _BIG)
_SMALL_REDUCED = ('norm_ffn1_g', 'norm_mix_g', 'conv_b', 'conv_ln_g', 'conv_ln_b', 'pool_w', 'pool_scale',
                  'norm_ffn2_g', 'final_norm_g', 'conv_dw', 'short_dw')


def _pack(arrs, rows_multiple=8):
    flat = jnp.concatenate([a.reshape(-1) for a in arrs])
    n = flat.shape[0]
    per = 128 * rows_multiple
    padded = -(-n // per) * per
    return jnp.pad(flat, (0, padded - n)).reshape(padded // 128, 128)


def _unpack(buf, shapes):
    flat = buf.reshape(-1)
    out, o = [], 0
    for s in shapes:
        k = 1
        for d in s:
            k *= d
        out.append(flat[o:o + k].reshape(s))
        o += k
    return out


def _halves(a):
    return a.reshape(2, a.shape[0] // 2, a.shape[1])


def _step(P, M, V, x, loss_target):
    tr = lambda a: jnp.transpose(a, (0, 2, 1))
    bf = lambda a: a.astype(BF16)
    L = P['w_in'].shape[0]
    cw, sw = P['conv_dw'].shape[-1], P['short_dw'].shape[-1]
    ffa = [jnp.stack([bf(tr(P[f'ffn{f}_w_gate'])), bf(tr(P[f'ffn{f}_w_up']))], axis=1) for f in (1, 2)]
    ffb = [bf(P[f'ffn{f}_w_down']) for f in (1, 2)]
    win = bf(P['w_in'])
    wp = jnp.concatenate([bf(P['w_pa']), bf(P['w_pb']), bf(P['w_pc'])], axis=1)
    wo = bf(P['w_o'])
    dws = jnp.zeros((L, 64, 128), F32)
    dws = dws.at[:, 0:CONV_W, 0:cw].set(P['conv_dw']).at[:, 32:32 + SHORT_W, 0:sw].set(P['short_dw'])
    shards = [dict(f1a=ffa[0][l], f1b=_halves(ffb[0][l]), f2a=ffa[1][l], f2b=_halves(ffb[1][l]),
                   mx=(_halves(win[l]), _halves(wp[l]), _halves(wo[l]), _halves(dws[l]))) for l in range(L)]

    xi, yi, ci = lax.axis_index("x"), lax.axis_index("y"), lax.axis_index("c")
    chip = 2 * xi + yi
    cidx = jnp.stack([ci]).astype(jnp.int32)
    idx = jnp.stack([ci, chip]).astype(jnp.int32)
    count = [0]

    def sum_block(gs, r1):
        t0 = count[0]
        count[0] += len(gs)
        parts = []
        for k, (g, r) in enumerate(zip(gs, r1)):
            W = g.shape[-1]
            p = _sum_sibling(t0 + k, cidx, g.reshape(2, -1, W), r.reshape(-1, W))
            parts.append(p.reshape(g.shape[1:]))
        return parts

    wa1, wb1 = _run_carried(_Carried("gather", [shards[0]["f1a"], shards[0]["f1b"]]), "first")
    small = {n: P[n] for n in _SMALL if n not in ('conv_dw', 'short_dw')}
    loss, dx, blocks, small_g = _fwd_bwd(x, loss_target, shards, small, (cw, sw), (wa1, _merge(wb1)), sum_block)

    finals = {}
    for t, (l, name, parts, r2) in enumerate(blocks):
        prev = finals.get(name, [None] * len(parts))
        finals[name] = [_sum_final(f"{t}_{k}", idx, p, r, l, L, pv) for k, (p, r, pv) in enumerate(zip(parts, r2, prev))]
    shared, tot = _share_final_and_reduce_small(finals["f1"] + finals["f2"] + finals["mx"],
                                                _pack([small_g[n] for n in _SMALL_REDUCED] + [loss.reshape(1)]))
    f_f1, f_f2, f_in, f_p, f_o = [f.reshape(L, -1, f.shape[-1]) for f in shared]
    *tot, loss = _unpack(tot, [small_g[n].shape for n in _SMALL_REDUCED] + [()])
    tot = dict(zip(_SMALL_REDUCED, tot))
    tot['conv_dw'] = lax.dynamic_slice_in_dim(tot['conv_dw'], chip * cw, cw, axis=2)
    tot['short_dw'] = lax.dynamic_slice_in_dim(tot['short_dw'], chip * sw, sw, axis=2)

    F4 = P['ffn1_w_down'].shape[1]
    dc, ds = P['w_pa'].shape[1], P['w_pb'].shape[1]
    src = {'ffn1_w_gate': (f_f1, 0), 'ffn1_w_up': (f_f1, F4), 'ffn1_w_down': (f_f1, 2 * F4),
           'ffn2_w_gate': (f_f2, 0), 'ffn2_w_up': (f_f2, F4), 'ffn2_w_down': (f_f2, 2 * F4),
           'w_in': (f_in, 0), 'w_pa': (f_p, 0), 'w_pb': (f_p, dc), 'w_pc': (f_p, dc + ds), 'w_o': (f_o, 0)}
    grads, deltas, new_m, new_v = {}, {}, {}, {}
    for n in _BIG:
        gfull, row0 = src[n]
        if n in _TRANSPOSED:
            outs = _adam_big(n, tr(P[n]), tr(M[n]), tr(V[n]), gfull, row0)
            grads[n], deltas[n], new_m[n], new_v[n] = [tr(o) for o in outs]
        else:
            grads[n], deltas[n], new_m[n], new_v[n] = _adam_big(n, P[n], M[n], V[n], gfull, row0)
    as2d = lambda a: a.reshape(1, -1) if a.ndim == 1 else a
    d_s, m_s, v_s = _adam_small([as2d(P[n]) for n in _SMALL], [as2d(tot[n]) for n in _SMALL],
                                [as2d(M[n]) for n in _SMALL], [as2d(V[n]) for n in _SMALL])
    for n, d, mm, vv in zip(_SMALL, d_s, m_s, v_s):
        shp = P[n].shape
        grads[n], deltas[n], new_m[n], new_v[n] = tot[n], d.reshape(shp), mm.reshape(shp), vv.reshape(shp)

    return (loss, dx, *[grads[n] for n in _WEIGHTS], *[deltas[n] for n in _WEIGHTS],
            *[new_m[n] for n in _WEIGHTS], *[new_v[n] for n in _WEIGHTS])


def kernel(x, norm_ffn1_g, ffn1_w_gate, ffn1_w_up, ffn1_w_down, norm_mix_g, w_in, conv_dw, conv_b, conv_ln_g, conv_ln_b, w_pa, short_dw, w_pb, pool_w, pool_scale, w_pc, w_o, norm_ffn2_g, ffn2_w_gate, ffn2_w_up, ffn2_w_down, final_norm_g, loss_target, m_norm_ffn1_g, m_ffn1_w_gate, m_ffn1_w_up, m_ffn1_w_down, m_norm_mix_g, m_w_in, m_conv_dw, m_conv_b, m_conv_ln_g, m_conv_ln_b, m_w_pa, m_short_dw, m_w_pb, m_pool_w, m_pool_scale, m_w_pc, m_w_o, m_norm_ffn2_g, m_ffn2_w_gate, m_ffn2_w_up, m_ffn2_w_down, m_final_norm_g, v_norm_ffn1_g, v_ffn1_w_gate, v_ffn1_w_up, v_ffn1_w_down, v_norm_mix_g, v_w_in, v_conv_dw, v_conv_b, v_conv_ln_g, v_conv_ln_b, v_w_pa, v_short_dw, v_w_pb, v_pool_w, v_pool_scale, v_w_pc, v_w_o, v_norm_ffn2_g, v_ffn2_w_gate, v_ffn2_w_up, v_ffn2_w_down, v_final_norm_g):
    args = locals()
    P = {n: args[n] for n in _WEIGHTS}
    M = {n: args["m_" + n] for n in _WEIGHTS}
    V = {n: args["v_" + n] for n in _WEIGHTS}
    return _step(P, M, V, x, loss_target)
```

```python
import jax
import jax.numpy as jnp
from jax import lax
from jax.experimental import pallas as pl
from jax.experimental.pallas import tpu as pltpu

F32 = jnp.float32
BF16 = jnp.bfloat16
EPS = 1e-6
NS = 4
CONV_W = 31
SHORT_W = 3
POOL_WINDOWS = (2, 4, 8, 16)
HALO = 32
ADAM_LR, ADAM_B1, ADAM_B2, ADAM_EPS, ADAM_WD, ADAM_STEP = 0.001, 0.9, 0.999, 1e-08, 0.01, 10
MESH = pl.DeviceIdType.MESH
ANY = pl.BlockSpec(memory_space=pl.ANY)
VMEM_LIMIT = 56 * 1024 * 1024
FFN_BWD_SUBTILES = 4
BIG_BLOCKS_VMEM_LIMIT = 60 * 1024 * 1024


def _pallas(body, **kw):
    return pl.pallas_call(body, **kw)


def _cp(n_axes, vmem_limit=VMEM_LIMIT):
    return pltpu.CompilerParams(dimension_semantics=("arbitrary",) * n_axes, vmem_limit_bytes=vmem_limit)


def _nn(a, b):
    return jnp.dot(a, b, preferred_element_type=F32)


def _nt(a, b):
    return lax.dot_general(a, b, (((1,), (1,)), ((), ())), preferred_element_type=F32)


def _tn(a, b):
    return lax.dot_general(a, b, (((0,), (0,)), ((), ())), preferred_element_type=F32)


def _sigmoid(v):
    return 1.0 / (1.0 + jnp.exp(-v))


def _rms_stats(x):
    rs = lax.rsqrt(jnp.mean(x * x, axis=-1, keepdims=True) + EPS)
    return x * rs, rs


def _rms_bwd(dh, x, g):
    xh, rs = _rms_stats(x)
    dhg = dh * g
    dx = rs * (dhg - xh * jnp.mean(dhg * xh, axis=-1, keepdims=True))
    return dx, jnp.sum(dh * xh, axis=0, keepdims=True)


def _tile(n, pref):
    t = min(n, pref)
    assert n % t == 0, (n, t)
    return t


def _place():
    x, y, c = lax.axis_index("x"), lax.axis_index("y"), lax.axis_index("c")
    chips = [(1 - x, y), (x, 1 - y), (1 - x, 1 - y)]
    return x, y, c, chips


def _gather_copies(ins, outs, sems):
    send_sems, recv_sems, local_sems = sems
    x, y, c, _ = _place()
    me, at_x, at_y, diag = 2 * x + y, 2 * (1 - x) + y, 2 * x + (1 - y), 2 * (1 - x) + (1 - y)
    to_x, to_y, sib = (1 - x, y, c), (x, 1 - y, c), (x, y, 1 - c)

    def remote(ai, k, blk, to, src=None):
        return pltpu.make_async_remote_copy(src_ref=blk if src is None else src, dst_ref=blk,
                                            send_sem=send_sems.at[ai, k], recv_sem=recv_sems.at[ai, k],
                                            device_id=to, device_id_type=MESH)

    g = dict(local=[], first=[], landed=[], relay=[], relayed=[], passed=[], passed_diag=[], from_sib=[])
    for ai in range(len(ins)):
        o = outs[ai]
        h = ins[ai].shape[1] // 2
        lo, hi = pl.ds(0, h), pl.ds(h, h)
        g["local"].append(pltpu.make_async_copy(ins[ai], o.at[me], local_sems.at[ai]))
        g["first"] += [remote(ai, 0, o.at[me, c], to_x, src=ins[ai].at[c]),
                       remote(ai, 1, o.at[me, c], to_y, src=ins[ai].at[c])]
        g["landed"] += [remote(ai, 0, o.at[at_x, c], to_x), remote(ai, 1, o.at[at_y, c], to_y)]
        g["relay"] += [remote(ai, 2, o.at[at_x, c, lo], to_y), remote(ai, 3, o.at[at_y, c, hi], to_x)]
        g["relayed"] += [remote(ai, 2, o.at[diag, c, lo], to_y), remote(ai, 3, o.at[diag, c, hi], to_x)]
        g["passed"] += [remote(ai, 4, o.at[at_x, c], sib), remote(ai, 5, o.at[at_y, c], sib)]
        g["passed_diag"].append(remote(ai, 6, o.at[diag, c], sib))
        g["from_sib"] += [remote(ai, 4, o.at[at_x, 1 - c], sib), remote(ai, 5, o.at[at_y, 1 - c], sib),
                          remote(ai, 6, o.at[diag, 1 - c], sib)]
    return g


def _gather_start(ins, outs, sems):
    g = _gather_copies(ins, outs, sems)
    for cp in g["local"] + g["first"]:
        cp.start()


def _gather_middle(ins, outs, sems):
    g = _gather_copies(ins, outs, sems)
    for arrive, fwd, on in zip(g["landed"], g["passed"], g["relay"]):
        arrive.wait_recv()
        fwd.start()
        on.start()


def _gather_finish(ins, outs, sems):
    g = _gather_copies(ins, outs, sems)
    n = len(g["passed_diag"])
    for ai in range(n):
        g["relayed"][2 * ai].wait_recv()
        g["relayed"][2 * ai + 1].wait_recv()
        g["passed_diag"][ai].start()
    for cp in g["from_sib"]:
        cp.wait_recv()
    for cp in g["first"] + g["relay"] + g["passed"] + g["passed_diag"]:
        cp.wait_send()
    for cp in g["local"]:
        cp.wait()


def _scatter_copies(ins, outs, sems):
    send_sems, recv_sems = sems
    x, y, c, chips = _place()
    return [pltpu.make_async_remote_copy(
        src_ref=ins[ai].at[2 * chip[0] + chip[1]], dst_ref=outs[ai].at[k],
        send_sem=send_sems.at[ai, k], recv_sem=recv_sems.at[ai, k], device_id=(*chip, c), device_id_type=MESH)
        for ai in range(len(ins)) for k, chip in enumerate(chips)]


def _exchange_copies(ins, outs, sems):
    send_sems, recv_sems = sems
    x, y, c, _ = _place()
    return [pltpu.make_async_remote_copy(
        src_ref=ins[ai].at[1 - c], dst_ref=outs[ai], send_sem=send_sems.at[ai], recv_sem=recv_sems.at[ai],
        device_id=(x, y, 1 - c), device_id_type=MESH) for ai in range(len(ins))]


class _Carried:
    def __init__(self, kind="gather", arrs=()):
        self.kind, self.arrs, self.n = kind, tuple(arrs), len(arrs)
        self.specs = [ANY] * self.n
        if kind == "gather":
            self.out_shape = [jax.ShapeDtypeStruct((NS,) + a.shape, a.dtype) for a in self.arrs]
            sems = [(self.n, 7), (self.n, 7), (self.n,)]
        elif kind == "exchange":
            self.out_shape = [jax.ShapeDtypeStruct(a.shape[1:], a.dtype) for a in self.arrs]
            sems = [(self.n,), (self.n,)]
        else:
            self.out_shape = [jax.ShapeDtypeStruct((3,) + a.shape[1:], a.dtype) for a in self.arrs]
            sems = [(self.n, 3), (self.n, 3)]
        self.scratch = [pltpu.SemaphoreType.DMA(s) for s in sems] if self.n else []

    def split(self, refs, n_in, n_out):
        n = self.n
        a, b, c = n_in + n, n_in + n + n_out, n_in + 2 * n + n_out
        n_sem = len(self.scratch)
        own_scr = refs[c:len(refs) - n_sem]
        return refs[:n_in], refs[a:b], own_scr, (refs[n_in:a], refs[b:c], refs[len(refs) - n_sem:])

    def start(self, carried):
        ins, outs, sems = carried
        if self.kind == "gather":
            _gather_start(ins, outs, sems)
        else:
            for cp in (_exchange_copies if self.kind == "exchange" else _scatter_copies)(ins, outs, sems):
                cp.start()

    def middle(self, carried):
        if self.kind == "gather":
            _gather_middle(*carried)

    def finish(self, carried):
        ins, outs, sems = carried
        if self.kind == "gather":
            _gather_finish(ins, outs, sems)
        else:
            for cp in (_exchange_copies if self.kind == "exchange" else _scatter_copies)(ins, outs, sems):
                cp.wait()

    def when(self, cond, carried, what):
        if self.n:
            pl.when(cond)(lambda: what(carried))


_NOTHING = _Carried()


def _run_carried(car, tag):
    def body(*refs):
        _, _, _, carried = car.split(refs, 0, 0)
        car.start(carried)
        car.middle(carried)
        car.finish(carried)

    return _pallas(
        body, name=f"{car.kind}_{tag}",
        in_specs=car.specs, out_specs=car.specs, out_shape=car.out_shape, scratch_shapes=car.scratch,
        compiler_params=pltpu.CompilerParams(has_side_effects=True),
    )(*car.arrs)


def _ffn_weight_specs(F4, D):
    return [pl.BlockSpec((None, None, F4, D), lambda i, j: (j, 0, 0, 0)),
            pl.BlockSpec((None, None, F4, D), lambda i, j: (j, 1, 0, 0)),
            pl.BlockSpec((None, F4, D), lambda i, j: (j, 0, 0))]


def _ffn_fwd(x, g, wa, wb, l, f, car=_NOTHING):
    T, D = x.shape
    F4 = wb.shape[1]
    tm = _tile(T, 1024)
    ni = T // tm

    def body(*refs):
        ((x_ref, g_ref, wg_ref, wu_ref, wd_ref), (xo_ref, h_ref, p_ref, q_ref, z_ref), (acc,),
         carried) = car.split(refs, 5, 5)
        i = pl.program_id(0)
        j = pl.program_id(1)
        car.when((i == 0) & (j == 0), carried, car.start)
        car.when((i == ni // 2) & (j == 0), carried, car.middle)

        @pl.when(j == 0)
        def _():
            xh, _ = _rms_stats(x_ref[...])
            h_ref[...] = (xh * g_ref[...]).astype(BF16)
            acc[...] = jnp.zeros_like(acc)

        h = h_ref[...]
        a = _nt(h, wg_ref[...])
        b = _nt(h, wu_ref[...])
        sg = _sigmoid(a)
        silu = a * sg
        p_ref[...] = (b * (sg + silu * (1.0 - sg))).astype(BF16)
        q_ref[...] = silu.astype(BF16)
        z = (silu * b).astype(BF16)
        z_ref[...] = z
        acc[...] += _nn(z, wd_ref[...])

        @pl.when(j == NS - 1)
        def _():
            xo_ref[...] = x_ref[...] + 0.5 * acc[...]

        car.when((i == ni - 1) & (j == NS - 1), carried, car.finish)

    return _pallas(
        body, name=f"ffn_fwd_{l}_{f}", grid=(ni, NS),
        in_specs=[pl.BlockSpec((tm, D), lambda i, j: (i, 0)), pl.BlockSpec((1, D), lambda i, j: (0, 0))]
        + _ffn_weight_specs(F4, D) + car.specs,
        out_specs=[pl.BlockSpec((tm, D), lambda i, j: (i, 0)),
                   pl.BlockSpec((tm, D), lambda i, j: (i, 0))]
        + [pl.BlockSpec((None, tm, F4), lambda i, j: (j, i, 0))] * 3 + car.specs,
        out_shape=[jax.ShapeDtypeStruct((T, D), F32), jax.ShapeDtypeStruct((T, D), BF16)]
        + [jax.ShapeDtypeStruct((NS, T, F4), BF16)] * 3 + car.out_shape,
        scratch_shapes=[pltpu.VMEM((tm, D), F32)] + car.scratch,
        compiler_params=_cp(2),
    )(x, g, wa, wa, wb, *car.arrs)


def _ffn_bwd(x, g, dy, p, q, wa, wb, l, f, car=_NOTHING):
    T, D = x.shape
    F4 = wb.shape[1]
    tm = _tile(T, 1024)
    ni = T // tm

    def body(*refs):
        ((x_ref, g_ref, dy_ref, p_ref, q_ref, wg_ref, wu_ref, wd_ref),
         (dx_ref, dg_ref, da_ref, db_ref, dyb_ref), _, carried) = car.split(refs, 8, 5)
        dh = dx_ref
        i = pl.program_id(0)
        j = pl.program_id(1)
        car.when((i == 0) & (j == 0), carried, car.start)

        @pl.when(j == 0)
        def _():
            dyb_ref[...] = (0.5 * dy_ref[...]).astype(BF16)
            dh[...] = jnp.zeros_like(dh)

        @pl.when((i == 0) & (j == 0))
        def _():
            dg_ref[...] = jnp.zeros_like(dg_ref)

        for r in range(FFN_BWD_SUBTILES):
            rows = slice(r * (tm // FFN_BWD_SUBTILES), (r + 1) * (tm // FFN_BWD_SUBTILES))
            dz = _nt(dyb_ref[rows, :], wd_ref[...])
            da = (dz * p_ref[rows, :].astype(F32)).astype(BF16)
            db = (dz * q_ref[rows, :].astype(F32)).astype(BF16)
            da_ref[rows, :] = da
            db_ref[rows, :] = db
            dh[rows, :] += _nn(da, wg_ref[...]) + _nn(db, wu_ref[...])

        @pl.when(j == NS - 1)
        def _():
            dxn, dg = _rms_bwd(dh[...], x_ref[...], g_ref[...])
            dx_ref[...] = dy_ref[...] + dxn
            dg_ref[...] += dg

        car.when((i == ni - 1) & (j == NS - 1), carried, car.finish)

    tok = pl.BlockSpec((tm, D), lambda i, j: (i, 0))
    vec = pl.BlockSpec((1, D), lambda i, j: (0, 0))
    chunk = pl.BlockSpec((None, tm, F4), lambda i, j: (j, i, 0))
    return _pallas(
        body, name=f"ffn_bwd_{l}_{f}", grid=(ni, NS),
        in_specs=[tok, vec, tok, chunk, chunk] + _ffn_weight_specs(F4, D) + car.specs,
        out_specs=[tok, vec, chunk, chunk, tok] + car.specs,
        out_shape=[jax.ShapeDtypeStruct((T, D), F32), jax.ShapeDtypeStruct((1, D), F32),
                   jax.ShapeDtypeStruct((NS, T, F4), BF16), jax.ShapeDtypeStruct((NS, T, F4), BF16),
                   jax.ShapeDtypeStruct((T, D), BF16)] + car.out_shape,
        scratch_shapes=car.scratch,
        compiler_params=_cp(2, BIG_BLOCKS_VMEM_LIMIT),
    )(x, g, dy, p, q, wa, wa, wb, *car.arrs)


def _ffn_bwd_first(dy, p, q, wb, l, f, car=_NOTHING):
    T, D = dy.shape
    F4 = wb.shape[1]
    tm = _tile(T, 1024)
    ni = T // tm

    def body(*refs):
        (dy_ref, p_ref, q_ref, wd_ref), (da_ref, db_ref, dyb_ref), _, carried = car.split(refs, 4, 3)
        i = pl.program_id(0)
        j = pl.program_id(1)
        car.when((i == 0) & (j == 0), carried, car.start)

        @pl.when(j == 0)
        def _():
            dyb_ref[...] = (0.5 * dy_ref[...]).astype(BF16)

        for r in range(FFN_BWD_SUBTILES):
            rows = slice(r * (tm // FFN_BWD_SUBTILES), (r + 1) * (tm // FFN_BWD_SUBTILES))
            dz = _nt(dyb_ref[rows, :], wd_ref[...])
            da_ref[rows, :] = (dz * p_ref[rows, :].astype(F32)).astype(BF16)
            db_ref[rows, :] = (dz * q_ref[rows, :].astype(F32)).astype(BF16)

        car.when((i == ni - 1) & (j == NS - 1), carried, car.finish)

    tok = pl.BlockSpec((tm, D), lambda i, j: (i, 0))
    chunk = pl.BlockSpec((None, tm, F4), lambda i, j: (j, i, 0))
    return _pallas(
        body, name=f"ffn_bwd_first_{l}_{f}", grid=(ni, NS),
        in_specs=[tok, chunk, chunk, _ffn_weight_specs(F4, D)[2]] + car.specs,
        out_specs=[chunk, chunk, tok] + car.specs,
        out_shape=[jax.ShapeDtypeStruct((NS, T, F4), BF16), jax.ShapeDtypeStruct((NS, T, F4), BF16),
                   jax.ShapeDtypeStruct((T, D), BF16)] + car.out_shape,
        scratch_shapes=car.scratch,
        compiler_params=_cp(2),
    )(dy, p, q, wb, *car.arrs)


def _ffn_bwd_second(x, g, dy, da, db, wa, l, f, car=_NOTHING):
    T, D = x.shape
    F4 = da.shape[-1]
    tm = _tile(T, 512)
    ni = T // tm

    def body(*refs):
        (x_ref, g_ref, dy_ref, da_ref, db_ref, wg_ref, wu_ref), (dx_ref, dg_ref), (dh,), carried = car.split(refs, 7, 2)
        i = pl.program_id(0)
        j = pl.program_id(1)
        car.when((i == 0) & (j == 0), carried, car.start)

        @pl.when(j == 0)
        def _():
            dh[...] = jnp.zeros_like(dh)

        @pl.when((i == 0) & (j == 0))
        def _():
            dg_ref[...] = jnp.zeros_like(dg_ref)

        dh[...] += _nn(da_ref[...], wg_ref[...]) + _nn(db_ref[...], wu_ref[...])

        @pl.when(j == NS - 1)
        def _():
            dxn, dg = _rms_bwd(dh[...], x_ref[...], g_ref[...])
            dx_ref[...] = dy_ref[...] + dxn
            dg_ref[...] += dg

        car.when((i == ni - 1) & (j == NS - 1), carried, car.finish)

    tok = pl.BlockSpec((tm, D), lambda i, j: (i, 0))
    vec = pl.BlockSpec((1, D), lambda i, j: (0, 0))
    chunk = pl.BlockSpec((None, tm, F4), lambda i, j: (j, i, 0))
    return _pallas(
        body, name=f"ffn_bwd_second_{l}_{f}", grid=(ni, NS),
        in_specs=[tok, vec, tok, chunk, chunk] + _ffn_weight_specs(F4, D)[:2] + car.specs,
        out_specs=[tok, vec] + car.specs,
        out_shape=[jax.ShapeDtypeStruct((T, D), F32), jax.ShapeDtypeStruct((1, D), F32)] + car.out_shape,
        scratch_shapes=[pltpu.VMEM((tm, D), F32)] + car.scratch,
        compiler_params=_cp(2),
    )(x, g, dy, da, db, wa, wa, *car.arrs)


def _ffn_dw(h, dyb, da, db, z, l, f, car=_NOTHING):
    T, D = h.shape
    F4 = da.shape[-1]
    tk = _tile(T, 1024)
    nt = T // tk
    R2 = 3 * F4 // 2

    def body(*refs):
        (h_ref, dyb_ref, da_ref, db_ref, z_ref), (g_ref,), (accg, accu, accd), carried = car.split(refs, 5, 1)
        t = pl.program_id(1)
        car.when((pl.program_id(0) == 0) & (t == 0), carried, car.start)

        @pl.when(t == 0)
        def _():
            accg[...] = jnp.zeros_like(accg)
            accu[...] = jnp.zeros_like(accu)
            accd[...] = jnp.zeros_like(accd)

        hv = h_ref[...]
        accg[...] += _tn(da_ref[...], hv)
        accu[...] += _tn(db_ref[...], hv)
        accd[...] += _tn(z_ref[...], dyb_ref[...])

        @pl.when(t == nt - 1)
        def _():
            g_ref[0, 0:F4, :] = accg[...].astype(BF16)
            g_ref[0, F4:R2, :] = accu[0:R2 - F4, :].astype(BF16)
            g_ref[1, 0:2 * F4 - R2, :] = accu[R2 - F4:F4, :].astype(BF16)
            g_ref[1, 2 * F4 - R2:R2, :] = accd[...].astype(BF16)

        car.when((pl.program_id(0) == NS - 1) & (t == nt - 1), carried, car.finish)

    tok = pl.BlockSpec((tk, D), lambda s, t: (t, 0))
    chunk = pl.BlockSpec((None, tk, F4), lambda s, t: (s, t, 0))
    return _pallas(
        body, name=f"ffn_dw_{l}_{f}", grid=(NS, nt),
        in_specs=[tok, tok, chunk, chunk, chunk] + car.specs,
        out_specs=[pl.BlockSpec((2, None, R2, D), lambda s, t: (0, s, 0, 0))] + car.specs,
        out_shape=[jax.ShapeDtypeStruct((2, NS, R2, D), BF16)] + car.out_shape,
        scratch_shapes=[pltpu.VMEM((F4, D), F32), pltpu.VMEM((F4, D), F32), pltpu.VMEM((F4, D), F32)] + car.scratch,
        compiler_params=_cp(2),
    )(h, dyb, da, db, z, *car.arrs)


def _mix_in(x, g, win, l, car=_NOTHING):
    T, D = x.shape
    C4 = win.shape[-1]
    tm = _tile(T, 1024)
    ni = T // tm

    def body(*refs):
        (x_ref, g_ref, w_ref), (h_ref, u_ref), _, carried = car.split(refs, 3, 2)
        i = pl.program_id(0)
        j = pl.program_id(1)
        car.when((i == 0) & (j == 0), carried, car.start)
        car.when((i == ni // 2) & (j == 0), carried, car.middle)

        @pl.when(j == 0)
        def _():
            xh, _ = _rms_stats(x_ref[...])
            h_ref[...] = (xh * g_ref[...]).astype(BF16)

        u_ref[...] = _nn(h_ref[...], w_ref[...]).astype(BF16)
        car.when((i == ni - 1) & (j == NS - 1), carried, car.finish)

    return _pallas(
        body, name=f"mix_in_{l}", grid=(ni, NS),
        in_specs=[pl.BlockSpec((tm, D), lambda i, j: (i, 0)), pl.BlockSpec((1, D), lambda i, j: (0, 0)),
                  pl.BlockSpec((None, D, C4), lambda i, j: (j, 0, 0))] + car.specs,
        out_specs=[pl.BlockSpec((tm, D), lambda i, j: (i, 0)), pl.BlockSpec((tm, C4), lambda i, j: (i, j))] + car.specs,
        out_shape=[jax.ShapeDtypeStruct((T, D), BF16), jax.ShapeDtypeStruct((T, NS * C4), BF16)] + car.out_shape,
        scratch_shapes=car.scratch,
        compiler_params=_cp(2),
    )(x, g, win, *car.arrs)


def _pool_lane_window(n):
    lane = lax.broadcasted_iota(jnp.int32, (1, n), 1) // (n // len(POOL_WINDOWS))
    w = jnp.full((1, n), float(POOL_WINDOWS[-1]), F32)
    for gi in range(len(POOL_WINDOWS) - 1):
        w = jnp.where(lane == gi, float(POOL_WINDOWS[gi]), w)
    return lane, w


def _pool_select(lane, sums):
    out = sums[-1]
    for gi in range(len(sums) - 1):
        out = jnp.where(lane == gi, sums[gi], out)
    return out


def _back(v, s):
    return v if s == 0 else pltpu.roll(v, s, 0)


def _fwd_shift(v, s):
    return v if s == 0 else pltpu.roll(v, v.shape[0] - s, 0)


def _mix_seq_fwd(u, cdw, cb, lg, lb, sdw, pwblk, ps, bl, l, car=_NOTHING):
    T = u.shape[0]
    S = T // bl
    ts = _tile(S, 256)
    nt = S // ts
    DC, DS = cdw.shape[-1], sdw.shape[-1]
    o_ag, o_bg, o_cg, o_bx, o_p, o_end = DC, 2 * DC, 2 * DC + DS, 2 * DC + 2 * DS, 2 * DC + 3 * DS, 2 * DC + 4 * DS

    def body(*refs):
        ((up_ref, uc_ref, cdw_ref, cb_ref, lg_ref, lb_ref, sdw_ref, pw_ref, ps_ref),
         (act_ref, cv_ref), _, carried) = car.split(refs, 9, 2)
        i = pl.program_id(1)
        car.when((pl.program_id(0) == 0) & (i == 0), carried, car.start)
        car.when((pl.program_id(0) == bl // 2) & (i == 0), carried, car.middle)
        keep = jnp.where(i > 0, 1.0, 0.0).astype(F32)

        def ext(lo, hi):
            p = up_ref[ts - HALO:ts, lo:hi].astype(F32) * keep
            return jnp.concatenate([p, uc_ref[:, lo:hi].astype(F32)], axis=0)

        glu = ext(0, o_ag) * _sigmoid(ext(o_ag, o_bg))
        cv = jnp.zeros((ts, DC), F32) + cb_ref[...]
        for s in range(CONV_W):
            cv = cv + _back(glu, s)[HALO:, :] * cdw_ref[CONV_W - 1 - s:CONV_W - s, :]
        cv_ref[...] = cv
        mu = jnp.mean(cv, axis=-1, keepdims=True)
        xc = cv - mu
        lnv = xc * lax.rsqrt(jnp.mean(xc * xc, axis=-1, keepdims=True) + EPS) * lg_ref[...] + lb_ref[...]
        act_ref[:, 0:DC] = (lnv * _sigmoid(lnv)).astype(BF16)

        q = ext(o_cg, o_bx) * ext(o_bx, o_p)
        sc = jnp.zeros((ts, DS), F32)
        for s in range(SHORT_W):
            sc = sc + _back(q, s)[HALO:, :] * sdw_ref[SHORT_W - 1 - s:SHORT_W - s, :]
        act_ref[:, DC:DC + DS] = (uc_ref[:, o_bg:o_cg].astype(F32) * sc).astype(BF16)

        p = ext(o_p, o_end)
        lane, wl = _pool_lane_window(DS)
        sums, cur, sh = [], p, 1
        for _ in POOL_WINDOWS:
            cur = cur + _back(cur, sh)
            sums.append(cur[HALO:, :])
            sh *= 2
        pos = (i * ts + lax.broadcasted_iota(jnp.int32, (ts, 1), 0) + 1).astype(F32)
        pooled = _pool_select(lane, sums) / jnp.minimum(pos, wl) - p[HALO:, :]
        act_ref[:, DC + DS:DC + 2 * DS] = (_nn(pooled.astype(BF16), pw_ref[...]) * ps_ref[...]).astype(BF16)
        car.when((pl.program_id(0) == bl - 1) & (i == nt - 1), carried, car.finish)

    ucol = 2 * DC + 4 * DS
    full = lambda a: pl.BlockSpec(a.shape, lambda b, i: (0,) * a.ndim)
    return _pallas(
        body, name=f"mix_seq_fwd_{l}", grid=(bl, nt),
        in_specs=[pl.BlockSpec((ts, ucol), lambda b, i: (b * nt + jnp.maximum(i - 1, 0), 0)),
                  pl.BlockSpec((ts, ucol), lambda b, i: (b * nt + i, 0)),
                  full(cdw), full(cb), full(lg), full(lb), full(sdw), full(pwblk), full(ps)] + car.specs,
        out_specs=[pl.BlockSpec((ts, DC + 2 * DS), lambda b, i: (b * nt + i, 0)),
                   pl.BlockSpec((ts, DC), lambda b, i: (b * nt + i, 0))] + car.specs,
        out_shape=[jax.ShapeDtypeStruct((T, DC + 2 * DS), BF16), jax.ShapeDtypeStruct((T, DC), F32)] + car.out_shape,
        scratch_shapes=car.scratch,
        compiler_params=_cp(2),
    )(u, u, cdw, cb, lg, lb, sdw, pwblk, ps, *car.arrs)


def _mix_out(x, act, u, wp, wo, l, car=_NOTHING):
    T, D = x.shape
    tm = _tile(T, 512)
    ni = T // tm
    DA = act.shape[-1]
    DC, DS = DA // 2, DA // 4
    NB = D // NS
    gcol = (2 * DC + 4 * DS) // D

    def body(*refs):
        ((x_ref, act_ref, g0_ref, g1_ref, g2_ref, wp_ref, wo_ref),
         (xo_ref, y_ref, m_ref), _, carried) = car.split(refs, 7, 3)
        car.when(pl.program_id(0) == 0, carried, car.start)
        car.when(pl.program_id(0) == ni // 2, carried, car.middle)
        parts = [(0, DC), (DC, DC + DS), (DC + DS, DC + 2 * DS)]
        m = jnp.zeros((tm, D), F32)
        for k, (lo, hi) in enumerate(parts):
            av = act_ref[:, lo:hi]
            y = jnp.concatenate([_nn(av, wp_ref[s, lo:hi, :]) for s in range(NS)], axis=1)
            y_ref[:, k * D:(k + 1) * D] = y.astype(BF16)
            gl = (g0_ref, g1_ref, g2_ref)[k][...].astype(F32)
            m = m + _sigmoid(gl) * y
        mb = m.astype(BF16)
        m_ref[...] = mb
        out = _nn(mb[:, 0:NB], wo_ref[0])
        for s in range(1, NS):
            out = out + _nn(mb[:, s * NB:(s + 1) * NB], wo_ref[s])
        xo_ref[...] = x_ref[...] + out
        car.when(pl.program_id(0) == ni - 1, carried, car.finish)

    tok = lambda w: pl.BlockSpec((tm, w), lambda i: (i, 0))
    return _pallas(
        body, name=f"mix_out_{l}", grid=(ni,),
        in_specs=[tok(D), tok(DA),
                  pl.BlockSpec((tm, D), lambda i: (i, gcol)), pl.BlockSpec((tm, D), lambda i: (i, gcol + 1)),
                  pl.BlockSpec((tm, D), lambda i: (i, gcol + 2)),
                  pl.BlockSpec((NS, DA, NB), lambda i: (0, 0, 0)),
                  pl.BlockSpec((NS, NB, D), lambda i: (0, 0, 0))] + car.specs,
        out_specs=[tok(D), tok(3 * D), tok(D)] + car.specs,
        out_shape=[jax.ShapeDtypeStruct((T, D), F32), jax.ShapeDtypeStruct((T, 3 * D), BF16),
                   jax.ShapeDtypeStruct((T, D), BF16)] + car.out_shape,
        scratch_shapes=car.scratch,
        compiler_params=_cp(1),
    )(x, act, u, u, u, wp, wo, *car.arrs)


def _mix_out_bwd(dxn, y, u, act, m, wp, wo, l, car=_NOTHING):
    T, D = dxn.shape
    tm = _tile(T, 512)
    nt = T // tm
    DA = act.shape[-1]
    DC, DS = DA // 2, DA // 4
    NB = D // NS
    UC = u.shape[-1]
    g_lo = 2 * DC + 4 * DS
    gcol = g_lo // D
    parts = [(0, DC), (DC, DC + DS), (DC + DS, DC + 2 * DS)]

    def body(*refs):
        ((dx_ref, y_ref, g0_ref, g1_ref, g2_ref, act_ref, m_ref, wp_ref, wo_ref),
         (du_ref, dact_ref, gwo_ref, gwp_ref), (acc_wo, acc_wp), carried) = car.split(refs, 9, 4)
        i = pl.program_id(0)
        car.when(i == 0, carried, car.start)

        @pl.when(i == 0)
        def _():
            acc_wo[...] = jnp.zeros_like(acc_wo)
            acc_wp[...] = jnp.zeros_like(acc_wp)

        dxb = dx_ref[...].astype(BF16)
        dm = jnp.concatenate([_nt(dxb, wo_ref[s]) for s in range(NS)], axis=1)
        acc_wo[...] += _tn(m_ref[...], dxb)
        du_ref[:, 0:g_lo] = jnp.zeros((tm, g_lo), BF16)
        for k, (lo, hi) in enumerate(parts):
            sg = _sigmoid((g0_ref, g1_ref, g2_ref)[k][...].astype(F32))
            yk = y_ref[:, k * D:(k + 1) * D].astype(F32)
            du_ref[:, g_lo + k * D:g_lo + (k + 1) * D] = (dm * yk * sg * (1.0 - sg)).astype(BF16)
            dyk = (dm * sg).astype(BF16)
            dk = _nt(dyk[:, 0:NB], wp_ref[0, lo:hi, :])
            for s in range(1, NS):
                dk = dk + _nt(dyk[:, s * NB:(s + 1) * NB], wp_ref[s, lo:hi, :])
            dact_ref[:, lo:hi] = dk
            acc_wp[lo:hi, :] += _tn(act_ref[:, lo:hi], dyk)

        @pl.when(i == nt - 1)
        def _():
            for s in range(NS):
                for hf in range(2):
                    r0 = s * NB + hf * (NB // 2)
                    gwo_ref[hf, s] = acc_wo[r0:r0 + NB // 2, :].astype(BF16)
                    gwp_ref[hf, s] = acc_wp[hf * (DA // 2):(hf + 1) * (DA // 2), s * NB:(s + 1) * NB].astype(BF16)

        car.when(i == nt - 1, carried, car.finish)

    tok = lambda w: pl.BlockSpec((tm, w), lambda i: (i, 0))
    whole = lambda shp: pl.BlockSpec(shp, lambda i: (0,) * len(shp))
    return _pallas(
        body, name=f"mix_out_bwd_{l}", grid=(nt,),
        in_specs=[tok(D), tok(3 * D),
                  pl.BlockSpec((tm, D), lambda i: (i, gcol)), pl.BlockSpec((tm, D), lambda i: (i, gcol + 1)),
                  pl.BlockSpec((tm, D), lambda i: (i, gcol + 2)),
                  tok(DA), tok(D), whole((NS, DA, NB)), whole((NS, NB, D))] + car.specs,
        out_specs=[tok(UC), tok(DA), whole((2, NS, NB // 2, D)), whole((2, NS, DA // 2, NB))] + car.specs,
        out_shape=[jax.ShapeDtypeStruct((T, UC), BF16), jax.ShapeDtypeStruct((T, DA), F32),
                   jax.ShapeDtypeStruct((2, NS, NB // 2, D), BF16),
                   jax.ShapeDtypeStruct((2, NS, DA // 2, NB), BF16)] + car.out_shape,
        scratch_shapes=[pltpu.VMEM((D, D), F32), pltpu.VMEM((DA, D), F32)] + car.scratch,
        compiler_params=_cp(1, BIG_BLOCKS_VMEM_LIMIT),
    )(dxn, y, u, u, u, act, m, wp, wo, *car.arrs)


def _mix_seq_bwd(du, u, dact, cv, cdw, lg, lb, sdw, pwblk, ps, bl, l, car=_NOTHING):
    T = u.shape[0]
    S = T // bl
    ts = _tile(S, 256)
    nt = S // ts
    DC, DS = cdw.shape[-1], sdw.shape[-1]
    DA = DC + 2 * DS
    o_ag, o_bg, o_cg, o_bx, o_p, o_end = DC, 2 * DC, 2 * DC + DS, 2 * DC + 2 * DS, 2 * DC + 3 * DS, 2 * DC + 4 * DS
    n_f = ts + HALO

    def body(*refs):
        ((_, up_ref, uc_ref, un_ref, dac_ref, dan_ref, cvc_ref, cvn_ref,
          cdw_ref, lg_ref, lb_ref, sdw_ref, pw_ref, ps_ref),
         (du_ref, gcdw_ref, g512_ref, g256_ref, gpw_ref), _, carried) = car.split(refs, 14, 5)
        b = pl.program_id(0)
        i = pl.program_id(1)
        car.when((b == 0) & (i == 0), carried, car.start)
        keep_p = jnp.where(i > 0, 1.0, 0.0).astype(F32)
        keep_n = jnp.where(i < nt - 1, 1.0, 0.0).astype(F32)

        @pl.when((b == 0) & (i == 0))
        def _():
            gcdw_ref[...] = jnp.zeros_like(gcdw_ref)
            g512_ref[...] = jnp.zeros_like(g512_ref)
            g256_ref[...] = jnp.zeros_like(g256_ref)
            gpw_ref[...] = jnp.zeros_like(gpw_ref)

        def back(lo, hi):
            p = up_ref[ts - HALO:ts, lo:hi].astype(F32) * keep_p
            return jnp.concatenate([p, uc_ref[:, lo:hi].astype(F32)], axis=0)

        def fwd(cur, nxt, lo, hi, mask):
            n = nxt[0:HALO, lo:hi].astype(F32)
            if mask:
                n = n * keep_n
            return jnp.concatenate([cur[:, lo:hi].astype(F32), n], axis=0)

        cvx = fwd(cvc_ref, cvn_ref, 0, DC, False)
        dA = fwd(dac_ref, dan_ref, 0, DC, True)
        mu = jnp.mean(cvx, axis=-1, keepdims=True)
        xc = cvx - mu
        rs = lax.rsqrt(jnp.mean(xc * xc, axis=-1, keepdims=True) + EPS)
        xh = xc * rs
        lnv = xh * lg_ref[...] + lb_ref[...]
        sg = _sigmoid(lnv)
        dln = dA * (sg * (1.0 + lnv * (1.0 - sg)))
        dxh = dln * lg_ref[...]
        dcv = rs * (dxh - jnp.mean(dxh, axis=-1, keepdims=True) - xh * jnp.mean(dxh * xh, axis=-1, keepdims=True))
        g512_ref[0:1, :] += jnp.sum(dcv[0:ts], axis=0, keepdims=True)
        g512_ref[1:2, :] += jnp.sum((dln * xh)[0:ts], axis=0, keepdims=True)
        g512_ref[2:3, :] += jnp.sum(dln[0:ts], axis=0, keepdims=True)

        av = back(0, o_ag)
        sga = _sigmoid(back(o_ag, o_bg))
        glu = av * sga
        dcv_c = dcv[0:ts]
        dglu = jnp.zeros((ts, DC), F32)
        for s in range(CONV_W):
            k = CONV_W - 1 - s
            dglu = dglu + _fwd_shift(dcv, s)[0:ts, :] * cdw_ref[k:k + 1, :]
            gcdw_ref[k:k + 1, :] += jnp.sum(_back(glu, s)[HALO:, :] * dcv_c, axis=0, keepdims=True)
        sga_c = sga[HALO:, :]
        du_ref[:, 0:o_ag] = (dglu * sga_c).astype(BF16)
        du_ref[:, o_ag:o_bg] = (dglu * av[HALO:, :] * sga_c * (1.0 - sga_c)).astype(BF16)

        cg = back(o_cg, o_bx)
        bx = back(o_bx, o_p)
        q = cg * bx
        sc = jnp.zeros((ts, DS), F32)
        for s in range(SHORT_W):
            sc = sc + _back(q, s)[HALO:, :] * sdw_ref[SHORT_W - 1 - s:SHORT_W - s, :]
        dB = fwd(dac_ref, dan_ref, DC, DC + DS, True)
        ds = dB * fwd(uc_ref, un_ref, o_bg, o_cg, False)
        du_ref[:, o_bg:o_cg] = (dB[0:ts] * sc).astype(BF16)
        ds_c = ds[0:ts]
        dq = jnp.zeros((ts, DS), F32)
        for s in range(SHORT_W):
            k = SHORT_W - 1 - s
            dq = dq + _fwd_shift(ds, s)[0:ts, :] * sdw_ref[k:k + 1, :]
            g256_ref[k:k + 1, :] += jnp.sum(_back(q, s)[HALO:, :] * ds_c, axis=0, keepdims=True)
        du_ref[:, o_cg:o_bx] = (dq * bx[HALO:, :]).astype(BF16)
        du_ref[:, o_bx:o_p] = (dq * cg[HALO:, :]).astype(BF16)

        p = back(o_p, o_end)
        lane, wl = _pool_lane_window(DS)
        sums, cur, sh = [], p, 1
        for _ in POOL_WINDOWS:
            cur = cur + _back(cur, sh)
            sums.append(cur[HALO:, :])
            sh *= 2
        pos_c = (i * ts + lax.broadcasted_iota(jnp.int32, (ts, 1), 0) + 1).astype(F32)
        pooled = (_pool_select(lane, sums) / jnp.minimum(pos_c, wl) - p[HALO:, :]).astype(BF16)
        pwv = _nn(pooled, pw_ref[...])
        dC = fwd(dac_ref, dan_ref, DC + DS, DA, True)
        g256_ref[SHORT_W:SHORT_W + 1, :] += jnp.sum(dC[0:ts] * pwv, axis=0, keepdims=True)
        dpw = (dC * ps_ref[...]).astype(BF16)
        gpw_ref[...] += _tn(pooled, dpw[0:ts])
        dpl = _nt(dpw, pw_ref[...])
        pos_f = (i * ts + lax.broadcasted_iota(jnp.int32, (n_f, 1), 0) + 1).astype(F32)
        e = dpl / jnp.minimum(pos_f, wl)
        fsums, cur, sh = [], e, 1
        for _ in POOL_WINDOWS:
            cur = cur + _fwd_shift(cur, sh)
            fsums.append(cur[0:ts, :])
            sh *= 2
        du_ref[:, o_p:o_end] = (_pool_select(lane, fsums) - dpl[0:ts]).astype(BF16)
        car.when((b == bl - 1) & (i == nt - 1), carried, car.finish)

    full = lambda a: pl.BlockSpec(a.shape, lambda b, i: (0,) * a.ndim)
    row = lambda w, f: pl.BlockSpec((ts, w), lambda b, i: (b * nt + f(i), 0))
    prv = lambda i: jnp.maximum(i - 1, 0)
    nxt = lambda i: jnp.minimum(i + 1, nt - 1)
    cur = lambda i: i
    return _pallas(
        body, name=f"mix_seq_bwd_{l}", grid=(bl, nt),
        in_specs=[ANY, row(o_end, prv), row(o_end, cur), row(o_end, nxt),
                  row(DA, cur), row(DA, nxt), row(DC, cur), row(DC, nxt),
                  full(cdw), full(lg), full(lb), full(sdw), full(pwblk), full(ps)] + car.specs,
        out_specs=[row(o_end, cur), full(cdw),
                   pl.BlockSpec((8, DC), lambda b, i: (0, 0)), pl.BlockSpec((8, DS), lambda b, i: (0, 0)),
                   full(pwblk)] + car.specs,
        out_shape=[jax.ShapeDtypeStruct(du.shape, BF16), jax.ShapeDtypeStruct(cdw.shape, F32),
                   jax.ShapeDtypeStruct((8, DC), F32), jax.ShapeDtypeStruct((8, DS), F32),
                   jax.ShapeDtypeStruct(pwblk.shape, F32)] + car.out_shape,
        scratch_shapes=car.scratch,
        input_output_aliases={0: 0},
        compiler_params=_cp(2),
    )(du, u, u, u, dact, dact, cv, cv, cdw, lg, lb, sdw, pwblk, ps, *car.arrs)


def _mix_in_bwd(x, g, dxn, du, win, l):
    T, D = x.shape
    C4 = win.shape[-1]
    tm = _tile(T, 1024)

    def body(x_ref, g_ref, dxn_ref, du_ref, w_ref, dx_ref, dg_ref, dh):
        i = pl.program_id(0)
        j = pl.program_id(1)

        @pl.when(j == 0)
        def _():
            dh[...] = jnp.zeros_like(dh)

        @pl.when((i == 0) & (j == 0))
        def _():
            dg_ref[...] = jnp.zeros_like(dg_ref)

        dh[...] += _nt(du_ref[...], w_ref[...])

        @pl.when(j == NS - 1)
        def _():
            dxr, dg = _rms_bwd(dh[...], x_ref[...], g_ref[...])
            dx_ref[...] = dxn_ref[...] + dxr
            dg_ref[...] += dg

    tok = pl.BlockSpec((tm, D), lambda i, j: (i, 0))
    vec = pl.BlockSpec((1, D), lambda i, j: (0, 0))
    return _pallas(
        body, name=f"mix_in_bwd_{l}", grid=(T // tm, NS),
        in_specs=[tok, vec, tok, pl.BlockSpec((tm, C4), lambda i, j: (i, j)),
                  pl.BlockSpec((None, D, C4), lambda i, j: (j, 0, 0))],
        out_specs=[tok, vec],
        out_shape=[jax.ShapeDtypeStruct((T, D), F32), jax.ShapeDtypeStruct((1, D), F32)],
        scratch_shapes=[pltpu.VMEM((tm, D), F32)],
        compiler_params=_cp(2),
    )(x, g, dxn, du, win)


def _mix_in_dw(h, du, l):
    T, D = h.shape
    C4 = du.shape[-1] // NS
    tk = _tile(T, 1024)
    nt = T // tk

    def body(h_ref, du_ref, g_ref, acc):
        t = pl.program_id(1)

        @pl.when(t == 0)
        def _():
            acc[...] = jnp.zeros_like(acc)

        acc[...] += _tn(h_ref[...], du_ref[...])

        @pl.when(t == nt - 1)
        def _():
            g_ref[0] = acc[0:D // 2, :].astype(BF16)
            g_ref[1] = acc[D // 2:D, :].astype(BF16)

    return _pallas(
        body, name=f"mix_in_dw_{l}", grid=(NS, nt),
        in_specs=[pl.BlockSpec((tk, D), lambda s, t: (t, 0)), pl.BlockSpec((tk, C4), lambda s, t: (t, s))],
        out_specs=pl.BlockSpec((2, None, D // 2, C4), lambda s, t: (0, s, 0, 0)),
        out_shape=jax.ShapeDtypeStruct((2, NS, D // 2, C4), BF16),
        scratch_shapes=[pltpu.VMEM((D, C4), F32)],
        compiler_params=_cp(2),
    )(h, du)


def _loss_head(x, g, target):
    T, D = x.shape
    tm = _tile(T, 512)

    def body(x_ref, g_ref, t_ref, dx_ref, loss_ref, dg_ref):
        @pl.when(pl.program_id(0) == 0)
        def _():
            loss_ref[...] = jnp.zeros_like(loss_ref)
            dg_ref[...] = jnp.zeros_like(dg_ref)

        xv = x_ref[...]
        xh, rs = _rms_stats(xv)
        gv = g_ref[...]
        e = xh * gv - t_ref[...]
        loss_ref[...] += 0.5 * jnp.sum(jnp.mean(e * e, axis=-1, keepdims=True))
        dy = e * (1.0 / D)
        dyg = dy * gv
        dx_ref[...] = rs * (dyg - xh * jnp.mean(dyg * xh, axis=-1, keepdims=True))
        dg_ref[...] += jnp.sum(dy * xh, axis=0, keepdims=True)

    tok = pl.BlockSpec((tm, D), lambda i: (i, 0))
    vec = pl.BlockSpec((1, D), lambda i: (0, 0))
    return _pallas(
        body, name="loss_head", grid=(T // tm,),
        in_specs=[tok, vec, tok],
        out_specs=[tok, pl.BlockSpec((8, 128), lambda i: (0, 0)), vec],
        out_shape=[jax.ShapeDtypeStruct((T, D), F32), jax.ShapeDtypeStruct((8, 128), F32),
                   jax.ShapeDtypeStruct((1, D), F32)],
        compiler_params=_cp(1),
    )(x, g, target)


def _block_diag(pw):
    G, c, _ = pw.shape
    out = jnp.zeros((G * c, G * c), pw.dtype)
    for gi in range(G):
        out = lax.dynamic_update_slice(out, pw[gi], (gi * c, gi * c))
    return out


def _pad_rows(a, n):
    return jnp.pad(a, ((0, n - a.shape[0]), (0, 0)))


def _merge(g):
    return g.reshape(g.shape[0], g.shape[1] * g.shape[2], g.shape[3])


def _split_dws(dws_g, cw, sw):
    cdw = jnp.transpose(dws_g[:, 0:CONV_W, 0:cw], (1, 0, 2)).reshape(CONV_W, NS * cw)
    sdw = jnp.transpose(dws_g[:, 32:32 + SHORT_W, 0:sw], (1, 0, 2)).reshape(SHORT_W, NS * sw)
    return cdw, sdw


def _fwd_bwd(x3, target3, shards, small, dw_widths, first, sum_block):
    bl, S, D = x3.shape
    T = bl * S
    x = x3.reshape(T, D)
    target = target3.reshape(T, D)
    L = len(shards)
    cw, sw = dw_widths
    row = lambda v: v[None, :]
    gather = lambda arrs: _Carried("gather", arrs)
    exchange = lambda arrs: _Carried("exchange", arrs)
    scatter = lambda arrs: _Carried("scatter", arrs)

    saved = []
    wa1, wb1 = first
    for l in range(L):
        sh = shards[l]
        nxt = shards[l + 1] if l + 1 < L else None
        sp = dict(pwblk=_block_diag(small["pool_w"][l]).astype(BF16), cb=row(small["conv_b"][l]),
                  lg=row(small["conv_ln_g"][l]), lb=row(small["conv_ln_b"][l]), ps=row(small["pool_scale"][l]),
                  g1=row(small["norm_ffn1_g"][l]), gm=row(small["norm_mix_g"][l]), g2=row(small["norm_ffn2_g"][l]))
        x0 = x
        x1, h1, p1, q1, z1, win_g, wp_g, wo_g, dws_g = _ffn_fwd(x0, sp["g1"], wa1, wb1, l, 0, gather(sh["mx"]))
        win_g, wp_g, wo_g = _merge(win_g), _merge(wp_g), _merge(wo_g)
        cdw, sdw = _split_dws(_merge(dws_g), cw, sw)
        sp["cdw"], sp["sdw"] = _pad_rows(cdw, HALO), _pad_rows(sdw, 8)
        hm, u, wa2 = _mix_in(x1, sp["gm"], win_g, l, gather([sh["f2a"]]))
        act, cv, wb2 = _mix_seq_fwd(u, sp["cdw"], sp["cb"], sp["lg"], sp["lb"], sp["sdw"], sp["pwblk"], sp["ps"],
                                    bl, l, gather([sh["f2b"]]))
        wb2 = _merge(wb2)
        res_o = _mix_out(x1, act, u, wp_g, wo_g, l, gather([nxt["f1b"]]) if nxt else _NOTHING)
        x2, y, m = res_o[:3]
        res_f = _ffn_fwd(x2, sp["g2"], wa2, wb2, l, 1, gather([nxt["f1a"]]) if nxt else _NOTHING)
        x, h2, p2, q2, z2 = res_f[:5]
        saved.append(dict(sp=sp, x0=x0, x1=x1, x2=x2, h1=h1, p1=p1, q1=q1, z1=z1, hm=hm, u=u, act=act, cv=cv, y=y, m=m,
                          h2=h2, p2=p2, q2=q2, z2=z2, wa1=wa1, wb1=wb1, wa2=wa2, wb2=wb2, win=win_g, wp=wp_g, wo=wo_g))
        if nxt:
            wa1, wb1 = res_f[5], _merge(res_o[3])

    dx, loss_blk, dgf = _loss_head(x, row(small["final_norm_g"]), target)
    loss = loss_blk[0, 0]

    sg = {k: [None] * L for k in ("norm_ffn1_g", "norm_mix_g", "norm_ffn2_g", "conv_dw", "conv_b", "conv_ln_g",
                                  "conv_ln_b", "short_dw", "pool_w", "pool_scale")}
    blocks = []
    g_up, l_up = [], None
    G, c = small["pool_w"].shape[1:3]
    for l in reversed(range(L)):
        sv = saved[l]
        sp = sv["sp"]
        res = _ffn_bwd(sv["x2"], sp["g2"], dx, sv["p2"], sv["q2"], sv["wa2"], sv["wb2"], l, 1, exchange(g_up))
        dx, dg2, da, db, dyb = res[:5]
        p_up = sum_block(g_up, res[5:]) if g_up else []
        res = _ffn_dw(sv["h2"], dyb, da, db, sv["z2"], l, 1, scatter(p_up))
        g_f2 = [res[0]]
        if p_up:
            blocks.append((l_up, "f1", p_up, res[1:]))
        res = _mix_out_bwd(dx, sv["y"], sv["u"], sv["act"], sv["m"], sv["wp"], sv["wo"], l, exchange(g_f2))
        du, dact, g_o, g_p = res[:4]
        p_f2 = sum_block(g_f2, res[4:])
        res = _mix_seq_bwd(du, sv["u"], dact, sv["cv"], sp["cdw"], sp["lg"], sp["lb"],
                           sp["sdw"], sp["pwblk"], sp["ps"], bl, l, scatter(p_f2))
        du, gcdw, g512, g256, gpw = res[:5]
        blocks.append((l, "f2", p_f2, res[5:]))
        dx, dgm = _mix_in_bwd(sv["x1"], sp["gm"], dx, du, sv["win"], l)
        g_mx = [_mix_in_dw(sv["hm"], du, l), g_p, g_o]
        if l > 0:
            res = _ffn_bwd(sv["x0"], sp["g1"], dx, sv["p1"], sv["q1"], sv["wa1"], sv["wb1"], l, 0, exchange(g_mx))
            dx, dg1, da, db, dyb = res[:5]
            p_mx = sum_block(g_mx, res[5:])
            res = _ffn_dw(sv["h1"], dyb, da, db, sv["z1"], l, 0, scatter(p_mx))
            blocks.append((l, "mx", p_mx, res[1:]))
            g_up, l_up = [res[0]], l
        else:
            res = _ffn_bwd_first(dx, sv["p1"], sv["q1"], sv["wb1"], l, 0, exchange(g_mx))
            da, db, dyb = res[:3]
            p_mx = sum_block(g_mx, res[3:])
            res = _ffn_dw(sv["h1"], dyb, da, db, sv["z1"], l, 0, scatter(p_mx))
            blocks.append((l, "mx", p_mx, res[1:]))
            g_f1 = [res[0]]
            p_f1 = sum_block(g_f1, _run_carried(exchange(g_f1), "last"))
            res = _ffn_bwd_second(sv["x0"], sp["g1"], dx, da, db, sv["wa1"], l, 0, scatter(p_f1))
            dx, dg1 = res[:2]
            blocks.append((l, "f1", p_f1, res[2:]))
        sg["norm_ffn1_g"][l], sg["norm_mix_g"][l], sg["norm_ffn2_g"][l] = dg1[0], dgm[0], dg2[0]
        sg["conv_dw"][l] = gcdw[:CONV_W]
        sg["conv_b"][l], sg["conv_ln_g"][l], sg["conv_ln_b"][l] = g512[0], g512[1], g512[2]
        sg["short_dw"][l] = g256[:SHORT_W]
        sg["pool_scale"][l] = g256[SHORT_W]
        sg["pool_w"][l] = jnp.stack([gpw[gi * c:(gi + 1) * c, gi * c:(gi + 1) * c] for gi in range(G)])
    small_g = {k: jnp.stack(v) for k, v in sg.items()}
    small_g["final_norm_g"] = dgf[0]
    return loss, dx.reshape(bl, S, D), blocks, small_g


def _share_final_and_reduce_small(fs, v):
    n = len(fs)
    L = fs[0].shape[0]
    R, W = v.shape

    def body(*refs):
        v_ref = refs[0]
        out_ref = refs[1 + n]
        outs = refs[2 + n:2 + 2 * n]
        buf, send_sems, recv_sems, share_send, share_recv = refs[2 + 2 * n:]
        x, y, c, _ = _place()
        sib = (x, y, 1 - c)
        me = 4 * x + 2 * y + c

        def share(ai, l, half):
            blk = outs[ai].at[l, half]
            return pltpu.make_async_remote_copy(src_ref=blk, dst_ref=blk, send_sem=share_send.at[ai, l],
                                                recv_sem=share_recv.at[ai, l], device_id=sib, device_id_type=MESH)

        for ai in range(n):
            for l in range(L):
                share(ai, l, c).start()

        buf[me] = v_ref[...]
        cps = []
        for k in range(1, 8):
            kx, ky, kc = (k >> 2) & 1, (k >> 1) & 1, k & 1
            to = (1 - x if kx else x, 1 - y if ky else y, 1 - c if kc else c)
            cp = pltpu.make_async_remote_copy(src_ref=v_ref, dst_ref=buf.at[me], send_sem=send_sems.at[k - 1],
                                              recv_sem=recv_sems.at[k - 1], device_id=to, device_id_type=MESH)
            cp.start()
            cps.append(cp)
        for cp in cps:
            cp.wait()
        acc = buf[0]
        for d in range(1, 8):
            acc = acc + buf[d]
        out_ref[...] = acc

        for ai in range(n):
            for l in range(L):
                share(ai, l, 1 - c).wait_recv()
                share(ai, l, c).wait_send()

    vmem = pl.BlockSpec(memory_space=pltpu.VMEM)
    out = _pallas(
        body, name="share_final_and_reduce_small",
        in_specs=[vmem] + [ANY] * n, out_specs=[vmem] + [ANY] * n,
        out_shape=[jax.ShapeDtypeStruct((R, W), F32)] + [jax.ShapeDtypeStruct(f.shape, f.dtype) for f in fs],
        scratch_shapes=[pltpu.VMEM((8, R, W), F32), pltpu.SemaphoreType.DMA((7,)), pltpu.SemaphoreType.DMA((7,)),
                        pltpu.SemaphoreType.DMA((n, L)), pltpu.SemaphoreType.DMA((n, L))],
        input_output_aliases={1 + i: 1 + i for i in range(n)},
        compiler_params=pltpu.CompilerParams(has_side_effects=True, vmem_limit_bytes=VMEM_LIMIT),
    )(v, *fs)
    return out[1:], out[0]


def _row_tile(n, w, streams):
    for t in (1056, 1024, 704, 512, 352, 256, 128, 64, 32, 16):
        if n % t == 0 and 2 * streams * t * w * 4 <= VMEM_LIMIT // 2:
            return t
    raise ValueError((n, w))


def _sum_sibling(tag, cidx, g, r):
    _, N, W = g.shape
    tr = _row_tile(N, W, 3)

    def body(c_ref, g_ref, r_ref, o_ref):
        del c_ref
        o_ref[...] = (g_ref[...].astype(F32) + r_ref[...].astype(F32)).astype(BF16)

    return _pallas(
        body, name=f"grad_sum_sibling_{tag}",
        grid_spec=pltpu.PrefetchScalarGridSpec(
            num_scalar_prefetch=1, grid=(N // tr,),
            in_specs=[pl.BlockSpec((None, tr, W), lambda i, c: (c[0], i, 0)),
                      pl.BlockSpec((tr, W), lambda i, c: (i, 0))],
            out_specs=pl.BlockSpec((tr, W), lambda i, c: (i, 0))),
        out_shape=jax.ShapeDtypeStruct((N, W), BF16),
        compiler_params=_cp(1),
    )(cidx, g, r)


def _sum_final(tag, idx, p, r2, l, L, prev):
    _, r, W = p.shape
    tr = _row_tile(r, W, 5)

    def body(*refs):
        p_ref, r2_ref = refs[1:3]
        o_ref = refs[-1]
        acc = p_ref[...].astype(F32)
        for k in range(3):
            acc = acc + r2_ref[k].astype(F32)
        o_ref[...] = acc

    in_specs = [pl.BlockSpec((None, tr, W), lambda i, s: (s[1], i, 0)),
                pl.BlockSpec((3, tr, W), lambda i, s: (0, i, 0))]
    args = [idx, p, r2]
    aliases = {}
    if prev is not None:
        in_specs.append(ANY)
        args.append(prev)
        aliases = {3: 0}
    return _pallas(
        body, name=f"grad_sum_final_{tag}",
        grid_spec=pltpu.PrefetchScalarGridSpec(
            num_scalar_prefetch=1, grid=(r // tr,), in_specs=in_specs,
            out_specs=pl.BlockSpec((None, None, tr, W), lambda i, s: (l, s[0], i, 0))),
        out_shape=jax.ShapeDtypeStruct((L, 2, r, W), F32),
        input_output_aliases=aliases,
        compiler_params=_cp(1),
    )(*args)


def _adam_math(w, g, m, v):
    m = ADAM_B1 * m + (1.0 - ADAM_B1) * g
    v = ADAM_B2 * v + (1.0 - ADAM_B2) * (g * g)
    m_hat = m / (1.0 - ADAM_B1 ** ADAM_STEP)
    v_hat = v / (1.0 - ADAM_B2 ** ADAM_STEP)
    delta = -ADAM_LR * (m_hat / (jnp.sqrt(v_hat) + ADAM_EPS) + ADAM_WD * w)
    return delta, m, v


def _adam_big(name, w, m, v, gfull, row0):
    L, r, W = w.shape
    tr = _row_tile(r, W, 8)
    assert row0 % tr == 0
    off = row0 // tr

    def body(w_ref, m_ref, v_ref, g_ref, go_ref, d_ref, mo_ref, vo_ref):
        g = g_ref[...]
        d, mn, vn = _adam_math(w_ref[...], g, m_ref[...], v_ref[...])
        go_ref[...] = g
        d_ref[...] = d
        mo_ref[...] = mn
        vo_ref[...] = vn

    blk = pl.BlockSpec((None, tr, W), lambda l, i: (l, i, 0))
    shp = jax.ShapeDtypeStruct(w.shape, F32)
    return _pallas(
        body, name=f"adam_{name}", grid=(L, r // tr),
        in_specs=[blk, blk, blk, pl.BlockSpec((None, tr, W), lambda l, i: (l, off + i, 0))],
        out_specs=[blk] * 4, out_shape=[shp] * 4,
        compiler_params=_cp(2),
    )(w, m, v, gfull)


def _adam_small(ws, gs, ms, vs):
    n = len(ws)

    def body(*refs):
        for k in range(n):
            w_ref, g_ref, m_ref, v_ref = (refs[j * n + k] for j in range(4))
            d_ref, mo_ref, vo_ref = (refs[(4 + j) * n + k] for j in range(3))
            d, mn, vn = _adam_math(w_ref[...], g_ref[...], m_ref[...], v_ref[...])
            d_ref[...] = d
            mo_ref[...] = mn
            vo_ref[...] = vn

    spec = pl.BlockSpec(memory_space=pltpu.VMEM)
    shp = [jax.ShapeDtypeStruct(w.shape, F32) for w in ws]
    out = _pallas(body, name="adam_small", in_specs=[spec] * (4 * n), out_specs=[spec] * (3 * n),
                  out_shape=shp * 3)(*ws, *gs, *ms, *vs)
    return out[:n], out[n:2 * n], out[2 * n:]


_WEIGHTS = ['norm_ffn1_g', 'ffn1_w_gate', 'ffn1_w_up', 'ffn1_w_down', 'norm_mix_g', 'w_in', 'conv_dw', 'conv_b',
            'conv_ln_g', 'conv_ln_b', 'w_pa', 'short_dw', 'w_pb', 'pool_w', 'pool_scale', 'w_pc', 'w_o',
            'norm_ffn2_g', 'ffn2_w_gate', 'ffn2_w_up', 'ffn2_w_down', 'final_norm_g']
_BIG = ('ffn1_w_gate', 'ffn1_w_up', 'ffn1_w_down', 'w_in', 'w_pa', 'w_pb', 'w_pc', 'w_o',
        'ffn2_w_gate', 'ffn2_w_up', 'ffn2_w_down')
_TRANSPOSED = ('ffn1_w_gate', 'ffn1_w_up', 'ffn2_w_gate', 'ffn2_w_up')
_SMALL = tuple(n for n in _WEIGHTS if n not in _BIG)
_SMALL_REDUCED = ('norm_ffn1_g', 'norm_mix_g', 'conv_b', 'conv_ln_g', 'conv_ln_b', 'pool_w', 'pool_scale',
                  'norm_ffn2_g', 'final_norm_g', 'conv_dw', 'short_dw')


def _pack(arrs, rows_multiple=8):
    flat = jnp.concatenate([a.reshape(-1) for a in arrs])
    n = flat.shape[0]
    per = 128 * rows_multiple
    padded = -(-n // per) * per
    return jnp.pad(flat, (0, padded - n)).reshape(padded // 128, 128)


def _unpack(buf, shapes):
    flat = buf.reshape(-1)
    out, o = [], 0
    for s in shapes:
        k = 1
        for d in s:
            k *= d
        out.append(flat[o:o + k].reshape(s))
        o += k
    return out


def _halves(a):
    return a.reshape(2, a.shape[0] // 2, a.shape[1])


def _step(P, M, V, x, loss_target):
    tr = lambda a: jnp.transpose(a, (0, 2, 1))
    bf = lambda a: a.astype(BF16)
    L = P['w_in'].shape[0]
    cw, sw = P['conv_dw'].shape[-1], P['short_dw'].shape[-1]
    ffa = [jnp.stack([bf(tr(P[f'ffn{f}_w_gate'])), bf(tr(P[f'ffn{f}_w_up']))], axis=1) for f in (1, 2)]
    ffb = [bf(P[f'ffn{f}_w_down']) for f in (1, 2)]
    win = bf(P['w_in'])
    wp = jnp.concatenate([bf(P['w_pa']), bf(P['w_pb']), bf(P['w_pc'])], axis=1)
    wo = bf(P['w_o'])
    dws = jnp.zeros((L, 64, 128), F32)
    dws = dws.at[:, 0:CONV_W, 0:cw].set(P['conv_dw']).at[:, 32:32 + SHORT_W, 0:sw].set(P['short_dw'])
    shards = [dict(f1a=ffa[0][l], f1b=_halves(ffb[0][l]), f2a=ffa[1][l], f2b=_halves(ffb[1][l]),
                   mx=(_halves(win[l]), _halves(wp[l]), _halves(wo[l]), _halves(dws[l]))) for l in range(L)]

    xi, yi, ci = lax.axis_index("x"), lax.axis_index("y"), lax.axis_index("c")
    chip = 2 * xi + yi
    cidx = jnp.stack([ci]).astype(jnp.int32)
    idx = jnp.stack([ci, chip]).astype(jnp.int32)
    count = [0]

    def sum_block(gs, r1):
        t0 = count[0]
        count[0] += len(gs)
        parts = []
        for k, (g, r) in enumerate(zip(gs, r1)):
            W = g.shape[-1]
            p = _sum_sibling(t0 + k, cidx, g.reshape(2, -1, W), r.reshape(-1, W))
            parts.append(p.reshape(g.shape[1:]))
        return parts

    wa1, wb1 = _run_carried(_Carried("gather", [shards[0]["f1a"], shards[0]["f1b"]]), "first")
    small = {n: P[n] for n in _SMALL if n not in ('conv_dw', 'short_dw')}
    loss, dx, blocks, small_g = _fwd_bwd(x, loss_target, shards, small, (cw, sw), (wa1, _merge(wb1)), sum_block)

    finals = {}
    for t, (l, name, parts, r2) in enumerate(blocks):
        prev = finals.get(name, [None] * len(parts))
        finals[name] = [_sum_final(f"{t}_{k}", idx, p, r, l, L, pv) for k, (p, r, pv) in enumerate(zip(parts, r2, prev))]
    shared, tot = _share_final_and_reduce_small(finals["f1"] + finals["f2"] + finals["mx"],
                                                _pack([small_g[n] for n in _SMALL_REDUCED] + [loss.reshape(1)]))
    f_f1, f_f2, f_in, f_p, f_o = [f.reshape(L, -1, f.shape[-1]) for f in shared]
    *tot, loss = _unpack(tot, [small_g[n].shape for n in _SMALL_REDUCED] + [()])
    tot = dict(zip(_SMALL_REDUCED, tot))
    tot['conv_dw'] = lax.dynamic_slice_in_dim(tot['conv_dw'], chip * cw, cw, axis=2)
    tot['short_dw'] = lax.dynamic_slice_in_dim(tot['short_dw'], chip * sw, sw, axis=2)

    F4 = P['ffn1_w_down'].shape[1]
    dc, ds = P['w_pa'].shape[1], P['w_pb'].shape[1]
    src = {'ffn1_w_gate': (f_f1, 0), 'ffn1_w_up': (f_f1, F4), 'ffn1_w_down': (f_f1, 2 * F4),
           'ffn2_w_gate': (f_f2, 0), 'ffn2_w_up': (f_f2, F4), 'ffn2_w_down': (f_f2, 2 * F4),
           'w_in': (f_in, 0), 'w_pa': (f_p, 0), 'w_pb': (f_p, dc), 'w_pc': (f_p, dc + ds), 'w_o': (f_o, 0)}
    grads, deltas, new_m, new_v = {}, {}, {}, {}
    for n in _BIG:
        gfull, row0 = src[n]
        if n in _TRANSPOSED:
            outs = _adam_big(n, tr(P[n]), tr(M[n]), tr(V[n]), gfull, row0)
            grads[n], deltas[n], new_m[n], new_v[n] = [tr(o) for o in outs]
        else:
            grads[n], deltas[n], new_m[n], new_v[n] = _adam_big(n, P[n], M[n], V[n], gfull, row0)
    as2d = lambda a: a.reshape(1, -1) if a.ndim == 1 else a
    d_s, m_s, v_s = _adam_small([as2d(P[n]) for n in _SMALL], [as2d(tot[n]) for n in _SMALL],
                                [as2d(M[n]) for n in _SMALL], [as2d(V[n]) for n in _SMALL])
    for n, d, mm, vv in zip(_SMALL, d_s, m_s, v_s):
        shp = P[n].shape
        grads[n], deltas[n], new_m[n], new_v[n] = tot[n], d.reshape(shp), mm.reshape(shp), vv.reshape(shp)

    return (loss, dx, *[grads[n] for n in _WEIGHTS], *[deltas[n] for n in _WEIGHTS],
            *[new_m[n] for n in _WEIGHTS], *[new_v[n] for n in _WEIGHTS])


def kernel(x, norm_ffn1_g, ffn1_w_gate, ffn1_w_up, ffn1_w_down, norm_mix_g, w_in, conv_dw, conv_b, conv_ln_g, conv_ln_b, w_pa, short_dw, w_pb, pool_w, pool_scale, w_pc, w_o, norm_ffn2_g, ffn2_w_gate, ffn2_w_up, ffn2_w_down, final_norm_g, loss_target, m_norm_ffn1_g, m_ffn1_w_gate, m_ffn1_w_up, m_ffn1_w_down, m_norm_mix_g, m_w_in, m_conv_dw, m_conv_b, m_conv_ln_g, m_conv_ln_b, m_w_pa, m_short_dw, m_w_pb, m_pool_w, m_pool_scale, m_w_pc, m_w_o, m_norm_ffn2_g, m_ffn2_w_gate, m_ffn2_w_up, m_ffn2_w_down, m_final_norm_g, v_norm_ffn1_g, v_ffn1_w_gate, v_ffn1_w_up, v_ffn1_w_down, v_norm_mix_g, v_w_in, v_conv_dw, v_conv_b, v_conv_ln_g, v_conv_ln_b, v_w_pa, v_short_dw, v_w_pb, v_pool_w, v_pool_scale, v_w_pc, v_w_o, v_norm_ffn2_g, v_ffn2_w_gate, v_ffn2_w_up, v_ffn2_w_down, v_final_norm_g):
    args = locals()
    P = {n: args[n] for n in _WEIGHTS}
    M = {n: args["m_" + n] for n in _WEIGHTS}
    V = {n: args["v_" + n] for n in _WEIGHTS}
    return _step(P, M, V, x, loss_target)
```

```python
import jax
import jax.numpy as jnp
from jax import lax
from jax.experimental import pallas as pl
from jax.experimental.pallas import tpu as pltpu

F32 = jnp.float32
BF16 = jnp.bfloat16
EPS = 1e-6
NS = 4
CONV_W = 31
SHORT_W = 3
POOL_WINDOWS = (2, 4, 8, 16)
HALO = 32
ADAM_LR, ADAM_B1, ADAM_B2, ADAM_EPS, ADAM_WD, ADAM_STEP = 0.001, 0.9, 0.999, 1e-08, 0.01, 10
MESH = pl.DeviceIdType.MESH
ANY = pl.BlockSpec(memory_space=pl.ANY)
VMEM_LIMIT = 56 * 1024 * 1024
FFN_BWD_SUBTILES = 4
BIG_BLOCKS_VMEM_LIMIT = 60 * 1024 * 1024


def _pallas(body, **kw):
    return pl.pallas_call(body, **kw)


def _cp(n_axes, vmem_limit=VMEM_LIMIT):
    return pltpu.CompilerParams(dimension_semantics=("arbitrary",) * n_axes, vmem_limit_bytes=vmem_limit)


def _nn(a, b):
    return jnp.dot(a, b, preferred_element_type=F32)


def _nt(a, b):
    return lax.dot_general(a, b, (((1,), (1,)), ((), ())), preferred_element_type=F32)


def _tn(a, b):
    return lax.dot_general(a, b, (((0,), (0,)), ((), ())), preferred_element_type=F32)


def _sigmoid(v):
    return 1.0 / (1.0 + jnp.exp(-v))


def _rms_stats(x):
    rs = lax.rsqrt(jnp.mean(x * x, axis=-1, keepdims=True) + EPS)
    return x * rs, rs


def _rms_bwd(dh, x, g):
    xh, rs = _rms_stats(x)
    dhg = dh * g
    dx = rs * (dhg - xh * jnp.mean(dhg * xh, axis=-1, keepdims=True))
    return dx, jnp.sum(dh * xh, axis=0, keepdims=True)


def _tile(n, pref):
    t = min(n, pref)
    assert n % t == 0, (n, t)
    return t


def _place():
    x, y, c = lax.axis_index("x"), lax.axis_index("y"), lax.axis_index("c")
    chips = [(1 - x, y), (x, 1 - y), (1 - x, 1 - y)]
    return x, y, c, chips


def _gather_copies(ins, outs, sems):
    send_sems, recv_sems, local_sems = sems
    x, y, c, _ = _place()
    me, at_x, at_y, diag = 2 * x + y, 2 * (1 - x) + y, 2 * x + (1 - y), 2 * (1 - x) + (1 - y)
    to_x, to_y, sib = (1 - x, y, c), (x, 1 - y, c), (x, y, 1 - c)

    def remote(ai, k, blk, to, src=None):
        return pltpu.make_async_remote_copy(src_ref=blk if src is None else src, dst_ref=blk,
                                            send_sem=send_sems.at[ai, k], recv_sem=recv_sems.at[ai, k],
                                            device_id=to, device_id_type=MESH)

    g = dict(local=[], first=[], landed=[], relay=[], relayed=[], passed=[], passed_diag=[], from_sib=[])
    for ai in range(len(ins)):
        o = outs[ai]
        h = ins[ai].shape[1] // 2
        lo, hi = pl.ds(0, h), pl.ds(h, h)
        g["local"].append(pltpu.make_async_copy(ins[ai], o.at[me], local_sems.at[ai]))
        g["first"] += [remote(ai, 0, o.at[me, c], to_x, src=ins[ai].at[c]),
                       remote(ai, 1, o.at[me, c], to_y, src=ins[ai].at[c])]
        g["landed"] += [remote(ai, 0, o.at[at_x, c], to_x), remote(ai, 1, o.at[at_y, c], to_y)]
        g["relay"] += [remote(ai, 2, o.at[at_x, c, lo], to_y), remote(ai, 3, o.at[at_y, c, hi], to_x)]
        g["relayed"] += [remote(ai, 2, o.at[diag, c, lo], to_y), remote(ai, 3, o.at[diag, c, hi], to_x)]
        g["passed"] += [remote(ai, 4, o.at[at_x, c], sib), remote(ai, 5, o.at[at_y, c], sib)]
        g["passed_diag"].append(remote(ai, 6, o.at[diag, c], sib))
        g["from_sib"] += [remote(ai, 4, o.at[at_x, 1 - c], sib), remote(ai, 5, o.at[at_y, 1 - c], sib),
                          remote(ai, 6, o.at[diag, 1 - c], sib)]
    return g


def _gather_start(ins, outs, sems):
    g = _gather_copies(ins, outs, sems)
    for cp in g["local"] + g["first"]:
        cp.start()


def _gather_middle(ins, outs, sems):
    g = _gather_copies(ins, outs, sems)
    for arrive, fwd, on in zip(g["landed"], g["passed"], g["relay"]):
        arrive.wait_recv()
        fwd.start()
        on.start()


def _gather_finish(ins, outs, sems):
    g = _gather_copies(ins, outs, sems)
    n = len(g["passed_diag"])
    for ai in range(n):
        g["relayed"][2 * ai].wait_recv()
        g["relayed"][2 * ai + 1].wait_recv()
        g["passed_diag"][ai].start()
    for cp in g["from_sib"]:
        cp.wait_recv()
    for cp in g["first"] + g["relay"] + g["passed"] + g["passed_diag"]:
        cp.wait_send()
    for cp in g["local"]:
        cp.wait()


def _scatter_copies(ins, outs, sems):
    send_sems, recv_sems = sems
    x, y, c, chips = _place()
    return [pltpu.make_async_remote_copy(
        src_ref=ins[ai].at[2 * chip[0] + chip[1]], dst_ref=outs[ai].at[k],
        send_sem=send_sems.at[ai, k], recv_sem=recv_sems.at[ai, k], device_id=(*chip, c), device_id_type=MESH)
        for ai in range(len(ins)) for k, chip in enumerate(chips)]


def _exchange_copies(ins, outs, sems):
    send_sems, recv_sems = sems
    x, y, c, _ = _place()
    return [pltpu.make_async_remote_copy(
        src_ref=ins[ai].at[1 - c], dst_ref=outs[ai], send_sem=send_sems.at[ai], recv_sem=recv_sems.at[ai],
        device_id=(x, y, 1 - c), device_id_type=MESH) for ai in range(len(ins))]


class _Carried:
    def __init__(self, kind="gather", arrs=()):
        self.kind, self.arrs, self.n = kind, tuple(arrs), len(arrs)
        self.specs = [ANY] * self.n
        if kind == "gather":
            self.out_shape = [jax.ShapeDtypeStruct((NS,) + a.shape, a.dtype) for a in self.arrs]
            sems = [(self.n, 7), (self.n, 7), (self.n,)]
        elif kind == "exchange":
            self.out_shape = [jax.ShapeDtypeStruct(a.shape[1:], a.dtype) for a in self.arrs]
            sems = [(self.n,), (self.n,)]
        else:
            self.out_shape = [jax.ShapeDtypeStruct((3,) + a.shape[1:], a.dtype) for a in self.arrs]
            sems = [(self.n, 3), (self.n, 3)]
        self.scratch = [pltpu.SemaphoreType.DMA(s) for s in sems] if self.n else []

    def split(self, refs, n_in, n_out):
        n = self.n
        a, b, c = n_in + n, n_in + n + n_out, n_in + 2 * n + n_out
        n_sem = len(self.scratch)
        own_scr = refs[c:len(refs) - n_sem]
        return refs[:n_in], refs[a:b], own_scr, (refs[n_in:a], refs[b:c], refs[len(refs) - n_sem:])

    def start(self, carried):
        ins, outs, sems = carried
        if self.kind == "gather":
            _gather_start(ins, outs, sems)
        else:
            for cp in (_exchange_copies if self.kind == "exchange" else _scatter_copies)(ins, outs, sems):
                cp.start()

    def middle(self, carried):
        if self.kind == "gather":
            _gather_middle(*carried)

    def finish(self, carried):
        ins, outs, sems = carried
        if self.kind == "gather":
            _gather_finish(ins, outs, sems)
        else:
            for cp in (_exchange_copies if self.kind == "exchange" else _scatter_copies)(ins, outs, sems):
                cp.wait()

    def when(self, cond, carried, what):
        if self.n:
            pl.when(cond)(lambda: what(carried))


_NOTHING = _Carried()


def _run_carried(car, tag):
    def body(*refs):
        _, _, _, carried = car.split(refs, 0, 0)
        car.start(carried)
        car.middle(carried)
        car.finish(carried)

    return _pallas(
        body, name=f"{car.kind}_{tag}",
        in_specs=car.specs, out_specs=car.specs, out_shape=car.out_shape, scratch_shapes=car.scratch,
        compiler_params=pltpu.CompilerParams(has_side_effects=True),
    )(*car.arrs)


def _ffn_weight_specs(F4, D):
    return [pl.BlockSpec((None, None, F4, D), lambda i, j: (j, 0, 0, 0)),
            pl.BlockSpec((None, None, F4, D), lambda i, j: (j, 1, 0, 0)),
            pl.BlockSpec((None, F4, D), lambda i, j: (j, 0, 0))]


def _ffn_fwd(x, g, wa, wb, l, f, car=_NOTHING):
    T, D = x.shape
    F4 = wb.shape[1]
    tm = _tile(T, 1024)
    ni = T // tm

    def body(*refs):
        ((x_ref, g_ref, wg_ref, wu_ref, wd_ref), (xo_ref, h_ref, p_ref, q_ref, z_ref), (acc,),
         carried) = car.split(refs, 5, 5)
        i = pl.program_id(0)
        j = pl.program_id(1)
        car.when((i == 0) & (j == 0), carried, car.start)
        car.when((i == ni // 2) & (j == 0), carried, car.middle)

        @pl.when(j == 0)
        def _():
            xh, _ = _rms_stats(x_ref[...])
            h_ref[...] = (xh * g_ref[...]).astype(BF16)
            acc[...] = jnp.zeros_like(acc)

        h = h_ref[...]
        a = _nt(h, wg_ref[...])
        b = _nt(h, wu_ref[...])
        sg = _sigmoid(a)
        silu = a * sg
        p_ref[...] = (b * (sg + silu * (1.0 - sg))).astype(BF16)
        q_ref[...] = silu.astype(BF16)
        z = (silu * b).astype(BF16)
        z_ref[...] = z
        acc[...] += _nn(z, wd_ref[...])

        @pl.when(j == NS - 1)
        def _():
            xo_ref[...] = x_ref[...] + 0.5 * acc[...]

        car.when((i == ni - 1) & (j == NS - 1), carried, car.finish)

    return _pallas(
        body, name=f"ffn_fwd_{l}_{f}", grid=(ni, NS),
        in_specs=[pl.BlockSpec((tm, D), lambda i, j: (i, 0)), pl.BlockSpec((1, D), lambda i, j: (0, 0))]
        + _ffn_weight_specs(F4, D) + car.specs,
        out_specs=[pl.BlockSpec((tm, D), lambda i, j: (i, 0)),
                   pl.BlockSpec((tm, D), lambda i, j: (i, 0))]
        + [pl.BlockSpec((None, tm, F4), lambda i, j: (j, i, 0))] * 3 + car.specs,
        out_shape=[jax.ShapeDtypeStruct((T, D), F32), jax.ShapeDtypeStruct((T, D), BF16)]
        + [jax.ShapeDtypeStruct((NS, T, F4), BF16)] * 3 + car.out_shape,
        scratch_shapes=[pltpu.VMEM((tm, D), F32)] + car.scratch,
        compiler_params=_cp(2),
    )(x, g, wa, wa, wb, *car.arrs)


def _ffn_bwd(x, g, dy, p, q, wa, wb, l, f, car=_NOTHING):
    T, D = x.shape
    F4 = wb.shape[1]
    tm = _tile(T, 1024)
    ni = T // tm

    def body(*refs):
        ((x_ref, g_ref, dy_ref, p_ref, q_ref, wg_ref, wu_ref, wd_ref),
         (dx_ref, dg_ref, da_ref, db_ref, dyb_ref), _, carried) = car.split(refs, 8, 5)
        dh = dx_ref
        i = pl.program_id(0)
        j = pl.program_id(1)
        car.when((i == 0) & (j == 0), carried, car.start)

        @pl.when(j == 0)
        def _():
            dyb_ref[...] = (0.5 * dy_ref[...]).astype(BF16)
            dh[...] = jnp.zeros_like(dh)

        @pl.when((i == 0) & (j == 0))
        def _():
            dg_ref[...] = jnp.zeros_like(dg_ref)

        for r in range(FFN_BWD_SUBTILES):
            rows = slice(r * (tm // FFN_BWD_SUBTILES), (r + 1) * (tm // FFN_BWD_SUBTILES))
            dz = _nt(dyb_ref[rows, :], wd_ref[...])
            da = (dz * p_ref[rows, :].astype(F32)).astype(BF16)
            db = (dz * q_ref[rows, :].astype(F32)).astype(BF16)
            da_ref[rows, :] = da
            db_ref[rows, :] = db
            dh[rows, :] += _nn(da, wg_ref[...]) + _nn(db, wu_ref[...])

        @pl.when(j == NS - 1)
        def _():
            dxn, dg = _rms_bwd(dh[...], x_ref[...], g_ref[...])
            dx_ref[...] = dy_ref[...] + dxn
            dg_ref[...] += dg

        car.when((i == ni - 1) & (j == NS - 1), carried, car.finish)

    tok = pl.BlockSpec((tm, D), lambda i, j: (i, 0))
    vec = pl.BlockSpec((1, D), lambda i, j: (0, 0))
    chunk = pl.BlockSpec((None, tm, F4), lambda i, j: (j, i, 0))
    return _pallas(
        body, name=f"ffn_bwd_{l}_{f}", grid=(ni, NS),
        in_specs=[tok, vec, tok, chunk, chunk] + _ffn_weight_specs(F4, D) + car.specs,
        out_specs=[tok, vec, chunk, chunk, tok] + car.specs,
        out_shape=[jax.ShapeDtypeStruct((T, D), F32), jax.ShapeDtypeStruct((1, D), F32),
                   jax.ShapeDtypeStruct((NS, T, F4), BF16), jax.ShapeDtypeStruct((NS, T, F4), BF16),
                   jax.ShapeDtypeStruct((T, D), BF16)] + car.out_shape,
        scratch_shapes=car.scratch,
        compiler_params=_cp(2, BIG_BLOCKS_VMEM_LIMIT),
    )(x, g, dy, p, q, wa, wa, wb, *car.arrs)


def _ffn_bwd_first(dy, p, q, wb, l, f, car=_NOTHING):
    T, D = dy.shape
    F4 = wb.shape[1]
    tm = _tile(T, 1024)
    ni = T // tm

    def body(*refs):
        (dy_ref, p_ref, q_ref, wd_ref), (da_ref, db_ref, dyb_ref), _, carried = car.split(refs, 4, 3)
        i = pl.program_id(0)
        j = pl.program_id(1)
        car.when((i == 0) & (j == 0), carried, car.start)

        @pl.when(j == 0)
        def _():
            dyb_ref[...] = (0.5 * dy_ref[...]).astype(BF16)

        for r in range(FFN_BWD_SUBTILES):
            rows = slice(r * (tm // FFN_BWD_SUBTILES), (r + 1) * (tm // FFN_BWD_SUBTILES))
            dz = _nt(dyb_ref[rows, :], wd_ref[...])
            da_ref[rows, :] = (dz * p_ref[rows, :].astype(F32)).astype(BF16)
            db_ref[rows, :] = (dz * q_ref[rows, :].astype(F32)).astype(BF16)

        car.when((i == ni - 1) & (j == NS - 1), carried, car.finish)

    tok = pl.BlockSpec((tm, D), lambda i, j: (i, 0))
    chunk = pl.BlockSpec((None, tm, F4), lambda i, j: (j, i, 0))
    return _pallas(
        body, name=f"ffn_bwd_first_{l}_{f}", grid=(ni, NS),
        in_specs=[tok, chunk, chunk, _ffn_weight_specs(F4, D)[2]] + car.specs,
        out_specs=[chunk, chunk, tok] + car.specs,
        out_shape=[jax.ShapeDtypeStruct((NS, T, F4), BF16), jax.ShapeDtypeStruct((NS, T, F4), BF16),
                   jax.ShapeDtypeStruct((T, D), BF16)] + car.out_shape,
        scratch_shapes=car.scratch,
        compiler_params=_cp(2),
    )(dy, p, q, wb, *car.arrs)


def _ffn_bwd_second(x, g, dy, da, db, wa, l, f, car=_NOTHING):
    T, D = x.shape
    F4 = da.shape[-1]
    tm = _tile(T, 512)
    ni = T // tm

    def body(*refs):
        (x_ref, g_ref, dy_ref, da_ref, db_ref, wg_ref, wu_ref), (dx_ref, dg_ref), (dh,), carried = car.split(refs, 7, 2)
        i = pl.program_id(0)
        j = pl.program_id(1)
        car.when((i == 0) & (j == 0), carried, car.start)

        @pl.when(j == 0)
        def _():
            dh[...] = jnp.zeros_like(dh)

        @pl.when((i == 0) & (j == 0))
        def _():
            dg_ref[...] = jnp.zeros_like(dg_ref)

        dh[...] += _nn(da_ref[...], wg_ref[...]) + _nn(db_ref[...], wu_ref[...])

        @pl.when(j == NS - 1)
        def _():
            dxn, dg = _rms_bwd(dh[...], x_ref[...], g_ref[...])
            dx_ref[...] = dy_ref[...] + dxn
            dg_ref[...] += dg

        car.when((i == ni - 1) & (j == NS - 1), carried, car.finish)

    tok = pl.BlockSpec((tm, D), lambda i, j: (i, 0))
    vec = pl.BlockSpec((1, D), lambda i, j: (0, 0))
    chunk = pl.BlockSpec((None, tm, F4), lambda i, j: (j, i, 0))
    return _pallas(
        body, name=f"ffn_bwd_second_{l}_{f}", grid=(ni, NS),
        in_specs=[tok, vec, tok, chunk, chunk] + _ffn_weight_specs(F4, D)[:2] + car.specs,
        out_specs=[tok, vec] + car.specs,
        out_shape=[jax.ShapeDtypeStruct((T, D), F32), jax.ShapeDtypeStruct((1, D), F32)] + car.out_shape,
        scratch_shapes=[pltpu.VMEM((tm, D), F32)] + car.scratch,
        compiler_params=_cp(2),
    )(x, g, dy, da, db, wa, wa, *car.arrs)


def _ffn_dw(h, dyb, da, db, z, l, f, car=_NOTHING):
    T, D = h.shape
    F4 = da.shape[-1]
    tk = _tile(T, 2048)
    nt = T // tk
    R2 = 3 * F4 // 2

    def body(*refs):
        (h_ref, dyb_ref, da_ref, db_ref, z_ref), (g_ref,), (accg, accu, accd), carried = car.split(refs, 5, 1)
        t = pl.program_id(1)
        car.when((pl.program_id(0) == 0) & (t == 0), carried, car.start)

        @pl.when(t == 0)
        def _():
            accg[...] = jnp.zeros_like(accg)
            accu[...] = jnp.zeros_like(accu)
            accd[...] = jnp.zeros_like(accd)

        hv = h_ref[...]
        accg[...] += _tn(da_ref[...], hv)
        accu[...] += _tn(db_ref[...], hv)
        accd[...] += _tn(z_ref[...], dyb_ref[...])

        @pl.when(t == nt - 1)
        def _():
            g_ref[0, 0:F4, :] = accg[...].astype(BF16)
            g_ref[0, F4:R2, :] = accu[0:R2 - F4, :].astype(BF16)
            g_ref[1, 0:2 * F4 - R2, :] = accu[R2 - F4:F4, :].astype(BF16)
            g_ref[1, 2 * F4 - R2:R2, :] = accd[...].astype(BF16)

        car.when((pl.program_id(0) == NS - 1) & (t == nt - 1), carried, car.finish)

    tok = pl.BlockSpec((tk, D), lambda s, t: (t, 0))
    chunk = pl.BlockSpec((None, tk, F4), lambda s, t: (s, t, 0))
    return _pallas(
        body, name=f"ffn_dw_{l}_{f}", grid=(NS, nt),
        in_specs=[tok, tok, chunk, chunk, chunk] + car.specs,
        out_specs=[pl.BlockSpec((2, None, R2, D), lambda s, t: (0, s, 0, 0))] + car.specs,
        out_shape=[jax.ShapeDtypeStruct((2, NS, R2, D), BF16)] + car.out_shape,
        scratch_shapes=[pltpu.VMEM((F4, D), F32), pltpu.VMEM((F4, D), F32), pltpu.VMEM((F4, D), F32)] + car.scratch,
        compiler_params=_cp(2),
    )(h, dyb, da, db, z, *car.arrs)


def _mix_in(x, g, win, l, car=_NOTHING):
    T, D = x.shape
    C4 = win.shape[-1]
    tm = _tile(T, 1024)
    ni = T // tm

    def body(*refs):
        (x_ref, g_ref, w_ref), (h_ref, u_ref), _, carried = car.split(refs, 3, 2)
        i = pl.program_id(0)
        j = pl.program_id(1)
        car.when((i == 0) & (j == 0), carried, car.start)
        car.when((i == ni // 2) & (j == 0), carried, car.middle)

        @pl.when(j == 0)
        def _():
            xh, _ = _rms_stats(x_ref[...])
            h_ref[...] = (xh * g_ref[...]).astype(BF16)

        u_ref[...] = _nn(h_ref[...], w_ref[...]).astype(BF16)
        car.when((i == ni - 1) & (j == NS - 1), carried, car.finish)

    return _pallas(
        body, name=f"mix_in_{l}", grid=(ni, NS),
        in_specs=[pl.BlockSpec((tm, D), lambda i, j: (i, 0)), pl.BlockSpec((1, D), lambda i, j: (0, 0)),
                  pl.BlockSpec((None, D, C4), lambda i, j: (j, 0, 0))] + car.specs,
        out_specs=[pl.BlockSpec((tm, D), lambda i, j: (i, 0)), pl.BlockSpec((tm, C4), lambda i, j: (i, j))] + car.specs,
        out_shape=[jax.ShapeDtypeStruct((T, D), BF16), jax.ShapeDtypeStruct((T, NS * C4), BF16)] + car.out_shape,
        scratch_shapes=car.scratch,
        compiler_params=_cp(2),
    )(x, g, win, *car.arrs)


def _pool_lane_window(n):
    lane = lax.broadcasted_iota(jnp.int32, (1, n), 1) // (n // len(POOL_WINDOWS))
    w = jnp.full((1, n), float(POOL_WINDOWS[-1]), F32)
    for gi in range(len(POOL_WINDOWS) - 1):
        w = jnp.where(lane == gi, float(POOL_WINDOWS[gi]), w)
    return lane, w


def _pool_select(lane, sums):
    out = sums[-1]
    for gi in range(len(sums) - 1):
        out = jnp.where(lane == gi, sums[gi], out)
    return out


def _back(v, s):
    return v if s == 0 else pltpu.roll(v, s, 0)


def _fwd_shift(v, s):
    return v if s == 0 else pltpu.roll(v, v.shape[0] - s, 0)


def _mix_seq_fwd(u, cdw, cb, lg, lb, sdw, pwblk, ps, bl, l, car=_NOTHING):
    T = u.shape[0]
    S = T // bl
    ts = _tile(S, 256)
    nt = S // ts
    DC, DS = cdw.shape[-1], sdw.shape[-1]
    o_ag, o_bg, o_cg, o_bx, o_p, o_end = DC, 2 * DC, 2 * DC + DS, 2 * DC + 2 * DS, 2 * DC + 3 * DS, 2 * DC + 4 * DS

    def body(*refs):
        ((up_ref, uc_ref, cdw_ref, cb_ref, lg_ref, lb_ref, sdw_ref, pw_ref, ps_ref),
         (act_ref, cv_ref), _, carried) = car.split(refs, 9, 2)
        i = pl.program_id(1)
        car.when((pl.program_id(0) == 0) & (i == 0), carried, car.start)
        car.when((pl.program_id(0) == bl // 2) & (i == 0), carried, car.middle)
        keep = jnp.where(i > 0, 1.0, 0.0).astype(F32)

        def ext(lo, hi):
            p = up_ref[ts - HALO:ts, lo:hi].astype(F32) * keep
            return jnp.concatenate([p, uc_ref[:, lo:hi].astype(F32)], axis=0)

        glu = ext(0, o_ag) * _sigmoid(ext(o_ag, o_bg))
        cv = jnp.zeros((ts, DC), F32) + cb_ref[...]
        for s in range(CONV_W):
            cv = cv + _back(glu, s)[HALO:, :] * cdw_ref[CONV_W - 1 - s:CONV_W - s, :]
        cv_ref[...] = cv
        mu = jnp.mean(cv, axis=-1, keepdims=True)
        xc = cv - mu
        lnv = xc * lax.rsqrt(jnp.mean(xc * xc, axis=-1, keepdims=True) + EPS) * lg_ref[...] + lb_ref[...]
        act_ref[:, 0:DC] = (lnv * _sigmoid(lnv)).astype(BF16)

        q = ext(o_cg, o_bx) * ext(o_bx, o_p)
        sc = jnp.zeros((ts, DS), F32)
        for s in range(SHORT_W):
            sc = sc + _back(q, s)[HALO:, :] * sdw_ref[SHORT_W - 1 - s:SHORT_W - s, :]
        act_ref[:, DC:DC + DS] = (uc_ref[:, o_bg:o_cg].astype(F32) * sc).astype(BF16)

        p = ext(o_p, o_end)
        lane, wl = _pool_lane_window(DS)
        sums, cur, sh = [], p, 1
        for _ in POOL_WINDOWS:
            cur = cur + _back(cur, sh)
            sums.append(cur[HALO:, :])
            sh *= 2
        pos = (i * ts + lax.broadcasted_iota(jnp.int32, (ts, 1), 0) + 1).astype(F32)
        pooled = _pool_select(lane, sums) / jnp.minimum(pos, wl) - p[HALO:, :]
        act_ref[:, DC + DS:DC + 2 * DS] = (_nn(pooled.astype(BF16), pw_ref[...]) * ps_ref[...]).astype(BF16)
        car.when((pl.program_id(0) == bl - 1) & (i == nt - 1), carried, car.finish)

    ucol = 2 * DC + 4 * DS
    full = lambda a: pl.BlockSpec(a.shape, lambda b, i: (0,) * a.ndim)
    return _pallas(
        body, name=f"mix_seq_fwd_{l}", grid=(bl, nt),
        in_specs=[pl.BlockSpec((ts, ucol), lambda b, i: (b * nt + jnp.maximum(i - 1, 0), 0)),
                  pl.BlockSpec((ts, ucol), lambda b, i: (b * nt + i, 0)),
                  full(cdw), full(cb), full(lg), full(lb), full(sdw), full(pwblk), full(ps)] + car.specs,
        out_specs=[pl.BlockSpec((ts, DC + 2 * DS), lambda b, i: (b * nt + i, 0)),
                   pl.BlockSpec((ts, DC), lambda b, i: (b * nt + i, 0))] + car.specs,
        out_shape=[jax.ShapeDtypeStruct((T, DC + 2 * DS), BF16), jax.ShapeDtypeStruct((T, DC), F32)] + car.out_shape,
        scratch_shapes=car.scratch,
        compiler_params=_cp(2),
    )(u, u, cdw, cb, lg, lb, sdw, pwblk, ps, *car.arrs)


def _mix_out(x, act, u, wp, wo, l, car=_NOTHING):
    T, D = x.shape
    tm = _tile(T, 512)
    ni = T // tm
    DA = act.shape[-1]
    DC, DS = DA // 2, DA // 4
    NB = D // NS
    gcol = (2 * DC + 4 * DS) // D

    def body(*refs):
        ((x_ref, act_ref, g0_ref, g1_ref, g2_ref, wp_ref, wo_ref),
         (xo_ref, y_ref, m_ref), _, carried) = car.split(refs, 7, 3)
        car.when(pl.program_id(0) == 0, carried, car.start)
        car.when(pl.program_id(0) == ni // 2, carried, car.middle)
        parts = [(0, DC), (DC, DC + DS), (DC + DS, DC + 2 * DS)]
        m = jnp.zeros((tm, D), F32)
        for k, (lo, hi) in enumerate(parts):
            av = act_ref[:, lo:hi]
            y = jnp.concatenate([_nn(av, wp_ref[s, lo:hi, :]) for s in range(NS)], axis=1)
            y_ref[:, k * D:(k + 1) * D] = y.astype(BF16)
            gl = (g0_ref, g1_ref, g2_ref)[k][...].astype(F32)
            m = m + _sigmoid(gl) * y
        mb = m.astype(BF16)
        m_ref[...] = mb
        out = _nn(mb[:, 0:NB], wo_ref[0])
        for s in range(1, NS):
            out = out + _nn(mb[:, s * NB:(s + 1) * NB], wo_ref[s])
        xo_ref[...] = x_ref[...] + out
        car.when(pl.program_id(0) == ni - 1, carried, car.finish)

    tok = lambda w: pl.BlockSpec((tm, w), lambda i: (i, 0))
    return _pallas(
        body, name=f"mix_out_{l}", grid=(ni,),
        in_specs=[tok(D), tok(DA),
                  pl.BlockSpec((tm, D), lambda i: (i, gcol)), pl.BlockSpec((tm, D), lambda i: (i, gcol + 1)),
                  pl.BlockSpec((tm, D), lambda i: (i, gcol + 2)),
                  pl.BlockSpec((NS, DA, NB), lambda i: (0, 0, 0)),
                  pl.BlockSpec((NS, NB, D), lambda i: (0, 0, 0))] + car.specs,
        out_specs=[tok(D), tok(3 * D), tok(D)] + car.specs,
        out_shape=[jax.ShapeDtypeStruct((T, D), F32), jax.ShapeDtypeStruct((T, 3 * D), BF16),
                   jax.ShapeDtypeStruct((T, D), BF16)] + car.out_shape,
        scratch_shapes=car.scratch,
        compiler_params=_cp(1),
    )(x, act, u, u, u, wp, wo, *car.arrs)


def _mix_out_bwd(dxn, y, u, act, m, wp, wo, l, car=_NOTHING):
    T, D = dxn.shape
    tm = _tile(T, 512)
    nt = T // tm
    DA = act.shape[-1]
    DC, DS = DA // 2, DA // 4
    NB = D // NS
    UC = u.shape[-1]
    g_lo = 2 * DC + 4 * DS
    gcol = g_lo // D
    parts = [(0, DC), (DC, DC + DS), (DC + DS, DC + 2 * DS)]

    def body(*refs):
        ((dx_ref, y_ref, g0_ref, g1_ref, g2_ref, act_ref, m_ref, wp_ref, wo_ref),
         (du_ref, dact_ref, gwo_ref, gwp_ref), (acc_wo, acc_wp), carried) = car.split(refs, 9, 4)
        i = pl.program_id(0)
        car.when(i == 0, carried, car.start)

        @pl.when(i == 0)
        def _():
            acc_wo[...] = jnp.zeros_like(acc_wo)
            acc_wp[...] = jnp.zeros_like(acc_wp)

        dxb = dx_ref[...].astype(BF16)
        dm = jnp.concatenate([_nt(dxb, wo_ref[s]) for s in range(NS)], axis=1)
        acc_wo[...] += _tn(m_ref[...], dxb)
        du_ref[:, 0:g_lo] = jnp.zeros((tm, g_lo), BF16)
        for k, (lo, hi) in enumerate(parts):
            sg = _sigmoid((g0_ref, g1_ref, g2_ref)[k][...].astype(F32))
            yk = y_ref[:, k * D:(k + 1) * D].astype(F32)
            du_ref[:, g_lo + k * D:g_lo + (k + 1) * D] = (dm * yk * sg * (1.0 - sg)).astype(BF16)
            dyk = (dm * sg).astype(BF16)
            dk = _nt(dyk[:, 0:NB], wp_ref[0, lo:hi, :])
            for s in range(1, NS):
                dk = dk + _nt(dyk[:, s * NB:(s + 1) * NB], wp_ref[s, lo:hi, :])
            dact_ref[:, lo:hi] = dk
            acc_wp[lo:hi, :] += _tn(act_ref[:, lo:hi], dyk)

        @pl.when(i == nt - 1)
        def _():
            for s in range(NS):
                for hf in range(2):
                    r0 = s * NB + hf * (NB // 2)
                    gwo_ref[hf, s] = acc_wo[r0:r0 + NB // 2, :].astype(BF16)
                    gwp_ref[hf, s] = acc_wp[hf * (DA // 2):(hf + 1) * (DA // 2), s * NB:(s + 1) * NB].astype(BF16)

        car.when(i == nt - 1, carried, car.finish)

    tok = lambda w: pl.BlockSpec((tm, w), lambda i: (i, 0))
    whole = lambda shp: pl.BlockSpec(shp, lambda i: (0,) * len(shp))
    return _pallas(
        body, name=f"mix_out_bwd_{l}", grid=(nt,),
        in_specs=[tok(D), tok(3 * D),
                  pl.BlockSpec((tm, D), lambda i: (i, gcol)), pl.BlockSpec((tm, D), lambda i: (i, gcol + 1)),
                  pl.BlockSpec((tm, D), lambda i: (i, gcol + 2)),
                  tok(DA), tok(D), whole((NS, DA, NB)), whole((NS, NB, D))] + car.specs,
        out_specs=[tok(UC), tok(DA), whole((2, NS, NB // 2, D)), whole((2, NS, DA // 2, NB))] + car.specs,
        out_shape=[jax.ShapeDtypeStruct((T, UC), BF16), jax.ShapeDtypeStruct((T, DA), F32),
                   jax.ShapeDtypeStruct((2, NS, NB // 2, D), BF16),
                   jax.ShapeDtypeStruct((2, NS, DA // 2, NB), BF16)] + car.out_shape,
        scratch_shapes=[pltpu.VMEM((D, D), F32), pltpu.VMEM((DA, D), F32)] + car.scratch,
        compiler_params=_cp(1, BIG_BLOCKS_VMEM_LIMIT),
    )(dxn, y, u, u, u, act, m, wp, wo, *car.arrs)


def _mix_seq_bwd(du, u, dact, cv, cdw, lg, lb, sdw, pwblk, ps, bl, l, car=_NOTHING):
    T = u.shape[0]
    S = T // bl
    ts = _tile(S, 256)
    nt = S // ts
    DC, DS = cdw.shape[-1], sdw.shape[-1]
    DA = DC + 2 * DS
    o_ag, o_bg, o_cg, o_bx, o_p, o_end = DC, 2 * DC, 2 * DC + DS, 2 * DC + 2 * DS, 2 * DC + 3 * DS, 2 * DC + 4 * DS
    n_f = ts + HALO

    def body(*refs):
        ((_, up_ref, uc_ref, un_ref, dac_ref, dan_ref, cvc_ref, cvn_ref,
          cdw_ref, lg_ref, lb_ref, sdw_ref, pw_ref, ps_ref),
         (du_ref, gcdw_ref, g512_ref, g256_ref, gpw_ref), _, carried) = car.split(refs, 14, 5)
        b = pl.program_id(0)
        i = pl.program_id(1)
        car.when((b == 0) & (i == 0), carried, car.start)
        keep_p = jnp.where(i > 0, 1.0, 0.0).astype(F32)
        keep_n = jnp.where(i < nt - 1, 1.0, 0.0).astype(F32)

        @pl.when((b == 0) & (i == 0))
        def _():
            gcdw_ref[...] = jnp.zeros_like(gcdw_ref)
            g512_ref[...] = jnp.zeros_like(g512_ref)
            g256_ref[...] = jnp.zeros_like(g256_ref)
            gpw_ref[...] = jnp.zeros_like(gpw_ref)

        def back(lo, hi):
            p = up_ref[ts - HALO:ts, lo:hi].astype(F32) * keep_p
            return jnp.concatenate([p, uc_ref[:, lo:hi].astype(F32)], axis=0)

        def fwd(cur, nxt, lo, hi, mask):
            n = nxt[0:HALO, lo:hi].astype(F32)
            if mask:
                n = n * keep_n
            return jnp.concatenate([cur[:, lo:hi].astype(F32), n], axis=0)

        cvx = fwd(cvc_ref, cvn_ref, 0, DC, False)
        dA = fwd(dac_ref, dan_ref, 0, DC, True)
        mu = jnp.mean(cvx, axis=-1, keepdims=True)
        xc = cvx - mu
        rs = lax.rsqrt(jnp.mean(xc * xc, axis=-1, keepdims=True) + EPS)
        xh = xc * rs
        lnv = xh * lg_ref[...] + lb_ref[...]
        sg = _sigmoid(lnv)
        dln = dA * (sg * (1.0 + lnv * (1.0 - sg)))
        dxh = dln * lg_ref[...]
        dcv = rs * (dxh - jnp.mean(dxh, axis=-1, keepdims=True) - xh * jnp.mean(dxh * xh, axis=-1, keepdims=True))
        g512_ref[0:1, :] += jnp.sum(dcv[0:ts], axis=0, keepdims=True)
        g512_ref[1:2, :] += jnp.sum((dln * xh)[0:ts], axis=0, keepdims=True)
        g512_ref[2:3, :] += jnp.sum(dln[0:ts], axis=0, keepdims=True)

        av = back(0, o_ag)
        sga = _sigmoid(back(o_ag, o_bg))
        glu = av * sga
        dcv_c = dcv[0:ts]
        dglu = jnp.zeros((ts, DC), F32)
        for s in range(CONV_W):
            k = CONV_W - 1 - s
            dglu = dglu + _fwd_shift(dcv, s)[0:ts, :] * cdw_ref[k:k + 1, :]
            gcdw_ref[k:k + 1, :] += jnp.sum(_back(glu, s)[HALO:, :] * dcv_c, axis=0, keepdims=True)
        sga_c = sga[HALO:, :]
        du_ref[:, 0:o_ag] = (dglu * sga_c).astype(BF16)
        du_ref[:, o_ag:o_bg] = (dglu * av[HALO:, :] * sga_c * (1.0 - sga_c)).astype(BF16)

        cg = back(o_cg, o_bx)
        bx = back(o_bx, o_p)
        q = cg * bx
        sc = jnp.zeros((ts, DS), F32)
        for s in range(SHORT_W):
            sc = sc + _back(q, s)[HALO:, :] * sdw_ref[SHORT_W - 1 - s:SHORT_W - s, :]
        dB = fwd(dac_ref, dan_ref, DC, DC + DS, True)
        ds = dB * fwd(uc_ref, un_ref, o_bg, o_cg, False)
        du_ref[:, o_bg:o_cg] = (dB[0:ts] * sc).astype(BF16)
        ds_c = ds[0:ts]
        dq = jnp.zeros((ts, DS), F32)
        for s in range(SHORT_W):
            k = SHORT_W - 1 - s
            dq = dq + _fwd_shift(ds, s)[0:ts, :] * sdw_ref[k:k + 1, :]
            g256_ref[k:k + 1, :] += jnp.sum(_back(q, s)[HALO:, :] * ds_c, axis=0, keepdims=True)
        du_ref[:, o_cg:o_bx] = (dq * bx[HALO:, :]).astype(BF16)
        du_ref[:, o_bx:o_p] = (dq * cg[HALO:, :]).astype(BF16)

        p = back(o_p, o_end)
        lane, wl = _pool_lane_window(DS)
        sums, cur, sh = [], p, 1
        for _ in POOL_WINDOWS:
            cur = cur + _back(cur, sh)
            sums.append(cur[HALO:, :])
            sh *= 2
        pos_c = (i * ts + lax.broadcasted_iota(jnp.int32, (ts, 1), 0) + 1).astype(F32)
        pooled = (_pool_select(lane, sums) / jnp.minimum(pos_c, wl) - p[HALO:, :]).astype(BF16)
        pwv = _nn(pooled, pw_ref[...])
        dC = fwd(dac_ref, dan_ref, DC + DS, DA, True)
        g256_ref[SHORT_W:SHORT_W + 1, :] += jnp.sum(dC[0:ts] * pwv, axis=0, keepdims=True)
        dpw = (dC * ps_ref[...]).astype(BF16)
        gpw_ref[...] += _tn(pooled, dpw[0:ts])
        dpl = _nt(dpw, pw_ref[...])
        pos_f = (i * ts + lax.broadcasted_iota(jnp.int32, (n_f, 1), 0) + 1).astype(F32)
        e = dpl / jnp.minimum(pos_f, wl)
        fsums, cur, sh = [], e, 1
        for _ in POOL_WINDOWS:
            cur = cur + _fwd_shift(cur, sh)
            fsums.append(cur[0:ts, :])
            sh *= 2
        du_ref[:, o_p:o_end] = (_pool_select(lane, fsums) - dpl[0:ts]).astype(BF16)
        car.when((b == bl - 1) & (i == nt - 1), carried, car.finish)

    full = lambda a: pl.BlockSpec(a.shape, lambda b, i: (0,) * a.ndim)
    row = lambda w, f: pl.BlockSpec((ts, w), lambda b, i: (b * nt + f(i), 0))
    prv = lambda i: jnp.maximum(i - 1, 0)
    nxt = lambda i: jnp.minimum(i + 1, nt - 1)
    cur = lambda i: i
    return _pallas(
        body, name=f"mix_seq_bwd_{l}", grid=(bl, nt),
        in_specs=[ANY, row(o_end, prv), row(o_end, cur), row(o_end, nxt),
                  row(DA, cur), row(DA, nxt), row(DC, cur), row(DC, nxt),
                  full(cdw), full(lg), full(lb), full(sdw), full(pwblk), full(ps)] + car.specs,
        out_specs=[row(o_end, cur), full(cdw),
                   pl.BlockSpec((8, DC), lambda b, i: (0, 0)), pl.BlockSpec((8, DS), lambda b, i: (0, 0)),
                   full(pwblk)] + car.specs,
        out_shape=[jax.ShapeDtypeStruct(du.shape, BF16), jax.ShapeDtypeStruct(cdw.shape, F32),
                   jax.ShapeDtypeStruct((8, DC), F32), jax.ShapeDtypeStruct((8, DS), F32),
                   jax.ShapeDtypeStruct(pwblk.shape, F32)] + car.out_shape,
        scratch_shapes=car.scratch,
        input_output_aliases={0: 0},
        compiler_params=_cp(2),
    )(du, u, u, u, dact, dact, cv, cv, cdw, lg, lb, sdw, pwblk, ps, *car.arrs)


def _mix_in_bwd(x, g, dxn, du, win, l):
    T, D = x.shape
    C4 = win.shape[-1]
    tm = _tile(T, 1024)

    def body(x_ref, g_ref, dxn_ref, du_ref, w_ref, dx_ref, dg_ref, dh):
        i = pl.program_id(0)
        j = pl.program_id(1)

        @pl.when(j == 0)
        def _():
            dh[...] = jnp.zeros_like(dh)

        @pl.when((i == 0) & (j == 0))
        def _():
            dg_ref[...] = jnp.zeros_like(dg_ref)

        dh[...] += _nt(du_ref[...], w_ref[...])

        @pl.when(j == NS - 1)
        def _():
            dxr, dg = _rms_bwd(dh[...], x_ref[...], g_ref[...])
            dx_ref[...] = dxn_ref[...] + dxr
            dg_ref[...] += dg

    tok = pl.BlockSpec((tm, D), lambda i, j: (i, 0))
    vec = pl.BlockSpec((1, D), lambda i, j: (0, 0))
    return _pallas(
        body, name=f"mix_in_bwd_{l}", grid=(T // tm, NS),
        in_specs=[tok, vec, tok, pl.BlockSpec((tm, C4), lambda i, j: (i, j)),
                  pl.BlockSpec((None, D, C4), lambda i, j: (j, 0, 0))],
        out_specs=[tok, vec],
        out_shape=[jax.ShapeDtypeStruct((T, D), F32), jax.ShapeDtypeStruct((1, D), F32)],
        scratch_shapes=[pltpu.VMEM((tm, D), F32)],
        compiler_params=_cp(2),
    )(x, g, dxn, du, win)


def _mix_in_dw(h, du, l):
    T, D = h.shape
    C4 = du.shape[-1] // NS
    tk = _tile(T, 2048)
    nt = T // tk

    def body(h_ref, du_ref, g_ref, acc):
        t = pl.program_id(1)

        @pl.when(t == 0)
        def _():
            acc[...] = jnp.zeros_like(acc)

        acc[...] += _tn(h_ref[...], du_ref[...])

        @pl.when(t == nt - 1)
        def _():
            g_ref[0] = acc[0:D // 2, :].astype(BF16)
            g_ref[1] = acc[D // 2:D, :].astype(BF16)

    return _pallas(
        body, name=f"mix_in_dw_{l}", grid=(NS, nt),
        in_specs=[pl.BlockSpec((tk, D), lambda s, t: (t, 0)), pl.BlockSpec((tk, C4), lambda s, t: (t, s))],
        out_specs=pl.BlockSpec((2, None, D // 2, C4), lambda s, t: (0, s, 0, 0)),
        out_shape=jax.ShapeDtypeStruct((2, NS, D // 2, C4), BF16),
        scratch_shapes=[pltpu.VMEM((D, C4), F32)],
        compiler_params=_cp(2),
    )(h, du)


def _loss_head(x, g, target):
    T, D = x.shape
    tm = _tile(T, 512)

    def body(x_ref, g_ref, t_ref, dx_ref, loss_ref, dg_ref):
        @pl.when(pl.program_id(0) == 0)
        def _():
            loss_ref[...] = jnp.zeros_like(loss_ref)
            dg_ref[...] = jnp.zeros_like(dg_ref)

        xv = x_ref[...]
        xh, rs = _rms_stats(xv)
        gv = g_ref[...]
        e = xh * gv - t_ref[...]
        loss_ref[...] += 0.5 * jnp.sum(jnp.mean(e * e, axis=-1, keepdims=True))
        dy = e * (1.0 / D)
        dyg = dy * gv
        dx_ref[...] = rs * (dyg - xh * jnp.mean(dyg * xh, axis=-1, keepdims=True))
        dg_ref[...] += jnp.sum(dy * xh, axis=0, keepdims=True)

    tok = pl.BlockSpec((tm, D), lambda i: (i, 0))
    vec = pl.BlockSpec((1, D), lambda i: (0, 0))
    return _pallas(
        body, name="loss_head", grid=(T // tm,),
        in_specs=[tok, vec, tok],
        out_specs=[tok, pl.BlockSpec((8, 128), lambda i: (0, 0)), vec],
        out_shape=[jax.ShapeDtypeStruct((T, D), F32), jax.ShapeDtypeStruct((8, 128), F32),
                   jax.ShapeDtypeStruct((1, D), F32)],
        compiler_params=_cp(1),
    )(x, g, target)


def _block_diag(pw):
    G, c, _ = pw.shape
    out = jnp.zeros((G * c, G * c), pw.dtype)
    for gi in range(G):
        out = lax.dynamic_update_slice(out, pw[gi], (gi * c, gi * c))
    return out


def _pad_rows(a, n):
    return jnp.pad(a, ((0, n - a.shape[0]), (0, 0)))


def _merge(g):
    return g.reshape(g.shape[0], g.shape[1] * g.shape[2], g.shape[3])


def _split_dws(dws_g, cw, sw):
    cdw = jnp.transpose(dws_g[:, 0:CONV_W, 0:cw], (1, 0, 2)).reshape(CONV_W, NS * cw)
    sdw = jnp.transpose(dws_g[:, 32:32 + SHORT_W, 0:sw], (1, 0, 2)).reshape(SHORT_W, NS * sw)
    return cdw, sdw


def _fwd_bwd(x3, target3, shards, small, dw_widths, first, sum_block):
    bl, S, D = x3.shape
    T = bl * S
    x = x3.reshape(T, D)
    target = target3.reshape(T, D)
    L = len(shards)
    cw, sw = dw_widths
    row = lambda v: v[None, :]
    gather = lambda arrs: _Carried("gather", arrs)
    exchange = lambda arrs: _Carried("exchange", arrs)
    scatter = lambda arrs: _Carried("scatter", arrs)

    saved = []
    wa1, wb1 = first
    for l in range(L):
        sh = shards[l]
        nxt = shards[l + 1] if l + 1 < L else None
        sp = dict(pwblk=_block_diag(small["pool_w"][l]).astype(BF16), cb=row(small["conv_b"][l]),
                  lg=row(small["conv_ln_g"][l]), lb=row(small["conv_ln_b"][l]), ps=row(small["pool_scale"][l]),
                  g1=row(small["norm_ffn1_g"][l]), gm=row(small["norm_mix_g"][l]), g2=row(small["norm_ffn2_g"][l]))
        x0 = x
        x1, h1, p1, q1, z1, win_g, wp_g, wo_g, dws_g = _ffn_fwd(x0, sp["g1"], wa1, wb1, l, 0, gather(sh["mx"]))
        win_g, wp_g, wo_g = _merge(win_g), _merge(wp_g), _merge(wo_g)
        cdw, sdw = _split_dws(_merge(dws_g), cw, sw)
        sp["cdw"], sp["sdw"] = _pad_rows(cdw, HALO), _pad_rows(sdw, 8)
        hm, u, wa2 = _mix_in(x1, sp["gm"], win_g, l, gather([sh["f2a"]]))
        act, cv, wb2 = _mix_seq_fwd(u, sp["cdw"], sp["cb"], sp["lg"], sp["lb"], sp["sdw"], sp["pwblk"], sp["ps"],
                                    bl, l, gather([sh["f2b"]]))
        wb2 = _merge(wb2)
        res_o = _mix_out(x1, act, u, wp_g, wo_g, l, gather([nxt["f1b"]]) if nxt else _NOTHING)
        x2, y, m = res_o[:3]
        res_f = _ffn_fwd(x2, sp["g2"], wa2, wb2, l, 1, gather([nxt["f1a"]]) if nxt else _NOTHING)
        x, h2, p2, q2, z2 = res_f[:5]
        saved.append(dict(sp=sp, x0=x0, x1=x1, x2=x2, h1=h1, p1=p1, q1=q1, z1=z1, hm=hm, u=u, act=act, cv=cv, y=y, m=m,
                          h2=h2, p2=p2, q2=q2, z2=z2, wa1=wa1, wb1=wb1, wa2=wa2, wb2=wb2, win=win_g, wp=wp_g, wo=wo_g))
        if nxt:
            wa1, wb1 = res_f[5], _merge(res_o[3])

    dx, loss_blk, dgf = _loss_head(x, row(small["final_norm_g"]), target)
    loss = loss_blk[0, 0]

    sg = {k: [None] * L for k in ("norm_ffn1_g", "norm_mix_g", "norm_ffn2_g", "conv_dw", "conv_b", "conv_ln_g",
                                  "conv_ln_b", "short_dw", "pool_w", "pool_scale")}
    blocks = []
    g_up, l_up = [], None
    G, c = small["pool_w"].shape[1:3]
    for l in reversed(range(L)):
        sv = saved[l]
        sp = sv["sp"]
        res = _ffn_bwd(sv["x2"], sp["g2"], dx, sv["p2"], sv["q2"], sv["wa2"], sv["wb2"], l, 1, exchange(g_up))
        dx, dg2, da, db, dyb = res[:5]
        p_up = sum_block(g_up, res[5:]) if g_up else []
        res = _ffn_dw(sv["h2"], dyb, da, db, sv["z2"], l, 1, scatter(p_up))
        g_f2 = [res[0]]
        if p_up:
            blocks.append((l_up, "f1", p_up, res[1:]))
        res = _mix_out_bwd(dx, sv["y"], sv["u"], sv["act"], sv["m"], sv["wp"], sv["wo"], l, exchange(g_f2))
        du, dact, g_o, g_p = res[:4]
        p_f2 = sum_block(g_f2, res[4:])
        res = _mix_seq_bwd(du, sv["u"], dact, sv["cv"], sp["cdw"], sp["lg"], sp["lb"],
                           sp["sdw"], sp["pwblk"], sp["ps"], bl, l, scatter(p_f2))
        du, gcdw, g512, g256, gpw = res[:5]
        blocks.append((l, "f2", p_f2, res[5:]))
        dx, dgm = _mix_in_bwd(sv["x1"], sp["gm"], dx, du, sv["win"], l)
        g_mx = [_mix_in_dw(sv["hm"], du, l), g_p, g_o]
        if l > 0:
            res = _ffn_bwd(sv["x0"], sp["g1"], dx, sv["p1"], sv["q1"], sv["wa1"], sv["wb1"], l, 0, exchange(g_mx))
            dx, dg1, da, db, dyb = res[:5]
            p_mx = sum_block(g_mx, res[5:])
            res = _ffn_dw(sv["h1"], dyb, da, db, sv["z1"], l, 0, scatter(p_mx))
            blocks.append((l, "mx", p_mx, res[1:]))
            g_up, l_up = [res[0]], l
        else:
            res = _ffn_bwd_first(dx, sv["p1"], sv["q1"], sv["wb1"], l, 0, exchange(g_mx))
            da, db, dyb = res[:3]
            p_mx = sum_block(g_mx, res[3:])
            res = _ffn_dw(sv["h1"], dyb, da, db, sv["z1"], l, 0, scatter(p_mx))
            blocks.append((l, "mx", p_mx, res[1:]))
            g_f1 = [res[0]]
            p_f1 = sum_block(g_f1, _run_carried(exchange(g_f1), "last"))
            res = _ffn_bwd_second(sv["x0"], sp["g1"], dx, da, db, sv["wa1"], l, 0, scatter(p_f1))
            dx, dg1 = res[:2]
            blocks.append((l, "f1", p_f1, res[2:]))
        sg["norm_ffn1_g"][l], sg["norm_mix_g"][l], sg["norm_ffn2_g"][l] = dg1[0], dgm[0], dg2[0]
        sg["conv_dw"][l] = gcdw[:CONV_W]
        sg["conv_b"][l], sg["conv_ln_g"][l], sg["conv_ln_b"][l] = g512[0], g512[1], g512[2]
        sg["short_dw"][l] = g256[:SHORT_W]
        sg["pool_scale"][l] = g256[SHORT_W]
        sg["pool_w"][l] = jnp.stack([gpw[gi * c:(gi + 1) * c, gi * c:(gi + 1) * c] for gi in range(G)])
    small_g = {k: jnp.stack(v) for k, v in sg.items()}
    small_g["final_norm_g"] = dgf[0]
    return loss, dx.reshape(bl, S, D), blocks, small_g


def _share_final_and_reduce_small(fs, v):
    n = len(fs)
    L = fs[0].shape[0]
    R, W = v.shape

    def body(*refs):
        v_ref = refs[0]
        out_ref = refs[1 + n]
        outs = refs[2 + n:2 + 2 * n]
        buf, send_sems, recv_sems, share_send, share_recv = refs[2 + 2 * n:]
        x, y, c, _ = _place()
        sib = (x, y, 1 - c)
        me = 4 * x + 2 * y + c

        def share(ai, l, half):
            blk = outs[ai].at[l, half]
            return pltpu.make_async_remote_copy(src_ref=blk, dst_ref=blk, send_sem=share_send.at[ai, l],
                                                recv_sem=share_recv.at[ai, l], device_id=sib, device_id_type=MESH)

        for ai in range(n):
            for l in range(L):
                share(ai, l, c).start()

        buf[me] = v_ref[...]
        cps = []
        for k in range(1, 8):
            kx, ky, kc = (k >> 2) & 1, (k >> 1) & 1, k & 1
            to = (1 - x if kx else x, 1 - y if ky else y, 1 - c if kc else c)
            cp = pltpu.make_async_remote_copy(src_ref=v_ref, dst_ref=buf.at[me], send_sem=send_sems.at[k - 1],
                                              recv_sem=recv_sems.at[k - 1], device_id=to, device_id_type=MESH)
            cp.start()
            cps.append(cp)
        for cp in cps:
            cp.wait()
        acc = buf[0]
        for d in range(1, 8):
            acc = acc + buf[d]
        out_ref[...] = acc

        for ai in range(n):
            for l in range(L):
                share(ai, l, 1 - c).wait_recv()
                share(ai, l, c).wait_send()

    vmem = pl.BlockSpec(memory_space=pltpu.VMEM)
    out = _pallas(
        body, name="share_final_and_reduce_small",
        in_specs=[vmem] + [ANY] * n, out_specs=[vmem] + [ANY] * n,
        out_shape=[jax.ShapeDtypeStruct((R, W), F32)] + [jax.ShapeDtypeStruct(f.shape, f.dtype) for f in fs],
        scratch_shapes=[pltpu.VMEM((8, R, W), F32), pltpu.SemaphoreType.DMA((7,)), pltpu.SemaphoreType.DMA((7,)),
                        pltpu.SemaphoreType.DMA((n, L)), pltpu.SemaphoreType.DMA((n, L))],
        input_output_aliases={1 + i: 1 + i for i in range(n)},
        compiler_params=pltpu.CompilerParams(has_side_effects=True, vmem_limit_bytes=VMEM_LIMIT),
    )(v, *fs)
    return out[1:], out[0]


def _row_tile(n, w, streams):
    for t in (1056, 1024, 704, 512, 352, 256, 128, 64, 32, 16):
        if n % t == 0 and 2 * streams * t * w * 4 <= VMEM_LIMIT // 2:
            return t
    raise ValueError((n, w))


def _sum_sibling(tag, cidx, g, r):
    _, N, W = g.shape
    tr = _row_tile(N, W, 3)

    def body(c_ref, g_ref, r_ref, o_ref):
        del c_ref
        o_ref[...] = (g_ref[...].astype(F32) + r_ref[...].astype(F32)).astype(BF16)

    return _pallas(
        body, name=f"grad_sum_sibling_{tag}",
        grid_spec=pltpu.PrefetchScalarGridSpec(
            num_scalar_prefetch=1, grid=(N // tr,),
            in_specs=[pl.BlockSpec((None, tr, W), lambda i, c: (c[0], i, 0)),
                      pl.BlockSpec((tr, W), lambda i, c: (i, 0))],
            out_specs=pl.BlockSpec((tr, W), lambda i, c: (i, 0))),
        out_shape=jax.ShapeDtypeStruct((N, W), BF16),
        compiler_params=_cp(1),
    )(cidx, g, r)


def _sum_final(tag, idx, p, r2, l, L, prev):
    _, r, W = p.shape
    tr = _row_tile(r, W, 5)

    def body(*refs):
        p_ref, r2_ref = refs[1:3]
        o_ref = refs[-1]
        acc = p_ref[...].astype(F32)
        for k in range(3):
            acc = acc + r2_ref[k].astype(F32)
        o_ref[...] = acc

    in_specs = [pl.BlockSpec((None, tr, W), lambda i, s: (s[1], i, 0)),
                pl.BlockSpec((3, tr, W), lambda i, s: (0, i, 0))]
    args = [idx, p, r2]
    aliases = {}
    if prev is not None:
        in_specs.append(ANY)
        args.append(prev)
        aliases = {3: 0}
    return _pallas(
        body, name=f"grad_sum_final_{tag}",
        grid_spec=pltpu.PrefetchScalarGridSpec(
            num_scalar_prefetch=1, grid=(r // tr,), in_specs=in_specs,
            out_specs=pl.BlockSpec((None, None, tr, W), lambda i, s: (l, s[0], i, 0))),
        out_shape=jax.ShapeDtypeStruct((L, 2, r, W), F32),
        input_output_aliases=aliases,
        compiler_params=_cp(1),
    )(*args)


def _adam_math(w, g, m, v):
    m = ADAM_B1 * m + (1.0 - ADAM_B1) * g
    v = ADAM_B2 * v + (1.0 - ADAM_B2) * (g * g)
    m_hat = m / (1.0 - ADAM_B1 ** ADAM_STEP)
    v_hat = v / (1.0 - ADAM_B2 ** ADAM_STEP)
    delta = -ADAM_LR * (m_hat / (jnp.sqrt(v_hat) + ADAM_EPS) + ADAM_WD * w)
    return delta, m, v


def _adam_big(name, w, m, v, gfull, row0):
    L, r, W = w.shape
    tr = _row_tile(r, W, 8)
    assert row0 % tr == 0
    off = row0 // tr

    def body(w_ref, m_ref, v_ref, g_ref, go_ref, d_ref, mo_ref, vo_ref):
        g = g_ref[...]
        d, mn, vn = _adam_math(w_ref[...], g, m_ref[...], v_ref[...])
        go_ref[...] = g
        d_ref[...] = d
        mo_ref[...] = mn
        vo_ref[...] = vn

    blk = pl.BlockSpec((None, tr, W), lambda l, i: (l, i, 0))
    shp = jax.ShapeDtypeStruct(w.shape, F32)
    return _pallas(
        body, name=f"adam_{name}", grid=(L, r // tr),
        in_specs=[blk, blk, blk, pl.BlockSpec((None, tr, W), lambda l, i: (l, off + i, 0))],
        out_specs=[blk] * 4, out_shape=[shp] * 4,
        compiler_params=_cp(2),
    )(w, m, v, gfull)


def _adam_small(ws, gs, ms, vs):
    n = len(ws)

    def body(*refs):
        for k in range(n):
            w_ref, g_ref, m_ref, v_ref = (refs[j * n + k] for j in range(4))
            d_ref, mo_ref, vo_ref = (refs[(4 + j) * n + k] for j in range(3))
            d, mn, vn = _adam_math(w_ref[...], g_ref[...], m_ref[...], v_ref[...])
            d_ref[...] = d
            mo_ref[...] = mn
            vo_ref[...] = vn

    spec = pl.BlockSpec(memory_space=pltpu.VMEM)
    shp = [jax.ShapeDtypeStruct(w.shape, F32) for w in ws]
    out = _pallas(body, name="adam_small", in_specs=[spec] * (4 * n), out_specs=[spec] * (3 * n),
                  out_shape=shp * 3)(*ws, *gs, *ms, *vs)
    return out[:n], out[n:2 * n], out[2 * n:]


_WEIGHTS = ['norm_ffn1_g', 'ffn1_w_gate', 'ffn1_w_up', 'ffn1_w_down', 'norm_mix_g', 'w_in', 'conv_dw', 'conv_b',
            'conv_ln_g', 'conv_ln_b', 'w_pa', 'short_dw', 'w_pb', 'pool_w', 'pool_scale', 'w_pc', 'w_o',
            'norm_ffn2_g', 'ffn2_w_gate', 'ffn2_w_up', 'ffn2_w_down', 'final_norm_g']
_BIG = ('ffn1_w_gate', 'ffn1_w_up', 'ffn1_w_down', 'w_in', 'w_pa', 'w_pb', 'w_pc', 'w_o',
        'ffn2_w_gate', 'ffn2_w_up', 'ffn2_w_down')
_TRANSPOSED = ('ffn1_w_gate', 'ffn1_w_up', 'ffn2_w_gate', 'ffn2_w_up')
_SMALL = tuple(n for n in _WEIGHTS if n not in _BIG)
_SMALL_REDUCED = ('norm_ffn1_g', 'norm_mix_g', 'conv_b', 'conv_ln_g', 'conv_ln_b', 'pool_w', 'pool_scale',
                  'norm_ffn2_g', 'final_norm_g', 'conv_dw', 'short_dw')


def _pack(arrs, rows_multiple=8):
    flat = jnp.concatenate([a.reshape(-1) for a in arrs])
    n = flat.shape[0]
    per = 128 * rows_multiple
    padded = -(-n // per) * per
    return jnp.pad(flat, (0, padded - n)).reshape(padded // 128, 128)


def _unpack(buf, shapes):
    flat = buf.reshape(-1)
    out, o = [], 0
    for s in shapes:
        k = 1
        for d in s:
            k *= d
        out.append(flat[o:o + k].reshape(s))
        o += k
    return out


def _halves(a):
    return a.reshape(2, a.shape[0] // 2, a.shape[1])


def _step(P, M, V, x, loss_target):
    tr = lambda a: jnp.transpose(a, (0, 2, 1))
    bf = lambda a: a.astype(BF16)
    L = P['w_in'].shape[0]
    cw, sw = P['conv_dw'].shape[-1], P['short_dw'].shape[-1]
    ffa = [jnp.stack([bf(tr(P[f'ffn{f}_w_gate'])), bf(tr(P[f'ffn{f}_w_up']))], axis=1) for f in (1, 2)]
    ffb = [bf(P[f'ffn{f}_w_down']) for f in (1, 2)]
    win = bf(P['w_in'])
    wp = jnp.concatenate([bf(P['w_pa']), bf(P['w_pb']), bf(P['w_pc'])], axis=1)
    wo = bf(P['w_o'])
    dws = jnp.zeros((L, 64, 128), F32)
    dws = dws.at[:, 0:CONV_W, 0:cw].set(P['conv_dw']).at[:, 32:32 + SHORT_W, 0:sw].set(P['short_dw'])
    shards = [dict(f1a=ffa[0][l], f1b=_halves(ffb[0][l]), f2a=ffa[1][l], f2b=_halves(ffb[1][l]),
                   mx=(_halves(win[l]), _halves(wp[l]), _halves(wo[l]), _halves(dws[l]))) for l in range(L)]

    xi, yi, ci = lax.axis_index("x"), lax.axis_index("y"), lax.axis_index("c")
    chip = 2 * xi + yi
    cidx = jnp.stack([ci]).astype(jnp.int32)
    idx = jnp.stack([ci, chip]).astype(jnp.int32)
    count = [0]

    def sum_block(gs, r1):
        t0 = count[0]
        count[0] += len(gs)
        parts = []
        for k, (g, r) in enumerate(zip(gs, r1)):
            W = g.shape[-1]
            p = _sum_sibling(t0 + k, cidx, g.reshape(2, -1, W), r.reshape(-1, W))
            parts.append(p.reshape(g.shape[1:]))
        return parts

    wa1, wb1 = _run_carried(_Carried("gather", [shards[0]["f1a"], shards[0]["f1b"]]), "first")
    small = {n: P[n] for n in _SMALL if n not in ('conv_dw', 'short_dw')}
    loss, dx, blocks, small_g = _fwd_bwd(x, loss_target, shards, small, (cw, sw), (wa1, _merge(wb1)), sum_block)

    finals = {}
    for t, (l, name, parts, r2) in enumerate(blocks):
        prev = finals.get(name, [None] * len(parts))
        finals[name] = [_sum_final(f"{t}_{k}", idx, p, r, l, L, pv) for k, (p, r, pv) in enumerate(zip(parts, r2, prev))]
    shared, tot = _share_final_and_reduce_small(finals["f1"] + finals["f2"] + finals["mx"],
                                                _pack([small_g[n] for n in _SMALL_REDUCED] + [loss.reshape(1)]))
    f_f1, f_f2, f_in, f_p, f_o = [f.reshape(L, -1, f.shape[-1]) for f in shared]
    *tot, loss = _unpack(tot, [small_g[n].shape for n in _SMALL_REDUCED] + [()])
    tot = dict(zip(_SMALL_REDUCED, tot))
    tot['conv_dw'] = lax.dynamic_slice_in_dim(tot['conv_dw'], chip * cw, cw, axis=2)
    tot['short_dw'] = lax.dynamic_slice_in_dim(tot['short_dw'], chip * sw, sw, axis=2)

    F4 = P['ffn1_w_down'].shape[1]
    dc, ds = P['w_pa'].shape[1], P['w_pb'].shape[1]
    src = {'ffn1_w_gate': (f_f1, 0), 'ffn1_w_up': (f_f1, F4), 'ffn1_w_down': (f_f1, 2 * F4),
           'ffn2_w_gate': (f_f2, 0), 'ffn2_w_up': (f_f2, F4), 'ffn2_w_down': (f_f2, 2 * F4),
           'w_in': (f_in, 0), 'w_pa': (f_p, 0), 'w_pb': (f_p, dc), 'w_pc': (f_p, dc + ds), 'w_o': (f_o, 0)}
    grads, deltas, new_m, new_v = {}, {}, {}, {}
    for n in _BIG:
        gfull, row0 = src[n]
        if n in _TRANSPOSED:
            outs = _adam_big(n, tr(P[n]), tr(M[n]), tr(V[n]), gfull, row0)
            grads[n], deltas[n], new_m[n], new_v[n] = [tr(o) for o in outs]
        else:
            grads[n], deltas[n], new_m[n], new_v[n] = _adam_big(n, P[n], M[n], V[n], gfull, row0)
    as2d = lambda a: a.reshape(1, -1) if a.ndim == 1 else a
    d_s, m_s, v_s = _adam_small([as2d(P[n]) for n in _SMALL], [as2d(tot[n]) for n in _SMALL],
                                [as2d(M[n]) for n in _SMALL], [as2d(V[n]) for n in _SMALL])
    for n, d, mm, vv in zip(_SMALL, d_s, m_s, v_s):
        shp = P[n].shape
        grads[n], deltas[n], new_m[n], new_v[n] = tot[n], d.reshape(shp), mm.reshape(shp), vv.reshape(shp)

    return (loss, dx, *[grads[n] for n in _WEIGHTS], *[deltas[n] for n in _WEIGHTS],
            *[new_m[n] for n in _WEIGHTS], *[new_v[n] for n in _WEIGHTS])


def kernel(x, norm_ffn1_g, ffn1_w_gate, ffn1_w_up, ffn1_w_down, norm_mix_g, w_in, conv_dw, conv_b, conv_ln_g, conv_ln_b, w_pa, short_dw, w_pb, pool_w, pool_scale, w_pc, w_o, norm_ffn2_g, ffn2_w_gate, ffn2_w_up, ffn2_w_down, final_norm_g, loss_target, m_norm_ffn1_g, m_ffn1_w_gate, m_ffn1_w_up, m_ffn1_w_down, m_norm_mix_g, m_w_in, m_conv_dw, m_conv_b, m_conv_ln_g, m_conv_ln_b, m_w_pa, m_short_dw, m_w_pb, m_pool_w, m_pool_scale, m_w_pc, m_w_o, m_norm_ffn2_g, m_ffn2_w_gate, m_ffn2_w_up, m_ffn2_w_down, m_final_norm_g, v_norm_ffn1_g, v_ffn1_w_gate, v_ffn1_w_up, v_ffn1_w_down, v_norm_mix_g, v_w_in, v_conv_dw, v_conv_b, v_conv_ln_g, v_conv_ln_b, v_w_pa, v_short_dw, v_w_pb, v_pool_w, v_pool_scale, v_w_pc, v_w_o, v_norm_ffn2_g, v_ffn2_w_gate, v_ffn2_w_up, v_ffn2_w_down, v_final_norm_g):
    args = locals()
    P = {n: args[n] for n in _WEIGHTS}
    M = {n: args["m_" + n] for n in _WEIGHTS}
    V = {n: args["v_" + n] for n in _WEIGHTS}
    return _step(P, M, V, x, loss_target)
```
